```python
import math
import numpy as np
import jax
import jax.numpy as jnp
from jax import lax

D_MODEL = 1024
BATCH = 32
SEQ = 256
DEPTH = 4
DEC_BATCH = 8
DEC_SEQ = 4096
PAST_LEN = 512

GRID_W = 64
MIX_W = D_MODEL
N_GROUPS = 4
GROUP_W = MIX_W // N_GROUPS
N_HEADS = 4
DN_DK = GROUP_W // N_HEADS
DN_DV = GROUP_W // N_HEADS
SHORT_CONV = 3
REC_CHUNK = 64
DF_DH = GROUP_W // (2 * N_HEADS)
DF_DV = GROUP_W // N_HEADS
ROPE_THETA = 10000.0
Q_BLOCK = 128
HY_ORDER = 2
HY_BANDS = 8
HY_FEAT = 2 * HY_BANDS + 1
HY_HIDDEN = 64
HY_FAST_DECAY = 0.3
HY_SLOW_DECAY = 1.5
HY_TARGET = 1e-2
HY_SHIFT = 0.05
ML_DK = GROUP_W // N_HEADS
ML_DV = GROUP_W // N_HEADS
D_FF = 11 * D_MODEL // 4
N_EXPERTS = 8
TOP_K = 2
D_FF_EXPERT = D_FF // 2
N_DENSE = (DEPTH + 1) // 2
N_MOE = DEPTH // 2
EPS = 1e-6
F32 = jnp.float32

IN_SIZES = (GROUP_W, GROUP_W, GROUP_W, GROUP_W, 2 * N_HEADS, 2 * N_HEADS,
            GROUP_W, GROUP_W, GROUP_W,
            GROUP_W, GROUP_W, GROUP_W,
            GROUP_W, GROUP_W, GROUP_W, GROUP_W, 2 * N_HEADS, 2 * N_HEADS)
IN_WIDTH = sum(IN_SIZES)
IN_SPLIT_IDX = tuple(int(s) for s in np.cumsum(IN_SIZES)[:-1])

kernel_name = 'hybrid_dit_prefix_ctx_step'


def rms_norm(x, g):
    xf = x.astype(F32)
    y = xf * lax.rsqrt(jnp.mean(xf * xf, axis=-1, keepdims=True) + EPS)
    return (y * g.astype(F32)).astype(x.dtype)


def l2_norm(x):
    return x * lax.rsqrt(jnp.sum(x * x, axis=-1, keepdims=True) + EPS)


def centred_dwconv(x, w):
    k = w.shape[0]
    return lax.conv_general_dilated(x, w[:, None, :].astype(x.dtype), (1,), [(k // 2, k // 2)],
                                    dimension_numbers=('NWC', 'WIO', 'NWC'),
                                    feature_group_count=x.shape[-1])


def _flip(t):
    return jnp.flip(t, axis=1)


def _chunks(t):
    b, l, h = t.shape[:3]
    t = t.reshape((b, l // REC_CHUNK, REC_CHUNK, h) + t.shape[3:])
    return jnp.moveaxis(jnp.moveaxis(t, 1, 0), 3, 2)


def _unchunk(t):
    t = jnp.moveaxis(jnp.moveaxis(t, 2, 3), 0, 1)
    return t.reshape((t.shape[0], t.shape[1] * t.shape[2]) + t.shape[3:])


def gated_delta_chunked(q, k, v, g, beta, s0):
    c = REC_CHUNK
    qc, kc, vc = _chunks(q * q.shape[-1] ** -0.5), _chunks(k), _chunks(v)
    gc = jnp.cumsum(_chunks(g), axis=-1)
    bc = _chunks(beta)[..., None]
    tri = jnp.tril(jnp.ones((c, c), bool))
    strict = jnp.tril(jnp.ones((c, c), bool), -1)
    decay = jnp.exp(jnp.where(tri, gc[..., :, None] - gc[..., None, :], -jnp.inf))
    kk = jnp.einsum('nbhid,nbhjd->nbhij', kc, kc)
    m = jnp.eye(c, dtype=F32) + jnp.where(strict, bc * kk * decay, 0.0)
    u = lax.linalg.triangular_solve(m, vc * bc, left_side=True, lower=True, unit_diagonal=True)
    w = lax.linalg.triangular_solve(m, kc * bc * jnp.exp(gc)[..., None], left_side=True, lower=True,
                                    unit_diagonal=True)
    qk = jnp.einsum('nbhid,nbhjd->nbhij', qc, kc) * decay

    def step(s, xs):
        q_i, k_i, u_i, w_i, g_i, qk_i = xs
        v_new = u_i - jnp.einsum('bhck,bhkv->bhcv', w_i, s)
        o = (jnp.einsum('bhck,bhkv->bhcv', q_i * jnp.exp(g_i)[..., None], s)
             + jnp.einsum('bhij,bhjv->bhiv', qk_i, v_new))
        g_last = g_i[..., -1:]
        s = (s * jnp.exp(g_last)[..., None]
             + jnp.einsum('bhck,bhcv->bhkv', k_i * jnp.exp(g_last - g_i)[..., None], v_new))
        return s, o

    s, o = lax.scan(step, s0, (qc, kc, u, w, gc, qk))
    return _unchunk(o), s


def deltanet_bidir(q, k, v, g, beta, s0):
    o_f, s_f = gated_delta_chunked(q, k, v, g[:, :, 0], beta[:, :, 0], s0[:, 0])
    o_b, s_b = gated_delta_chunked(_flip(q), _flip(k), _flip(v), _flip(g[:, :, 1]), _flip(beta[:, :, 1]), s0[:, 1])
    return o_f + _flip(o_b), jnp.stack([s_f, s_b], axis=1)


def mlstm_chunked(q, k, v, i_pre, logf, c0, n0, m0):
    c = REC_CHUNK
    qc, kc, vc = _chunks(q), _chunks(k * k.shape[-1] ** -0.5), _chunks(v)
    ic = _chunks(i_pre)
    bcum = jnp.cumsum(_chunks(logf), axis=-1)
    tri = jnp.tril(jnp.ones((c, c), bool))
    dlog = jnp.where(tri, bcum[..., :, None] - bcum[..., None, :] + ic[..., None, :], -jnp.inf)
    dmax = jnp.max(dlog, axis=-1)
    wend = bcum[..., -1:] - bcum + ic
    qk = jnp.einsum('nbhid,nbhjd->nbhij', qc, kc)

    def step(carry, xs):
        c_s, n_s, m_s = carry
        q_i, k_i, v_i, b_i, dlog_i, dmax_i, wend_i, qk_i = xs
        a = b_i + m_s[..., None]
        m_t = jnp.maximum(a, dmax_i)
        inter = jnp.exp(a - m_t)
        s = qk_i * jnp.exp(dlog_i - m_t[..., None])
        num = (inter[..., None] * jnp.einsum('bhcd,bhde->bhce', q_i, c_s)
               + jnp.einsum('bhij,bhje->bhie', s, v_i))
        den = inter * jnp.einsum('bhcd,bhd->bhc', q_i, n_s) + jnp.sum(s, axis=-1)
        h = num / jnp.maximum(jnp.abs(den), jnp.exp(-m_t))[..., None]
        a_end = b_i[..., -1] + m_s
        m_new = jnp.maximum(a_end, jnp.max(wend_i, axis=-1))
        dec = jnp.exp(a_end - m_new)
        ws = jnp.exp(wend_i - m_new[..., None])
        c_new = dec[..., None, None] * c_s + jnp.einsum('bhcd,bhce->bhde', k_i * ws[..., None], v_i)
        n_new = dec[..., None] * n_s + jnp.einsum('bhc,bhcd->bhd', ws, k_i)
        return (c_new, n_new, m_new), h

    state, h = lax.scan(step, (c0, n0, m0), (qc, kc, vc, bcum, dlog, dmax, wend, qk))
    return _unchunk(h), state


def mlstm_bidir(q, k, v, i_pre, logf, c0, n0, m0):
    h_f, (c_f, n_f, m_f) = mlstm_chunked(q, k, v, i_pre[:, :, 0], logf[:, :, 0], c0[:, 0], n0[:, 0], m0[:, 0])
    h_b, (c_b, n_b, m_b) = mlstm_chunked(_flip(q), _flip(k), _flip(v), _flip(i_pre[:, :, 1]), _flip(logf[:, :, 1]),
                                         c0[:, 1], n0[:, 1], m0[:, 1])
    return (h_f + _flip(h_b), jnp.stack([c_f, c_b], axis=1), jnp.stack([n_f, n_b], axis=1),
            jnp.stack([m_f, m_b], axis=1))


def axial_rope(L):
    rows = L // GRID_W
    r = jnp.repeat(jnp.arange(rows, dtype=F32), GRID_W)
    col = jnp.tile(jnp.arange(GRID_W, dtype=F32), rows)
    n_freq = DF_DH // 4
    inv = ROPE_THETA ** (-jnp.arange(n_freq, dtype=F32) / n_freq)
    ang = jnp.concatenate([r[:, None] * inv, col[:, None] * inv], axis=-1)
    return jnp.cos(ang), jnp.sin(ang)


def apply_rope(x, cos, sin):
    half = x.shape[-1] // 2
    x1, x2 = x[..., :half], x[..., half:]
    cc = cos[None, :, None, None, :].astype(x.dtype)
    ss = sin[None, :, None, None, :].astype(x.dtype)
    return jnp.concatenate([x1 * cc - x2 * ss, x2 * cc + x1 * ss], axis=-1)


def diff_attention(q, k, v, lam):
    b, lq, h, _, d = q.shape
    nb = lq // Q_BLOCK
    qb = jnp.swapaxes(q.reshape(b, nb, Q_BLOCK, h, 2, d), 0, 1)
    scale = d ** -0.5

    def one(qi):
        s = jnp.einsum('bqhmd,bkhmd->bhmqk', qi, k).astype(F32) * scale
        p = jax.nn.softmax(s, axis=-1)
        wgt = p[:, :, 0] - lam * p[:, :, 1]
        return jnp.einsum('bhqk,bkhe->bqhe', wgt.astype(v.dtype), v)

    o = lax.map(one, qb)
    return jnp.swapaxes(o, 0, 1).reshape(b, lq, h, v.shape[-1])


def hyena_filters(L, w1, b1, w2, b2, w3, freq):
    pos = jnp.arange(L, dtype=F32)
    bands = jnp.arange(1, HY_BANDS + 1, dtype=F32)
    ang = (2.0 * math.pi / L) * pos[:, None] * bands[None, :]
    feat = jnp.concatenate([pos[:, None] / L, jnp.cos(ang), jnp.sin(ang)], axis=-1)
    freq = freq.astype(F32)
    h = jnp.sin(freq[0] * (feat @ w1.astype(F32) + b1.astype(F32)))
    h = jnp.sin(freq[1] * (h @ w2.astype(F32) + b2.astype(F32)))
    h = h @ w3.astype(F32)
    rates = jnp.linspace(-math.log(HY_TARGET) / HY_FAST_DECAY, -math.log(HY_TARGET) / HY_SLOW_DECAY,
                         GROUP_W, dtype=F32)
    rates = jnp.tile(rates, HY_ORDER)
    offset = jnp.abs(pos - L // 2) / L
    h = h * (jnp.exp(-offset[:, None] * rates[None, :]) + HY_SHIFT)
    h = h * lax.rsqrt(jnp.sum(h * h, axis=0, keepdims=True) + EPS)
    return h.reshape(L, HY_ORDER, GROUP_W)


def fft_long_conv(u, h):
    L = u.shape[1]
    n = 2 * L
    y = jnp.fft.irfft(jnp.fft.rfft(u, n=n, axis=1) * jnp.fft.rfft(h, n=n, axis=0)[None], n=n, axis=1)
    return y[:, L // 2: L // 2 + L]


def swiglu(x, wg, wu, wd):
    return (jax.nn.silu(x @ wg) * (x @ wu)) @ wd


def moe_swiglu(x, router, wg, wu, wd):
    b, l, d = x.shape
    xt = x.reshape(b * l, d)
    logits = (xt @ router).astype(F32)
    top_v, top_i = lax.top_k(logits, TOP_K)
    gates = jax.nn.softmax(top_v, axis=-1)
    combine = jnp.sum(jax.nn.one_hot(top_i, N_EXPERTS, dtype=F32) * gates[..., None], axis=1).astype(x.dtype)
    out = jnp.zeros_like(xt)
    for e in range(N_EXPERTS):
        out = out + combine[:, e:e + 1] * swiglu(xt, wg[e], wu[e], wd[e])
    return out.reshape(b, l, d)


def mixer(u, l, p, cache):
    b, L, _ = u.shape
    dt = u.dtype
    (dq, dk, dv, dgate, dbeta, da, fq, fk, fv, hx1, hx2, hv, mq, mk, mv, mo, mi, mf) = jnp.split(
        jnp.einsum('bld,de->ble', u, p['w_in'][l]), IN_SPLIT_IDX, axis=-1)
    latent = cache is not None

    qkv = jax.nn.silu(centred_dwconv(jnp.concatenate([dq, dk, dv], axis=-1), p['dn_conv'][l])).astype(F32)
    q, k, v = jnp.split(qkv, 3, axis=-1)
    q = l2_norm(q.reshape(b, L, N_HEADS, DN_DK))
    k = l2_norm(k.reshape(b, L, N_HEADS, DN_DK))
    v = v.reshape(b, L, N_HEADS, DN_DV)
    beta = jax.nn.sigmoid(dbeta.astype(F32).reshape(b, L, 2, N_HEADS))
    g = -jnp.exp(p['dn_a_log'][l].astype(F32)) * jax.nn.softplus(
        da.astype(F32).reshape(b, L, 2, N_HEADS) + p['dn_dt_bias'][l].astype(F32))
    s0 = cache[2].astype(F32) if latent else jnp.zeros((b, 2, N_HEADS, DN_DK, DN_DV), F32)
    o_dn, s_dn = deltanet_bidir(q, k, v, g, beta, s0)
    y_dn = (rms_norm(o_dn, p['dn_norm'][l]) * jax.nn.silu(dgate.astype(F32).reshape(b, L, N_HEADS, DN_DV)))
    y_dn = y_dn.reshape(b, L, GROUP_W).astype(dt)

    qd = rms_norm(fq.reshape(b, L, N_HEADS, 2, DF_DH), p['df_q_norm'][l])
    kd = rms_norm(fk.reshape(b, L, N_HEADS, 2, DF_DH), p['df_k_norm'][l])
    vd = fv.reshape(b, L, N_HEADS, DF_DV)
    if latent:
        cos, sin = axial_rope(L)
        qa = apply_rope(qd, cos, sin)
        ka = jnp.concatenate([apply_rope(kd, cos, sin), cache[0].astype(dt)], axis=1)
        va = jnp.concatenate([vd, cache[1].astype(dt)], axis=1)
    else:
        qa, ka, va = qd, kd, vd
    lam_init = 0.8 - 0.6 * math.exp(-0.3 * l)
    lp = p['df_lambda'][l].astype(F32)
    lam = jnp.exp(jnp.sum(lp[0] * lp[1])) - jnp.exp(jnp.sum(lp[2] * lp[3])) + lam_init
    o_df = diff_attention(qa, ka, va, lam)
    y_df = (rms_norm(o_df, p['df_norm'][l]) * (1.0 - lam_init)).reshape(b, L, GROUP_W).astype(dt)

    z = centred_dwconv(jnp.concatenate([hx1, hx2, hv], axis=-1), p['hy_conv'][l]).astype(F32)
    x1, x2, zv = jnp.split(z, 3, axis=-1)
    filt = hyena_filters(L, p['hy_w1'][l], p['hy_b1'][l], p['hy_w2'][l], p['hy_b2'][l], p['hy_w3'][l],
                         p['hy_freq'][l])
    skip = p['hy_skip'][l].astype(F32)
    z1 = x1 * (fft_long_conv(zv, filt[:, 0]) + skip[0] * zv)
    y_hy = (x2 * (fft_long_conv(z1, filt[:, 1]) + skip[1] * z1)).astype(dt)

    qm = mq.astype(F32).reshape(b, L, N_HEADS, ML_DK)
    km = mk.astype(F32).reshape(b, L, N_HEADS, ML_DK)
    vm = mv.astype(F32).reshape(b, L, N_HEADS, ML_DV)
    i_pre = mi.astype(F32).reshape(b, L, 2, N_HEADS) + p['ml_i_bias'][l].astype(F32)
    logf = jax.nn.log_sigmoid(mf.astype(F32).reshape(b, L, 2, N_HEADS) + p['ml_f_bias'][l].astype(F32))
    if latent:
        c0, n0, m0 = cache[3].astype(F32), cache[4].astype(F32), cache[5].astype(F32)
    else:
        c0 = jnp.zeros((b, 2, N_HEADS, ML_DK, ML_DV), F32)
        n0 = jnp.zeros((b, 2, N_HEADS, ML_DK), F32)
        m0 = jnp.zeros((b, 2, N_HEADS), F32)
    h_ml, c_ml, n_ml, m_ml = mlstm_bidir(qm, km, vm, i_pre, logf, c0, n0, m0)
    y_ml = rms_norm(h_ml, p['ml_norm'][l]) * jax.nn.sigmoid(mo.astype(F32).reshape(b, L, N_HEADS, ML_DV))
    y_ml = y_ml.reshape(b, L, GROUP_W).astype(dt)

    y = jnp.einsum('ble,ed->bld', jnp.concatenate([y_dn, y_df, y_hy, y_ml], axis=-1), p['w_out'][l])
    ctx = None if latent else (kd, vd, s_dn, c_ml, n_ml, m_ml)
    return y, ctx


def trunk_layer(x, cond, l, p, cache):
    mod = jnp.einsum('bd,de->be', jax.nn.silu(cond), p['w_mod'][l]) + p['b_mod'][l]
    sh1, sc1, g1, sh2, sc2, g2 = jnp.split(mod[:, None, :], 6, axis=-1)
    h = rms_norm(x, p['norm1_g'][l]) * (1.0 + sc1) + sh1
    y, ctx = mixer(h, l, p, cache)
    x = x + g1 * y
    h = rms_norm(x, p['norm2_g'][l]) * (1.0 + sc2) + sh2
    j = l // 2
    if l % 2 == 0:
        f = swiglu(h, p['ffn_w_gate'][j], p['ffn_w_up'][j], p['ffn_w_down'][j])
    else:
        f = moe_swiglu(h, p['moe_router'][j], p['moe_w_gate'][j], p['moe_w_up'][j], p['moe_w_down'][j])
    return x + g2 * f, ctx


def setup_inputs(seed: int = 0) -> dict:
    key = jax.random.key(seed)
    ks = list(jax.random.split(key, 64))

    def nrm(shape, scale=1.0):
        return scale * jax.random.normal(ks.pop(), shape, F32)

    def gain(shape):
        return 1.0 + 0.05 * jax.random.normal(ks.pop(), shape, F32)

    dt_init = jnp.exp(jax.random.uniform(ks.pop(), (DEPTH, 2, N_HEADS), F32, math.log(1e-3), math.log(1e-1)))
    return {
        'x_prompt': nrm((BATCH, SEQ, D_MODEL)),
        'x_sample': nrm((DEC_BATCH, DEC_SEQ, D_MODEL)),
        'cache_diff_k': nrm((DEC_BATCH, DEPTH, PAST_LEN, N_HEADS, 2, DF_DH)),
        'cache_diff_v': nrm((DEC_BATCH, DEPTH, PAST_LEN, N_HEADS, DF_DV)),
        'state_delta': nrm((DEC_BATCH, DEPTH, 2, N_HEADS, DN_DK, DN_DV), 0.1),
        'state_mlstm_c': nrm((DEC_BATCH, DEPTH, 2, N_HEADS, ML_DK, ML_DV), 0.1),
        'state_mlstm_n': nrm((DEC_BATCH, DEPTH, 2, N_HEADS, ML_DK), 0.1),
        'state_mlstm_m': nrm((DEC_BATCH, DEPTH, 2, N_HEADS)),
        'c': nrm((DEC_BATCH, D_MODEL)),
        'c_ctx': nrm((D_MODEL,)),
        'norm1_g': gain((DEPTH, D_MODEL)),
        'norm2_g': gain((DEPTH, D_MODEL)),
        'w_mod': nrm((DEPTH, D_MODEL, 6 * D_MODEL), 0.5 * D_MODEL ** -0.5),
        'b_mod': nrm((DEPTH, 6 * D_MODEL), 0.02),
        'w_in': nrm((DEPTH, D_MODEL, IN_WIDTH), D_MODEL ** -0.5),
        'w_out': nrm((DEPTH, MIX_W, D_MODEL), MIX_W ** -0.5),
        'dn_conv': nrm((DEPTH, SHORT_CONV, 3 * GROUP_W), SHORT_CONV ** -0.5),
        'dn_a_log': jnp.log(jax.random.uniform(ks.pop(), (DEPTH, 2, N_HEADS), F32, 1.0, 16.0)),
        'dn_dt_bias': jnp.log(jnp.expm1(dt_init)),
        'dn_norm': gain((DEPTH, DN_DV)),
        'df_q_norm': gain((DEPTH, DF_DH)),
        'df_k_norm': gain((DEPTH, DF_DH)),
        'df_lambda': nrm((DEPTH, 4, DF_DH), 0.1),
        'df_norm': gain((DEPTH, DF_DV)),
        'hy_conv': nrm((DEPTH, SHORT_CONV, 3 * GROUP_W), SHORT_CONV ** -0.5),
        'hy_w1': nrm((DEPTH, HY_FEAT, HY_HIDDEN), HY_FEAT ** -0.5),
        'hy_b1': nrm((DEPTH, HY_HIDDEN), 0.1),
        'hy_w2': nrm((DEPTH, HY_HIDDEN, HY_HIDDEN), HY_HIDDEN ** -0.5),
        'hy_b2': nrm((DEPTH, HY_HIDDEN), 0.1),
        'hy_w3': nrm((DEPTH, HY_HIDDEN, HY_ORDER * GROUP_W), HY_HIDDEN ** -0.5),
        'hy_freq': gain((DEPTH, 2, HY_HIDDEN)),
        'hy_skip': nrm((DEPTH, HY_ORDER, GROUP_W)),
        'ml_i_bias': nrm((DEPTH, 2, N_HEADS), 0.1),
        'ml_f_bias': jnp.linspace(3.0, 6.0, N_HEADS, dtype=F32)[None, None, :] + nrm((DEPTH, 2, N_HEADS), 0.1),
        'ml_norm': gain((DEPTH, ML_DV)),
        'ffn_w_gate': nrm((N_DENSE, D_MODEL, D_FF), D_MODEL ** -0.5),
        'ffn_w_up': nrm((N_DENSE, D_MODEL, D_FF), D_MODEL ** -0.5),
        'ffn_w_down': nrm((N_DENSE, D_FF, D_MODEL), D_FF ** -0.5),
        'moe_router': nrm((N_MOE, D_MODEL, N_EXPERTS), D_MODEL ** -0.5),
        'moe_w_gate': nrm((N_MOE, N_EXPERTS, D_MODEL, D_FF_EXPERT), D_MODEL ** -0.5),
        'moe_w_up': nrm((N_MOE, N_EXPERTS, D_MODEL, D_FF_EXPERT), D_MODEL ** -0.5),
        'moe_w_down': nrm((N_MOE, N_EXPERTS, D_FF_EXPERT, D_MODEL), D_FF_EXPERT ** -0.5),
    }


def reference(x_prompt, x_sample, cache_diff_k, cache_diff_v, state_delta, state_mlstm_c, state_mlstm_n,
              state_mlstm_m, c, c_ctx, norm1_g, norm2_g, w_mod, b_mod, w_in, w_out, dn_conv, dn_a_log,
              dn_dt_bias, dn_norm, df_q_norm, df_k_norm, df_lambda, df_norm, hy_conv, hy_w1, hy_b1, hy_w2,
              hy_b2, hy_w3, hy_freq, hy_skip, ml_i_bias, ml_f_bias, ml_norm, ffn_w_gate, ffn_w_up, ffn_w_down,
              moe_router, moe_w_gate, moe_w_up, moe_w_down):
    p = dict(norm1_g=norm1_g, norm2_g=norm2_g, w_mod=w_mod, b_mod=b_mod, w_in=w_in, w_out=w_out,
             dn_conv=dn_conv, dn_a_log=dn_a_log, dn_dt_bias=dn_dt_bias, dn_norm=dn_norm,
             df_q_norm=df_q_norm, df_k_norm=df_k_norm, df_lambda=df_lambda, df_norm=df_norm,
             hy_conv=hy_conv, hy_w1=hy_w1, hy_b1=hy_b1, hy_w2=hy_w2, hy_b2=hy_b2, hy_w3=hy_w3,
             hy_freq=hy_freq, hy_skip=hy_skip, ml_i_bias=ml_i_bias, ml_f_bias=ml_f_bias, ml_norm=ml_norm,
             ffn_w_gate=ffn_w_gate, ffn_w_up=ffn_w_up, ffn_w_down=ffn_w_down, moe_router=moe_router,
             moe_w_gate=moe_w_gate, moe_w_up=moe_w_up, moe_w_down=moe_w_down)

    x = x_prompt
    ks_, vs_, sd_, cm_, nm_, mm_ = [], [], [], [], [], []
    cond_ctx = c_ctx[None, :]
    for l in range(DEPTH):
        x, (kd, vd, s_dn, c_ml, n_ml, m_ml) = trunk_layer(x, cond_ctx, l, p, None)
        ks_.append(kd)
        vs_.append(vd)
        sd_.append(s_dn)
        cm_.append(c_ml)
        nm_.append(n_ml)
        mm_.append(m_ml)
    y_prompt = x
    new_diff_k = jnp.stack(ks_, axis=1)
    new_diff_v = jnp.stack(vs_, axis=1)
    new_delta = jnp.stack(sd_, axis=1)
    new_mlstm_c = jnp.stack(cm_, axis=1)
    new_mlstm_n = jnp.stack(nm_, axis=1)
    new_mlstm_m = jnp.stack(mm_, axis=1)

    x = x_sample
    for l in range(DEPTH):
        cache = (cache_diff_k[:, l], cache_diff_v[:, l], state_delta[:, l], state_mlstm_c[:, l],
                 state_mlstm_n[:, l], state_mlstm_m[:, l])
        x, _ = trunk_layer(x, c, l, p, cache)
    y_sample = x

    return (y_prompt, y_sample, new_diff_k, new_diff_v, new_delta, new_mlstm_c, new_mlstm_n, new_mlstm_m)
```

```python
import functools
import math

import jax
import jax.numpy as jnp
from jax import lax
from jax.experimental import pallas as pl
from jax.experimental.pallas import tpu as pltpu

F32 = jnp.float32
BF16 = jnp.bfloat16

N_HEADS = 4
HEAD_DIM = 64
GROUP_W = N_HEADS * HEAD_DIM
DF_DH = 32
CHUNK = 64
GRID_W = 64
ROPE_THETA = 10000.0
HY_BANDS = 8
HY_FAST_DECAY = 0.3
HY_SLOW_DECAY = 1.5
HY_TARGET = 1e-2
HY_SHIFT = 0.05
N_EXPERTS = 8
EPS = 1e-6
NEG = -1e30
LANES = 128
VMEM_LIMIT = 56 * 1024 * 1024

COL_DN, COL_HY, COL_ML, COL_DF = 0, 1, 2, 3
COL_DGATE, COL_MO = 12, 13
COL_SMALL = 28
Z_WIDTH = 29 * LANES
LANE_BETA, LANE_A, LANE_I, LANE_F = 0, 8, 16, 24


def _tile(n, pref):
    t = min(n, pref)
    while n % t:
        t -= LANES if t > LANES else 8
    return t


def _cp(*sem):
    return pltpu.CompilerParams(dimension_semantics=sem, vmem_limit_bytes=VMEM_LIMIT)


def _split3(x):
    x1 = x.astype(BF16)
    r = x - x1.astype(F32)
    x2 = r.astype(BF16)
    r = r - x2.astype(F32)
    return x1, x2, r.astype(BF16)


def _dot(a, b):
    return jnp.dot(a.astype(BF16), b.astype(BF16), preferred_element_type=F32)


def _dot_nt(a, b):
    return lax.dot_general(a.astype(BF16), b.astype(BF16), (((1,), (1,)), ((), ())),
                           preferred_element_type=F32)


def _dot_tn(a, b):
    return lax.dot_general(a.astype(BF16), b.astype(BF16), (((0,), (0,)), ((), ())),
                           preferred_element_type=F32)


def _dot_exact_l(m, x):
    return sum(jnp.dot(m, p, preferred_element_type=F32) for p in _split3(x))


def _dot_exact_r(x, m):
    return sum(jnp.dot(p, m, preferred_element_type=F32) for p in _split3(x))


def _dot_hi(a, b):
    a1, a2, _ = _split3(a)
    b1, b2, _ = _split3(b)
    return (jnp.dot(a1, b1, preferred_element_type=F32) + jnp.dot(a1, b2, preferred_element_type=F32)
            + jnp.dot(a2, b1, preferred_element_type=F32))


def _seg_mean(x, seg):
    w = x.shape[-1]
    sh = int(math.log2(seg))
    r = lax.shift_right_logical(lax.broadcasted_iota(jnp.int32, (w, w), 0), sh)
    c = lax.shift_right_logical(lax.broadcasted_iota(jnp.int32, (w, w), 1), sh)
    bd = jnp.where(r == c, 1.0, 0.0).astype(BF16)
    return _dot_exact_r(x, bd) * (1.0 / seg)


def _row_bcast(col):
    n = col.shape[0]
    lane = lax.broadcasted_iota(jnp.int32, (n, LANES), 1)
    sel = jnp.where(lane == 0, 1.0, 0.0).astype(BF16)
    src = jnp.where(lane == 0, col, 0.0)
    return sum(lax.dot_general(sel, p, (((1,), (1,)), ((), ())), preferred_element_type=F32)
               for p in _split3(src))


def _lane_pick(x, lane_idx):
    lane = lax.broadcasted_iota(jnp.int32, x.shape, 1)
    return jnp.sum(jnp.where(lane == lane_idx, x, 0.0), axis=1, keepdims=True)


def _sigmoid(x):
    return 1.0 / (1.0 + jnp.exp(-x))


def _softplus(x):
    return jnp.maximum(x, 0.0) + jnp.log1p(jnp.exp(-jnp.abs(x)))


def _order_masks(d, n):
    row = lax.broadcasted_iota(jnp.int32, (n, n), 0)
    col = lax.broadcasted_iota(jnp.int32, (n, n), 1)
    rel = (row - col) * (1 - 2 * d)
    return rel >= 0, rel > 0


def _mod_kernel(c_ref, w_ref, b_ref, o_ref):
    c = c_ref[...]
    o_ref[0] = _dot(c * _sigmoid(c), w_ref[0]) + b_ref[0]


def _modulation(cond, w_mod, b_mod):
    depth, d, n = w_mod.shape
    r = cond.shape[0]
    tn = n // 4
    return pl.pallas_call(
        _mod_kernel,
        grid=(depth, n // tn),
        in_specs=[pl.BlockSpec((r, d), lambda l, j: (0, 0)),
                  pl.BlockSpec((1, d, tn), lambda l, j: (l, 0, j)),
                  pl.BlockSpec((1, 1, tn), lambda l, j: (l, 0, j))],
        out_specs=pl.BlockSpec((1, r, tn), lambda l, j: (l, 0, j)),
        out_shape=jax.ShapeDtypeStruct((depth, r, n), F32),
        compiler_params=_cp("parallel", "parallel"),
        name="modulation",
    )(cond, w_mod, b_mod.reshape(depth, 1, n))


def _modulated_norm(x, gain, shift, scale):
    var = jnp.mean(x * x, axis=-1, keepdims=True)
    return x * lax.rsqrt(var + EPS) * gain * (1.0 + scale) + shift


def _proj_in_kernel(x_ref, mod_ref, g_ref, w_ref, o_ref):
    h = _modulated_norm(x_ref[0], g_ref[...], mod_ref[0, 0:1, :], mod_ref[0, 1:2, :])
    o_ref[0] = jnp.dot(h.astype(BF16), w_ref[...], preferred_element_type=F32)


def _proj_in(x, mod, gain, w):
    b, l, d = x.shape
    tm = _tile(l, 256)
    per_batch = mod.shape[0] > 1
    return pl.pallas_call(
        _proj_in_kernel,
        grid=(b, l // tm),
        in_specs=[pl.BlockSpec((1, tm, d), lambda i, j: (i, j, 0)),
                  pl.BlockSpec((1, 6, d), lambda i, j: (i if per_batch else 0, 0, 0)),
                  pl.BlockSpec((1, d), lambda i, j: (0, 0)),
                  pl.BlockSpec((d, Z_WIDTH), lambda i, j: (0, 0))],
        out_specs=pl.BlockSpec((1, tm, Z_WIDTH), lambda i, j: (i, j, 0)),
        out_shape=jax.ShapeDtypeStruct((b, l, Z_WIDTH), F32),
        compiler_params=_cp("parallel", "parallel"),
        name="proj_in",
    )(x, mod, gain.reshape(1, d), w)


def _dwconv_kernel(z_ref, zp_ref, zn_ref, w_ref, o_ref, *, mode):
    i = pl.program_id(1)
    z = z_ref[0]
    tm = z.shape[0]
    prev_row = jnp.where(i > 0, zp_ref[0, 7:8, :], 0.0)
    next_row = jnp.where(i < pl.num_programs(1) - 1, zn_ref[0, 0:1, :], 0.0)
    rid = lax.broadcasted_iota(jnp.int32, z.shape, 0)
    zm1 = jnp.where(rid == 0, prev_row, pltpu.roll(z, 1, 0))
    zp1 = jnp.where(rid == tm - 1, next_row, pltpu.roll(z, tm - 1, 0))
    y = zm1 * w_ref[0:1, :] + z * w_ref[1:2, :] + zp1 * w_ref[2:3, :]
    if mode == "deltanet":
        y = y * _sigmoid(y)
        q, k, v = y[:, :GROUP_W], y[:, GROUP_W:2 * GROUP_W], y[:, 2 * GROUP_W:]
        q = q * lax.rsqrt(_seg_mean(q * q, HEAD_DIM) * HEAD_DIM + EPS) * (HEAD_DIM ** -0.5)
        k = k * lax.rsqrt(_seg_mean(k * k, HEAD_DIM) * HEAD_DIM + EPS)
        o_ref[0, :, 0:GROUP_W] = q
        o_ref[0, :, GROUP_W:2 * GROUP_W] = k
        o_ref[0, :, 2 * GROUP_W:] = v
    else:
        o_ref[0] = y


def _dwconv(z, w, col_block, mode):
    b, l, _ = z.shape
    c = 3 * GROUP_W
    tm = _tile(l, 512)
    hb = tm // 8
    last = l // 8 - 1
    return pl.pallas_call(
        functools.partial(_dwconv_kernel, mode=mode),
        grid=(b, l // tm),
        in_specs=[pl.BlockSpec((1, tm, c), lambda i, j: (i, j, col_block)),
                  pl.BlockSpec((1, 8, c), lambda i, j: (i, jnp.maximum(j * hb - 1, 0), col_block)),
                  pl.BlockSpec((1, 8, c), lambda i, j: (i, jnp.minimum((j + 1) * hb, last), col_block)),
                  pl.BlockSpec((3, c), lambda i, j: (0, 0))],
        out_specs=pl.BlockSpec((1, tm, c), lambda i, j: (i, j, 0)),
        out_shape=jax.ShapeDtypeStruct((b, l, c), F32),
        compiler_params=_cp("parallel", "parallel"),
        name="dwconv_" + mode,
    )(z, z, z, w)


def _unit_tri_inverse(a):
    n = a.shape[0]
    row = lax.broadcasted_iota(jnp.int32, (n, n), 0)
    col = lax.broadcasted_iota(jnp.int32, (n, n), 1)

    def same_block(log2_size):
        return lax.shift_right_logical(row, log2_size) == lax.shift_right_logical(col, log2_size)

    p = jnp.where(same_block(3), -a, 0.0)
    t = jnp.where(row == col, 1.0, 0.0) + p
    for _ in range(2):
        p = _dot(p, p)
        t = t + _dot(t, p)
    for log2_size in range(4, int(math.log2(n)) + 1):
        off = jnp.where(same_block(log2_size) & jnp.logical_not(same_block(log2_size - 1)), a, 0.0)
        t = t - _dot(t, _dot(off, t))
    return t


def _deltanet_kernel(qkv_ref, sm_ref, par_ref, s0_ref, o_ref, s_ref, *, n_chunks):
    d = pl.program_id(1)

    @pl.when(pl.program_id(2) == 0)
    def _():
        s_ref[...] = s0_ref[...]

    incl, strict = _order_masks(d, CHUNK)
    incl_b = jnp.where(incl, 1.0, 0.0).astype(BF16)

    def chunk_body(c, carry):
        r0 = pl.multiple_of(jnp.where(d == 0, c, n_chunks - 1 - c) * CHUNK, CHUNK)
        rows = pl.ds(r0, CHUNK)
        sm = sm_ref[0, rows, :]
        beta_all = _sigmoid(sm)
        g_all = -jnp.exp(par_ref[0:1, :]) * _softplus(sm + par_ref[1:2, :])
        gc_all = _dot_exact_l(incl_b, g_all)
        g_last_all = jnp.where(d == 0, gc_all[CHUNK - 1:CHUNK, :], gc_all[0:1, :])
        outs = []
        for h in range(N_HEADS):
            lo, hi = h * HEAD_DIM, (h + 1) * HEAD_DIM
            q = qkv_ref[0, rows, lo:hi]
            k = qkv_ref[0, rows, GROUP_W + lo:GROUP_W + hi]
            v = qkv_ref[0, rows, 2 * GROUP_W + lo:2 * GROUP_W + hi]
            beta = _lane_pick(beta_all, LANE_BETA + d * N_HEADS + h)
            gc = _lane_pick(gc_all, LANE_A + d * N_HEADS + h)
            g_last = _lane_pick(g_last_all, LANE_A + d * N_HEADS + h)
            decay = jnp.exp(jnp.where(incl, gc - _row_bcast(gc), NEG))
            a = jnp.where(strict, beta * _dot_nt(k, k) * decay, 0.0)
            t = _unit_tri_inverse(a)
            egc = jnp.exp(gc)
            uw = _dot(t, jnp.concatenate([v * beta, k * (beta * egc)], axis=1))
            u, w = uw[:, :HEAD_DIM], uw[:, HEAD_DIM:]
            s = s_ref[0, 0, h]
            v_new = u - _dot(w, s)
            outs.append(_dot(q * egc, s) + _dot(_dot_nt(q, k) * decay, v_new))
            s_ref[0, 0, h] = s * jnp.exp(g_last) + _dot_tn(k * jnp.exp(g_last - gc), v_new)
        o_ref[0, 0, rows, :] = jnp.concatenate(outs, axis=1)
        return carry

    lax.fori_loop(0, n_chunks, chunk_body, 0)


def _rec_tile(l):
    return _tile(l, 512)


def _deltanet(qkv, z, par, s0):
    b, l, _ = qkv.shape
    tile = _rec_tile(l)
    nt = l // tile

    def t_idx(dd, i):
        return jnp.where(dd == 0, i, nt - 1 - i)

    state_spec = pl.BlockSpec((1, 1, N_HEADS, HEAD_DIM, HEAD_DIM), lambda i, dd, j: (i, dd, 0, 0, 0))
    return pl.pallas_call(
        functools.partial(_deltanet_kernel, n_chunks=tile // CHUNK),
        grid=(b, 2, nt),
        in_specs=[pl.BlockSpec((1, tile, 3 * GROUP_W), lambda i, dd, j: (i, t_idx(dd, j), 0)),
                  pl.BlockSpec((1, tile, LANES), lambda i, dd, j: (i, t_idx(dd, j), COL_SMALL)),
                  pl.BlockSpec((8, LANES), lambda i, dd, j: (0, 0)),
                  state_spec],
        out_specs=[pl.BlockSpec((1, 1, tile, GROUP_W), lambda i, dd, j: (dd, i, t_idx(dd, j), 0)),
                   state_spec],
        out_shape=[jax.ShapeDtypeStruct((2, b, l, GROUP_W), F32),
                   jax.ShapeDtypeStruct((b, 2, N_HEADS, HEAD_DIM, HEAD_DIM), F32)],
        compiler_params=_cp("parallel", "arbitrary", "arbitrary"),
        name="deltanet",
    )(qkv, z, par, s0)


def _mlstm_kernel(z_ref, sm_ref, par_ref, c0_ref, n0_ref, m0_ref, o_ref, c_ref, n_ref, m_ref, *, n_chunks):
    d = pl.program_id(1)

    @pl.when(pl.program_id(2) == 0)
    def _():
        c_ref[...] = c0_ref[...]
        n_ref[...] = n0_ref[...]
        m_ref[...] = m0_ref[...]

    incl, _ = _order_masks(d, CHUNK)
    incl_b = jnp.where(incl, 1.0, 0.0).astype(BF16)

    def chunk_body(c, carry):
        r0 = pl.multiple_of(jnp.where(d == 0, c, n_chunks - 1 - c) * CHUNK, CHUNK)
        rows = pl.ds(r0, CHUNK)
        sm = sm_ref[0, rows, :]
        i_all = sm + par_ref[0:1, :]
        f_all = -_softplus(-(sm + par_ref[1:2, :]))
        bc_all = _dot_exact_l(incl_b, f_all)
        b_tot_all = jnp.where(d == 0, bc_all[CHUNK - 1:CHUNK, :], bc_all[0:1, :])
        outs = []
        for h in range(N_HEADS):
            lo, hi = h * HEAD_DIM, (h + 1) * HEAD_DIM
            q = z_ref[0, rows, lo:hi]
            k = z_ref[0, rows, GROUP_W + lo:GROUP_W + hi] * (HEAD_DIM ** -0.5)
            v = z_ref[0, rows, 2 * GROUP_W + lo:2 * GROUP_W + hi]
            ic = _lane_pick(i_all, LANE_I + d * N_HEADS + h)
            bc = _lane_pick(bc_all, LANE_F + d * N_HEADS + h)
            b_tot = _lane_pick(b_tot_all, LANE_F + d * N_HEADS + h)
            c_s = c_ref[0, 0, h]
            n_s = n_ref[0, 0, h:h + 1, :]
            m_s = m_ref[0, 0, h:h + 1, 0:1]
            dlog = jnp.where(incl, bc + _row_bcast(ic - bc), NEG)
            dmax = jnp.max(dlog, axis=1, keepdims=True)
            wend = b_tot - bc + ic
            a = bc + m_s
            m_t = jnp.maximum(a, dmax)
            inter = jnp.exp(a - m_t)
            s = _dot_nt(q, k) * jnp.exp(dlog - m_t)
            num = inter * _dot(q, c_s) + _dot(s, v)
            den = inter * jnp.sum(q * n_s, axis=1, keepdims=True) + jnp.sum(s, axis=1, keepdims=True)
            outs.append(num / jnp.maximum(jnp.abs(den), jnp.exp(-m_t)))
            a_end = b_tot + m_s
            m_new = jnp.maximum(a_end, jnp.max(wend, axis=0, keepdims=True))
            dec = jnp.exp(a_end - m_new)
            kw = k * jnp.exp(wend - m_new)
            c_ref[0, 0, h] = dec * c_s + _dot_tn(kw, v)
            n_ref[0, 0, h:h + 1, :] = dec * n_s + jnp.sum(kw, axis=0, keepdims=True)
            m_ref[0, 0, h:h + 1, :] = jnp.broadcast_to(m_new, (1, LANES))
        o_ref[0, 0, rows, :] = jnp.concatenate(outs, axis=1)
        return carry

    lax.fori_loop(0, n_chunks, chunk_body, 0)


def _mlstm(z, par, c0, n0, m0):
    b, l, _ = z.shape
    tile = _rec_tile(l)
    nt = l // tile

    def t_idx(dd, i):
        return jnp.where(dd == 0, i, nt - 1 - i)

    c_spec = pl.BlockSpec((1, 1, N_HEADS, HEAD_DIM, HEAD_DIM), lambda i, dd, j: (i, dd, 0, 0, 0))
    n_spec = pl.BlockSpec((1, 1, N_HEADS, HEAD_DIM), lambda i, dd, j: (i, dd, 0, 0))
    m_spec = pl.BlockSpec((1, 1, N_HEADS, LANES), lambda i, dd, j: (i, dd, 0, 0))
    return pl.pallas_call(
        functools.partial(_mlstm_kernel, n_chunks=tile // CHUNK),
        grid=(b, 2, nt),
        in_specs=[pl.BlockSpec((1, tile, 3 * GROUP_W), lambda i, dd, j: (i, t_idx(dd, j), COL_ML)),
                  pl.BlockSpec((1, tile, LANES), lambda i, dd, j: (i, t_idx(dd, j), COL_SMALL)),
                  pl.BlockSpec((8, LANES), lambda i, dd, j: (0, 0)),
                  c_spec, n_spec, m_spec],
        out_specs=[pl.BlockSpec((1, 1, tile, GROUP_W), lambda i, dd, j: (dd, i, t_idx(dd, j), 0)),
                   c_spec, n_spec, m_spec],
        out_shape=[jax.ShapeDtypeStruct((2, b, l, GROUP_W), F32),
                   jax.ShapeDtypeStruct((b, 2, N_HEADS, HEAD_DIM, HEAD_DIM), F32),
                   jax.ShapeDtypeStruct((b, 2, N_HEADS, HEAD_DIM), F32),
                   jax.ShapeDtypeStruct((b, 2, N_HEADS, LANES), F32)],
        compiler_params=_cp("parallel", "arbitrary", "arbitrary"),
        name="mlstm",
    )(z, z, par, c0, n0, m0)


def _diff_prep_kernel(z_ref, gq_ref, gk_ref, cos_ref, sin_ref, q_ref, kt_ref, v_ref, kd_ref, *, rope):
    z = z_ref[0]
    q, k, v = z[:, :GROUP_W], z[:, GROUP_W:2 * GROUP_W], z[:, 2 * GROUP_W:]
    q = q * lax.rsqrt(_seg_mean(q * q, DF_DH) + EPS) * gq_ref[...]
    k = k * lax.rsqrt(_seg_mean(k * k, DF_DH) + EPS) * gk_ref[...]
    kd_ref[0] = k
    if rope:
        lane = lax.broadcasted_iota(jnp.int32, q.shape, 1)
        first = (lane & (DF_DH - 1)) < DF_DH // 2

        def rot(x):
            swapped = jnp.where(first, pltpu.roll(x, GROUP_W - DF_DH // 2, 1), pltpu.roll(x, DF_DH // 2, 1))
            return x * cos_ref[...] + swapped * sin_ref[...]

        q, k = rot(q), rot(k)
    kt = k.T
    for h in range(N_HEADS):
        lo, hi = h * HEAD_DIM, (h + 1) * HEAD_DIM
        q_ref[0, h] = q[:, lo:hi]
        v_ref[0, h] = v[:, lo:hi]
        kt_ref[0, h] = kt[lo:hi, :]


def _cache_prep_kernel(k_ref, v_ref, kt_ref, vh_ref):
    kt = k_ref[0].T
    v = v_ref[0]
    for h in range(N_HEADS):
        lo, hi = h * HEAD_DIM, (h + 1) * HEAD_DIM
        vh_ref[0, h] = v[:, lo:hi]
        kt_ref[0, h] = kt[lo:hi, :]


def _rope_tables(l):
    rows = l // GRID_W
    r = jnp.repeat(jnp.arange(rows, dtype=F32), GRID_W)
    col = jnp.tile(jnp.arange(GRID_W, dtype=F32), rows)
    n_freq = DF_DH // 4
    inv = ROPE_THETA ** (-jnp.arange(n_freq, dtype=F32) / n_freq)
    ang = jnp.concatenate([r[:, None] * inv, col[:, None] * inv], axis=-1)
    cos, sin = jnp.cos(ang), jnp.sin(ang)
    reps = GROUP_W // DF_DH
    return (jnp.tile(jnp.concatenate([cos, cos], axis=-1), (1, reps)),
            jnp.tile(jnp.concatenate([-sin, sin], axis=-1), (1, reps)))


def _diff_prep(z, gq, gk, rope):
    b, l, _ = z.shape
    tm = _tile(l, 512)
    if rope:
        cos, sin = _rope_tables(l)
    else:
        cos = sin = jnp.zeros((l, GROUP_W), F32)
    head_rows = pl.BlockSpec((1, N_HEADS, tm, HEAD_DIM), lambda i, j: (i, 0, j, 0))
    return pl.pallas_call(
        functools.partial(_diff_prep_kernel, rope=rope),
        grid=(b, l // tm),
        in_specs=[pl.BlockSpec((1, tm, 3 * GROUP_W), lambda i, j: (i, j, COL_DF)),
                  pl.BlockSpec((1, GROUP_W), lambda i, j: (0, 0)),
                  pl.BlockSpec((1, GROUP_W), lambda i, j: (0, 0)),
                  pl.BlockSpec((tm, GROUP_W), lambda i, j: (j, 0)),
                  pl.BlockSpec((tm, GROUP_W), lambda i, j: (j, 0))],
        out_specs=[head_rows,
                   pl.BlockSpec((1, N_HEADS, HEAD_DIM, tm), lambda i, j: (i, 0, 0, j)),
                   head_rows,
                   pl.BlockSpec((1, tm, GROUP_W), lambda i, j: (i, j, 0))],
        out_shape=[jax.ShapeDtypeStruct((b, N_HEADS, l, HEAD_DIM), F32),
                   jax.ShapeDtypeStruct((b, N_HEADS, HEAD_DIM, l), F32),
                   jax.ShapeDtypeStruct((b, N_HEADS, l, HEAD_DIM), F32),
                   jax.ShapeDtypeStruct((b, l, GROUP_W), F32)],
        compiler_params=_cp("parallel", "parallel"),
        name="diff_prep",
    )(z, jnp.tile(gq, GROUP_W // DF_DH).reshape(1, GROUP_W), jnp.tile(gk, GROUP_W // DF_DH).reshape(1, GROUP_W),
      cos, sin)


def _cache_prep(ck, cv):
    b, p, _ = ck.shape
    tm = _tile(p, 512)
    return pl.pallas_call(
        _cache_prep_kernel,
        grid=(b, p // tm),
        in_specs=[pl.BlockSpec((1, tm, GROUP_W), lambda i, j: (i, j, 0)),
                  pl.BlockSpec((1, tm, GROUP_W), lambda i, j: (i, j, 0))],
        out_specs=[pl.BlockSpec((1, N_HEADS, HEAD_DIM, tm), lambda i, j: (i, 0, 0, j)),
                   pl.BlockSpec((1, N_HEADS, tm, HEAD_DIM), lambda i, j: (i, 0, j, 0))],
        out_shape=[jax.ShapeDtypeStruct((b, N_HEADS, HEAD_DIM, p), F32),
                   jax.ShapeDtypeStruct((b, N_HEADS, p, HEAD_DIM), F32)],
        compiler_params=_cp("parallel", "parallel"),
        name="cache_prep",
    )(ck, cv)


def _diff_attn_kernel(q_ref, kt_ref, v_ref, lam_ref, g_ref, o_ref, m_ref, l_ref, acc_ref, *, lam_init):
    ik = pl.program_id(3)

    @pl.when(ik == 0)
    def _():
        m_ref[...] = jnp.full(m_ref.shape, NEG, F32)
        l_ref[...] = jnp.zeros(l_ref.shape, F32)
        acc_ref[...] = jnp.zeros(acc_ref.shape, F32)

    q = q_ref[0, 0]
    kt = kt_ref[0, 0]
    v = v_ref[0, 0].astype(BF16)
    lane = lax.broadcasted_iota(jnp.int32, q.shape, 1)
    for m in range(2):
        qm = jnp.where(lane < DF_DH if m == 0 else lane >= DF_DH, q, 0.0)
        s = _dot(qm, kt) * (DF_DH ** -0.5)
        m_old = m_ref[m]
        m_new = jnp.maximum(m_old, jnp.max(s, axis=1, keepdims=True))
        alpha = jnp.exp(m_old - m_new)
        p = jnp.exp(s - m_new)
        l_ref[m] = alpha * l_ref[m] + jnp.sum(p, axis=1, keepdims=True)
        acc_ref[m] = alpha * acc_ref[m] + jnp.dot(p.astype(BF16), v, preferred_element_type=F32)
        m_ref[m] = m_new

    @pl.when(ik == pl.num_programs(3) - 1)
    def _():
        lp = lam_ref[...]
        lam = (jnp.exp(jnp.sum(lp[0:1] * lp[1:2], axis=1, keepdims=True))
               - jnp.exp(jnp.sum(lp[2:3] * lp[3:4], axis=1, keepdims=True)) + lam_init)
        o = acc_ref[0] / l_ref[0] - lam * (acc_ref[1] / l_ref[1])
        var = jnp.mean(o * o, axis=1, keepdims=True)
        o_ref[0, 0] = o * lax.rsqrt(var + EPS) * g_ref[...] * (1.0 - lam_init)


def _diff_attn(q, kt, v, lam_par, gain, lam_init):
    b, _, l, _ = q.shape
    lk = kt.shape[-1]
    tq = _tile(l, 512)
    tk = _tile(lk, 512)
    return pl.pallas_call(
        functools.partial(_diff_attn_kernel, lam_init=lam_init),
        grid=(b, N_HEADS, l // tq, lk // tk),
        in_specs=[pl.BlockSpec((1, 1, tq, HEAD_DIM), lambda i, h, a, c: (i, h, a, 0)),
                  pl.BlockSpec((1, 1, HEAD_DIM, tk), lambda i, h, a, c: (i, h, 0, c)),
                  pl.BlockSpec((1, 1, tk, HEAD_DIM), lambda i, h, a, c: (i, h, c, 0)),
                  pl.BlockSpec((4, DF_DH), lambda i, h, a, c: (0, 0)),
                  pl.BlockSpec((1, HEAD_DIM), lambda i, h, a, c: (0, 0))],
        out_specs=pl.BlockSpec((1, 1, tq, HEAD_DIM), lambda i, h, a, c: (i, h, a, 0)),
        out_shape=jax.ShapeDtypeStruct((b, N_HEADS, l, HEAD_DIM), F32),
        scratch_shapes=[pltpu.VMEM((2, tq, 1), F32), pltpu.VMEM((2, tq, 1), F32),
                        pltpu.VMEM((2, tq, HEAD_DIM), F32)],
        compiler_params=_cp("parallel", "parallel", "parallel", "arbitrary"),
        name="diff_attn",
    )(q, kt, v, lam_par, gain.reshape(1, HEAD_DIM))


def _hy_filter_kernel(feat_ref, win_ref, w1_ref, b1_ref, w2_ref, b2_ref, w3_ref, fr_ref, o_ref):
    h = jnp.sin(fr_ref[0:1, :] * (_dot_hi(feat_ref[...], w1_ref[...]) + b1_ref[...]))
    h = jnp.sin(fr_ref[1:2, :] * (_dot_hi(h, w2_ref[...]) + b2_ref[...]))
    h = _dot_hi(h, w3_ref[...]) * win_ref[...]
    o_ref[...] = h * lax.rsqrt(jnp.sum(h * h, axis=0, keepdims=True) + EPS)


def _hy_constants(l):
    pos = jnp.arange(l, dtype=F32)
    bands = jnp.arange(1, HY_BANDS + 1, dtype=F32)
    ang = (2.0 * math.pi / l) * pos[:, None] * bands[None, :]
    feat = jnp.concatenate([pos[:, None] / l, jnp.cos(ang), jnp.sin(ang)], axis=-1)
    feat = jnp.pad(feat, ((0, 0), (0, LANES - feat.shape[1])))
    rates = jnp.linspace(-math.log(HY_TARGET) / HY_FAST_DECAY, -math.log(HY_TARGET) / HY_SLOW_DECAY,
                         GROUP_W, dtype=F32)
    rates = jnp.tile(rates, 2)
    offset = jnp.abs(pos - l // 2) / l
    return feat, jnp.exp(-offset[:, None] * rates[None, :]) + HY_SHIFT


def _hy_filters(feat, win, w1, b1, w2, b2, w3, freq):
    l = feat.shape[0]
    hid = w2.shape[0]
    w1p = jnp.pad(w1, ((0, LANES - w1.shape[0]), (0, 0)))
    full = lambda shape: pl.BlockSpec(shape, lambda o: (0, 0))
    return pl.pallas_call(
        _hy_filter_kernel,
        grid=(2,),
        in_specs=[full((l, LANES)), pl.BlockSpec((l, GROUP_W), lambda o: (0, o)),
                  full((LANES, hid)), full((1, hid)), full((hid, hid)), full((1, hid)),
                  pl.BlockSpec((hid, GROUP_W), lambda o: (0, o)), full((2, hid))],
        out_specs=pl.BlockSpec((l, GROUP_W), lambda o: (0, o)),
        out_shape=jax.ShapeDtypeStruct((l, 2 * GROUP_W), F32),
        compiler_params=_cp("parallel"),
        name="hy_filters",
    )(feat, win, w1p, b1.reshape(1, hid), w2, b2.reshape(1, hid), w3, freq)


def _dft_matrices(l):
    n = 2 * l
    k = jnp.arange(l, dtype=jnp.int32)
    ang = (2.0 * math.pi / n) * ((k[:, None] * k[None, :]) % n).astype(F32)
    alt = jnp.where(k % 2 == 0, 1.0, -1.0).astype(F32)
    fwd_re = jnp.cos(ang)
    fwd_im = jnp.where(k[:, None] == 0, alt[None, :], -jnp.sin(ang))
    fwd = jnp.stack([fwd_re, fwd_im]).astype(BF16)
    t = k + l // 2
    ang_i = (2.0 * math.pi / n) * ((t[:, None] * k[None, :]) % n).astype(F32)
    alt_t = jnp.where(t % 2 == 0, 1.0, -1.0).astype(F32)
    inv_re = jnp.where(k[None, :] == 0, 1.0 / n, (2.0 / n) * jnp.cos(ang_i))
    inv_im = jnp.where(k[None, :] == 0, alt_t[:, None] / n, -(2.0 / n) * jnp.sin(ang_i))
    inv = jnp.concatenate([inv_re, inv_im], axis=1).astype(BF16)
    return fwd, inv


def _dft_filter_kernel(f_ref, h_ref, o_ref):
    h = h_ref[...].astype(BF16)
    o_ref[0] = jnp.dot(f_ref[0], h, preferred_element_type=F32)
    o_ref[1] = jnp.dot(f_ref[1], h, preferred_element_type=F32)


def _dft_filter(fwd, filt):
    l, c = filt.shape
    tm = _tile(l, 256)
    return pl.pallas_call(
        _dft_filter_kernel,
        grid=(l // tm,),
        in_specs=[pl.BlockSpec((2, tm, l), lambda i: (0, i, 0)), pl.BlockSpec((l, c), lambda i: (0, 0))],
        out_specs=pl.BlockSpec((2, tm, c), lambda i: (0, i, 0)),
        out_shape=jax.ShapeDtypeStruct((2, l, c), F32),
        compiler_params=_cp("parallel"),
        name="dft_filter",
    )(fwd, filt)


def _dft_fwd_mul_kernel(f_ref, u_ref, h_ref, o_ref):
    u = u_ref[0].astype(BF16)
    ur = jnp.dot(f_ref[0], u, preferred_element_type=F32)
    ui = jnp.dot(f_ref[1], u, preferred_element_type=F32)
    hr, hi = h_ref[0], h_ref[1]
    row = lax.broadcasted_iota(jnp.int32, ur.shape, 0) + pl.program_id(0) * ur.shape[0]
    packed = row == 0
    o_ref[0, 0] = (ur * hr - jnp.where(packed, 0.0, ui * hi)).astype(BF16)
    o_ref[0, 1] = jnp.where(packed, ui * hi, ur * hi + ui * hr).astype(BF16)


def _dft_fwd_mul(fwd, u, u_col, hf, order):
    b, l, _ = u.shape
    tm = _tile(l, 256)
    return pl.pallas_call(
        _dft_fwd_mul_kernel,
        grid=(l // tm, b),
        in_specs=[pl.BlockSpec((2, tm, l), lambda i, j: (0, i, 0)),
                  pl.BlockSpec((1, l, GROUP_W), lambda i, j: (j, 0, u_col)),
                  pl.BlockSpec((2, tm, GROUP_W), lambda i, j: (0, i, order))],
        out_specs=pl.BlockSpec((1, 2, tm, GROUP_W), lambda i, j: (j, 0, i, 0)),
        out_shape=jax.ShapeDtypeStruct((b, 2, l, GROUP_W), BF16),
        compiler_params=_cp("parallel", "parallel"),
        name="dft_fwd_mul",
    )(fwd, u, hf)


def _dft_inv_gate_kernel(g_ref, y_ref, x_ref, u_ref, skip_ref, o_ref):
    conv = jnp.dot(g_ref[...], y_ref[0], preferred_element_type=F32)
    o_ref[0] = x_ref[0] * (conv + skip_ref[...] * u_ref[0])


def _dft_inv_gate(inv, y, xg, x_col, u, u_col, skip):
    b, _, l, _ = y.shape
    tm = _tile(l, 256)
    return pl.pallas_call(
        _dft_inv_gate_kernel,
        grid=(l // tm, b),
        in_specs=[pl.BlockSpec((tm, 2 * l), lambda i, j: (i, 0)),
                  pl.BlockSpec((1, 2 * l, GROUP_W), lambda i, j: (j, 0, 0)),
                  pl.BlockSpec((1, tm, GROUP_W), lambda i, j: (j, i, x_col)),
                  pl.BlockSpec((1, tm, GROUP_W), lambda i, j: (j, i, u_col)),
                  pl.BlockSpec((1, GROUP_W), lambda i, j: (0, 0))],
        out_specs=pl.BlockSpec((1, tm, GROUP_W), lambda i, j: (j, i, 0)),
        out_shape=jax.ShapeDtypeStruct((b, l, GROUP_W), F32),
        compiler_params=_cp("parallel", "parallel"),
        name="dft_inv_gate",
    )(inv, y.reshape(b, 2 * l, GROUP_W), xg, u, skip.reshape(1, GROUP_W))


def _head_norm(x, gain):
    return x * lax.rsqrt(_seg_mean(x * x, HEAD_DIM) + EPS) * gain


def _proj_out_kernel(dn_ref, gate_ref, df_ref, hy_ref, ml_ref, mo_ref, x_ref, mod_ref, gdn_ref, gml_ref,
                     w_ref, o_ref):
    gate = gate_ref[0]
    y_dn = _head_norm(dn_ref[0, 0] + dn_ref[1, 0], gdn_ref[...]) * (gate * _sigmoid(gate))
    y_ml = _head_norm(ml_ref[0, 0] + ml_ref[1, 0], gml_ref[...]) * _sigmoid(mo_ref[0])
    y = _dot(y_dn, w_ref[0:GROUP_W, :])
    for h in range(N_HEADS):
        lo = GROUP_W + h * HEAD_DIM
        y = y + _dot(df_ref[0, h], w_ref[lo:lo + HEAD_DIM, :])
    y = y + _dot(hy_ref[0], w_ref[2 * GROUP_W:3 * GROUP_W, :])
    y = y + _dot(y_ml, w_ref[3 * GROUP_W:, :])
    o_ref[0] = x_ref[0] + mod_ref[0, 2:3, :] * y


def _proj_out(o_dn, z, o_df, y_hy, h_ml, x, mod, g_dn, g_ml, w):
    b, l, d = x.shape
    tm = _tile(l, 512)
    per_batch = mod.shape[0] > 1
    two_dir = pl.BlockSpec((2, 1, tm, GROUP_W), lambda i, j: (0, i, j, 0))
    return pl.pallas_call(
        _proj_out_kernel,
        grid=(b, l // tm),
        in_specs=[two_dir,
                  pl.BlockSpec((1, tm, GROUP_W), lambda i, j: (i, j, COL_DGATE)),
                  pl.BlockSpec((1, N_HEADS, tm, HEAD_DIM), lambda i, j: (i, 0, j, 0)),
                  pl.BlockSpec((1, tm, GROUP_W), lambda i, j: (i, j, 0)),
                  two_dir,
                  pl.BlockSpec((1, tm, GROUP_W), lambda i, j: (i, j, COL_MO)),
                  pl.BlockSpec((1, tm, d), lambda i, j: (i, j, 0)),
                  pl.BlockSpec((1, 6, d), lambda i, j: (i if per_batch else 0, 0, 0)),
                  pl.BlockSpec((1, GROUP_W), lambda i, j: (0, 0)),
                  pl.BlockSpec((1, GROUP_W), lambda i, j: (0, 0)),
                  pl.BlockSpec((d, d), lambda i, j: (0, 0))],
        out_specs=pl.BlockSpec((1, tm, d), lambda i, j: (i, j, 0)),
        out_shape=jax.ShapeDtypeStruct((b, l, d), F32),
        compiler_params=_cp("parallel", "parallel"),
        name="proj_out",
    )(o_dn, z, o_df, y_hy, h_ml, z, x, mod, jnp.tile(g_dn, N_HEADS).reshape(1, GROUP_W),
      jnp.tile(g_ml, N_HEADS).reshape(1, GROUP_W), w)


def _ffn_kernel(x_ref, mod_ref, g_ref, wg_ref, wu_ref, wd_ref, o_ref, h_ref, acc_ref):
    j = pl.program_id(2)

    @pl.when(j == 0)
    def _():
        h_ref[...] = _modulated_norm(x_ref[0], g_ref[...], mod_ref[0, 3:4, :], mod_ref[0, 4:5, :]).astype(BF16)
        acc_ref[...] = jnp.zeros(acc_ref.shape, F32)

    h = h_ref[...]
    g = jnp.dot(h, wg_ref[...], preferred_element_type=F32)
    u = jnp.dot(h, wu_ref[...], preferred_element_type=F32)
    acc_ref[...] += jnp.dot((g * _sigmoid(g) * u).astype(BF16), wd_ref[...], preferred_element_type=F32)

    @pl.when(j == pl.num_programs(2) - 1)
    def _():
        o_ref[0] = x_ref[0] + mod_ref[0, 5:6, :] * acc_ref[...]


def _ffn(x, mod, gain, wg, wu, wd):
    b, l, d = x.shape
    f = wg.shape[1]
    tm = _tile(l, 1024)
    tf = 256
    per_batch = mod.shape[0] > 1
    return pl.pallas_call(
        _ffn_kernel,
        grid=(b, l // tm, f // tf),
        in_specs=[pl.BlockSpec((1, tm, d), lambda i, a, j: (i, a, 0)),
                  pl.BlockSpec((1, 6, d), lambda i, a, j: (i if per_batch else 0, 0, 0)),
                  pl.BlockSpec((1, d), lambda i, a, j: (0, 0)),
                  pl.BlockSpec((d, tf), lambda i, a, j: (0, j)),
                  pl.BlockSpec((d, tf), lambda i, a, j: (0, j)),
                  pl.BlockSpec((tf, d), lambda i, a, j: (j, 0))],
        out_specs=pl.BlockSpec((1, tm, d), lambda i, a, j: (i, a, 0)),
        out_shape=jax.ShapeDtypeStruct((b, l, d), F32),
        scratch_shapes=[pltpu.VMEM((tm, d), BF16), pltpu.VMEM((tm, d), F32)],
        compiler_params=_cp("parallel", "parallel", "arbitrary"),
        name="ffn_dense",
    )(x, mod, gain.reshape(1, d), wg, wu, wd)


def _top2_combine(logits):
    lane = lax.broadcasted_iota(jnp.int32, logits.shape, 1)
    v1 = jnp.max(logits, axis=1, keepdims=True)
    i1 = jnp.min(jnp.where(logits == v1, lane, LANES), axis=1, keepdims=True)
    rest = jnp.where(lane == i1, NEG, logits)
    v2 = jnp.max(rest, axis=1, keepdims=True)
    i2 = jnp.min(jnp.where(rest == v2, lane, LANES), axis=1, keepdims=True)
    e2 = jnp.exp(v2 - v1)
    return jnp.where(lane == i1, 1.0 / (1.0 + e2), 0.0) + jnp.where(lane == i2, e2 / (1.0 + e2), 0.0)


def _moe_kernel(x_ref, mod_ref, g_ref, r_ref, wg_ref, wu_ref, wd_ref, o_ref, h_ref, comb_ref, acc_ref):
    e = pl.program_id(2)

    @pl.when(e == 0)
    def _():
        hn = _modulated_norm(x_ref[0], g_ref[...], mod_ref[0, 3:4, :], mod_ref[0, 4:5, :])
        h_ref[...] = hn.astype(BF16)
        lane = lax.broadcasted_iota(jnp.int32, (hn.shape[0], LANES), 1)
        comb_ref[...] = _top2_combine(jnp.where(lane < N_EXPERTS, _dot_hi(hn, r_ref[...]), NEG))
        acc_ref[...] = jnp.zeros(acc_ref.shape, F32)

    h = h_ref[...]
    g = jnp.dot(h, wg_ref[0], preferred_element_type=F32)
    u = jnp.dot(h, wu_ref[0], preferred_element_type=F32)
    a = g * _sigmoid(g) * u * _lane_pick(comb_ref[...], e)
    acc_ref[...] += jnp.dot(a.astype(BF16), wd_ref[0], preferred_element_type=F32)

    @pl.when(e == pl.num_programs(2) - 1)
    def _():
        o_ref[0] = x_ref[0] + mod_ref[0, 5:6, :] * acc_ref[...]


def _moe(x, mod, gain, router, wg, wu, wd):
    b, l, d = x.shape
    ne, _, f = wg.shape
    tm = _tile(l, 512)
    per_batch = mod.shape[0] > 1
    router_p = jnp.pad(router, ((0, 0), (0, LANES - ne)))
    return pl.pallas_call(
        _moe_kernel,
        grid=(b, l // tm, ne),
        in_specs=[pl.BlockSpec((1, tm, d), lambda i, a, e: (i, a, 0)),
                  pl.BlockSpec((1, 6, d), lambda i, a, e: (i if per_batch else 0, 0, 0)),
                  pl.BlockSpec((1, d), lambda i, a, e: (0, 0)),
                  pl.BlockSpec((d, LANES), lambda i, a, e: (0, 0)),
                  pl.BlockSpec((1, d, f), lambda i, a, e: (e, 0, 0)),
                  pl.BlockSpec((1, d, f), lambda i, a, e: (e, 0, 0)),
                  pl.BlockSpec((1, f, d), lambda i, a, e: (e, 0, 0))],
        out_specs=pl.BlockSpec((1, tm, d), lambda i, a, e: (i, a, 0)),
        out_shape=jax.ShapeDtypeStruct((b, l, d), F32),
        scratch_shapes=[pltpu.VMEM((tm, d), BF16), pltpu.VMEM((tm, LANES), F32), pltpu.VMEM((tm, d), F32)],
        compiler_params=_cp("parallel", "parallel", "arbitrary"),
        name="moe",
    )(x, mod, gain.reshape(1, d), router_p, wg, wu, wd)


def _lane_row(values, lane0):
    row = jnp.zeros((LANES,), F32)
    return lax.dynamic_update_slice(row, values.reshape(-1).astype(F32), (lane0,))


def _reorder_w_in(w_in):
    g = GROUP_W
    o = [0, g, 2 * g, 3 * g, 4 * g, 4 * g + 8, 4 * g + 16]
    dq_dk_dv = w_in[..., o[0]:o[3]]
    dgate = w_in[..., o[3]:o[4]]
    dbeta_da = w_in[..., o[4]:o[6]]
    base = o[6]
    df = w_in[..., base:base + 3 * g]
    hy = w_in[..., base + 3 * g:base + 6 * g]
    ml = w_in[..., base + 6 * g:base + 9 * g]
    mo = w_in[..., base + 9 * g:base + 10 * g]
    mi_mf = w_in[..., base + 10 * g:base + 10 * g + 16]
    pad = jnp.zeros(w_in.shape[:-1] + (LANES - 32,), w_in.dtype)
    return jnp.concatenate([dq_dk_dv, hy, ml, df, dgate, mo, dbeta_da, mi_mf, pad], axis=-1).astype(BF16)


def _layer(l, x, mod, cache, p, hy_consts, dft):
    b, seq, _ = x.shape
    latent = cache is not None
    z = _proj_in(x, mod, p["norm1_g"][l], p["w_in"][l])

    qkv = _dwconv(z, p["dn_conv"][l], COL_DN, "deltanet")
    dn_par = jnp.zeros((8, LANES), F32).at[0].set(_lane_row(p["dn_a_log"][l], LANE_A)).at[1].set(
        _lane_row(p["dn_dt_bias"][l], LANE_A))
    s0 = cache[2] if latent else jnp.zeros((b, 2, N_HEADS, HEAD_DIM, HEAD_DIM), F32)
    o_dn, s_dn = _deltanet(qkv, z, dn_par, s0)

    ml_par = jnp.zeros((8, LANES), F32).at[0].set(_lane_row(p["ml_i_bias"][l], LANE_I)).at[1].set(
        _lane_row(p["ml_f_bias"][l], LANE_F))
    if latent:
        c0, n0 = cache[3], cache[4]
        m0 = jnp.broadcast_to(cache[5][..., None], cache[5].shape + (LANES,))
    else:
        c0 = jnp.zeros((b, 2, N_HEADS, HEAD_DIM, HEAD_DIM), F32)
        n0 = jnp.zeros((b, 2, N_HEADS, HEAD_DIM), F32)
        m0 = jnp.zeros((b, 2, N_HEADS, LANES), F32)
    h_ml, c_ml, n_ml, m_ml = _mlstm(z, ml_par, c0, n0, m0)

    q, kt, v, kd = _diff_prep(z, p["df_q_norm"][l], p["df_k_norm"][l], latent)
    if latent:
        ckt, cv = _cache_prep(cache[0].reshape(b, -1, GROUP_W), cache[1].reshape(b, -1, GROUP_W))
        kt = jnp.concatenate([kt, ckt], axis=-1)
        v = jnp.concatenate([v, cv], axis=2)
    lam_init = 0.8 - 0.6 * math.exp(-0.3 * l)
    o_df = _diff_attn(q, kt, v, p["df_lambda"][l], p["df_norm"][l], lam_init)

    feat, win = hy_consts
    fwd, inv = dft
    zc = _dwconv(z, p["hy_conv"][l], COL_HY, "plain")
    filt = _hy_filters(feat, win, p["hy_w1"][l], p["hy_b1"][l], p["hy_w2"][l], p["hy_b2"][l], p["hy_w3"][l],
                       p["hy_freq"][l])
    hf = _dft_filter(fwd, filt)
    y1 = _dft_fwd_mul(fwd, zc, 2, hf, 0)
    z1 = _dft_inv_gate(inv, y1, zc, 0, zc, 2, p["hy_skip"][l, 0])
    y2 = _dft_fwd_mul(fwd, z1, 0, hf, 1)
    y_hy = _dft_inv_gate(inv, y2, zc, 1, z1, 0, p["hy_skip"][l, 1])

    x = _proj_out(o_dn, z, o_df, y_hy, h_ml, x, mod, p["dn_norm"][l], p["ml_norm"][l], p["w_out"][l])
    j = l // 2
    if l % 2 == 0:
        x = _ffn(x, mod, p["norm2_g"][l], p["ffn_w_gate"][j], p["ffn_w_up"][j], p["ffn_w_down"][j])
    else:
        x = _moe(x, mod, p["norm2_g"][l], p["moe_router"][j], p["moe_w_gate"][j], p["moe_w_up"][j],
                 p["moe_w_down"][j])
    fv = z[:, :, 3 * 3 * GROUP_W + 2 * GROUP_W:3 * 3 * GROUP_W + 3 * GROUP_W]
    return x, (kd, fv, s_dn, c_ml, n_ml, m_ml[..., 0])


def kernel(x_prompt, x_sample, cache_diff_k, cache_diff_v, state_delta, state_mlstm_c, state_mlstm_n, state_mlstm_m, c, c_ctx, norm1_g, norm2_g, w_mod, b_mod, w_in, w_out, dn_conv, dn_a_log, dn_dt_bias, dn_norm, df_q_norm, df_k_norm, df_lambda, df_norm, hy_conv, hy_w1, hy_b1, hy_w2, hy_b2, hy_w3, hy_freq, hy_skip, ml_i_bias, ml_f_bias, ml_norm, ffn_w_gate, ffn_w_up, ffn_w_down, moe_router, moe_w_gate, moe_w_up, moe_w_down):
    depth = w_in.shape[0]
    d_model = x_prompt.shape[-1]
    batch, seq, _ = x_prompt.shape
    dec_batch, dec_seq, _ = x_sample.shape
    p = dict(norm1_g=norm1_g, norm2_g=norm2_g, w_in=_reorder_w_in(w_in), w_out=w_out.astype(BF16),
             dn_conv=dn_conv, dn_a_log=dn_a_log, dn_dt_bias=dn_dt_bias, dn_norm=dn_norm,
             df_q_norm=df_q_norm, df_k_norm=df_k_norm, df_lambda=df_lambda, df_norm=df_norm,
             hy_conv=hy_conv, hy_w1=hy_w1, hy_b1=hy_b1, hy_w2=hy_w2, hy_b2=hy_b2, hy_w3=hy_w3,
             hy_freq=hy_freq, hy_skip=hy_skip, ml_i_bias=ml_i_bias, ml_f_bias=ml_f_bias, ml_norm=ml_norm,
             ffn_w_gate=ffn_w_gate.astype(BF16), ffn_w_up=ffn_w_up.astype(BF16),
             ffn_w_down=ffn_w_down.astype(BF16), moe_router=moe_router,
             moe_w_gate=moe_w_gate.astype(BF16), moe_w_up=moe_w_up.astype(BF16),
             moe_w_down=moe_w_down.astype(BF16))

    n_cond = 1 + dec_batch
    rows = -(-n_cond // 8) * 8
    cond = jnp.concatenate([c_ctx[None, :], c, jnp.zeros((rows - n_cond, d_model), F32)], axis=0)
    mod = _modulation(cond, w_mod, b_mod).reshape(depth, rows, 6, d_model)

    hy_ctx, dft_ctx = _hy_constants(seq), _dft_matrices(seq)
    x = x_prompt
    ctx = []
    for l in range(depth):
        x, out = _layer(l, x, mod[l, 0:1], None, p, hy_ctx, dft_ctx)
        ctx.append(out)
    y_prompt = x
    new_k, new_v, new_s, new_c, new_n, new_m = (jnp.stack([o[i] for o in ctx], axis=1) for i in range(6))
    new_k = new_k.reshape(batch, depth, seq, N_HEADS, 2, DF_DH)
    new_v = new_v.reshape(batch, depth, seq, N_HEADS, HEAD_DIM)

    hy_lat, dft_lat = _hy_constants(dec_seq), _dft_matrices(dec_seq)
    x = x_sample
    for l in range(depth):
        cache = (cache_diff_k[:, l], cache_diff_v[:, l], state_delta[:, l], state_mlstm_c[:, l],
                 state_mlstm_n[:, l], state_mlstm_m[:, l])
        x, _ = _layer(l, x, mod[l, 1:1 + dec_batch], cache, p, hy_lat, dft_lat)
    return (y_prompt, x, new_k, new_v, new_s, new_c, new_n, new_m)
```

```python
import functools
import math
from typing import Any, NamedTuple

import jax
import jax.numpy as jnp
from jax import lax
from jax.experimental import pallas as pl
from jax.experimental.pallas import tpu as pltpu

F32 = jnp.float32
BF16 = jnp.bfloat16

N_HEADS = 4
HEAD_DIM = 64
GROUP_W = N_HEADS * HEAD_DIM
DF_DH = 32
CHUNK = 64
GRID_W = 64
ROPE_THETA = 10000.0
HY_BANDS = 8
HY_FAST_DECAY = 0.3
HY_SLOW_DECAY = 1.5
HY_TARGET = 1e-2
HY_SHIFT = 0.05
N_EXPERTS = 8
EPS = 1e-6
NEG = -1e30
LANES = 128
VMEM_LIMIT = 56 * 1024 * 1024

COL_DN, COL_HY, COL_ML, COL_DF = 0, 1, 2, 3
COL_DGATE, COL_MO = 12, 13
COL_SMALL = 28
Z_WIDTH = 29 * LANES
LANE_BETA, LANE_A, LANE_I, LANE_F = 0, 8, 16, 24


def _tile(n, pref):
    t = min(n, pref)
    while n % t:
        t -= LANES if t > LANES else 8
    return t


def _cp(*sem):
    return pltpu.CompilerParams(dimension_semantics=sem, vmem_limit_bytes=VMEM_LIMIT)


def _split3(x):
    x1 = x.astype(BF16)
    r = x - x1.astype(F32)
    x2 = r.astype(BF16)
    r = r - x2.astype(F32)
    return x1, x2, r.astype(BF16)


def _dot(a, b):
    return jnp.dot(a.astype(BF16), b.astype(BF16), preferred_element_type=F32)


def _dot_nt(a, b):
    return lax.dot_general(a.astype(BF16), b.astype(BF16), (((1,), (1,)), ((), ())),
                           preferred_element_type=F32)


def _dot_tn(a, b):
    return lax.dot_general(a.astype(BF16), b.astype(BF16), (((0,), (0,)), ((), ())),
                           preferred_element_type=F32)


def _dot_exact_l(m, x):
    return sum(jnp.dot(m, p, preferred_element_type=F32) for p in _split3(x))


def _dot_exact_r(x, m):
    return sum(jnp.dot(p, m, preferred_element_type=F32) for p in _split3(x))


def _dot_hi(a, b):
    a1, a2, _ = _split3(a)
    b1, b2, _ = _split3(b)
    return (jnp.dot(a1, b1, preferred_element_type=F32) + jnp.dot(a1, b2, preferred_element_type=F32)
            + jnp.dot(a2, b1, preferred_element_type=F32))


def _seg_mean(x, seg):
    w = x.shape[-1]
    sh = int(math.log2(seg))
    r = lax.shift_right_logical(lax.broadcasted_iota(jnp.int32, (w, w), 0), sh)
    c = lax.shift_right_logical(lax.broadcasted_iota(jnp.int32, (w, w), 1), sh)
    bd = jnp.where(r == c, 1.0, 0.0).astype(BF16)
    return _dot_exact_r(x, bd) * (1.0 / seg)


def _row_bcast(col):
    n = col.shape[0]
    lane = lax.broadcasted_iota(jnp.int32, (n, LANES), 1)
    sel = jnp.where(lane == 0, 1.0, 0.0).astype(BF16)
    src = jnp.where(lane == 0, col, 0.0)
    return sum(lax.dot_general(sel, p, (((1,), (1,)), ((), ())), preferred_element_type=F32)
               for p in _split3(src))


def _lane_pick(x, lane_idx):
    lane = lax.broadcasted_iota(jnp.int32, x.shape, 1)
    return jnp.sum(jnp.where(lane == lane_idx, x, 0.0), axis=1, keepdims=True)


def _sigmoid(x):
    return 1.0 / (1.0 + jnp.exp(-x))


def _softplus(x):
    return jnp.maximum(x, 0.0) + jnp.log1p(jnp.exp(-jnp.abs(x)))


def _mod_kernel(c_ref, w_ref, b_ref, o_ref):
    c = c_ref[...]
    o_ref[0] = _dot(c * _sigmoid(c), w_ref[0]) + b_ref[0]


def _modulation(cond, w_mod, b_mod):
    depth, d, n = w_mod.shape
    r = cond.shape[0]
    tn = n // 4
    return pl.pallas_call(
        _mod_kernel,
        grid=(depth, n // tn),
        in_specs=[pl.BlockSpec((r, d), lambda l, j: (0, 0)),
                  pl.BlockSpec((1, d, tn), lambda l, j: (l, 0, j)),
                  pl.BlockSpec((1, 1, tn), lambda l, j: (l, 0, j))],
        out_specs=pl.BlockSpec((1, r, tn), lambda l, j: (l, 0, j)),
        out_shape=jax.ShapeDtypeStruct((depth, r, n), F32),
        compiler_params=_cp("parallel", "parallel"),
        name="modulation",
    )(cond, w_mod, b_mod.reshape(depth, 1, n))


def _modulated_norm(x, gain, shift, scale):
    var = jnp.mean(x * x, axis=-1, keepdims=True)
    return x * lax.rsqrt(var + EPS) * gain * (1.0 + scale) + shift


def _proj_in_kernel(x_ref, mod_ref, g_ref, w_ref, o_ref):
    h = _modulated_norm(x_ref[0], g_ref[...], mod_ref[0, 0:1, :], mod_ref[0, 1:2, :])
    o_ref[0] = jnp.dot(h.astype(BF16), w_ref[...], preferred_element_type=F32)


def _proj_in(x, mod, gain, w):
    b, l, d = x.shape
    tm = _tile(l, 256)
    per_batch = mod.shape[0] > 1
    return pl.pallas_call(
        _proj_in_kernel,
        grid=(b, l // tm),
        in_specs=[pl.BlockSpec((1, tm, d), lambda i, j: (i, j, 0)),
                  pl.BlockSpec((1, 6, d), lambda i, j: (i if per_batch else 0, 0, 0)),
                  pl.BlockSpec((1, d), lambda i, j: (0, 0)),
                  pl.BlockSpec((d, Z_WIDTH), lambda i, j: (0, 0))],
        out_specs=pl.BlockSpec((1, tm, Z_WIDTH), lambda i, j: (i, j, 0)),
        out_shape=jax.ShapeDtypeStruct((b, l, Z_WIDTH), F32),
        compiler_params=_cp("parallel", "parallel"),
        name="proj_in",
    )(x, mod, gain.reshape(1, d), w)


def _dwconv_kernel(z_ref, zp_ref, zn_ref, w_ref, o_ref, *, mode):
    i = pl.program_id(1)
    z = z_ref[0]
    tm = z.shape[0]
    prev_row = jnp.where(i > 0, zp_ref[0, 7:8, :], 0.0)
    next_row = jnp.where(i < pl.num_programs(1) - 1, zn_ref[0, 0:1, :], 0.0)
    rid = lax.broadcasted_iota(jnp.int32, z.shape, 0)
    zm1 = jnp.where(rid == 0, prev_row, pltpu.roll(z, 1, 0))
    zp1 = jnp.where(rid == tm - 1, next_row, pltpu.roll(z, tm - 1, 0))
    y = zm1 * w_ref[0:1, :] + z * w_ref[1:2, :] + zp1 * w_ref[2:3, :]
    if mode == "deltanet":
        y = y * _sigmoid(y)
        q, k, v = y[:, :GROUP_W], y[:, GROUP_W:2 * GROUP_W], y[:, 2 * GROUP_W:]
        q = q * lax.rsqrt(_seg_mean(q * q, HEAD_DIM) * HEAD_DIM + EPS) * (HEAD_DIM ** -0.5)
        k = k * lax.rsqrt(_seg_mean(k * k, HEAD_DIM) * HEAD_DIM + EPS)
        o_ref[0, :, 0:GROUP_W] = q
        o_ref[0, :, GROUP_W:2 * GROUP_W] = k
        o_ref[0, :, 2 * GROUP_W:] = v
    else:
        o_ref[0] = y


def _dwconv(z, w, col_block, mode):
    b, l, _ = z.shape
    c = 3 * GROUP_W
    tm = _tile(l, 512)
    hb = tm // 8
    last = l // 8 - 1
    return pl.pallas_call(
        functools.partial(_dwconv_kernel, mode=mode),
        grid=(b, l // tm),
        in_specs=[pl.BlockSpec((1, tm, c), lambda i, j: (i, j, col_block)),
                  pl.BlockSpec((1, 8, c), lambda i, j: (i, jnp.maximum(j * hb - 1, 0), col_block)),
                  pl.BlockSpec((1, 8, c), lambda i, j: (i, jnp.minimum((j + 1) * hb, last), col_block)),
                  pl.BlockSpec((3, c), lambda i, j: (0, 0))],
        out_specs=pl.BlockSpec((1, tm, c), lambda i, j: (i, j, 0)),
        out_shape=jax.ShapeDtypeStruct((b, l, c), F32),
        compiler_params=_cp("parallel", "parallel"),
        name="dwconv_" + mode,
    )(z, z, z, w)


STACK = N_HEADS * CHUNK


def _stack_heads(x):
    return jnp.concatenate([x] * N_HEADS, axis=0)


def _block_diag(x, head_eq):
    return jnp.where(head_eq, _stack_heads(x), 0.0)


def _fold_heads(x):
    return x[0:CHUNK] + x[CHUNK:2 * CHUNK] + x[2 * CHUNK:3 * CHUNK] + x[3 * CHUNK:]


def _chunk_masks(d):
    row = lax.broadcasted_iota(jnp.int32, (STACK, STACK), 0)
    col = lax.broadcasted_iota(jnp.int32, (STACK, STACK), 1)
    head_eq = lax.shift_right_logical(row, 6) == lax.shift_right_logical(col, 6)
    rel = ((row & (CHUNK - 1)) - (col & (CHUNK - 1))) * (1 - 2 * d)
    r64 = lax.broadcasted_iota(jnp.int32, (CHUNK, CHUNK), 0)
    c64 = lax.broadcasted_iota(jnp.int32, (CHUNK, CHUNK), 1)
    cum = jnp.where((r64 - c64) * (1 - 2 * d) >= 0, 1.0, 0.0).astype(BF16)
    return head_eq, head_eq & (rel >= 0), head_eq & (rel > 0), cum


def _head_cols(x, lane0):
    xs = _stack_heads(x)
    row = lax.broadcasted_iota(jnp.int32, xs.shape, 0)
    lane = lax.broadcasted_iota(jnp.int32, xs.shape, 1)
    return jnp.sum(jnp.where(lane == lane0 + lax.shift_right_logical(row, 6), xs, 0.0), axis=1, keepdims=True)


def _head_lanes(x, lane0):
    src = lax.broadcasted_iota(jnp.int32, (LANES, GROUP_W), 0)
    dst = lax.broadcasted_iota(jnp.int32, (LANES, GROUP_W), 1)
    expand = jnp.where(src == lane0 + lax.shift_right_logical(dst, 6), 1.0, 0.0).astype(BF16)
    return _dot_exact_r(x, expand)


def _lanes_to_col(x):
    row = lax.broadcasted_iota(jnp.int32, (STACK, GROUP_W), 0)
    lane = lax.broadcasted_iota(jnp.int32, (STACK, GROUP_W), 1)
    sel = jnp.where(lane == lax.shift_right_logical(row, 6) * HEAD_DIM, 1.0, 0.0).astype(BF16)
    xs = jnp.broadcast_to(x, (8, GROUP_W))
    out = sum(lax.dot_general(sel, p, (((1,), (1,)), ((), ())), preferred_element_type=F32) for p in _split3(xs))
    return out[:, 0:1]


def _load_diag_state(dst_ref, src_ref):
    dst_ref[...] = jnp.zeros(dst_ref.shape, F32)
    for r in range(dst_ref.shape[0]):
        for d in range(2):
            for h in range(N_HEADS):
                lo, hi = h * HEAD_DIM, (h + 1) * HEAD_DIM
                dst_ref[r, d, lo:hi, lo:hi] = src_ref[r, d, h]


def _store_diag_state(dst_ref, src_ref):
    for r in range(src_ref.shape[0]):
        for d in range(2):
            for h in range(N_HEADS):
                lo, hi = h * HEAD_DIM, (h + 1) * HEAD_DIM
                dst_ref[r, d, h] = src_ref[r, d, lo:hi, lo:hi]


def _rec_specs(b, l):
    rows = 2 if b % 2 == 0 else 1
    tile = _tile(l, 512)
    nt = l // tile
    fwd = lambda width, col: pl.BlockSpec((rows, tile, width), lambda i, j: (i, j, col))
    bwd = lambda width, col: pl.BlockSpec((rows, tile, width), lambda i, j: (i, nt - 1 - j, col))
    state = pl.BlockSpec((rows, 2, N_HEADS, HEAD_DIM, HEAD_DIM), lambda i, j: (i, 0, 0, 0, 0))
    return rows, tile, nt, fwd, bwd, state


class _Chain(NamedTuple):
    r: int
    d: int
    x_ref: Any
    sm_ref: Any
    o_ref: Any
    rows: Any


def _each(fn, *cols):
    return [fn(*args) for args in zip(*cols)]


def _unit_tri_inverses(mats):
    n = mats[0].shape[0]
    row = lax.broadcasted_iota(jnp.int32, (n, n), 0)
    col = lax.broadcasted_iota(jnp.int32, (n, n), 1)

    def same_block(log2_size):
        return lax.shift_right_logical(row, log2_size) == lax.shift_right_logical(col, log2_size)

    p = [jnp.where(same_block(3), -a, 0.0) for a in mats]
    t = [jnp.where(row == col, 1.0, 0.0) + x for x in p]
    for _ in range(2):
        p = _each(_dot, p, p)
        t = _each(jnp.add, t, _each(_dot, t, p))
    for log2_size in range(4, int(math.log2(CHUNK)) + 1):
        level = same_block(log2_size) & jnp.logical_not(same_block(log2_size - 1))
        off_t = [_dot(jnp.where(level, a, 0.0), x) for a, x in zip(mats, t)]
        t = _each(jnp.subtract, t, _each(_dot, t, off_t))
    return t


def _deltanet_chunks(chains, par_ref, s_ref):
    masks = {d: _chunk_masks(d) for d in (0, 1)}
    head_eq = masks[0][0]
    incl = [masks[c.d][1] for c in chains]
    strict = [masks[c.d][2] for c in chains]
    cum = [masks[c.d][3] for c in chains]
    lane_a = [LANE_A + c.d * N_HEADS for c in chains]
    lane_b = [LANE_BETA + c.d * N_HEADS for c in chains]
    q = [c.x_ref[c.r, c.rows, 0:GROUP_W] for c in chains]
    k = [c.x_ref[c.r, c.rows, GROUP_W:2 * GROUP_W] for c in chains]
    v = [c.x_ref[c.r, c.rows, 2 * GROUP_W:3 * GROUP_W] for c in chains]
    sm = [c.sm_ref[c.r, c.rows, :] for c in chains]
    beta_all = _each(_sigmoid, sm)
    g_all = [-jnp.exp(par_ref[0:1, :]) * _softplus(x + par_ref[1:2, :]) for x in sm]
    gc_all = _each(_dot_exact_l, cum, g_all)
    gc = _each(_head_lanes, gc_all, lane_a)
    beta = _each(_head_lanes, beta_all, lane_b)
    g_last = [x[CHUNK - 1:CHUNK] if c.d == 0 else x[0:1] for x, c in zip(gc, chains)]
    gc_col = _each(_head_cols, gc_all, lane_a)
    beta_col = _each(_head_cols, beta_all, lane_b)
    gc_row = _each(_row_bcast, gc_col)
    decay = [jnp.exp(jnp.where(m, x - y, NEG)) for m, x, y in zip(incl, gc_col, gc_row)]
    k_rows = _each(_stack_heads, k)
    kk = [_dot_nt(jnp.where(head_eq, x, 0.0), x) for x in k_rows]
    a = [jnp.where(m, b * x * dc, 0.0) for m, b, x, dc in zip(strict, beta_col, kk, decay)]
    t = _unit_tri_inverses(a)
    egc = _each(jnp.exp, gc)
    u = [_fold_heads(_dot(x, _block_diag(y * b, head_eq))) for x, y, b in zip(t, v, beta)]
    w = [_fold_heads(_dot(x, _block_diag(y * (b * e), head_eq))) for x, y, b, e in zip(t, k, beta, egc)]
    s = [s_ref[c.r, c.d] for c in chains]
    ws_qs = [_dot(jnp.concatenate([x, y * e], axis=0), z) for x, y, e, z in zip(w, q, egc, s)]
    v_new = [x - y[0:CHUNK] for x, y in zip(u, ws_qs)]
    qk = [_dot_nt(_block_diag(x, head_eq), y) * dc for x, y, dc in zip(q, k_rows, decay)]
    o_intra = [_fold_heads(_dot(x, _block_diag(y, head_eq))) for x, y in zip(qk, v_new)]
    s_add = [_dot_tn(x * jnp.exp(gl - g), y) for x, gl, g, y in zip(k, g_last, gc, v_new)]
    for c, x, y, z, gl, sa in zip(chains, ws_qs, o_intra, s, g_last, s_add):
        c.o_ref[c.r, c.rows, :] = x[CHUNK:] + y
        s_ref[c.r, c.d] = z * jnp.exp(gl) + jnp.where(head_eq, sa, 0.0)


def _chunk_chains(n_rows, n_chunks, c, f_refs, b_refs):
    rows_f = pl.ds(pl.multiple_of(c * CHUNK, CHUNK), CHUNK)
    rows_b = pl.ds(pl.multiple_of((n_chunks - 1 - c) * CHUNK, CHUNK), CHUNK)
    return [_Chain(r, d, *refs, rows) for r in range(n_rows)
            for d, refs, rows in ((0, f_refs, rows_f), (1, b_refs, rows_b))]


def _deltanet_kernel(qf_ref, smf_ref, qb_ref, smb_ref, par_ref, s0_ref, of_ref, ob_ref, sout_ref, s_ref, *,
                     n_chunks):
    j = pl.program_id(1)

    @pl.when(j == 0)
    def _():
        _load_diag_state(s_ref, s0_ref)

    def chunk_body(c, carry):
        chains = _chunk_chains(s_ref.shape[0], n_chunks, c, (qf_ref, smf_ref, of_ref), (qb_ref, smb_ref, ob_ref))
        _deltanet_chunks(chains, par_ref, s_ref)
        return carry

    lax.fori_loop(0, n_chunks, chunk_body, 0)

    @pl.when(j == pl.num_programs(1) - 1)
    def _():
        _store_diag_state(sout_ref, s_ref)


def _deltanet(qkv, z, par, s0):
    b, l, _ = qkv.shape
    rows, tile, nt, fwd, bwd, state = _rec_specs(b, l)
    return pl.pallas_call(
        functools.partial(_deltanet_kernel, n_chunks=tile // CHUNK),
        grid=(b // rows, nt),
        in_specs=[fwd(3 * GROUP_W, 0), fwd(LANES, COL_SMALL), bwd(3 * GROUP_W, 0), bwd(LANES, COL_SMALL),
                  pl.BlockSpec((8, LANES), lambda i, j: (0, 0)), state],
        out_specs=[fwd(GROUP_W, 0), bwd(GROUP_W, 0), state],
        out_shape=[jax.ShapeDtypeStruct((b, l, GROUP_W), F32), jax.ShapeDtypeStruct((b, l, GROUP_W), F32),
                   jax.ShapeDtypeStruct((b, 2, N_HEADS, HEAD_DIM, HEAD_DIM), F32)],
        scratch_shapes=[pltpu.VMEM((rows, 2, STACK, STACK), F32)],
        compiler_params=_cp("parallel", "arbitrary"),
        name="deltanet",
    )(qkv, z, qkv, z, par, s0)


def _mlstm_chunks(chains, par_ref, c_ref, n_ref, m_ref):
    masks = {d: _chunk_masks(d) for d in (0, 1)}
    head_eq = masks[0][0]
    incl = [masks[c.d][1] for c in chains]
    cum = [masks[c.d][3] for c in chains]
    lane_i = [LANE_I + c.d * N_HEADS for c in chains]
    lane_f = [LANE_F + c.d * N_HEADS for c in chains]
    q = [c.x_ref[c.r, c.rows, 0:GROUP_W] for c in chains]
    k = [c.x_ref[c.r, c.rows, GROUP_W:2 * GROUP_W] * (HEAD_DIM ** -0.5) for c in chains]
    v = [c.x_ref[c.r, c.rows, 2 * GROUP_W:3 * GROUP_W] for c in chains]
    sm = [c.sm_ref[c.r, c.rows, :] for c in chains]
    i_all = [x + par_ref[0:1, :] for x in sm]
    f_all = [-_softplus(-(x + par_ref[1:2, :])) for x in sm]
    bc_all = _each(_dot_exact_l, cum, f_all)
    ic = _each(_head_lanes, i_all, lane_i)
    bc = _each(_head_lanes, bc_all, lane_f)
    b_tot = [x[CHUNK - 1:CHUNK] if c.d == 0 else x[0:1] for x, c in zip(bc, chains)]
    ic_col = _each(_head_cols, i_all, lane_i)
    bc_col = _each(_head_cols, bc_all, lane_f)
    c_s = [c_ref[c.r, c.d] for c in chains]
    n_s = [n_ref[c.r, c.d] for c in chains]
    m_s = [m_ref[c.r, c.d] for c in chains]
    w_row = _each(_row_bcast, _each(jnp.subtract, ic_col, bc_col))
    dlog = [jnp.where(m, x + y, NEG) for m, x, y in zip(incl, bc_col, w_row)]
    a = _each(jnp.add, bc_col, _each(_lanes_to_col, m_s))
    m_t = [jnp.maximum(x, jnp.max(y, axis=1, keepdims=True)) for x, y in zip(a, dlog)]
    inter = [jnp.exp(x - y) for x, y in zip(a, m_t)]
    q_bd = [_block_diag(x, head_eq) for x in q]
    s = [_dot_nt(x, _stack_heads(y)) * jnp.exp(dl - mt) for x, y, dl, mt in zip(q_bd, k, dlog, m_t)]
    inter_part = _each(_dot, q_bd, c_s)
    intra_part = [_dot(x, _block_diag(y, head_eq)) for x, y in zip(s, v)]
    den = [it * jnp.sum(x * n, axis=1, keepdims=True) + jnp.sum(y, axis=1, keepdims=True)
           for it, x, n, y in zip(inter, q_bd, n_s, s)]
    wend = [bt - x + y for bt, x, y in zip(b_tot, bc, ic)]
    a_end = _each(jnp.add, b_tot, m_s)
    m_new = [jnp.maximum(x, jnp.max(y, axis=0, keepdims=True)) for x, y in zip(a_end, wend)]
    dec = [jnp.exp(x - y) for x, y in zip(a_end, m_new)]
    kw = [x * jnp.exp(y - z) for x, y, z in zip(k, wend, m_new)]
    c_add = _each(_dot_tn, kw, v)
    for i, c in enumerate(chains):
        num = inter[i] * inter_part[i] + intra_part[i]
        c.o_ref[c.r, c.rows, :] = _fold_heads(num / jnp.maximum(jnp.abs(den[i]), jnp.exp(-m_t[i])))
        c_ref[c.r, c.d] = dec[i] * c_s[i] + jnp.where(head_eq, c_add[i], 0.0)
        n_ref[c.r, c.d] = dec[i] * n_s[i] + jnp.sum(kw[i], axis=0, keepdims=True)
        m_ref[c.r, c.d] = m_new[i]


def _mlstm_kernel(zf_ref, smf_ref, zb_ref, smb_ref, par_ref, c0_ref, n0_ref, m0_ref,
                  of_ref, ob_ref, cout_ref, nout_ref, mout_ref, c_ref, n_ref, m_ref, *, n_chunks):
    j = pl.program_id(1)

    @pl.when(j == 0)
    def _():
        _load_diag_state(c_ref, c0_ref)
        n_ref[...] = n0_ref[...]
        m_ref[...] = m0_ref[...]

    def chunk_body(c, carry):
        chains = _chunk_chains(c_ref.shape[0], n_chunks, c, (zf_ref, smf_ref, of_ref), (zb_ref, smb_ref, ob_ref))
        _mlstm_chunks(chains, par_ref, c_ref, n_ref, m_ref)
        return carry

    lax.fori_loop(0, n_chunks, chunk_body, 0)

    @pl.when(j == pl.num_programs(1) - 1)
    def _():
        _store_diag_state(cout_ref, c_ref)
        nout_ref[...] = n_ref[...]
        mout_ref[...] = m_ref[...]


def _mlstm(z, par, c0, n0, m0):
    b, l, _ = z.shape
    rows, tile, nt, fwd, bwd, state = _rec_specs(b, l)
    vec = pl.BlockSpec((rows, 2, 1, GROUP_W), lambda i, j: (i, 0, 0, 0))
    return pl.pallas_call(
        functools.partial(_mlstm_kernel, n_chunks=tile // CHUNK),
        grid=(b // rows, nt),
        in_specs=[fwd(3 * GROUP_W, COL_ML), fwd(LANES, COL_SMALL), bwd(3 * GROUP_W, COL_ML), bwd(LANES, COL_SMALL),
                  pl.BlockSpec((8, LANES), lambda i, j: (0, 0)), state, vec, vec],
        out_specs=[fwd(GROUP_W, 0), bwd(GROUP_W, 0), state, vec, vec],
        out_shape=[jax.ShapeDtypeStruct((b, l, GROUP_W), F32), jax.ShapeDtypeStruct((b, l, GROUP_W), F32),
                   jax.ShapeDtypeStruct((b, 2, N_HEADS, HEAD_DIM, HEAD_DIM), F32),
                   jax.ShapeDtypeStruct((b, 2, 1, GROUP_W), F32), jax.ShapeDtypeStruct((b, 2, 1, GROUP_W), F32)],
        scratch_shapes=[pltpu.VMEM((rows, 2, STACK, STACK), F32), pltpu.VMEM((rows, 2, 1, GROUP_W), F32),
                        pltpu.VMEM((rows, 2, 1, GROUP_W), F32)],
        compiler_params=_cp("parallel", "arbitrary"),
        name="mlstm",
    )(z, z, z, z, par, c0, n0, m0)


def _diff_prep_kernel(z_ref, gq_ref, gk_ref, cos_ref, sin_ref, q_ref, kt_ref, v_ref, kd_ref, *, rope):
    z = z_ref[0]
    q, k, v = z[:, :GROUP_W], z[:, GROUP_W:2 * GROUP_W], z[:, 2 * GROUP_W:]
    q = q * lax.rsqrt(_seg_mean(q * q, DF_DH) + EPS) * gq_ref[...]
    k = k * lax.rsqrt(_seg_mean(k * k, DF_DH) + EPS) * gk_ref[...]
    kd_ref[0] = k
    if rope:
        lane = lax.broadcasted_iota(jnp.int32, q.shape, 1)
        first = (lane & (DF_DH - 1)) < DF_DH // 2

        def rot(x):
            swapped = jnp.where(first, pltpu.roll(x, GROUP_W - DF_DH // 2, 1), pltpu.roll(x, DF_DH // 2, 1))
            return x * cos_ref[...] + swapped * sin_ref[...]

        q, k = rot(q), rot(k)
    _store_attn_operands(q * (DF_DH ** -0.5), k, v, q_ref, kt_ref, v_ref)


def _store_attn_operands(q, k, v, q_ref, kt_ref, v_ref):
    kt = k.T
    lane = lax.broadcasted_iota(jnp.int32, (k.shape[0], HEAD_DIM), 1)
    ones = jnp.ones((k.shape[0], HEAD_DIM), BF16)
    for h in range(N_HEADS):
        lo, hi = h * HEAD_DIM, (h + 1) * HEAD_DIM
        if q is not None:
            q_ref[0, h, 0] = jnp.where(lane < DF_DH, q[:, lo:hi], 0.0).astype(BF16)
            q_ref[0, h, 1] = jnp.where(lane >= DF_DH, q[:, lo:hi], 0.0).astype(BF16)
        v_ref[0, h] = jnp.concatenate([v[:, lo:hi].astype(BF16), ones], axis=1)
        kt_ref[0, h] = kt[lo:hi, :].astype(BF16)


def _cache_prep_kernel(k_ref, v_ref, kt_ref, vh_ref):
    _store_attn_operands(None, k_ref[0], v_ref[0], None, kt_ref, vh_ref)


def _rope_tables(l):
    rows = l // GRID_W
    r = jnp.repeat(jnp.arange(rows, dtype=F32), GRID_W)
    col = jnp.tile(jnp.arange(GRID_W, dtype=F32), rows)
    n_freq = DF_DH // 4
    inv = ROPE_THETA ** (-jnp.arange(n_freq, dtype=F32) / n_freq)
    ang = jnp.concatenate([r[:, None] * inv, col[:, None] * inv], axis=-1)
    cos, sin = jnp.cos(ang), jnp.sin(ang)
    reps = GROUP_W // DF_DH
    return (jnp.tile(jnp.concatenate([cos, cos], axis=-1), (1, reps)),
            jnp.tile(jnp.concatenate([-sin, sin], axis=-1), (1, reps)))


def _diff_prep(z, gq, gk, rope):
    b, l, _ = z.shape
    tm = _tile(l, 512)
    if rope:
        cos, sin = _rope_tables(l)
    else:
        cos = sin = jnp.zeros((l, GROUP_W), F32)
    return pl.pallas_call(
        functools.partial(_diff_prep_kernel, rope=rope),
        grid=(b, l // tm),
        in_specs=[pl.BlockSpec((1, tm, 3 * GROUP_W), lambda i, j: (i, j, COL_DF)),
                  pl.BlockSpec((1, GROUP_W), lambda i, j: (0, 0)),
                  pl.BlockSpec((1, GROUP_W), lambda i, j: (0, 0)),
                  pl.BlockSpec((tm, GROUP_W), lambda i, j: (j, 0)),
                  pl.BlockSpec((tm, GROUP_W), lambda i, j: (j, 0))],
        out_specs=[pl.BlockSpec((1, N_HEADS, 2, tm, HEAD_DIM), lambda i, j: (i, 0, 0, j, 0)),
                   pl.BlockSpec((1, N_HEADS, HEAD_DIM, tm), lambda i, j: (i, 0, 0, j)),
                   pl.BlockSpec((1, N_HEADS, tm, 2 * HEAD_DIM), lambda i, j: (i, 0, j, 0)),
                   pl.BlockSpec((1, tm, GROUP_W), lambda i, j: (i, j, 0))],
        out_shape=[jax.ShapeDtypeStruct((b, N_HEADS, 2, l, HEAD_DIM), BF16),
                   jax.ShapeDtypeStruct((b, N_HEADS, HEAD_DIM, l), BF16),
                   jax.ShapeDtypeStruct((b, N_HEADS, l, 2 * HEAD_DIM), BF16),
                   jax.ShapeDtypeStruct((b, l, GROUP_W), F32)],
        compiler_params=_cp("parallel", "parallel"),
        name="diff_prep",
    )(z, jnp.tile(gq, GROUP_W // DF_DH).reshape(1, GROUP_W), jnp.tile(gk, GROUP_W // DF_DH).reshape(1, GROUP_W),
      cos, sin)


def _cache_prep(ck, cv):
    b, p, _ = ck.shape
    tm = _tile(p, 512)
    return pl.pallas_call(
        _cache_prep_kernel,
        grid=(b, p // tm),
        in_specs=[pl.BlockSpec((1, tm, GROUP_W), lambda i, j: (i, j, 0)),
                  pl.BlockSpec((1, tm, GROUP_W), lambda i, j: (i, j, 0))],
        out_specs=[pl.BlockSpec((1, N_HEADS, HEAD_DIM, tm), lambda i, j: (i, 0, 0, j)),
                   pl.BlockSpec((1, N_HEADS, tm, 2 * HEAD_DIM), lambda i, j: (i, 0, j, 0))],
        out_shape=[jax.ShapeDtypeStruct((b, N_HEADS, HEAD_DIM, p), BF16),
                   jax.ShapeDtypeStruct((b, N_HEADS, p, 2 * HEAD_DIM), BF16)],
        compiler_params=_cp("parallel", "parallel"),
        name="cache_prep",
    )(ck, cv)


def _diff_attn_kernel(q_ref, kt_ref, v_ref, lam_ref, g_ref, o_ref, m_ref, acc_ref, *, lam_init):
    ik = pl.program_id(3)

    @pl.when(ik == 0)
    def _():
        m_ref[...] = jnp.full(m_ref.shape, NEG, F32)
        acc_ref[...] = jnp.zeros(acc_ref.shape, F32)

    kt = kt_ref[0, 0]
    v = v_ref[0, 0]
    for m in range(2):
        s = jnp.dot(q_ref[0, 0, m], kt, preferred_element_type=F32)
        m_old = m_ref[m]
        m_new = jnp.maximum(m_old, jnp.max(s, axis=1, keepdims=True))
        p = jnp.exp(s - m_new[:, 0:1]).astype(BF16)
        acc_ref[m] = jnp.exp(m_old - m_new) * acc_ref[m] + jnp.dot(p, v, preferred_element_type=F32)
        m_ref[m] = m_new

    @pl.when(ik == pl.num_programs(3) - 1)
    def _():
        lp = lam_ref[...]
        lam = (jnp.exp(jnp.sum(lp[0:1] * lp[1:2], axis=1, keepdims=True))
               - jnp.exp(jnp.sum(lp[2:3] * lp[3:4], axis=1, keepdims=True)) + lam_init)
        a0, a1 = acc_ref[0], acc_ref[1]
        o = (a0[:, :HEAD_DIM] / a0[:, HEAD_DIM:HEAD_DIM + 1]
             - lam * (a1[:, :HEAD_DIM] / a1[:, HEAD_DIM:HEAD_DIM + 1]))
        var = jnp.mean(o * o, axis=1, keepdims=True)
        o_ref[0, 0] = o * lax.rsqrt(var + EPS) * g_ref[...] * (1.0 - lam_init)


def _diff_attn(q, kt, v, lam_par, gain, lam_init):
    b, _, _, l, _ = q.shape
    lk = kt.shape[-1]
    tq = _tile(l, 512)
    tk = _tile(lk, 1536)
    return pl.pallas_call(
        functools.partial(_diff_attn_kernel, lam_init=lam_init),
        grid=(b, N_HEADS, l // tq, lk // tk),
        in_specs=[pl.BlockSpec((1, 1, 2, tq, HEAD_DIM), lambda i, h, a, c: (i, h, 0, a, 0)),
                  pl.BlockSpec((1, 1, HEAD_DIM, tk), lambda i, h, a, c: (i, h, 0, c)),
                  pl.BlockSpec((1, 1, tk, 2 * HEAD_DIM), lambda i, h, a, c: (i, h, c, 0)),
                  pl.BlockSpec((4, DF_DH), lambda i, h, a, c: (0, 0)),
                  pl.BlockSpec((1, HEAD_DIM), lambda i, h, a, c: (0, 0))],
        out_specs=pl.BlockSpec((1, 1, tq, HEAD_DIM), lambda i, h, a, c: (i, h, a, 0)),
        out_shape=jax.ShapeDtypeStruct((b, N_HEADS, l, HEAD_DIM), F32),
        scratch_shapes=[pltpu.VMEM((2, tq, LANES), F32), pltpu.VMEM((2, tq, 2 * HEAD_DIM), F32)],
        compiler_params=_cp("parallel", "parallel", "parallel", "arbitrary"),
        name="diff_attn",
    )(q, kt, v, lam_par, gain.reshape(1, HEAD_DIM))


def _hy_filter_kernel(feat_ref, win_ref, w1_ref, b1_ref, w2_ref, b2_ref, w3_ref, fr_ref, o_ref):
    h = jnp.sin(fr_ref[0:1, :] * (_dot_hi(feat_ref[...], w1_ref[...]) + b1_ref[...]))
    h = jnp.sin(fr_ref[1:2, :] * (_dot_hi(h, w2_ref[...]) + b2_ref[...]))
    h = _dot_hi(h, w3_ref[...]) * win_ref[...]
    o_ref[...] = h * lax.rsqrt(jnp.sum(h * h, axis=0, keepdims=True) + EPS)


def _hy_constants(l):
    pos = jnp.arange(l, dtype=F32)
    bands = jnp.arange(1, HY_BANDS + 1, dtype=F32)
    ang = (2.0 * math.pi / l) * pos[:, None] * bands[None, :]
    feat = jnp.concatenate([pos[:, None] / l, jnp.cos(ang), jnp.sin(ang)], axis=-1)
    feat = jnp.pad(feat, ((0, 0), (0, LANES - feat.shape[1])))
    rates = jnp.linspace(-math.log(HY_TARGET) / HY_FAST_DECAY, -math.log(HY_TARGET) / HY_SLOW_DECAY,
                         GROUP_W, dtype=F32)
    rates = jnp.tile(rates, 2)
    offset = jnp.abs(pos - l // 2) / l
    return feat, jnp.exp(-offset[:, None] * rates[None, :]) + HY_SHIFT


def _hy_filters(feat, win, w1, b1, w2, b2, w3, freq):
    l = feat.shape[0]
    hid = w2.shape[0]
    w1p = jnp.pad(w1, ((0, LANES - w1.shape[0]), (0, 0)))
    full = lambda shape: pl.BlockSpec(shape, lambda o: (0, 0))
    return pl.pallas_call(
        _hy_filter_kernel,
        grid=(2,),
        in_specs=[full((l, LANES)), pl.BlockSpec((l, GROUP_W), lambda o: (0, o)),
                  full((LANES, hid)), full((1, hid)), full((hid, hid)), full((1, hid)),
                  pl.BlockSpec((hid, GROUP_W), lambda o: (0, o)), full((2, hid))],
        out_specs=pl.BlockSpec((l, GROUP_W), lambda o: (0, o)),
        out_shape=jax.ShapeDtypeStruct((l, 2 * GROUP_W), F32),
        compiler_params=_cp("parallel"),
        name="hy_filters",
    )(feat, win, w1p, b1.reshape(1, hid), w2, b2.reshape(1, hid), w3, freq)


def _dft_matrices(l):
    n = 2 * l
    k = jnp.arange(l, dtype=jnp.int32)
    ang = (2.0 * math.pi / n) * ((k[:, None] * k[None, :]) % n).astype(F32)
    alt = jnp.where(k % 2 == 0, 1.0, -1.0).astype(F32)
    fwd_re = jnp.cos(ang)
    fwd_im = jnp.where(k[:, None] == 0, alt[None, :], -jnp.sin(ang))
    fwd = jnp.stack([fwd_re, fwd_im]).astype(BF16)
    t = k + l // 2
    ang_i = (2.0 * math.pi / n) * ((t[:, None] * k[None, :]) % n).astype(F32)
    alt_t = jnp.where(t % 2 == 0, 1.0, -1.0).astype(F32)
    inv_re = jnp.where(k[None, :] == 0, 1.0 / n, (2.0 / n) * jnp.cos(ang_i))
    inv_im = jnp.where(k[None, :] == 0, alt_t[:, None] / n, -(2.0 / n) * jnp.sin(ang_i))
    inv = jnp.concatenate([inv_re, inv_im], axis=1).astype(BF16)
    return fwd, inv


def _dft_filter_kernel(f_ref, h_ref, o_ref):
    h = h_ref[...].astype(BF16)
    o_ref[0] = jnp.dot(f_ref[0], h, preferred_element_type=F32)
    o_ref[1] = jnp.dot(f_ref[1], h, preferred_element_type=F32)


def _dft_filter(fwd, filt):
    l, c = filt.shape
    tm = _tile(l, 256)
    return pl.pallas_call(
        _dft_filter_kernel,
        grid=(l // tm,),
        in_specs=[pl.BlockSpec((2, tm, l), lambda i: (0, i, 0)), pl.BlockSpec((l, c), lambda i: (0, 0))],
        out_specs=pl.BlockSpec((2, tm, c), lambda i: (0, i, 0)),
        out_shape=jax.ShapeDtypeStruct((2, l, c), F32),
        compiler_params=_cp("parallel"),
        name="dft_filter",
    )(fwd, filt)


def _dft_fwd_mul_kernel(f_ref, u_ref, h_ref, o_ref):
    u = u_ref[0].astype(BF16)
    ur = jnp.dot(f_ref[0], u, preferred_element_type=F32)
    ui = jnp.dot(f_ref[1], u, preferred_element_type=F32)
    hr, hi = h_ref[0], h_ref[1]
    row = lax.broadcasted_iota(jnp.int32, ur.shape, 0) + pl.program_id(0) * ur.shape[0]
    packed = row == 0
    o_ref[0, 0] = (ur * hr - jnp.where(packed, 0.0, ui * hi)).astype(BF16)
    o_ref[0, 1] = jnp.where(packed, ui * hi, ur * hi + ui * hr).astype(BF16)


def _dft_fwd_mul(fwd, u, u_col, hf, order):
    b, l, _ = u.shape
    tm = _tile(l, 256)
    return pl.pallas_call(
        _dft_fwd_mul_kernel,
        grid=(l // tm, b),
        in_specs=[pl.BlockSpec((2, tm, l), lambda i, j: (0, i, 0)),
                  pl.BlockSpec((1, l, GROUP_W), lambda i, j: (j, 0, u_col)),
                  pl.BlockSpec((2, tm, GROUP_W), lambda i, j: (0, i, order))],
        out_specs=pl.BlockSpec((1, 2, tm, GROUP_W), lambda i, j: (j, 0, i, 0)),
        out_shape=jax.ShapeDtypeStruct((b, 2, l, GROUP_W), BF16),
        compiler_params=_cp("parallel", "parallel"),
        name="dft_fwd_mul",
    )(fwd, u, hf)


def _dft_inv_gate_kernel(g_ref, y_ref, x_ref, u_ref, skip_ref, o_ref):
    conv = jnp.dot(g_ref[...], y_ref[0], preferred_element_type=F32)
    o_ref[0] = x_ref[0] * (conv + skip_ref[...] * u_ref[0])


def _dft_inv_gate(inv, y, xg, x_col, u, u_col, skip):
    b, _, l, _ = y.shape
    tm = _tile(l, 256)
    return pl.pallas_call(
        _dft_inv_gate_kernel,
        grid=(l // tm, b),
        in_specs=[pl.BlockSpec((tm, 2 * l), lambda i, j: (i, 0)),
                  pl.BlockSpec((1, 2 * l, GROUP_W), lambda i, j: (j, 0, 0)),
                  pl.BlockSpec((1, tm, GROUP_W), lambda i, j: (j, i, x_col)),
                  pl.BlockSpec((1, tm, GROUP_W), lambda i, j: (j, i, u_col)),
                  pl.BlockSpec((1, GROUP_W), lambda i, j: (0, 0))],
        out_specs=pl.BlockSpec((1, tm, GROUP_W), lambda i, j: (j, i, 0)),
        out_shape=jax.ShapeDtypeStruct((b, l, GROUP_W), F32),
        compiler_params=_cp("parallel", "parallel"),
        name="dft_inv_gate",
    )(inv, y.reshape(b, 2 * l, GROUP_W), xg, u, skip.reshape(1, GROUP_W))


def _head_norm(x, gain):
    return x * lax.rsqrt(_seg_mean(x * x, HEAD_DIM) + EPS) * gain


def _proj_out_kernel(dnf_ref, dnb_ref, gate_ref, df_ref, hy_ref, mlf_ref, mlb_ref, mo_ref, x_ref, mod_ref,
                     gdn_ref, gml_ref, w_ref, o_ref):
    gate = gate_ref[0]
    y_dn = _head_norm(dnf_ref[0] + dnb_ref[0], gdn_ref[...]) * (gate * _sigmoid(gate))
    y_ml = _head_norm(mlf_ref[0] + mlb_ref[0], gml_ref[...]) * _sigmoid(mo_ref[0])
    y = _dot(y_dn, w_ref[0:GROUP_W, :])
    for h in range(N_HEADS):
        lo = GROUP_W + h * HEAD_DIM
        y = y + _dot(df_ref[0, h], w_ref[lo:lo + HEAD_DIM, :])
    y = y + _dot(hy_ref[0], w_ref[2 * GROUP_W:3 * GROUP_W, :])
    y = y + _dot(y_ml, w_ref[3 * GROUP_W:, :])
    o_ref[0] = x_ref[0] + mod_ref[0, 2:3, :] * y


def _proj_out(o_dn, z, o_df, y_hy, h_ml, x, mod, g_dn, g_ml, w):
    b, l, d = x.shape
    tm = _tile(l, 512)
    per_batch = mod.shape[0] > 1
    group = lambda col: pl.BlockSpec((1, tm, GROUP_W), lambda i, j: (i, j, col))
    return pl.pallas_call(
        _proj_out_kernel,
        grid=(b, l // tm),
        in_specs=[group(0), group(0), group(COL_DGATE),
                  pl.BlockSpec((1, N_HEADS, tm, HEAD_DIM), lambda i, j: (i, 0, j, 0)),
                  group(0), group(0), group(0), group(COL_MO),
                  pl.BlockSpec((1, tm, d), lambda i, j: (i, j, 0)),
                  pl.BlockSpec((1, 6, d), lambda i, j: (i if per_batch else 0, 0, 0)),
                  pl.BlockSpec((1, GROUP_W), lambda i, j: (0, 0)),
                  pl.BlockSpec((1, GROUP_W), lambda i, j: (0, 0)),
                  pl.BlockSpec((d, d), lambda i, j: (0, 0))],
        out_specs=pl.BlockSpec((1, tm, d), lambda i, j: (i, j, 0)),
        out_shape=jax.ShapeDtypeStruct((b, l, d), F32),
        compiler_params=_cp("parallel", "parallel"),
        name="proj_out",
    )(o_dn[0], o_dn[1], z, o_df, y_hy, h_ml[0], h_ml[1], z, x, mod, jnp.tile(g_dn, N_HEADS).reshape(1, GROUP_W),
      jnp.tile(g_ml, N_HEADS).reshape(1, GROUP_W), w)


def _ffn_kernel(x_ref, mod_ref, g_ref, wg_ref, wu_ref, wd_ref, o_ref, h_ref, acc_ref):
    j = pl.program_id(2)

    @pl.when(j == 0)
    def _():
        h_ref[...] = _modulated_norm(x_ref[0], g_ref[...], mod_ref[0, 3:4, :], mod_ref[0, 4:5, :]).astype(BF16)
        acc_ref[...] = jnp.zeros(acc_ref.shape, F32)

    h = h_ref[...]
    g = jnp.dot(h, wg_ref[...], preferred_element_type=F32)
    u = jnp.dot(h, wu_ref[...], preferred_element_type=F32)
    acc_ref[...] += jnp.dot((g * _sigmoid(g) * u).astype(BF16), wd_ref[...], preferred_element_type=F32)

    @pl.when(j == pl.num_programs(2) - 1)
    def _():
        o_ref[0] = x_ref[0] + mod_ref[0, 5:6, :] * acc_ref[...]


def _ffn(x, mod, gain, wg, wu, wd):
    b, l, d = x.shape
    f = wg.shape[1]
    tm = _tile(l, 1024)
    tf = 256
    per_batch = mod.shape[0] > 1
    return pl.pallas_call(
        _ffn_kernel,
        grid=(b, l // tm, f // tf),
        in_specs=[pl.BlockSpec((1, tm, d), lambda i, a, j: (i, a, 0)),
                  pl.BlockSpec((1, 6, d), lambda i, a, j: (i if per_batch else 0, 0, 0)),
                  pl.BlockSpec((1, d), lambda i, a, j: (0, 0)),
                  pl.BlockSpec((d, tf), lambda i, a, j: (0, j)),
                  pl.BlockSpec((d, tf), lambda i, a, j: (0, j)),
                  pl.BlockSpec((tf, d), lambda i, a, j: (j, 0))],
        out_specs=pl.BlockSpec((1, tm, d), lambda i, a, j: (i, a, 0)),
        out_shape=jax.ShapeDtypeStruct((b, l, d), F32),
        scratch_shapes=[pltpu.VMEM((tm, d), BF16), pltpu.VMEM((tm, d), F32)],
        compiler_params=_cp("parallel", "parallel", "arbitrary"),
        name="ffn_dense",
    )(x, mod, gain.reshape(1, d), wg, wu, wd)


def _top2_combine(logits):
    lane = lax.broadcasted_iota(jnp.int32, logits.shape, 1)
    v1 = jnp.max(logits, axis=1, keepdims=True)
    i1 = jnp.min(jnp.where(logits == v1, lane, LANES), axis=1, keepdims=True)
    rest = jnp.where(lane == i1, NEG, logits)
    v2 = jnp.max(rest, axis=1, keepdims=True)
    i2 = jnp.min(jnp.where(rest == v2, lane, LANES), axis=1, keepdims=True)
    e2 = jnp.exp(v2 - v1)
    return jnp.where(lane == i1, 1.0 / (1.0 + e2), 0.0) + jnp.where(lane == i2, e2 / (1.0 + e2), 0.0)


def _moe_kernel(x_ref, mod_ref, g_ref, r_ref, wg_ref, wu_ref, wd_ref, o_ref, h_ref, comb_ref, acc_ref):
    e = pl.program_id(2)

    @pl.when(e == 0)
    def _():
        hn = _modulated_norm(x_ref[0], g_ref[...], mod_ref[0, 3:4, :], mod_ref[0, 4:5, :])
        h_ref[...] = hn.astype(BF16)
        lane = lax.broadcasted_iota(jnp.int32, (hn.shape[0], LANES), 1)
        comb_ref[...] = _top2_combine(jnp.where(lane < N_EXPERTS, _dot_hi(hn, r_ref[...]), NEG))
        acc_ref[...] = jnp.zeros(acc_ref.shape, F32)

    h = h_ref[...]
    g = jnp.dot(h, wg_ref[0], preferred_element_type=F32)
    u = jnp.dot(h, wu_ref[0], preferred_element_type=F32)
    a = g * _sigmoid(g) * u * _lane_pick(comb_ref[...], e)
    acc_ref[...] += jnp.dot(a.astype(BF16), wd_ref[0], preferred_element_type=F32)

    @pl.when(e == pl.num_programs(2) - 1)
    def _():
        o_ref[0] = x_ref[0] + mod_ref[0, 5:6, :] * acc_ref[...]


def _moe(x, mod, gain, router, wg, wu, wd):
    b, l, d = x.shape
    ne, _, f = wg.shape
    tm = _tile(l, 512)
    per_batch = mod.shape[0] > 1
    router_p = jnp.pad(router, ((0, 0), (0, LANES - ne)))
    return pl.pallas_call(
        _moe_kernel,
        grid=(b, l // tm, ne),
        in_specs=[pl.BlockSpec((1, tm, d), lambda i, a, e: (i, a, 0)),
                  pl.BlockSpec((1, 6, d), lambda i, a, e: (i if per_batch else 0, 0, 0)),
                  pl.BlockSpec((1, d), lambda i, a, e: (0, 0)),
                  pl.BlockSpec((d, LANES), lambda i, a, e: (0, 0)),
                  pl.BlockSpec((1, d, f), lambda i, a, e: (e, 0, 0)),
                  pl.BlockSpec((1, d, f), lambda i, a, e: (e, 0, 0)),
                  pl.BlockSpec((1, f, d), lambda i, a, e: (e, 0, 0))],
        out_specs=pl.BlockSpec((1, tm, d), lambda i, a, e: (i, a, 0)),
        out_shape=jax.ShapeDtypeStruct((b, l, d), F32),
        scratch_shapes=[pltpu.VMEM((tm, d), BF16), pltpu.VMEM((tm, LANES), F32), pltpu.VMEM((tm, d), F32)],
        compiler_params=_cp("parallel", "parallel", "arbitrary"),
        name="moe",
    )(x, mod, gain.reshape(1, d), router_p, wg, wu, wd)


def _lane_row(values, lane0):
    row = jnp.zeros((LANES,), F32)
    return lax.dynamic_update_slice(row, values.reshape(-1).astype(F32), (lane0,))


def _reorder_w_in(w_in):
    g = GROUP_W
    o = [0, g, 2 * g, 3 * g, 4 * g, 4 * g + 8, 4 * g + 16]
    dq_dk_dv = w_in[..., o[0]:o[3]]
    dgate = w_in[..., o[3]:o[4]]
    dbeta_da = w_in[..., o[4]:o[6]]
    base = o[6]
    df = w_in[..., base:base + 3 * g]
    hy = w_in[..., base + 3 * g:base + 6 * g]
    ml = w_in[..., base + 6 * g:base + 9 * g]
    mo = w_in[..., base + 9 * g:base + 10 * g]
    mi_mf = w_in[..., base + 10 * g:base + 10 * g + 16]
    pad = jnp.zeros(w_in.shape[:-1] + (LANES - 32,), w_in.dtype)
    return jnp.concatenate([dq_dk_dv, hy, ml, df, dgate, mo, dbeta_da, mi_mf, pad], axis=-1).astype(BF16)


def _layer(l, x, mod, cache, p, hy_consts, dft):
    b, seq, _ = x.shape
    latent = cache is not None
    z = _proj_in(x, mod, p["norm1_g"][l], p["w_in"][l])

    qkv = _dwconv(z, p["dn_conv"][l], COL_DN, "deltanet")
    dn_par = jnp.zeros((8, LANES), F32).at[0].set(_lane_row(p["dn_a_log"][l], LANE_A)).at[1].set(
        _lane_row(p["dn_dt_bias"][l], LANE_A))
    s0 = cache[2] if latent else jnp.zeros((b, 2, N_HEADS, HEAD_DIM, HEAD_DIM), F32)
    o_dn_f, o_dn_b, s_dn = _deltanet(qkv, z, dn_par, s0)
    o_dn = (o_dn_f, o_dn_b)

    ml_par = jnp.zeros((8, LANES), F32).at[0].set(_lane_row(p["ml_i_bias"][l], LANE_I)).at[1].set(
        _lane_row(p["ml_f_bias"][l], LANE_F))
    if latent:
        c0 = cache[3]
        n0 = cache[4].reshape(b, 2, 1, GROUP_W)
        m0 = jnp.repeat(cache[5], HEAD_DIM, axis=-1).reshape(b, 2, 1, GROUP_W)
    else:
        c0 = jnp.zeros((b, 2, N_HEADS, HEAD_DIM, HEAD_DIM), F32)
        n0 = m0 = jnp.zeros((b, 2, 1, GROUP_W), F32)
    h_ml_f, h_ml_b, c_ml, n_ml, m_ml = _mlstm(z, ml_par, c0, n0, m0)
    h_ml = (h_ml_f, h_ml_b)
    n_ml = n_ml.reshape(b, 2, N_HEADS, HEAD_DIM)
    m_ml = m_ml.reshape(b, 2, N_HEADS, HEAD_DIM)[..., 0]

    q, kt, v, kd = _diff_prep(z, p["df_q_norm"][l], p["df_k_norm"][l], latent)
    if latent:
        ckt, cv = _cache_prep(cache[0].reshape(b, -1, GROUP_W), cache[1].reshape(b, -1, GROUP_W))
        kt = jnp.concatenate([kt, ckt], axis=-1)
        v = jnp.concatenate([v, cv], axis=2)
    lam_init = 0.8 - 0.6 * math.exp(-0.3 * l)
    o_df = _diff_attn(q, kt, v, p["df_lambda"][l], p["df_norm"][l], lam_init)

    feat, win = hy_consts
    fwd, inv = dft
    zc = _dwconv(z, p["hy_conv"][l], COL_HY, "plain")
    filt = _hy_filters(feat, win, p["hy_w1"][l], p["hy_b1"][l], p["hy_w2"][l], p["hy_b2"][l], p["hy_w3"][l],
                       p["hy_freq"][l])
    hf = _dft_filter(fwd, filt)
    y1 = _dft_fwd_mul(fwd, zc, 2, hf, 0)
    z1 = _dft_inv_gate(inv, y1, zc, 0, zc, 2, p["hy_skip"][l, 0])
    y2 = _dft_fwd_mul(fwd, z1, 0, hf, 1)
    y_hy = _dft_inv_gate(inv, y2, zc, 1, z1, 0, p["hy_skip"][l, 1])

    x = _proj_out(o_dn, z, o_df, y_hy, h_ml, x, mod, p["dn_norm"][l], p["ml_norm"][l], p["w_out"][l])
    j = l // 2
    if l % 2 == 0:
        x = _ffn(x, mod, p["norm2_g"][l], p["ffn_w_gate"][j], p["ffn_w_up"][j], p["ffn_w_down"][j])
    else:
        x = _moe(x, mod, p["norm2_g"][l], p["moe_router"][j], p["moe_w_gate"][j], p["moe_w_up"][j],
                 p["moe_w_down"][j])
    fv = z[:, :, 3 * 3 * GROUP_W + 2 * GROUP_W:3 * 3 * GROUP_W + 3 * GROUP_W]
    return x, (kd, fv, s_dn, c_ml, n_ml, m_ml)


def kernel(x_prompt, x_sample, cache_diff_k, cache_diff_v, state_delta, state_mlstm_c, state_mlstm_n, state_mlstm_m, c, c_ctx, norm1_g, norm2_g, w_mod, b_mod, w_in, w_out, dn_conv, dn_a_log, dn_dt_bias, dn_norm, df_q_norm, df_k_norm, df_lambda, df_norm, hy_conv, hy_w1, hy_b1, hy_w2, hy_b2, hy_w3, hy_freq, hy_skip, ml_i_bias, ml_f_bias, ml_norm, ffn_w_gate, ffn_w_up, ffn_w_down, moe_router, moe_w_gate, moe_w_up, moe_w_down):
    depth = w_in.shape[0]
    d_model = x_prompt.shape[-1]
    batch, seq, _ = x_prompt.shape
    dec_batch, dec_seq, _ = x_sample.shape
    p = dict(norm1_g=norm1_g, norm2_g=norm2_g, w_in=_reorder_w_in(w_in), w_out=w_out.astype(BF16),
             dn_conv=dn_conv, dn_a_log=dn_a_log, dn_dt_bias=dn_dt_bias, dn_norm=dn_norm,
             df_q_norm=df_q_norm, df_k_norm=df_k_norm, df_lambda=df_lambda, df_norm=df_norm,
             hy_conv=hy_conv, hy_w1=hy_w1, hy_b1=hy_b1, hy_w2=hy_w2, hy_b2=hy_b2, hy_w3=hy_w3,
             hy_freq=hy_freq, hy_skip=hy_skip, ml_i_bias=ml_i_bias, ml_f_bias=ml_f_bias, ml_norm=ml_norm,
             ffn_w_gate=ffn_w_gate.astype(BF16), ffn_w_up=ffn_w_up.astype(BF16),
             ffn_w_down=ffn_w_down.astype(BF16), moe_router=moe_router,
             moe_w_gate=moe_w_gate.astype(BF16), moe_w_up=moe_w_up.astype(BF16),
             moe_w_down=moe_w_down.astype(BF16))

    n_cond = 1 + dec_batch
    rows = -(-n_cond // 8) * 8
    cond = jnp.concatenate([c_ctx[None, :], c, jnp.zeros((rows - n_cond, d_model), F32)], axis=0)
    mod = _modulation(cond, w_mod, b_mod).reshape(depth, rows, 6, d_model)

    hy_ctx, dft_ctx = _hy_constants(seq), _dft_matrices(seq)
    x = x_prompt
    ctx = []
    for l in range(depth):
        x, out = _layer(l, x, mod[l, 0:1], None, p, hy_ctx, dft_ctx)
        ctx.append(out)
    y_prompt = x
    new_k, new_v, new_s, new_c, new_n, new_m = (jnp.stack([o[i] for o in ctx], axis=1) for i in range(6))
    new_k = new_k.reshape(batch, depth, seq, N_HEADS, 2, DF_DH)
    new_v = new_v.reshape(batch, depth, seq, N_HEADS, HEAD_DIM)

    hy_lat, dft_lat = _hy_constants(dec_seq), _dft_matrices(dec_seq)
    x = x_sample
    for l in range(depth):
        cache = (cache_diff_k[:, l], cache_diff_v[:, l], state_delta[:, l], state_mlstm_c[:, l],
                 state_mlstm_n[:, l], state_mlstm_m[:, l])
        x, _ = _layer(l, x, mod[l, 1:1 + dec_batch], cache, p, hy_lat, dft_lat)
    return (y_prompt, x, new_k, new_v, new_s, new_c, new_n, new_m)
```

```python
import functools
import math
from typing import Any, NamedTuple

import jax
import jax.numpy as jnp
from jax import lax
from jax.experimental import pallas as pl
from jax.experimental.pallas import tpu as pltpu

F32 = jnp.float32
BF16 = jnp.bfloat16

N_HEADS = 4
HEAD_DIM = 64
GROUP_W = N_HEADS * HEAD_DIM
DF_DH = 32
CHUNK = 64
GRID_W = 64
ROPE_THETA = 10000.0
HY_BANDS = 8
HY_FAST_DECAY = 0.3
HY_SLOW_DECAY = 1.5
HY_TARGET = 1e-2
HY_SHIFT = 0.05
N_EXPERTS = 8
EPS = 1e-6
NEG = -1e30
LANES = 128
VMEM_LIMIT = 56 * 1024 * 1024

COL_DN, COL_HY, COL_ML, COL_DF = 0, 1, 2, 3
COL_DGATE, COL_MO = 12, 13
COL_SMALL = 28
Z_WIDTH = 29 * LANES
LANE_BETA, LANE_A, LANE_I, LANE_F = 0, 8, 16, 24


def _tile(n, pref):
    t = min(n, pref)
    while n % t:
        t -= LANES if t > LANES else 8
    return t


def _cp(*sem):
    return pltpu.CompilerParams(dimension_semantics=sem, vmem_limit_bytes=VMEM_LIMIT)


def _split3(x):
    x1 = x.astype(BF16)
    r = x - x1.astype(F32)
    x2 = r.astype(BF16)
    r = r - x2.astype(F32)
    return x1, x2, r.astype(BF16)


def _dot(a, b):
    return jnp.dot(a.astype(BF16), b.astype(BF16), preferred_element_type=F32)


def _dot_nt(a, b):
    return lax.dot_general(a.astype(BF16), b.astype(BF16), (((1,), (1,)), ((), ())),
                           preferred_element_type=F32)


def _dot_tn(a, b):
    return lax.dot_general(a.astype(BF16), b.astype(BF16), (((0,), (0,)), ((), ())),
                           preferred_element_type=F32)


def _dot_exact_l(m, x):
    return sum(jnp.dot(m, p, preferred_element_type=F32) for p in _split3(x))


def _dot_exact_r(x, m):
    return sum(jnp.dot(p, m, preferred_element_type=F32) for p in _split3(x))


def _dot_hi(a, b):
    a1, a2, _ = _split3(a)
    b1, b2, _ = _split3(b)
    return (jnp.dot(a1, b1, preferred_element_type=F32) + jnp.dot(a1, b2, preferred_element_type=F32)
            + jnp.dot(a2, b1, preferred_element_type=F32))


def _seg_mean(x, seg):
    w = x.shape[-1]
    sh = int(math.log2(seg))
    r = lax.shift_right_logical(lax.broadcasted_iota(jnp.int32, (w, w), 0), sh)
    c = lax.shift_right_logical(lax.broadcasted_iota(jnp.int32, (w, w), 1), sh)
    bd = jnp.where(r == c, 1.0, 0.0).astype(BF16)
    return _dot_exact_r(x, bd) * (1.0 / seg)


def _row_bcast(col):
    n = col.shape[0]
    lane = lax.broadcasted_iota(jnp.int32, (n, LANES), 1)
    sel = jnp.where(lane == 0, 1.0, 0.0).astype(BF16)
    src = jnp.where(lane == 0, col, 0.0)
    return sum(lax.dot_general(sel, p, (((1,), (1,)), ((), ())), preferred_element_type=F32)
               for p in _split3(src))


def _lane_pick(x, lane_idx):
    lane = lax.broadcasted_iota(jnp.int32, x.shape, 1)
    return jnp.sum(jnp.where(lane == lane_idx, x, 0.0), axis=1, keepdims=True)


def _sigmoid(x):
    return 1.0 / (1.0 + jnp.exp(-x))


def _softplus(x):
    return jnp.maximum(x, 0.0) + jnp.log1p(jnp.exp(-jnp.abs(x)))


def _mod_kernel(c_ref, w_ref, b_ref, o_ref):
    c = c_ref[...]
    o_ref[0] = _dot(c * _sigmoid(c), w_ref[0]) + b_ref[0]


def _modulation(cond, w_mod, b_mod):
    depth, d, n = w_mod.shape
    r = cond.shape[0]
    tn = n // 4
    return pl.pallas_call(
        _mod_kernel,
        grid=(depth, n // tn),
        in_specs=[pl.BlockSpec((r, d), lambda l, j: (0, 0)),
                  pl.BlockSpec((1, d, tn), lambda l, j: (l, 0, j)),
                  pl.BlockSpec((1, 1, tn), lambda l, j: (l, 0, j))],
        out_specs=pl.BlockSpec((1, r, tn), lambda l, j: (l, 0, j)),
        out_shape=jax.ShapeDtypeStruct((depth, r, n), F32),
        compiler_params=_cp("parallel", "parallel"),
        name="modulation",
    )(cond, w_mod, b_mod.reshape(depth, 1, n))


def _modulated_norm(x, gain, shift, scale):
    var = jnp.mean(x * x, axis=-1, keepdims=True)
    return x * lax.rsqrt(var + EPS) * gain * (1.0 + scale) + shift


def _proj_in_kernel(x_ref, mod_ref, g_ref, w_ref, o_ref):
    h = _modulated_norm(x_ref[0], g_ref[...], mod_ref[0, 0:1, :], mod_ref[0, 1:2, :])
    o_ref[0] = jnp.dot(h.astype(BF16), w_ref[...], preferred_element_type=F32)


def _proj_in(x, mod, gain, w):
    b, l, d = x.shape
    tm = _tile(l, 256)
    per_batch = mod.shape[0] > 1
    return pl.pallas_call(
        _proj_in_kernel,
        grid=(b, l // tm),
        in_specs=[pl.BlockSpec((1, tm, d), lambda i, j: (i, j, 0)),
                  pl.BlockSpec((1, 6, d), lambda i, j: (i if per_batch else 0, 0, 0)),
                  pl.BlockSpec((1, d), lambda i, j: (0, 0)),
                  pl.BlockSpec((d, Z_WIDTH), lambda i, j: (0, 0))],
        out_specs=pl.BlockSpec((1, tm, Z_WIDTH), lambda i, j: (i, j, 0)),
        out_shape=jax.ShapeDtypeStruct((b, l, Z_WIDTH), F32),
        compiler_params=_cp("parallel", "parallel"),
        name="proj_in",
    )(x, mod, gain.reshape(1, d), w)


def _dwconv_kernel(z_ref, zp_ref, zn_ref, w_ref, o_ref, *lowp_refs, mode):
    i = pl.program_id(1)
    z = z_ref[0]
    tm = z.shape[0]
    prev_row = jnp.where(i > 0, zp_ref[0, 7:8, :], 0.0)
    next_row = jnp.where(i < pl.num_programs(1) - 1, zn_ref[0, 0:1, :], 0.0)
    rid = lax.broadcasted_iota(jnp.int32, z.shape, 0)
    zm1 = jnp.where(rid == 0, prev_row, pltpu.roll(z, 1, 0))
    zp1 = jnp.where(rid == tm - 1, next_row, pltpu.roll(z, tm - 1, 0))
    y = zm1 * w_ref[0:1, :] + z * w_ref[1:2, :] + zp1 * w_ref[2:3, :]
    if mode == "deltanet":
        y = y * _sigmoid(y)
        q, k, v = y[:, :GROUP_W], y[:, GROUP_W:2 * GROUP_W], y[:, 2 * GROUP_W:]
        q = q * lax.rsqrt(_seg_mean(q * q, HEAD_DIM) * HEAD_DIM + EPS) * (HEAD_DIM ** -0.5)
        k = k * lax.rsqrt(_seg_mean(k * k, HEAD_DIM) * HEAD_DIM + EPS)
        o_ref[0, :, 0:GROUP_W] = q
        o_ref[0, :, GROUP_W:2 * GROUP_W] = k
        o_ref[0, :, 2 * GROUP_W:] = v
    else:
        o_ref[0] = y
        lowp_refs[0][0] = y[:, 2 * GROUP_W:].astype(BF16)


def _dwconv(z, w, col_block, mode):
    b, l, _ = z.shape
    c = 3 * GROUP_W
    tm = _tile(l, 512)
    hb = tm // 8
    last = l // 8 - 1
    out_specs = [pl.BlockSpec((1, tm, c), lambda i, j: (i, j, 0))]
    out_shape = [jax.ShapeDtypeStruct((b, l, c), F32)]
    if mode == "plain":
        out_specs.append(pl.BlockSpec((1, tm, GROUP_W), lambda i, j: (i, j, 0)))
        out_shape.append(jax.ShapeDtypeStruct((b, l, GROUP_W), BF16))
    return pl.pallas_call(
        functools.partial(_dwconv_kernel, mode=mode),
        grid=(b, l // tm),
        in_specs=[pl.BlockSpec((1, tm, c), lambda i, j: (i, j, col_block)),
                  pl.BlockSpec((1, 8, c), lambda i, j: (i, jnp.maximum(j * hb - 1, 0), col_block)),
                  pl.BlockSpec((1, 8, c), lambda i, j: (i, jnp.minimum((j + 1) * hb, last), col_block)),
                  pl.BlockSpec((3, c), lambda i, j: (0, 0))],
        out_specs=out_specs,
        out_shape=out_shape,
        compiler_params=_cp("parallel", "parallel"),
        name="dwconv_" + mode,
    )(z, z, z, w)


STACK = N_HEADS * CHUNK


def _stack_heads(x):
    return jnp.concatenate([x] * N_HEADS, axis=0)


def _block_diag(x, head_eq):
    return jnp.where(head_eq, _stack_heads(x), 0.0)


def _fold_heads(x):
    return x[0:CHUNK] + x[CHUNK:2 * CHUNK] + x[2 * CHUNK:3 * CHUNK] + x[3 * CHUNK:]


def _chunk_masks(d):
    row = lax.broadcasted_iota(jnp.int32, (STACK, STACK), 0)
    col = lax.broadcasted_iota(jnp.int32, (STACK, STACK), 1)
    head_eq = lax.shift_right_logical(row, 6) == lax.shift_right_logical(col, 6)
    rel = ((row & (CHUNK - 1)) - (col & (CHUNK - 1))) * (1 - 2 * d)
    r64 = lax.broadcasted_iota(jnp.int32, (CHUNK, CHUNK), 0)
    c64 = lax.broadcasted_iota(jnp.int32, (CHUNK, CHUNK), 1)
    cum = jnp.where((r64 - c64) * (1 - 2 * d) >= 0, 1.0, 0.0).astype(BF16)
    return head_eq, head_eq & (rel >= 0), head_eq & (rel > 0), cum


def _head_cols(x, lane0):
    xs = _stack_heads(x)
    row = lax.broadcasted_iota(jnp.int32, xs.shape, 0)
    lane = lax.broadcasted_iota(jnp.int32, xs.shape, 1)
    return jnp.sum(jnp.where(lane == lane0 + lax.shift_right_logical(row, 6), xs, 0.0), axis=1, keepdims=True)


def _head_lanes(x, lane0):
    src = lax.broadcasted_iota(jnp.int32, (LANES, GROUP_W), 0)
    dst = lax.broadcasted_iota(jnp.int32, (LANES, GROUP_W), 1)
    expand = jnp.where(src == lane0 + lax.shift_right_logical(dst, 6), 1.0, 0.0).astype(BF16)
    return _dot_exact_r(x, expand)


def _lanes_to_col(x):
    row = lax.broadcasted_iota(jnp.int32, (STACK, GROUP_W), 0)
    lane = lax.broadcasted_iota(jnp.int32, (STACK, GROUP_W), 1)
    sel = jnp.where(lane == lax.shift_right_logical(row, 6) * HEAD_DIM, 1.0, 0.0).astype(BF16)
    xs = jnp.broadcast_to(x, (8, GROUP_W))
    out = sum(lax.dot_general(sel, p, (((1,), (1,)), ((), ())), preferred_element_type=F32) for p in _split3(xs))
    return out[:, 0:1]


def _load_diag_state(dst_ref, src_ref):
    dst_ref[...] = jnp.zeros(dst_ref.shape, F32)
    for r in range(dst_ref.shape[0]):
        for d in range(2):
            for h in range(N_HEADS):
                lo, hi = h * HEAD_DIM, (h + 1) * HEAD_DIM
                dst_ref[r, d, lo:hi, lo:hi] = src_ref[r, d, h]


def _store_diag_state(dst_ref, src_ref):
    for r in range(src_ref.shape[0]):
        for d in range(2):
            for h in range(N_HEADS):
                lo, hi = h * HEAD_DIM, (h + 1) * HEAD_DIM
                dst_ref[r, d, h] = src_ref[r, d, lo:hi, lo:hi]


def _rec_specs(b, l):
    rows = 2 if b % 2 == 0 else 1
    tile = _tile(l, 512)
    nt = l // tile
    fwd = lambda width, col: pl.BlockSpec((rows, tile, width), lambda i, j: (i, j, col))
    bwd = lambda width, col: pl.BlockSpec((rows, tile, width), lambda i, j: (i, nt - 1 - j, col))
    state = pl.BlockSpec((rows, 2, N_HEADS, HEAD_DIM, HEAD_DIM), lambda i, j: (i, 0, 0, 0, 0))
    return rows, tile, nt, fwd, bwd, state


class _Chain(NamedTuple):
    r: int
    d: int
    x_ref: Any
    sm_ref: Any
    o_ref: Any
    rows: Any


def _each(fn, *cols):
    return [fn(*args) for args in zip(*cols)]


def _unit_tri_inverses(mats):
    n = mats[0].shape[0]
    row = lax.broadcasted_iota(jnp.int32, (n, n), 0)
    col = lax.broadcasted_iota(jnp.int32, (n, n), 1)

    def same_block(log2_size):
        return lax.shift_right_logical(row, log2_size) == lax.shift_right_logical(col, log2_size)

    p = [jnp.where(same_block(3), -a, 0.0) for a in mats]
    t = [jnp.where(row == col, 1.0, 0.0) + x for x in p]
    for _ in range(2):
        p = _each(_dot, p, p)
        t = _each(jnp.add, t, _each(_dot, t, p))
    for log2_size in range(4, int(math.log2(CHUNK)) + 1):
        level = same_block(log2_size) & jnp.logical_not(same_block(log2_size - 1))
        off_t = [_dot(jnp.where(level, a, 0.0), x) for a, x in zip(mats, t)]
        t = _each(jnp.subtract, t, _each(_dot, t, off_t))
    return t


def _deltanet_chunks(chains, par_ref, s_ref):
    masks = {d: _chunk_masks(d) for d in (0, 1)}
    head_eq = masks[0][0]
    incl = [masks[c.d][1] for c in chains]
    strict = [masks[c.d][2] for c in chains]
    cum = [masks[c.d][3] for c in chains]
    lane_a = [LANE_A + c.d * N_HEADS for c in chains]
    lane_b = [LANE_BETA + c.d * N_HEADS for c in chains]
    q = [c.x_ref[c.r, c.rows, 0:GROUP_W] for c in chains]
    k = [c.x_ref[c.r, c.rows, GROUP_W:2 * GROUP_W] for c in chains]
    v = [c.x_ref[c.r, c.rows, 2 * GROUP_W:3 * GROUP_W] for c in chains]
    sm = [c.sm_ref[c.r, c.rows, :] for c in chains]
    beta_all = _each(_sigmoid, sm)
    g_all = [-jnp.exp(par_ref[0:1, :]) * _softplus(x + par_ref[1:2, :]) for x in sm]
    gc_all = _each(_dot_exact_l, cum, g_all)
    gc = _each(_head_lanes, gc_all, lane_a)
    beta = _each(_head_lanes, beta_all, lane_b)
    g_last = [x[CHUNK - 1:CHUNK] if c.d == 0 else x[0:1] for x, c in zip(gc, chains)]
    gc_col = _each(_head_cols, gc_all, lane_a)
    beta_col = _each(_head_cols, beta_all, lane_b)
    gc_row = _each(_row_bcast, gc_col)
    decay = [jnp.exp(jnp.where(m, x - y, NEG)) for m, x, y in zip(incl, gc_col, gc_row)]
    k_rows = _each(_stack_heads, k)
    kk = [_dot_nt(jnp.where(head_eq, x, 0.0), x) for x in k_rows]
    a = [jnp.where(m, b * x * dc, 0.0) for m, b, x, dc in zip(strict, beta_col, kk, decay)]
    t = _unit_tri_inverses(a)
    egc = _each(jnp.exp, gc)
    u = [_fold_heads(_dot(x, _block_diag(y * b, head_eq))) for x, y, b in zip(t, v, beta)]
    w = [_fold_heads(_dot(x, _block_diag(y * (b * e), head_eq))) for x, y, b, e in zip(t, k, beta, egc)]
    s = [s_ref[c.r, c.d] for c in chains]
    ws_qs = [_dot(jnp.concatenate([x, y * e], axis=0), z) for x, y, e, z in zip(w, q, egc, s)]
    v_new = [x - y[0:CHUNK] for x, y in zip(u, ws_qs)]
    qk = [_dot_nt(_block_diag(x, head_eq), y) * dc for x, y, dc in zip(q, k_rows, decay)]
    o_intra = [_fold_heads(_dot(x, _block_diag(y, head_eq))) for x, y in zip(qk, v_new)]
    s_add = [_dot_tn(x * jnp.exp(gl - g), y) for x, gl, g, y in zip(k, g_last, gc, v_new)]
    for c, x, y, z, gl, sa in zip(chains, ws_qs, o_intra, s, g_last, s_add):
        c.o_ref[c.r, c.rows, :] = x[CHUNK:] + y
        s_ref[c.r, c.d] = z * jnp.exp(gl) + jnp.where(head_eq, sa, 0.0)


def _chunk_chains(n_rows, n_chunks, c, f_refs, b_refs):
    rows_f = pl.ds(pl.multiple_of(c * CHUNK, CHUNK), CHUNK)
    rows_b = pl.ds(pl.multiple_of((n_chunks - 1 - c) * CHUNK, CHUNK), CHUNK)
    return [_Chain(r, d, *refs, rows) for r in range(n_rows)
            for d, refs, rows in ((0, f_refs, rows_f), (1, b_refs, rows_b))]


def _deltanet_kernel(qf_ref, smf_ref, qb_ref, smb_ref, par_ref, s0_ref, of_ref, ob_ref, sout_ref, s_ref, *,
                     n_chunks):
    j = pl.program_id(1)

    @pl.when(j == 0)
    def _():
        _load_diag_state(s_ref, s0_ref)

    def chunk_body(c, carry):
        chains = _chunk_chains(s_ref.shape[0], n_chunks, c, (qf_ref, smf_ref, of_ref), (qb_ref, smb_ref, ob_ref))
        _deltanet_chunks(chains, par_ref, s_ref)
        return carry

    lax.fori_loop(0, n_chunks, chunk_body, 0)

    @pl.when(j == pl.num_programs(1) - 1)
    def _():
        _store_diag_state(sout_ref, s_ref)


def _deltanet(qkv, z, par, s0):
    b, l, _ = qkv.shape
    rows, tile, nt, fwd, bwd, state = _rec_specs(b, l)
    return pl.pallas_call(
        functools.partial(_deltanet_kernel, n_chunks=tile // CHUNK),
        grid=(b // rows, nt),
        in_specs=[fwd(3 * GROUP_W, 0), fwd(LANES, COL_SMALL), bwd(3 * GROUP_W, 0), bwd(LANES, COL_SMALL),
                  pl.BlockSpec((8, LANES), lambda i, j: (0, 0)), state],
        out_specs=[fwd(GROUP_W, 0), bwd(GROUP_W, 0), state],
        out_shape=[jax.ShapeDtypeStruct((b, l, GROUP_W), F32), jax.ShapeDtypeStruct((b, l, GROUP_W), F32),
                   jax.ShapeDtypeStruct((b, 2, N_HEADS, HEAD_DIM, HEAD_DIM), F32)],
        scratch_shapes=[pltpu.VMEM((rows, 2, STACK, STACK), F32)],
        compiler_params=_cp("parallel", "arbitrary"),
        name="deltanet",
    )(qkv, z, qkv, z, par, s0)


def _mlstm_chunks(chains, par_ref, c_ref, n_ref, m_ref):
    masks = {d: _chunk_masks(d) for d in (0, 1)}
    head_eq = masks[0][0]
    incl = [masks[c.d][1] for c in chains]
    cum = [masks[c.d][3] for c in chains]
    lane_i = [LANE_I + c.d * N_HEADS for c in chains]
    lane_f = [LANE_F + c.d * N_HEADS for c in chains]
    q = [c.x_ref[c.r, c.rows, 0:GROUP_W] for c in chains]
    k = [c.x_ref[c.r, c.rows, GROUP_W:2 * GROUP_W] * (HEAD_DIM ** -0.5) for c in chains]
    v = [c.x_ref[c.r, c.rows, 2 * GROUP_W:3 * GROUP_W] for c in chains]
    sm = [c.sm_ref[c.r, c.rows, :] for c in chains]
    i_all = [x + par_ref[0:1, :] for x in sm]
    f_all = [-_softplus(-(x + par_ref[1:2, :])) for x in sm]
    bc_all = _each(_dot_exact_l, cum, f_all)
    ic = _each(_head_lanes, i_all, lane_i)
    bc = _each(_head_lanes, bc_all, lane_f)
    b_tot = [x[CHUNK - 1:CHUNK] if c.d == 0 else x[0:1] for x, c in zip(bc, chains)]
    ic_col = _each(_head_cols, i_all, lane_i)
    bc_col = _each(_head_cols, bc_all, lane_f)
    c_s = [c_ref[c.r, c.d] for c in chains]
    n_s = [n_ref[c.r, c.d] for c in chains]
    m_s = [m_ref[c.r, c.d] for c in chains]
    w_row = _each(_row_bcast, _each(jnp.subtract, ic_col, bc_col))
    dlog = [jnp.where(m, x + y, NEG) for m, x, y in zip(incl, bc_col, w_row)]
    a = _each(jnp.add, bc_col, _each(_lanes_to_col, m_s))
    m_t = [jnp.maximum(x, jnp.max(y, axis=1, keepdims=True)) for x, y in zip(a, dlog)]
    inter = [jnp.exp(x - y) for x, y in zip(a, m_t)]
    q_bd = [_block_diag(x, head_eq) for x in q]
    s = [_dot_nt(x, _stack_heads(y)) * jnp.exp(dl - mt) for x, y, dl, mt in zip(q_bd, k, dlog, m_t)]
    inter_part = _each(_dot, q_bd, c_s)
    intra_part = [_dot(x, _block_diag(y, head_eq)) for x, y in zip(s, v)]
    den = [it * jnp.sum(x * n, axis=1, keepdims=True) + jnp.sum(y, axis=1, keepdims=True)
           for it, x, n, y in zip(inter, q_bd, n_s, s)]
    wend = [bt - x + y for bt, x, y in zip(b_tot, bc, ic)]
    a_end = _each(jnp.add, b_tot, m_s)
    m_new = [jnp.maximum(x, jnp.max(y, axis=0, keepdims=True)) for x, y in zip(a_end, wend)]
    dec = [jnp.exp(x - y) for x, y in zip(a_end, m_new)]
    kw = [x * jnp.exp(y - z) for x, y, z in zip(k, wend, m_new)]
    c_add = _each(_dot_tn, kw, v)
    for i, c in enumerate(chains):
        num = inter[i] * inter_part[i] + intra_part[i]
        c.o_ref[c.r, c.rows, :] = _fold_heads(num / jnp.maximum(jnp.abs(den[i]), jnp.exp(-m_t[i])))
        c_ref[c.r, c.d] = dec[i] * c_s[i] + jnp.where(head_eq, c_add[i], 0.0)
        n_ref[c.r, c.d] = dec[i] * n_s[i] + jnp.sum(kw[i], axis=0, keepdims=True)
        m_ref[c.r, c.d] = m_new[i]


def _mlstm_kernel(zf_ref, smf_ref, zb_ref, smb_ref, par_ref, c0_ref, n0_ref, m0_ref,
                  of_ref, ob_ref, cout_ref, nout_ref, mout_ref, c_ref, n_ref, m_ref, *, n_chunks):
    j = pl.program_id(1)

    @pl.when(j == 0)
    def _():
        _load_diag_state(c_ref, c0_ref)
        n_ref[...] = n0_ref[...]
        m_ref[...] = m0_ref[...]

    def chunk_body(c, carry):
        chains = _chunk_chains(c_ref.shape[0], n_chunks, c, (zf_ref, smf_ref, of_ref), (zb_ref, smb_ref, ob_ref))
        _mlstm_chunks(chains, par_ref, c_ref, n_ref, m_ref)
        return carry

    lax.fori_loop(0, n_chunks, chunk_body, 0)

    @pl.when(j == pl.num_programs(1) - 1)
    def _():
        _store_diag_state(cout_ref, c_ref)
        nout_ref[...] = n_ref[...]
        mout_ref[...] = m_ref[...]


def _mlstm(z, par, c0, n0, m0):
    b, l, _ = z.shape
    rows, tile, nt, fwd, bwd, state = _rec_specs(b, l)
    vec = pl.BlockSpec((rows, 2, 1, GROUP_W), lambda i, j: (i, 0, 0, 0))
    return pl.pallas_call(
        functools.partial(_mlstm_kernel, n_chunks=tile // CHUNK),
        grid=(b // rows, nt),
        in_specs=[fwd(3 * GROUP_W, COL_ML), fwd(LANES, COL_SMALL), bwd(3 * GROUP_W, COL_ML), bwd(LANES, COL_SMALL),
                  pl.BlockSpec((8, LANES), lambda i, j: (0, 0)), state, vec, vec],
        out_specs=[fwd(GROUP_W, 0), bwd(GROUP_W, 0), state, vec, vec],
        out_shape=[jax.ShapeDtypeStruct((b, l, GROUP_W), F32), jax.ShapeDtypeStruct((b, l, GROUP_W), F32),
                   jax.ShapeDtypeStruct((b, 2, N_HEADS, HEAD_DIM, HEAD_DIM), F32),
                   jax.ShapeDtypeStruct((b, 2, 1, GROUP_W), F32), jax.ShapeDtypeStruct((b, 2, 1, GROUP_W), F32)],
        scratch_shapes=[pltpu.VMEM((rows, 2, STACK, STACK), F32), pltpu.VMEM((rows, 2, 1, GROUP_W), F32),
                        pltpu.VMEM((rows, 2, 1, GROUP_W), F32)],
        compiler_params=_cp("parallel", "arbitrary"),
        name="mlstm",
    )(z, z, z, z, par, c0, n0, m0)


def _diff_prep_kernel(z_ref, gq_ref, gk_ref, cos_ref, sin_ref, q_ref, kt_ref, v_ref, kd_ref, *, rope):
    z = z_ref[0]
    q, k, v = z[:, :GROUP_W], z[:, GROUP_W:2 * GROUP_W], z[:, 2 * GROUP_W:]
    q = q * lax.rsqrt(_seg_mean(q * q, DF_DH) + EPS) * gq_ref[...]
    k = k * lax.rsqrt(_seg_mean(k * k, DF_DH) + EPS) * gk_ref[...]
    kd_ref[0] = k
    if rope:
        lane = lax.broadcasted_iota(jnp.int32, q.shape, 1)
        first = (lane & (DF_DH - 1)) < DF_DH // 2

        def rot(x):
            swapped = jnp.where(first, pltpu.roll(x, GROUP_W - DF_DH // 2, 1), pltpu.roll(x, DF_DH // 2, 1))
            return x * cos_ref[...] + swapped * sin_ref[...]

        q, k = rot(q), rot(k)
    _store_attn_operands(q * (DF_DH ** -0.5), k, v, q_ref, kt_ref, v_ref)


def _store_attn_operands(q, k, v, q_ref, kt_ref, v_ref):
    kt = k.T
    lane = lax.broadcasted_iota(jnp.int32, (k.shape[0], HEAD_DIM), 1)
    ones = jnp.ones((k.shape[0], HEAD_DIM), BF16)
    for h in range(N_HEADS):
        lo, hi = h * HEAD_DIM, (h + 1) * HEAD_DIM
        if q is not None:
            q_ref[0, h, 0] = jnp.where(lane < DF_DH, q[:, lo:hi], 0.0).astype(BF16)
            q_ref[0, h, 1] = jnp.where(lane >= DF_DH, q[:, lo:hi], 0.0).astype(BF16)
        v_ref[0, h] = jnp.concatenate([v[:, lo:hi].astype(BF16), ones], axis=1)
        kt_ref[0, h] = kt[lo:hi, :].astype(BF16)


def _cache_prep_kernel(k_ref, v_ref, kt_ref, vh_ref):
    _store_attn_operands(None, k_ref[0], v_ref[0], None, kt_ref, vh_ref)


def _rope_tables(l):
    rows = l // GRID_W
    r = jnp.repeat(jnp.arange(rows, dtype=F32), GRID_W)
    col = jnp.tile(jnp.arange(GRID_W, dtype=F32), rows)
    n_freq = DF_DH // 4
    inv = ROPE_THETA ** (-jnp.arange(n_freq, dtype=F32) / n_freq)
    ang = jnp.concatenate([r[:, None] * inv, col[:, None] * inv], axis=-1)
    cos, sin = jnp.cos(ang), jnp.sin(ang)
    reps = GROUP_W // DF_DH
    return (jnp.tile(jnp.concatenate([cos, cos], axis=-1), (1, reps)),
            jnp.tile(jnp.concatenate([-sin, sin], axis=-1), (1, reps)))


def _diff_prep(z, gq, gk, rope):
    b, l, _ = z.shape
    tm = _tile(l, 512)
    if rope:
        cos, sin = _rope_tables(l)
    else:
        cos = sin = jnp.zeros((l, GROUP_W), F32)
    return pl.pallas_call(
        functools.partial(_diff_prep_kernel, rope=rope),
        grid=(b, l // tm),
        in_specs=[pl.BlockSpec((1, tm, 3 * GROUP_W), lambda i, j: (i, j, COL_DF)),
                  pl.BlockSpec((1, GROUP_W), lambda i, j: (0, 0)),
                  pl.BlockSpec((1, GROUP_W), lambda i, j: (0, 0)),
                  pl.BlockSpec((tm, GROUP_W), lambda i, j: (j, 0)),
                  pl.BlockSpec((tm, GROUP_W), lambda i, j: (j, 0))],
        out_specs=[pl.BlockSpec((1, N_HEADS, 2, tm, HEAD_DIM), lambda i, j: (i, 0, 0, j, 0)),
                   pl.BlockSpec((1, N_HEADS, HEAD_DIM, tm), lambda i, j: (i, 0, 0, j)),
                   pl.BlockSpec((1, N_HEADS, tm, 2 * HEAD_DIM), lambda i, j: (i, 0, j, 0)),
                   pl.BlockSpec((1, tm, GROUP_W), lambda i, j: (i, j, 0))],
        out_shape=[jax.ShapeDtypeStruct((b, N_HEADS, 2, l, HEAD_DIM), BF16),
                   jax.ShapeDtypeStruct((b, N_HEADS, HEAD_DIM, l), BF16),
                   jax.ShapeDtypeStruct((b, N_HEADS, l, 2 * HEAD_DIM), BF16),
                   jax.ShapeDtypeStruct((b, l, GROUP_W), F32)],
        compiler_params=_cp("parallel", "parallel"),
        name="diff_prep",
    )(z, jnp.tile(gq, GROUP_W // DF_DH).reshape(1, GROUP_W), jnp.tile(gk, GROUP_W // DF_DH).reshape(1, GROUP_W),
      cos, sin)


def _cache_prep(ck, cv):
    b, p, _ = ck.shape
    tm = _tile(p, 512)
    return pl.pallas_call(
        _cache_prep_kernel,
        grid=(b, p // tm),
        in_specs=[pl.BlockSpec((1, tm, GROUP_W), lambda i, j: (i, j, 0)),
                  pl.BlockSpec((1, tm, GROUP_W), lambda i, j: (i, j, 0))],
        out_specs=[pl.BlockSpec((1, N_HEADS, HEAD_DIM, tm), lambda i, j: (i, 0, 0, j)),
                   pl.BlockSpec((1, N_HEADS, tm, 2 * HEAD_DIM), lambda i, j: (i, 0, j, 0))],
        out_shape=[jax.ShapeDtypeStruct((b, N_HEADS, HEAD_DIM, p), BF16),
                   jax.ShapeDtypeStruct((b, N_HEADS, p, 2 * HEAD_DIM), BF16)],
        compiler_params=_cp("parallel", "parallel"),
        name="cache_prep",
    )(ck, cv)


def _diff_attn_kernel(q_ref, kt_ref, v_ref, lam_ref, g_ref, o_ref, m_ref, acc_ref, *, lam_init):
    ik = pl.program_id(3)

    @pl.when(ik == 0)
    def _():
        m_ref[...] = jnp.full(m_ref.shape, NEG, F32)
        acc_ref[...] = jnp.zeros(acc_ref.shape, F32)

    chains = [(h, m) for h in range(q_ref.shape[1]) for m in range(2)]
    s = [jnp.dot(q_ref[0, h, m], kt_ref[0, h], preferred_element_type=F32) for h, m in chains]
    m_old = [m_ref[h, m] for h, m in chains]
    m_new = [jnp.maximum(x, jnp.max(y, axis=1, keepdims=True)) for x, y in zip(m_old, s)]
    p = [jnp.exp(x - y[:, 0:1]).astype(BF16) for x, y in zip(s, m_new)]
    pv = [jnp.dot(x, v_ref[0, h], preferred_element_type=F32) for x, (h, m) in zip(p, chains)]
    for i, (h, m) in enumerate(chains):
        acc_ref[h, m] = jnp.exp(m_old[i] - m_new[i]) * acc_ref[h, m] + pv[i]
        m_ref[h, m] = m_new[i]

    @pl.when(ik == pl.num_programs(3) - 1)
    def _():
        lp = lam_ref[...]
        lam = (jnp.exp(jnp.sum(lp[0:1] * lp[1:2], axis=1, keepdims=True))
               - jnp.exp(jnp.sum(lp[2:3] * lp[3:4], axis=1, keepdims=True)) + lam_init)
        for h in range(q_ref.shape[1]):
            a0, a1 = acc_ref[h, 0], acc_ref[h, 1]
            o = (a0[:, :HEAD_DIM] / a0[:, HEAD_DIM:HEAD_DIM + 1]
                 - lam * (a1[:, :HEAD_DIM] / a1[:, HEAD_DIM:HEAD_DIM + 1]))
            var = jnp.mean(o * o, axis=1, keepdims=True)
            o_ref[0, h] = o * lax.rsqrt(var + EPS) * g_ref[...] * (1.0 - lam_init)


ATTN_HEADS_PER_STEP = 2


def _diff_attn(q, kt, v, lam_par, gain, lam_init):
    b, _, _, l, _ = q.shape
    lk = kt.shape[-1]
    hs = ATTN_HEADS_PER_STEP
    tq = _tile(l, 512)
    tk = _tile(lk, 1536)
    return pl.pallas_call(
        functools.partial(_diff_attn_kernel, lam_init=lam_init),
        grid=(b, N_HEADS // hs, l // tq, lk // tk),
        in_specs=[pl.BlockSpec((1, hs, 2, tq, HEAD_DIM), lambda i, h, a, c: (i, h, 0, a, 0)),
                  pl.BlockSpec((1, hs, HEAD_DIM, tk), lambda i, h, a, c: (i, h, 0, c)),
                  pl.BlockSpec((1, hs, tk, 2 * HEAD_DIM), lambda i, h, a, c: (i, h, c, 0)),
                  pl.BlockSpec((4, DF_DH), lambda i, h, a, c: (0, 0)),
                  pl.BlockSpec((1, HEAD_DIM), lambda i, h, a, c: (0, 0))],
        out_specs=pl.BlockSpec((1, hs, tq, HEAD_DIM), lambda i, h, a, c: (i, h, a, 0)),
        out_shape=jax.ShapeDtypeStruct((b, N_HEADS, l, HEAD_DIM), F32),
        scratch_shapes=[pltpu.VMEM((hs, 2, tq, LANES), F32), pltpu.VMEM((hs, 2, tq, 2 * HEAD_DIM), F32)],
        compiler_params=_cp("parallel", "parallel", "parallel", "arbitrary"),
        name="diff_attn",
    )(q, kt, v, lam_par, gain.reshape(1, HEAD_DIM))


def _hy_filter_kernel(feat_ref, win_ref, w1_ref, b1_ref, w2_ref, b2_ref, w3_ref, fr_ref, o_ref):
    h = jnp.sin(fr_ref[0:1, :] * (_dot_hi(feat_ref[...], w1_ref[...]) + b1_ref[...]))
    h = jnp.sin(fr_ref[1:2, :] * (_dot_hi(h, w2_ref[...]) + b2_ref[...]))
    h = _dot_hi(h, w3_ref[...]) * win_ref[...]
    o_ref[...] = h * lax.rsqrt(jnp.sum(h * h, axis=0, keepdims=True) + EPS)


def _hy_constants(l):
    pos = jnp.arange(l, dtype=F32)
    bands = jnp.arange(1, HY_BANDS + 1, dtype=F32)
    ang = (2.0 * math.pi / l) * pos[:, None] * bands[None, :]
    feat = jnp.concatenate([pos[:, None] / l, jnp.cos(ang), jnp.sin(ang)], axis=-1)
    feat = jnp.pad(feat, ((0, 0), (0, LANES - feat.shape[1])))
    rates = jnp.linspace(-math.log(HY_TARGET) / HY_FAST_DECAY, -math.log(HY_TARGET) / HY_SLOW_DECAY,
                         GROUP_W, dtype=F32)
    rates = jnp.tile(rates, 2)
    offset = jnp.abs(pos - l // 2) / l
    return feat, jnp.exp(-offset[:, None] * rates[None, :]) + HY_SHIFT


def _hy_filters(feat, win, w1, b1, w2, b2, w3, freq):
    l = feat.shape[0]
    hid = w2.shape[0]
    w1p = jnp.pad(w1, ((0, LANES - w1.shape[0]), (0, 0)))
    full = lambda shape: pl.BlockSpec(shape, lambda o: (0, 0))
    return pl.pallas_call(
        _hy_filter_kernel,
        grid=(2,),
        in_specs=[full((l, LANES)), pl.BlockSpec((l, GROUP_W), lambda o: (0, o)),
                  full((LANES, hid)), full((1, hid)), full((hid, hid)), full((1, hid)),
                  pl.BlockSpec((hid, GROUP_W), lambda o: (0, o)), full((2, hid))],
        out_specs=pl.BlockSpec((l, GROUP_W), lambda o: (0, o)),
        out_shape=jax.ShapeDtypeStruct((l, 2 * GROUP_W), F32),
        compiler_params=_cp("parallel"),
        name="hy_filters",
    )(feat, win, w1p, b1.reshape(1, hid), w2, b2.reshape(1, hid), w3, freq)


def _dft_matrices(l):
    n = 2 * l
    k = jnp.arange(l, dtype=jnp.int32)
    ang = (2.0 * math.pi / n) * ((k[:, None] * k[None, :]) % n).astype(F32)
    alt = jnp.where(k % 2 == 0, 1.0, -1.0).astype(F32)
    fwd_re = jnp.cos(ang)
    fwd_im = jnp.where(k[:, None] == 0, alt[None, :], -jnp.sin(ang))
    fwd = jnp.stack([fwd_re, fwd_im]).astype(BF16)
    t = k + l // 2
    ang_i = (2.0 * math.pi / n) * ((t[:, None] * k[None, :]) % n).astype(F32)
    alt_t = jnp.where(t % 2 == 0, 1.0, -1.0).astype(F32)
    inv_re = jnp.where(k[None, :] == 0, 1.0 / n, (2.0 / n) * jnp.cos(ang_i))
    inv_im = jnp.where(k[None, :] == 0, alt_t[:, None] / n, -(2.0 / n) * jnp.sin(ang_i))
    inv = jnp.concatenate([inv_re, inv_im], axis=1).astype(BF16)
    return fwd, inv


def _dft_filter_kernel(f_ref, h_ref, o_ref):
    h = h_ref[...].astype(BF16)
    o_ref[0] = jnp.dot(f_ref[0], h, preferred_element_type=F32)
    o_ref[1] = jnp.dot(f_ref[1], h, preferred_element_type=F32)


def _dft_filter(fwd, filt):
    l, c = filt.shape
    tm = _tile(l, 256)
    return pl.pallas_call(
        _dft_filter_kernel,
        grid=(l // tm,),
        in_specs=[pl.BlockSpec((2, tm, l), lambda i: (0, i, 0)), pl.BlockSpec((l, c), lambda i: (0, 0))],
        out_specs=pl.BlockSpec((2, tm, c), lambda i: (0, i, 0)),
        out_shape=jax.ShapeDtypeStruct((2, l, c), F32),
        compiler_params=_cp("parallel"),
        name="dft_filter",
    )(fwd, filt)


def _dft_fwd_mul_kernel(f_ref, u_ref, h_ref, o_ref):
    u = u_ref[0]
    ur = jnp.dot(f_ref[0], u, preferred_element_type=F32)
    ui = jnp.dot(f_ref[1], u, preferred_element_type=F32)
    hr, hi = h_ref[0], h_ref[1]
    row = lax.broadcasted_iota(jnp.int32, ur.shape, 0) + pl.program_id(0) * ur.shape[0]
    packed = row == 0
    o_ref[0, 0] = (ur * hr - jnp.where(packed, 0.0, ui * hi)).astype(BF16)
    o_ref[0, 1] = jnp.where(packed, ui * hi, ur * hi + ui * hr).astype(BF16)


def _dft_fwd_mul(fwd, u, hf, order):
    b, l, _ = u.shape
    tm = _tile(l, 512)
    return pl.pallas_call(
        _dft_fwd_mul_kernel,
        grid=(l // tm, b),
        in_specs=[pl.BlockSpec((2, tm, l), lambda i, j: (0, i, 0)),
                  pl.BlockSpec((1, l, GROUP_W), lambda i, j: (j, 0, 0)),
                  pl.BlockSpec((2, tm, GROUP_W), lambda i, j: (0, i, order))],
        out_specs=pl.BlockSpec((1, 2, tm, GROUP_W), lambda i, j: (j, 0, i, 0)),
        out_shape=jax.ShapeDtypeStruct((b, 2, l, GROUP_W), BF16),
        compiler_params=_cp("parallel", "parallel"),
        name="dft_fwd_mul",
    )(fwd, u, hf)


def _dft_inv_gate_kernel(g_ref, y_ref, x_ref, u_ref, skip_ref, o_ref, lowp_ref):
    conv = jnp.dot(g_ref[...], y_ref[0], preferred_element_type=F32)
    out = x_ref[0] * (conv + skip_ref[...] * u_ref[0])
    o_ref[0] = out
    lowp_ref[0] = out.astype(BF16)


def _dft_inv_gate(inv, y, xg, x_col, u, u_col, skip):
    b, _, l, _ = y.shape
    tm = _tile(l, 512)
    out_spec = pl.BlockSpec((1, tm, GROUP_W), lambda i, j: (j, i, 0))
    return pl.pallas_call(
        _dft_inv_gate_kernel,
        grid=(l // tm, b),
        in_specs=[pl.BlockSpec((tm, 2 * l), lambda i, j: (i, 0)),
                  pl.BlockSpec((1, 2 * l, GROUP_W), lambda i, j: (j, 0, 0)),
                  pl.BlockSpec((1, tm, GROUP_W), lambda i, j: (j, i, x_col)),
                  pl.BlockSpec((1, tm, GROUP_W), lambda i, j: (j, i, u_col)),
                  pl.BlockSpec((1, GROUP_W), lambda i, j: (0, 0))],
        out_specs=[out_spec, out_spec],
        out_shape=[jax.ShapeDtypeStruct((b, l, GROUP_W), F32), jax.ShapeDtypeStruct((b, l, GROUP_W), BF16)],
        compiler_params=_cp("parallel", "parallel"),
        name="dft_inv_gate",
    )(inv, y.reshape(b, 2 * l, GROUP_W), xg, u, skip.reshape(1, GROUP_W))


def _head_norm(x, gain):
    return x * lax.rsqrt(_seg_mean(x * x, HEAD_DIM) + EPS) * gain


def _proj_out_kernel(dnf_ref, dnb_ref, gate_ref, df_ref, hy_ref, mlf_ref, mlb_ref, mo_ref, x_ref, mod_ref,
                     gdn_ref, gml_ref, w_ref, o_ref):
    gate = gate_ref[0]
    y_dn = _head_norm(dnf_ref[0] + dnb_ref[0], gdn_ref[...]) * (gate * _sigmoid(gate))
    y_ml = _head_norm(mlf_ref[0] + mlb_ref[0], gml_ref[...]) * _sigmoid(mo_ref[0])
    y = _dot(y_dn, w_ref[0:GROUP_W, :])
    for h in range(N_HEADS):
        lo = GROUP_W + h * HEAD_DIM
        y = y + _dot(df_ref[0, h], w_ref[lo:lo + HEAD_DIM, :])
    y = y + _dot(hy_ref[0], w_ref[2 * GROUP_W:3 * GROUP_W, :])
    y = y + _dot(y_ml, w_ref[3 * GROUP_W:, :])
    o_ref[0] = x_ref[0] + mod_ref[0, 2:3, :] * y


def _proj_out(o_dn, z, o_df, y_hy, h_ml, x, mod, g_dn, g_ml, w):
    b, l, d = x.shape
    tm = _tile(l, 512)
    per_batch = mod.shape[0] > 1
    group = lambda col: pl.BlockSpec((1, tm, GROUP_W), lambda i, j: (i, j, col))
    return pl.pallas_call(
        _proj_out_kernel,
        grid=(b, l // tm),
        in_specs=[group(0), group(0), group(COL_DGATE),
                  pl.BlockSpec((1, N_HEADS, tm, HEAD_DIM), lambda i, j: (i, 0, j, 0)),
                  group(0), group(0), group(0), group(COL_MO),
                  pl.BlockSpec((1, tm, d), lambda i, j: (i, j, 0)),
                  pl.BlockSpec((1, 6, d), lambda i, j: (i if per_batch else 0, 0, 0)),
                  pl.BlockSpec((1, GROUP_W), lambda i, j: (0, 0)),
                  pl.BlockSpec((1, GROUP_W), lambda i, j: (0, 0)),
                  pl.BlockSpec((d, d), lambda i, j: (0, 0))],
        out_specs=pl.BlockSpec((1, tm, d), lambda i, j: (i, j, 0)),
        out_shape=jax.ShapeDtypeStruct((b, l, d), F32),
        compiler_params=_cp("parallel", "parallel"),
        name="proj_out",
    )(o_dn[0], o_dn[1], z, o_df, y_hy, h_ml[0], h_ml[1], z, x, mod, jnp.tile(g_dn, N_HEADS).reshape(1, GROUP_W),
      jnp.tile(g_ml, N_HEADS).reshape(1, GROUP_W), w)


def _ffn_kernel(x_ref, mod_ref, g_ref, wg_ref, wu_ref, wd_ref, o_ref, h_ref, acc_ref):
    j = pl.program_id(2)

    @pl.when(j == 0)
    def _():
        h_ref[...] = _modulated_norm(x_ref[0], g_ref[...], mod_ref[0, 3:4, :], mod_ref[0, 4:5, :]).astype(BF16)
        acc_ref[...] = jnp.zeros(acc_ref.shape, F32)

    h = h_ref[...]
    g = jnp.dot(h, wg_ref[...], preferred_element_type=F32)
    u = jnp.dot(h, wu_ref[...], preferred_element_type=F32)
    acc_ref[...] += jnp.dot((g * _sigmoid(g) * u).astype(BF16), wd_ref[...], preferred_element_type=F32)

    @pl.when(j == pl.num_programs(2) - 1)
    def _():
        o_ref[0] = x_ref[0] + mod_ref[0, 5:6, :] * acc_ref[...]


def _ffn(x, mod, gain, wg, wu, wd):
    b, l, d = x.shape
    f = wg.shape[1]
    tm = _tile(l, 1024)
    tf = 256
    per_batch = mod.shape[0] > 1
    return pl.pallas_call(
        _ffn_kernel,
        grid=(b, l // tm, f // tf),
        in_specs=[pl.BlockSpec((1, tm, d), lambda i, a, j: (i, a, 0)),
                  pl.BlockSpec((1, 6, d), lambda i, a, j: (i if per_batch else 0, 0, 0)),
                  pl.BlockSpec((1, d), lambda i, a, j: (0, 0)),
                  pl.BlockSpec((d, tf), lambda i, a, j: (0, j)),
                  pl.BlockSpec((d, tf), lambda i, a, j: (0, j)),
                  pl.BlockSpec((tf, d), lambda i, a, j: (j, 0))],
        out_specs=pl.BlockSpec((1, tm, d), lambda i, a, j: (i, a, 0)),
        out_shape=jax.ShapeDtypeStruct((b, l, d), F32),
        scratch_shapes=[pltpu.VMEM((tm, d), BF16), pltpu.VMEM((tm, d), F32)],
        compiler_params=_cp("parallel", "parallel", "arbitrary"),
        name="ffn_dense",
    )(x, mod, gain.reshape(1, d), wg, wu, wd)


def _top2_combine(logits):
    lane = lax.broadcasted_iota(jnp.int32, logits.shape, 1)
    v1 = jnp.max(logits, axis=1, keepdims=True)
    i1 = jnp.min(jnp.where(logits == v1, lane, LANES), axis=1, keepdims=True)
    rest = jnp.where(lane == i1, NEG, logits)
    v2 = jnp.max(rest, axis=1, keepdims=True)
    i2 = jnp.min(jnp.where(rest == v2, lane, LANES), axis=1, keepdims=True)
    e2 = jnp.exp(v2 - v1)
    return jnp.where(lane == i1, 1.0 / (1.0 + e2), 0.0) + jnp.where(lane == i2, e2 / (1.0 + e2), 0.0)


def _moe_kernel(x_ref, mod_ref, g_ref, r_ref, wg_ref, wu_ref, wd_ref, o_ref, h_ref, comb_ref, acc_ref):
    e = pl.program_id(2)

    @pl.when(e == 0)
    def _():
        hn = _modulated_norm(x_ref[0], g_ref[...], mod_ref[0, 3:4, :], mod_ref[0, 4:5, :])
        h_ref[...] = hn.astype(BF16)
        lane = lax.broadcasted_iota(jnp.int32, (hn.shape[0], LANES), 1)
        comb_ref[...] = _top2_combine(jnp.where(lane < N_EXPERTS, _dot_hi(hn, r_ref[...]), NEG))
        acc_ref[...] = jnp.zeros(acc_ref.shape, F32)

    h = h_ref[...]
    g = jnp.dot(h, wg_ref[0], preferred_element_type=F32)
    u = jnp.dot(h, wu_ref[0], preferred_element_type=F32)
    a = g * _sigmoid(g) * u * _lane_pick(comb_ref[...], e)
    acc_ref[...] += jnp.dot(a.astype(BF16), wd_ref[0], preferred_element_type=F32)

    @pl.when(e == pl.num_programs(2) - 1)
    def _():
        o_ref[0] = x_ref[0] + mod_ref[0, 5:6, :] * acc_ref[...]


def _moe(x, mod, gain, router, wg, wu, wd):
    b, l, d = x.shape
    ne, _, f = wg.shape
    tm = _tile(l, 512)
    per_batch = mod.shape[0] > 1
    router_p = jnp.pad(router, ((0, 0), (0, LANES - ne)))
    return pl.pallas_call(
        _moe_kernel,
        grid=(b, l // tm, ne),
        in_specs=[pl.BlockSpec((1, tm, d), lambda i, a, e: (i, a, 0)),
                  pl.BlockSpec((1, 6, d), lambda i, a, e: (i if per_batch else 0, 0, 0)),
                  pl.BlockSpec((1, d), lambda i, a, e: (0, 0)),
                  pl.BlockSpec((d, LANES), lambda i, a, e: (0, 0)),
                  pl.BlockSpec((1, d, f), lambda i, a, e: (e, 0, 0)),
                  pl.BlockSpec((1, d, f), lambda i, a, e: (e, 0, 0)),
                  pl.BlockSpec((1, f, d), lambda i, a, e: (e, 0, 0))],
        out_specs=pl.BlockSpec((1, tm, d), lambda i, a, e: (i, a, 0)),
        out_shape=jax.ShapeDtypeStruct((b, l, d), F32),
        scratch_shapes=[pltpu.VMEM((tm, d), BF16), pltpu.VMEM((tm, LANES), F32), pltpu.VMEM((tm, d), F32)],
        compiler_params=_cp("parallel", "parallel", "arbitrary"),
        name="moe",
    )(x, mod, gain.reshape(1, d), router_p, wg, wu, wd)


def _lane_row(values, lane0):
    row = jnp.zeros((LANES,), F32)
    return lax.dynamic_update_slice(row, values.reshape(-1).astype(F32), (lane0,))


def _reorder_w_in(w_in):
    g = GROUP_W
    o = [0, g, 2 * g, 3 * g, 4 * g, 4 * g + 8, 4 * g + 16]
    dq_dk_dv = w_in[..., o[0]:o[3]]
    dgate = w_in[..., o[3]:o[4]]
    dbeta_da = w_in[..., o[4]:o[6]]
    base = o[6]
    df = w_in[..., base:base + 3 * g]
    hy = w_in[..., base + 3 * g:base + 6 * g]
    ml = w_in[..., base + 6 * g:base + 9 * g]
    mo = w_in[..., base + 9 * g:base + 10 * g]
    mi_mf = w_in[..., base + 10 * g:base + 10 * g + 16]
    pad = jnp.zeros(w_in.shape[:-1] + (LANES - 32,), w_in.dtype)
    return jnp.concatenate([dq_dk_dv, hy, ml, df, dgate, mo, dbeta_da, mi_mf, pad], axis=-1).astype(BF16)


def _layer(l, x, mod, cache, p, hy_consts, dft):
    b, seq, _ = x.shape
    latent = cache is not None
    z = _proj_in(x, mod, p["norm1_g"][l], p["w_in"][l])

    qkv, = _dwconv(z, p["dn_conv"][l], COL_DN, "deltanet")
    dn_par = jnp.zeros((8, LANES), F32).at[0].set(_lane_row(p["dn_a_log"][l], LANE_A)).at[1].set(
        _lane_row(p["dn_dt_bias"][l], LANE_A))
    s0 = cache[2] if latent else jnp.zeros((b, 2, N_HEADS, HEAD_DIM, HEAD_DIM), F32)
    o_dn_f, o_dn_b, s_dn = _deltanet(qkv, z, dn_par, s0)
    o_dn = (o_dn_f, o_dn_b)

    ml_par = jnp.zeros((8, LANES), F32).at[0].set(_lane_row(p["ml_i_bias"][l], LANE_I)).at[1].set(
        _lane_row(p["ml_f_bias"][l], LANE_F))
    if latent:
        c0 = cache[3]
        n0 = cache[4].reshape(b, 2, 1, GROUP_W)
        m0 = jnp.repeat(cache[5], HEAD_DIM, axis=-1).reshape(b, 2, 1, GROUP_W)
    else:
        c0 = jnp.zeros((b, 2, N_HEADS, HEAD_DIM, HEAD_DIM), F32)
        n0 = m0 = jnp.zeros((b, 2, 1, GROUP_W), F32)
    h_ml_f, h_ml_b, c_ml, n_ml, m_ml = _mlstm(z, ml_par, c0, n0, m0)
    h_ml = (h_ml_f, h_ml_b)
    n_ml = n_ml.reshape(b, 2, N_HEADS, HEAD_DIM)
    m_ml = m_ml.reshape(b, 2, N_HEADS, HEAD_DIM)[..., 0]

    q, kt, v, kd = _diff_prep(z, p["df_q_norm"][l], p["df_k_norm"][l], latent)
    if latent:
        ckt, cv = _cache_prep(cache[0].reshape(b, -1, GROUP_W), cache[1].reshape(b, -1, GROUP_W))
        kt = jnp.concatenate([kt, ckt], axis=-1)
        v = jnp.concatenate([v, cv], axis=2)
    lam_init = 0.8 - 0.6 * math.exp(-0.3 * l)
    o_df = _diff_attn(q, kt, v, p["df_lambda"][l], p["df_norm"][l], lam_init)

    feat, win = hy_consts
    fwd, inv = dft
    zc, zv_lowp = _dwconv(z, p["hy_conv"][l], COL_HY, "plain")
    filt = _hy_filters(feat, win, p["hy_w1"][l], p["hy_b1"][l], p["hy_w2"][l], p["hy_b2"][l], p["hy_w3"][l],
                       p["hy_freq"][l])
    hf = _dft_filter(fwd, filt)
    y1 = _dft_fwd_mul(fwd, zv_lowp, hf, 0)
    z1, z1_lowp = _dft_inv_gate(inv, y1, zc, 0, zc, 2, p["hy_skip"][l, 0])
    y2 = _dft_fwd_mul(fwd, z1_lowp, hf, 1)
    y_hy, _ = _dft_inv_gate(inv, y2, zc, 1, z1, 0, p["hy_skip"][l, 1])

    x = _proj_out(o_dn, z, o_df, y_hy, h_ml, x, mod, p["dn_norm"][l], p["ml_norm"][l], p["w_out"][l])
    j = l // 2
    if mod.shape[0] == 1:
        x = x.reshape(1, b * seq, -1)
    if l % 2 == 0:
        x = _ffn(x, mod, p["norm2_g"][l], p["ffn_w_gate"][j], p["ffn_w_up"][j], p["ffn_w_down"][j])
    else:
        x = _moe(x, mod, p["norm2_g"][l], p["moe_router"][j], p["moe_w_gate"][j], p["moe_w_up"][j],
                 p["moe_w_down"][j])
    x = x.reshape(b, seq, -1)
    fv =z[:, :, 3 * 3 * GROUP_W + 2 * GROUP_W:3 * 3 * GROUP_W + 3 * GROUP_W]
    return x, (kd, fv, s_dn, c_ml, n_ml, m_ml)


def kernel(x_prompt, x_sample, cache_diff_k, cache_diff_v, state_delta, state_mlstm_c, state_mlstm_n, state_mlstm_m, c, c_ctx, norm1_g, norm2_g, w_mod, b_mod, w_in, w_out, dn_conv, dn_a_log, dn_dt_bias, dn_norm, df_q_norm, df_k_norm, df_lambda, df_norm, hy_conv, hy_w1, hy_b1, hy_w2, hy_b2, hy_w3, hy_freq, hy_skip, ml_i_bias, ml_f_bias, ml_norm, ffn_w_gate, ffn_w_up, ffn_w_down, moe_router, moe_w_gate, moe_w_up, moe_w_down):
    depth = w_in.shape[0]
    d_model = x_prompt.shape[-1]
    batch, seq, _ = x_prompt.shape
    dec_batch, dec_seq, _ = x_sample.shape
    p = dict(norm1_g=norm1_g, norm2_g=norm2_g, w_in=_reorder_w_in(w_in), w_out=w_out.astype(BF16),
             dn_conv=dn_conv, dn_a_log=dn_a_log, dn_dt_bias=dn_dt_bias, dn_norm=dn_norm,
             df_q_norm=df_q_norm, df_k_norm=df_k_norm, df_lambda=df_lambda, df_norm=df_norm,
             hy_conv=hy_conv, hy_w1=hy_w1, hy_b1=hy_b1, hy_w2=hy_w2, hy_b2=hy_b2, hy_w3=hy_w3,
             hy_freq=hy_freq, hy_skip=hy_skip, ml_i_bias=ml_i_bias, ml_f_bias=ml_f_bias, ml_norm=ml_norm,
             ffn_w_gate=ffn_w_gate.astype(BF16), ffn_w_up=ffn_w_up.astype(BF16),
             ffn_w_down=ffn_w_down.astype(BF16), moe_router=moe_router,
             moe_w_gate=moe_w_gate.astype(BF16), moe_w_up=moe_w_up.astype(BF16),
             moe_w_down=moe_w_down.astype(BF16))

    n_cond = 1 + dec_batch
    rows = -(-n_cond // 8) * 8
    cond = jnp.concatenate([c_ctx[None, :], c, jnp.zeros((rows - n_cond, d_model), F32)], axis=0)
    mod = _modulation(cond, w_mod, b_mod).reshape(depth, rows, 6, d_model)

    hy_ctx, dft_ctx = _hy_constants(seq), _dft_matrices(seq)
    x = x_prompt
    ctx = []
    for l in range(depth):
        x, out = _layer(l, x, mod[l, 0:1], None, p, hy_ctx, dft_ctx)
        ctx.append(out)
    y_prompt = x
    new_k, new_v, new_s, new_c, new_n, new_m = (jnp.stack([o[i] for o in ctx], axis=1) for i in range(6))
    new_k = new_k.reshape(batch, depth, seq, N_HEADS, 2, DF_DH)
    new_v = new_v.reshape(batch, depth, seq, N_HEADS, HEAD_DIM)

    hy_lat, dft_lat = _hy_constants(dec_seq), _dft_matrices(dec_seq)
    x = x_sample
    for l in range(depth):
        cache = (cache_diff_k[:, l], cache_diff_v[:, l], state_delta[:, l], state_mlstm_c[:, l],
                 state_mlstm_n[:, l], state_mlstm_m[:, l])
        x, _ = _layer(l, x, mod[l, 1:1 + dec_batch], cache, p, hy_lat, dft_lat)
    return (y_prompt, x, new_k, new_v, new_s, new_c, new_n, new_m)
```

```python
import functools
import math
from typing import Any, NamedTuple

import jax
import jax.numpy as jnp
from jax import lax
from jax.experimental import pallas as pl
from jax.experimental.pallas import tpu as pltpu
from jax.experimental.pallas import tpu_sc as plsc

F32 = jnp.float32
BF16 = jnp.bfloat16

N_HEADS = 4
HEAD_DIM = 64
GROUP_W = N_HEADS * HEAD_DIM
DF_DH = 32
CHUNK = 64
GRID_W = 64
ROPE_THETA = 10000.0
HY_BANDS = 8
HY_FAST_DECAY = 0.3
HY_SLOW_DECAY = 1.5
HY_TARGET = 1e-2
HY_SHIFT = 0.05
N_EXPERTS = 8
EPS = 1e-6
NEG = -1e30
LANES = 128
VMEM_LIMIT = 56 * 1024 * 1024

COL_DN, COL_HY, COL_ML, COL_DF = 0, 1, 2, 3
COL_DGATE, COL_MO = 12, 13
COL_SMALL = 28
Z_WIDTH = 29 * LANES
LANE_BETA, LANE_A, LANE_I, LANE_F = 0, 8, 16, 24


def _tile(n, pref):
    t = min(n, pref)
    while n % t:
        t -= LANES if t > LANES else 8
    return t


def _cp(*sem):
    return pltpu.CompilerParams(dimension_semantics=sem, vmem_limit_bytes=VMEM_LIMIT)


def _split3(x):
    x1 = x.astype(BF16)
    r = x - x1.astype(F32)
    x2 = r.astype(BF16)
    r = r - x2.astype(F32)
    return x1, x2, r.astype(BF16)


def _dot(a, b):
    return jnp.dot(a.astype(BF16), b.astype(BF16), preferred_element_type=F32)


def _dot_nt(a, b):
    return lax.dot_general(a.astype(BF16), b.astype(BF16), (((1,), (1,)), ((), ())),
                           preferred_element_type=F32)


def _dot_tn(a, b):
    return lax.dot_general(a.astype(BF16), b.astype(BF16), (((0,), (0,)), ((), ())),
                           preferred_element_type=F32)


def _dot_exact_l(m, x):
    return sum(jnp.dot(m, p, preferred_element_type=F32) for p in _split3(x))


def _dot_exact_r(x, m):
    return sum(jnp.dot(p, m, preferred_element_type=F32) for p in _split3(x))


def _dot_hi(a, b):
    a1, a2, _ = _split3(a)
    b1, b2, _ = _split3(b)
    return (jnp.dot(a1, b1, preferred_element_type=F32) + jnp.dot(a1, b2, preferred_element_type=F32)
            + jnp.dot(a2, b1, preferred_element_type=F32))


def _seg_mean(x, seg):
    w = x.shape[-1]
    sh = int(math.log2(seg))
    r = lax.shift_right_logical(lax.broadcasted_iota(jnp.int32, (w, w), 0), sh)
    c = lax.shift_right_logical(lax.broadcasted_iota(jnp.int32, (w, w), 1), sh)
    bd = jnp.where(r == c, 1.0, 0.0).astype(BF16)
    return _dot_exact_r(x, bd) * (1.0 / seg)


def _row_bcast(col):
    n = col.shape[0]
    lane = lax.broadcasted_iota(jnp.int32, (n, LANES), 1)
    sel = jnp.where(lane == 0, 1.0, 0.0).astype(BF16)
    src = jnp.where(lane == 0, col, 0.0)
    return sum(lax.dot_general(sel, p, (((1,), (1,)), ((), ())), preferred_element_type=F32)
               for p in _split3(src))


def _lane_pick(x, lane_idx):
    lane = lax.broadcasted_iota(jnp.int32, x.shape, 1)
    return jnp.sum(jnp.where(lane == lane_idx, x, 0.0), axis=1, keepdims=True)


def _sigmoid(x):
    return 1.0 / (1.0 + jnp.exp(-x))


def _softplus(x):
    return jnp.maximum(x, 0.0) + jnp.log1p(jnp.exp(-jnp.abs(x)))


def _mod_kernel(c_ref, w_ref, b_ref, o_ref):
    c = c_ref[...]
    o_ref[0] = _dot(c * _sigmoid(c), w_ref[0]) + b_ref[0]


def _modulation(cond, w_mod, b_mod):
    depth, d, n = w_mod.shape
    r = cond.shape[0]
    tn = n // 4
    return pl.pallas_call(
        _mod_kernel,
        grid=(depth, n // tn),
        in_specs=[pl.BlockSpec((r, d), lambda l, j: (0, 0)),
                  pl.BlockSpec((1, d, tn), lambda l, j: (l, 0, j)),
                  pl.BlockSpec((1, 1, tn), lambda l, j: (l, 0, j))],
        out_specs=pl.BlockSpec((1, r, tn), lambda l, j: (l, 0, j)),
        out_shape=jax.ShapeDtypeStruct((depth, r, n), F32),
        compiler_params=_cp("parallel", "parallel"),
        name="modulation",
    )(cond, w_mod, b_mod.reshape(depth, 1, n))


def _modulated_norm(x, gain, shift, scale):
    var = jnp.mean(x * x, axis=-1, keepdims=True)
    return x * lax.rsqrt(var + EPS) * gain * (1.0 + scale) + shift


def _proj_in_kernel(x_ref, mod_ref, g_ref, w_ref, o_ref):
    h = _modulated_norm(x_ref[0], g_ref[...], mod_ref[0, 0:1, :], mod_ref[0, 1:2, :])
    o_ref[0] = jnp.dot(h.astype(BF16), w_ref[...], preferred_element_type=F32)


def _proj_in(x, mod, gain, w):
    b, l, d = x.shape
    tm = _tile(l, 256)
    per_batch = mod.shape[0] > 1
    return pl.pallas_call(
        _proj_in_kernel,
        grid=(b, l // tm),
        in_specs=[pl.BlockSpec((1, tm, d), lambda i, j: (i, j, 0)),
                  pl.BlockSpec((1, 6, d), lambda i, j: (i if per_batch else 0, 0, 0)),
                  pl.BlockSpec((1, d), lambda i, j: (0, 0)),
                  pl.BlockSpec((d, Z_WIDTH), lambda i, j: (0, 0))],
        out_specs=pl.BlockSpec((1, tm, Z_WIDTH), lambda i, j: (i, j, 0)),
        out_shape=jax.ShapeDtypeStruct((b, l, Z_WIDTH), F32),
        compiler_params=_cp("parallel", "parallel"),
        name="proj_in",
    )(x, mod, gain.reshape(1, d), w)


def _dwconv_kernel(z_ref, zp_ref, zn_ref, w_ref, o_ref, *lowp_refs, mode):
    i = pl.program_id(1)
    z = z_ref[0]
    tm = z.shape[0]
    prev_row = jnp.where(i > 0, zp_ref[0, 7:8, :], 0.0)
    next_row = jnp.where(i < pl.num_programs(1) - 1, zn_ref[0, 0:1, :], 0.0)
    rid = lax.broadcasted_iota(jnp.int32, z.shape, 0)
    zm1 = jnp.where(rid == 0, prev_row, pltpu.roll(z, 1, 0))
    zp1 = jnp.where(rid == tm - 1, next_row, pltpu.roll(z, tm - 1, 0))
    y = zm1 * w_ref[0:1, :] + z * w_ref[1:2, :] + zp1 * w_ref[2:3, :]
    if mode == "deltanet":
        y = y * _sigmoid(y)
        q, k, v = y[:, :GROUP_W], y[:, GROUP_W:2 * GROUP_W], y[:, 2 * GROUP_W:]
        q = q * lax.rsqrt(_seg_mean(q * q, HEAD_DIM) * HEAD_DIM + EPS) * (HEAD_DIM ** -0.5)
        k = k * lax.rsqrt(_seg_mean(k * k, HEAD_DIM) * HEAD_DIM + EPS)
        o_ref[0, :, 0:GROUP_W] = q
        o_ref[0, :, GROUP_W:2 * GROUP_W] = k
        o_ref[0, :, 2 * GROUP_W:] = v
    else:
        o_ref[0] = y
        lowp_refs[0][0] = y[:, 2 * GROUP_W:].astype(BF16)


def _dwconv(z, w, col_block, mode):
    b, l, _ = z.shape
    c = 3 * GROUP_W
    tm = _tile(l, 512)
    hb = tm // 8
    last = l // 8 - 1
    out_specs = [pl.BlockSpec((1, tm, c), lambda i, j: (i, j, 0))]
    out_shape = [jax.ShapeDtypeStruct((b, l, c), F32)]
    if mode == "plain":
        out_specs.append(pl.BlockSpec((1, tm, GROUP_W), lambda i, j: (i, j, 0)))
        out_shape.append(jax.ShapeDtypeStruct((b, l, GROUP_W), BF16))
    return pl.pallas_call(
        functools.partial(_dwconv_kernel, mode=mode),
        grid=(b, l // tm),
        in_specs=[pl.BlockSpec((1, tm, c), lambda i, j: (i, j, col_block)),
                  pl.BlockSpec((1, 8, c), lambda i, j: (i, jnp.maximum(j * hb - 1, 0), col_block)),
                  pl.BlockSpec((1, 8, c), lambda i, j: (i, jnp.minimum((j + 1) * hb, last), col_block)),
                  pl.BlockSpec((3, c), lambda i, j: (0, 0))],
        out_specs=out_specs,
        out_shape=out_shape,
        compiler_params=_cp("parallel", "parallel"),
        name="dwconv_" + mode,
    )(z, z, z, w)


STACK = N_HEADS * CHUNK


def _stack_heads(x):
    return jnp.concatenate([x] * N_HEADS, axis=0)


def _block_diag(x, head_eq):
    return jnp.where(head_eq, _stack_heads(x), 0.0)


def _fold_heads(x):
    return x[0:CHUNK] + x[CHUNK:2 * CHUNK] + x[2 * CHUNK:3 * CHUNK] + x[3 * CHUNK:]


def _chunk_masks(d):
    row = lax.broadcasted_iota(jnp.int32, (STACK, STACK), 0)
    col = lax.broadcasted_iota(jnp.int32, (STACK, STACK), 1)
    head_eq = lax.shift_right_logical(row, 6) == lax.shift_right_logical(col, 6)
    rel = ((row & (CHUNK - 1)) - (col & (CHUNK - 1))) * (1 - 2 * d)
    r64 = lax.broadcasted_iota(jnp.int32, (CHUNK, CHUNK), 0)
    c64 = lax.broadcasted_iota(jnp.int32, (CHUNK, CHUNK), 1)
    cum = jnp.where((r64 - c64) * (1 - 2 * d) >= 0, 1.0, 0.0).astype(BF16)
    return head_eq, head_eq & (rel >= 0), head_eq & (rel > 0), cum


def _head_cols(x, lane0):
    xs = _stack_heads(x)
    row = lax.broadcasted_iota(jnp.int32, xs.shape, 0)
    lane = lax.broadcasted_iota(jnp.int32, xs.shape, 1)
    return jnp.sum(jnp.where(lane == lane0 + lax.shift_right_logical(row, 6), xs, 0.0), axis=1, keepdims=True)


def _head_lanes(x, lane0):
    src = lax.broadcasted_iota(jnp.int32, (LANES, GROUP_W), 0)
    dst = lax.broadcasted_iota(jnp.int32, (LANES, GROUP_W), 1)
    expand = jnp.where(src == lane0 + lax.shift_right_logical(dst, 6), 1.0, 0.0).astype(BF16)
    return _dot_exact_r(x, expand)


def _lanes_to_col(x):
    row = lax.broadcasted_iota(jnp.int32, (STACK, GROUP_W), 0)
    lane = lax.broadcasted_iota(jnp.int32, (STACK, GROUP_W), 1)
    sel = jnp.where(lane == lax.shift_right_logical(row, 6) * HEAD_DIM, 1.0, 0.0).astype(BF16)
    xs = jnp.broadcast_to(x, (8, GROUP_W))
    out = sum(lax.dot_general(sel, p, (((1,), (1,)), ((), ())), preferred_element_type=F32) for p in _split3(xs))
    return out[:, 0:1]


def _load_diag_state(dst_ref, src_ref):
    dst_ref[...] = jnp.zeros(dst_ref.shape, F32)
    for r in range(dst_ref.shape[0]):
        for d in range(2):
            for h in range(N_HEADS):
                lo, hi = h * HEAD_DIM, (h + 1) * HEAD_DIM
                dst_ref[r, d, lo:hi, lo:hi] = src_ref[r, d, h]


def _store_diag_state(dst_ref, src_ref):
    for r in range(src_ref.shape[0]):
        for d in range(2):
            for h in range(N_HEADS):
                lo, hi = h * HEAD_DIM, (h + 1) * HEAD_DIM
                dst_ref[r, d, h] = src_ref[r, d, lo:hi, lo:hi]


def _rec_specs(b, l):
    rows = 2 if b % 2 == 0 else 1
    tile = _tile(l, 512)
    nt = l // tile
    fwd = lambda width, col: pl.BlockSpec((rows, tile, width), lambda i, j: (i, j, col))
    bwd = lambda width, col: pl.BlockSpec((rows, tile, width), lambda i, j: (i, nt - 1 - j, col))
    state = pl.BlockSpec((rows, 2, N_HEADS, HEAD_DIM, HEAD_DIM), lambda i, j: (i, 0, 0, 0, 0))
    return rows, tile, nt, fwd, bwd, state


class _Chain(NamedTuple):
    r: int
    d: int
    x_ref: Any
    sm_ref: Any
    o_ref: Any
    rows: Any


def _each(fn, *cols):
    return [fn(*args) for args in zip(*cols)]


def _unit_tri_inverses(mats):
    n = mats[0].shape[0]
    row = lax.broadcasted_iota(jnp.int32, (n, n), 0)
    col = lax.broadcasted_iota(jnp.int32, (n, n), 1)

    def same_block(log2_size):
        return lax.shift_right_logical(row, log2_size) == lax.shift_right_logical(col, log2_size)

    p = [jnp.where(same_block(3), -a, 0.0) for a in mats]
    t = [jnp.where(row == col, 1.0, 0.0) + x for x in p]
    for _ in range(2):
        p = _each(_dot, p, p)
        t = _each(jnp.add, t, _each(_dot, t, p))
    for log2_size in range(4, int(math.log2(CHUNK)) + 1):
        level = same_block(log2_size) & jnp.logical_not(same_block(log2_size - 1))
        off_t = [_dot(jnp.where(level, a, 0.0), x) for a, x in zip(mats, t)]
        t = _each(jnp.subtract, t, _each(_dot, t, off_t))
    return t


def _deltanet_chunks(chains, par_ref, s_ref):
    masks = {d: _chunk_masks(d) for d in (0, 1)}
    head_eq = masks[0][0]
    incl = [masks[c.d][1] for c in chains]
    strict = [masks[c.d][2] for c in chains]
    cum = [masks[c.d][3] for c in chains]
    lane_a = [LANE_A + c.d * N_HEADS for c in chains]
    lane_b = [LANE_BETA + c.d * N_HEADS for c in chains]
    q = [c.x_ref[c.r, c.rows, 0:GROUP_W] for c in chains]
    k = [c.x_ref[c.r, c.rows, GROUP_W:2 * GROUP_W] for c in chains]
    v = [c.x_ref[c.r, c.rows, 2 * GROUP_W:3 * GROUP_W] for c in chains]
    sm = [c.sm_ref[c.r, c.rows, :] for c in chains]
    beta_all = _each(_sigmoid, sm)
    g_all = [-jnp.exp(par_ref[0:1, :]) * _softplus(x + par_ref[1:2, :]) for x in sm]
    gc_all = _each(_dot_exact_l, cum, g_all)
    gc = _each(_head_lanes, gc_all, lane_a)
    beta = _each(_head_lanes, beta_all, lane_b)
    g_last = [x[CHUNK - 1:CHUNK] if c.d == 0 else x[0:1] for x, c in zip(gc, chains)]
    gc_col = _each(_head_cols, gc_all, lane_a)
    beta_col = _each(_head_cols, beta_all, lane_b)
    gc_row = _each(_row_bcast, gc_col)
    decay = [jnp.exp(jnp.where(m, x - y, NEG)) for m, x, y in zip(incl, gc_col, gc_row)]
    k_rows = _each(_stack_heads, k)
    kk = [_dot_nt(jnp.where(head_eq, x, 0.0), x) for x in k_rows]
    a = [jnp.where(m, b * x * dc, 0.0) for m, b, x, dc in zip(strict, beta_col, kk, decay)]
    t = _unit_tri_inverses(a)
    egc = _each(jnp.exp, gc)
    u = [_fold_heads(_dot(x, _block_diag(y * b, head_eq))) for x, y, b in zip(t, v, beta)]
    w = [_fold_heads(_dot(x, _block_diag(y * (b * e), head_eq))) for x, y, b, e in zip(t, k, beta, egc)]
    s = [s_ref[c.r, c.d] for c in chains]
    ws_qs = [_dot(jnp.concatenate([x, y * e], axis=0), z) for x, y, e, z in zip(w, q, egc, s)]
    v_new = [x - y[0:CHUNK] for x, y in zip(u, ws_qs)]
    qk = [_dot_nt(_block_diag(x, head_eq), y) * dc for x, y, dc in zip(q, k_rows, decay)]
    o_intra = [_fold_heads(_dot(x, _block_diag(y, head_eq))) for x, y in zip(qk, v_new)]
    s_add = [_dot_tn(x * jnp.exp(gl - g), y) for x, gl, g, y in zip(k, g_last, gc, v_new)]
    for c, x, y, z, gl, sa in zip(chains, ws_qs, o_intra, s, g_last, s_add):
        c.o_ref[c.r, c.rows, :] = x[CHUNK:] + y
        s_ref[c.r, c.d] = z * jnp.exp(gl) + jnp.where(head_eq, sa, 0.0)


def _chunk_chains(n_rows, n_chunks, c, f_refs, b_refs):
    rows_f = pl.ds(pl.multiple_of(c * CHUNK, CHUNK), CHUNK)
    rows_b = pl.ds(pl.multiple_of((n_chunks - 1 - c) * CHUNK, CHUNK), CHUNK)
    return [_Chain(r, d, *refs, rows) for r in range(n_rows)
            for d, refs, rows in ((0, f_refs, rows_f), (1, b_refs, rows_b))]


def _deltanet_kernel(qf_ref, smf_ref, qb_ref, smb_ref, par_ref, s0_ref, of_ref, ob_ref, sout_ref, s_ref, *,
                     n_chunks):
    j = pl.program_id(1)

    @pl.when(j == 0)
    def _():
        _load_diag_state(s_ref, s0_ref)

    def chunk_body(c, carry):
        chains = _chunk_chains(s_ref.shape[0], n_chunks, c, (qf_ref, smf_ref, of_ref), (qb_ref, smb_ref, ob_ref))
        _deltanet_chunks(chains, par_ref, s_ref)
        return carry

    lax.fori_loop(0, n_chunks, chunk_body, 0)

    @pl.when(j == pl.num_programs(1) - 1)
    def _():
        _store_diag_state(sout_ref, s_ref)


def _deltanet(qkv, z, par, s0):
    b, l, _ = qkv.shape
    rows, tile, nt, fwd, bwd, state = _rec_specs(b, l)
    return pl.pallas_call(
        functools.partial(_deltanet_kernel, n_chunks=tile // CHUNK),
        grid=(b // rows, nt),
        in_specs=[fwd(3 * GROUP_W, 0), fwd(LANES, COL_SMALL), bwd(3 * GROUP_W, 0), bwd(LANES, COL_SMALL),
                  pl.BlockSpec((8, LANES), lambda i, j: (0, 0)), state],
        out_specs=[fwd(GROUP_W, 0), bwd(GROUP_W, 0), state],
        out_shape=[jax.ShapeDtypeStruct((b, l, GROUP_W), F32), jax.ShapeDtypeStruct((b, l, GROUP_W), F32),
                   jax.ShapeDtypeStruct((b, 2, N_HEADS, HEAD_DIM, HEAD_DIM), F32)],
        scratch_shapes=[pltpu.VMEM((rows, 2, STACK, STACK), F32)],
        compiler_params=_cp("parallel", "arbitrary"),
        name="deltanet",
    )(qkv, z, qkv, z, par, s0)


def _mlstm_chunks(chains, par_ref, c_ref, n_ref, m_ref):
    masks = {d: _chunk_masks(d) for d in (0, 1)}
    head_eq = masks[0][0]
    incl = [masks[c.d][1] for c in chains]
    cum = [masks[c.d][3] for c in chains]
    lane_i = [LANE_I + c.d * N_HEADS for c in chains]
    lane_f = [LANE_F + c.d * N_HEADS for c in chains]
    q = [c.x_ref[c.r, c.rows, 0:GROUP_W] for c in chains]
    k = [c.x_ref[c.r, c.rows, GROUP_W:2 * GROUP_W] * (HEAD_DIM ** -0.5) for c in chains]
    v = [c.x_ref[c.r, c.rows, 2 * GROUP_W:3 * GROUP_W] for c in chains]
    sm = [c.sm_ref[c.r, c.rows, :] for c in chains]
    i_all = [x + par_ref[0:1, :] for x in sm]
    f_all = [-_softplus(-(x + par_ref[1:2, :])) for x in sm]
    bc_all = _each(_dot_exact_l, cum, f_all)
    ic = _each(_head_lanes, i_all, lane_i)
    bc = _each(_head_lanes, bc_all, lane_f)
    b_tot = [x[CHUNK - 1:CHUNK] if c.d == 0 else x[0:1] for x, c in zip(bc, chains)]
    ic_col = _each(_head_cols, i_all, lane_i)
    bc_col = _each(_head_cols, bc_all, lane_f)
    c_s = [c_ref[c.r, c.d] for c in chains]
    n_s = [n_ref[c.r, c.d] for c in chains]
    m_s = [m_ref[c.r, c.d] for c in chains]
    w_row = _each(_row_bcast, _each(jnp.subtract, ic_col, bc_col))
    dlog = [jnp.where(m, x + y, NEG) for m, x, y in zip(incl, bc_col, w_row)]
    a = _each(jnp.add, bc_col, _each(_lanes_to_col, m_s))
    m_t = [jnp.maximum(x, jnp.max(y, axis=1, keepdims=True)) for x, y in zip(a, dlog)]
    inter = [jnp.exp(x - y) for x, y in zip(a, m_t)]
    q_bd = [_block_diag(x, head_eq) for x in q]
    s = [_dot_nt(x, _stack_heads(y)) * jnp.exp(dl - mt) for x, y, dl, mt in zip(q_bd, k, dlog, m_t)]
    inter_part = _each(_dot, q_bd, c_s)
    intra_part = [_dot(x, _block_diag(y, head_eq)) for x, y in zip(s, v)]
    den = [it * jnp.sum(x * n, axis=1, keepdims=True) + jnp.sum(y, axis=1, keepdims=True)
           for it, x, n, y in zip(inter, q_bd, n_s, s)]
    wend = [bt - x + y for bt, x, y in zip(b_tot, bc, ic)]
    a_end = _each(jnp.add, b_tot, m_s)
    m_new = [jnp.maximum(x, jnp.max(y, axis=0, keepdims=True)) for x, y in zip(a_end, wend)]
    dec = [jnp.exp(x - y) for x, y in zip(a_end, m_new)]
    kw = [x * jnp.exp(y - z) for x, y, z in zip(k, wend, m_new)]
    c_add = _each(_dot_tn, kw, v)
    for i, c in enumerate(chains):
        num = inter[i] * inter_part[i] + intra_part[i]
        c.o_ref[c.r, c.rows, :] = _fold_heads(num / jnp.maximum(jnp.abs(den[i]), jnp.exp(-m_t[i])))
        c_ref[c.r, c.d] = dec[i] * c_s[i] + jnp.where(head_eq, c_add[i], 0.0)
        n_ref[c.r, c.d] = dec[i] * n_s[i] + jnp.sum(kw[i], axis=0, keepdims=True)
        m_ref[c.r, c.d] = m_new[i]


def _mlstm_kernel(zf_ref, smf_ref, zb_ref, smb_ref, par_ref, c0_ref, n0_ref, m0_ref,
                  of_ref, ob_ref, cout_ref, nout_ref, mout_ref, c_ref, n_ref, m_ref, *, n_chunks):
    j = pl.program_id(1)

    @pl.when(j == 0)
    def _():
        _load_diag_state(c_ref, c0_ref)
        n_ref[...] = n0_ref[...]
        m_ref[...] = m0_ref[...]

    def chunk_body(c, carry):
        chains = _chunk_chains(c_ref.shape[0], n_chunks, c, (zf_ref, smf_ref, of_ref), (zb_ref, smb_ref, ob_ref))
        _mlstm_chunks(chains, par_ref, c_ref, n_ref, m_ref)
        return carry

    lax.fori_loop(0, n_chunks, chunk_body, 0)

    @pl.when(j == pl.num_programs(1) - 1)
    def _():
        _store_diag_state(cout_ref, c_ref)
        nout_ref[...] = n_ref[...]
        mout_ref[...] = m_ref[...]


def _mlstm(z, par, c0, n0, m0):
    b, l, _ = z.shape
    rows, tile, nt, fwd, bwd, state = _rec_specs(b, l)
    vec = pl.BlockSpec((rows, 2, 1, GROUP_W), lambda i, j: (i, 0, 0, 0))
    return pl.pallas_call(
        functools.partial(_mlstm_kernel, n_chunks=tile // CHUNK),
        grid=(b // rows, nt),
        in_specs=[fwd(3 * GROUP_W, COL_ML), fwd(LANES, COL_SMALL), bwd(3 * GROUP_W, COL_ML), bwd(LANES, COL_SMALL),
                  pl.BlockSpec((8, LANES), lambda i, j: (0, 0)), state, vec, vec],
        out_specs=[fwd(GROUP_W, 0), bwd(GROUP_W, 0), state, vec, vec],
        out_shape=[jax.ShapeDtypeStruct((b, l, GROUP_W), F32), jax.ShapeDtypeStruct((b, l, GROUP_W), F32),
                   jax.ShapeDtypeStruct((b, 2, N_HEADS, HEAD_DIM, HEAD_DIM), F32),
                   jax.ShapeDtypeStruct((b, 2, 1, GROUP_W), F32), jax.ShapeDtypeStruct((b, 2, 1, GROUP_W), F32)],
        scratch_shapes=[pltpu.VMEM((rows, 2, STACK, STACK), F32), pltpu.VMEM((rows, 2, 1, GROUP_W), F32),
                        pltpu.VMEM((rows, 2, 1, GROUP_W), F32)],
        compiler_params=_cp("parallel", "arbitrary"),
        name="mlstm",
    )(z, z, z, z, par, c0, n0, m0)


def _diff_prep_kernel(z_ref, gq_ref, gk_ref, cos_ref, sin_ref, q_ref, kt_ref, v_ref, kd_ref, *, rope):
    z = z_ref[0]
    q, k, v = z[:, :GROUP_W], z[:, GROUP_W:2 * GROUP_W], z[:, 2 * GROUP_W:]
    q = q * lax.rsqrt(_seg_mean(q * q, DF_DH) + EPS) * gq_ref[...]
    k = k * lax.rsqrt(_seg_mean(k * k, DF_DH) + EPS) * gk_ref[...]
    kd_ref[0] = k
    if rope:
        lane = lax.broadcasted_iota(jnp.int32, q.shape, 1)
        first = (lane & (DF_DH - 1)) < DF_DH // 2

        def rot(x):
            swapped = jnp.where(first, pltpu.roll(x, GROUP_W - DF_DH // 2, 1), pltpu.roll(x, DF_DH // 2, 1))
            return x * cos_ref[...] + swapped * sin_ref[...]

        q, k = rot(q), rot(k)
    _store_attn_operands(q * (DF_DH ** -0.5), k, v, q_ref, kt_ref, v_ref)


def _store_attn_operands(q, k, v, q_ref, kt_ref, v_ref):
    kt = k.T
    lane = lax.broadcasted_iota(jnp.int32, (k.shape[0], HEAD_DIM), 1)
    ones = jnp.ones((k.shape[0], HEAD_DIM), BF16)
    for h in range(N_HEADS):
        lo, hi = h * HEAD_DIM, (h + 1) * HEAD_DIM
        if q is not None:
            q_ref[0, h, 0] = jnp.where(lane < DF_DH, q[:, lo:hi], 0.0).astype(BF16)
            q_ref[0, h, 1] = jnp.where(lane >= DF_DH, q[:, lo:hi], 0.0).astype(BF16)
        v_ref[0, h] = jnp.concatenate([v[:, lo:hi].astype(BF16), ones], axis=1)
        kt_ref[0, h] = kt[lo:hi, :].astype(BF16)


def _cache_prep_kernel(k_ref, v_ref, kt_ref, vh_ref):
    _store_attn_operands(None, k_ref[0], v_ref[0], None, kt_ref, vh_ref)


def _rope_tables(l):
    rows = l // GRID_W
    r = jnp.repeat(jnp.arange(rows, dtype=F32), GRID_W)
    col = jnp.tile(jnp.arange(GRID_W, dtype=F32), rows)
    n_freq = DF_DH // 4
    inv = ROPE_THETA ** (-jnp.arange(n_freq, dtype=F32) / n_freq)
    ang = jnp.concatenate([r[:, None] * inv, col[:, None] * inv], axis=-1)
    cos, sin = jnp.cos(ang), jnp.sin(ang)
    reps = GROUP_W // DF_DH
    return (jnp.tile(jnp.concatenate([cos, cos], axis=-1), (1, reps)),
            jnp.tile(jnp.concatenate([-sin, sin], axis=-1), (1, reps)))


def _diff_prep(z, gq, gk, rope):
    b, l, _ = z.shape
    tm = _tile(l, 512)
    if rope:
        cos, sin = _rope_tables(l)
    else:
        cos = sin = jnp.zeros((l, GROUP_W), F32)
    return pl.pallas_call(
        functools.partial(_diff_prep_kernel, rope=rope),
        grid=(b, l // tm),
        in_specs=[pl.BlockSpec((1, tm, 3 * GROUP_W), lambda i, j: (i, j, COL_DF)),
                  pl.BlockSpec((1, GROUP_W), lambda i, j: (0, 0)),
                  pl.BlockSpec((1, GROUP_W), lambda i, j: (0, 0)),
                  pl.BlockSpec((tm, GROUP_W), lambda i, j: (j, 0)),
                  pl.BlockSpec((tm, GROUP_W), lambda i, j: (j, 0))],
        out_specs=[pl.BlockSpec((1, N_HEADS, 2, tm, HEAD_DIM), lambda i, j: (i, 0, 0, j, 0)),
                   pl.BlockSpec((1, N_HEADS, HEAD_DIM, tm), lambda i, j: (i, 0, 0, j)),
                   pl.BlockSpec((1, N_HEADS, tm, 2 * HEAD_DIM), lambda i, j: (i, 0, j, 0)),
                   pl.BlockSpec((1, tm, GROUP_W), lambda i, j: (i, j, 0))],
        out_shape=[jax.ShapeDtypeStruct((b, N_HEADS, 2, l, HEAD_DIM), BF16),
                   jax.ShapeDtypeStruct((b, N_HEADS, HEAD_DIM, l), BF16),
                   jax.ShapeDtypeStruct((b, N_HEADS, l, 2 * HEAD_DIM), BF16),
                   jax.ShapeDtypeStruct((b, l, GROUP_W), F32)],
        compiler_params=_cp("parallel", "parallel"),
        name="diff_prep",
    )(z, jnp.tile(gq, GROUP_W // DF_DH).reshape(1, GROUP_W), jnp.tile(gk, GROUP_W // DF_DH).reshape(1, GROUP_W),
      cos, sin)


def _cache_prep(ck, cv):
    b, p, _ = ck.shape
    tm = _tile(p, 512)
    return pl.pallas_call(
        _cache_prep_kernel,
        grid=(b, p // tm),
        in_specs=[pl.BlockSpec((1, tm, GROUP_W), lambda i, j: (i, j, 0)),
                  pl.BlockSpec((1, tm, GROUP_W), lambda i, j: (i, j, 0))],
        out_specs=[pl.BlockSpec((1, N_HEADS, HEAD_DIM, tm), lambda i, j: (i, 0, 0, j)),
                   pl.BlockSpec((1, N_HEADS, tm, 2 * HEAD_DIM), lambda i, j: (i, 0, j, 0))],
        out_shape=[jax.ShapeDtypeStruct((b, N_HEADS, HEAD_DIM, p), BF16),
                   jax.ShapeDtypeStruct((b, N_HEADS, p, 2 * HEAD_DIM), BF16)],
        compiler_params=_cp("parallel", "parallel"),
        name="cache_prep",
    )(ck, cv)


def _diff_attn_kernel(q_ref, kt_ref, v_ref, lam_ref, g_ref, o_ref, m_ref, acc_ref, *, lam_init):
    ik = pl.program_id(3)

    @pl.when(ik == 0)
    def _():
        m_ref[...] = jnp.full(m_ref.shape, NEG, F32)
        acc_ref[...] = jnp.zeros(acc_ref.shape, F32)

    chains = [(h, m) for h in range(q_ref.shape[1]) for m in range(2)]
    s = [jnp.dot(q_ref[0, h, m], kt_ref[0, h], preferred_element_type=F32) for h, m in chains]
    m_old = [m_ref[h, m] for h, m in chains]
    m_new = [jnp.maximum(x, jnp.max(y, axis=1, keepdims=True)) for x, y in zip(m_old, s)]
    p = [jnp.exp(x - y[:, 0:1]).astype(BF16) for x, y in zip(s, m_new)]
    pv = [jnp.dot(x, v_ref[0, h], preferred_element_type=F32) for x, (h, m) in zip(p, chains)]
    for i, (h, m) in enumerate(chains):
        acc_ref[h, m] = jnp.exp(m_old[i] - m_new[i]) * acc_ref[h, m] + pv[i]
        m_ref[h, m] = m_new[i]

    @pl.when(ik == pl.num_programs(3) - 1)
    def _():
        lp = lam_ref[...]
        lam = (jnp.exp(jnp.sum(lp[0:1] * lp[1:2], axis=1, keepdims=True))
               - jnp.exp(jnp.sum(lp[2:3] * lp[3:4], axis=1, keepdims=True)) + lam_init)
        for h in range(q_ref.shape[1]):
            a0, a1 = acc_ref[h, 0], acc_ref[h, 1]
            o = (a0[:, :HEAD_DIM] / a0[:, HEAD_DIM:HEAD_DIM + 1]
                 - lam * (a1[:, :HEAD_DIM] / a1[:, HEAD_DIM:HEAD_DIM + 1]))
            var = jnp.mean(o * o, axis=1, keepdims=True)
            o_ref[0, h] = o * lax.rsqrt(var + EPS) * g_ref[...] * (1.0 - lam_init)


ATTN_HEADS_PER_STEP = 2


def _diff_attn(q, kt, v, lam_par, gain, lam_init):
    b, _, _, l, _ = q.shape
    lk = kt.shape[-1]
    hs = ATTN_HEADS_PER_STEP
    tq = _tile(l, 512)
    tk = _tile(lk, 2304)
    return pl.pallas_call(
        functools.partial(_diff_attn_kernel, lam_init=lam_init),
        grid=(b, N_HEADS // hs, l // tq, lk // tk),
        in_specs=[pl.BlockSpec((1, hs, 2, tq, HEAD_DIM), lambda i, h, a, c: (i, h, 0, a, 0)),
                  pl.BlockSpec((1, hs, HEAD_DIM, tk), lambda i, h, a, c: (i, h, 0, c)),
                  pl.BlockSpec((1, hs, tk, 2 * HEAD_DIM), lambda i, h, a, c: (i, h, c, 0)),
                  pl.BlockSpec((4, DF_DH), lambda i, h, a, c: (0, 0)),
                  pl.BlockSpec((1, HEAD_DIM), lambda i, h, a, c: (0, 0))],
        out_specs=pl.BlockSpec((1, hs, tq, HEAD_DIM), lambda i, h, a, c: (i, h, a, 0)),
        out_shape=jax.ShapeDtypeStruct((b, N_HEADS, l, HEAD_DIM), F32),
        scratch_shapes=[pltpu.VMEM((hs, 2, tq, LANES), F32), pltpu.VMEM((hs, 2, tq, 2 * HEAD_DIM), F32)],
        compiler_params=_cp("parallel", "parallel", "parallel", "arbitrary"),
        name="diff_attn",
    )(q, kt, v, lam_par, gain.reshape(1, HEAD_DIM))


def _hy_filter_kernel(feat_ref, win_ref, w1_ref, b1_ref, w2_ref, b2_ref, w3_ref, fr_ref, o_ref):
    h = jnp.sin(fr_ref[0:1, :] * (_dot_hi(feat_ref[...], w1_ref[...]) + b1_ref[...]))
    h = jnp.sin(fr_ref[1:2, :] * (_dot_hi(h, w2_ref[...]) + b2_ref[...]))
    h = _dot_hi(h, w3_ref[...]) * win_ref[...]
    o_ref[...] = h * lax.rsqrt(jnp.sum(h * h, axis=0, keepdims=True) + EPS)


def _hy_constants(l):
    pos = jnp.arange(l, dtype=F32)
    bands = jnp.arange(1, HY_BANDS + 1, dtype=F32)
    ang = (2.0 * math.pi / l) * pos[:, None] * bands[None, :]
    feat = jnp.concatenate([pos[:, None] / l, jnp.cos(ang), jnp.sin(ang)], axis=-1)
    feat = jnp.pad(feat, ((0, 0), (0, LANES - feat.shape[1])))
    rates = jnp.linspace(-math.log(HY_TARGET) / HY_FAST_DECAY, -math.log(HY_TARGET) / HY_SLOW_DECAY,
                         GROUP_W, dtype=F32)
    rates = jnp.tile(rates, 2)
    offset = jnp.abs(pos - l // 2) / l
    return feat, jnp.exp(-offset[:, None] * rates[None, :]) + HY_SHIFT


def _hy_filters(feat, win, w1, b1, w2, b2, w3, freq):
    l = feat.shape[0]
    hid = w2.shape[0]
    w1p = jnp.pad(w1, ((0, LANES - w1.shape[0]), (0, 0)))
    full = lambda shape: pl.BlockSpec(shape, lambda o: (0, 0))
    return pl.pallas_call(
        _hy_filter_kernel,
        grid=(2,),
        in_specs=[full((l, LANES)), pl.BlockSpec((l, GROUP_W), lambda o: (0, o)),
                  full((LANES, hid)), full((1, hid)), full((hid, hid)), full((1, hid)),
                  pl.BlockSpec((hid, GROUP_W), lambda o: (0, o)), full((2, hid))],
        out_specs=pl.BlockSpec((l, GROUP_W), lambda o: (0, o)),
        out_shape=jax.ShapeDtypeStruct((l, 2 * GROUP_W), F32),
        compiler_params=_cp("parallel"),
        name="hy_filters",
    )(feat, win, w1p, b1.reshape(1, hid), w2, b2.reshape(1, hid), w3, freq)


def _dft_matrices(l):
    n = 2 * l
    k = jnp.arange(l, dtype=jnp.int32)
    alt = jnp.where(k % 2 == 0, 1.0, -1.0).astype(F32)
    cos_f, sin_f = _cos_sin_grid(0, l, n)
    fwd_im = jnp.where(k[:, None] == 0, alt[None, :], -sin_f)
    fwd = jnp.stack([cos_f, fwd_im]).astype(BF16)
    t = k + l // 2
    cos_i, sin_i = _cos_sin_grid(l // 2, l, n)
    alt_t = jnp.where(t % 2 == 0, 1.0, -1.0).astype(F32)
    inv_re = jnp.where(k[None, :] == 0, 1.0 / n, (2.0 / n) * cos_i)
    inv_im = jnp.where(k[None, :] == 0, alt_t[:, None] / n, -(2.0 / n) * sin_i)
    inv = jnp.concatenate([inv_re, inv_im], axis=1).astype(BF16)
    return fwd, inv


def _cos_sin_grid(row0, l, n):
    s = 1 << (int(math.log2(l)) // 2)
    assert row0 % s == 0 and l % s == 0
    c = jnp.arange(l, dtype=jnp.int32)
    hi = row0 + jnp.arange(l // s, dtype=jnp.int32) * s
    lo = jnp.arange(s, dtype=jnp.int32)

    def angle(r):
        return (2.0 * math.pi / n) * ((r[:, None] * c[None, :]) % n).astype(F32)

    a, b = angle(hi), angle(lo)
    ca, sa = jnp.cos(a)[:, None, :], jnp.sin(a)[:, None, :]
    cb, sb = jnp.cos(b)[None, :, :], jnp.sin(b)[None, :, :]
    return (ca * cb - sa * sb).reshape(l, l), (sa * cb + ca * sb).reshape(l, l)


def _dft_filter_kernel(f_ref, h_ref, o_ref):
    h = h_ref[...].astype(BF16)
    o_ref[0] = jnp.dot(f_ref[0], h, preferred_element_type=F32)
    o_ref[1] = jnp.dot(f_ref[1], h, preferred_element_type=F32)


def _dft_filter(fwd, filt):
    l, c = filt.shape
    tm = _tile(l, 256)
    return pl.pallas_call(
        _dft_filter_kernel,
        grid=(l // tm,),
        in_specs=[pl.BlockSpec((2, tm, l), lambda i: (0, i, 0)), pl.BlockSpec((l, c), lambda i: (0, 0))],
        out_specs=pl.BlockSpec((2, tm, c), lambda i: (0, i, 0)),
        out_shape=jax.ShapeDtypeStruct((2, l, c), F32),
        compiler_params=_cp("parallel"),
        name="dft_filter",
    )(fwd, filt)


def _dft_fwd_mul_kernel(f_ref, u_ref, h_ref, o_ref):
    u = u_ref[0]
    ur = jnp.dot(f_ref[0], u, preferred_element_type=F32)
    ui = jnp.dot(f_ref[1], u, preferred_element_type=F32)
    hr, hi = h_ref[0], h_ref[1]
    row = lax.broadcasted_iota(jnp.int32, ur.shape, 0) + pl.program_id(0) * ur.shape[0]
    packed = row == 0
    o_ref[0, 0] = (ur * hr - jnp.where(packed, 0.0, ui * hi)).astype(BF16)
    o_ref[0, 1] = jnp.where(packed, ui * hi, ur * hi + ui * hr).astype(BF16)


def _dft_fwd_mul(fwd, u, hf, order):
    b, l, _ = u.shape
    tm = _tile(l, 512)
    return pl.pallas_call(
        _dft_fwd_mul_kernel,
        grid=(l // tm, b),
        in_specs=[pl.BlockSpec((2, tm, l), lambda i, j: (0, i, 0)),
                  pl.BlockSpec((1, l, GROUP_W), lambda i, j: (j, 0, 0)),
                  pl.BlockSpec((2, tm, GROUP_W), lambda i, j: (0, i, order))],
        out_specs=pl.BlockSpec((1, 2, tm, GROUP_W), lambda i, j: (j, 0, i, 0)),
        out_shape=jax.ShapeDtypeStruct((b, 2, l, GROUP_W), BF16),
        compiler_params=_cp("parallel", "parallel"),
        name="dft_fwd_mul",
    )(fwd, u, hf)


def _dft_inv_gate_kernel(g_ref, y_ref, x_ref, u_ref, skip_ref, o_ref, lowp_ref):
    conv = jnp.dot(g_ref[...], y_ref[0], preferred_element_type=F32)
    out = x_ref[0] * (conv + skip_ref[...] * u_ref[0])
    o_ref[0] = out
    lowp_ref[0] = out.astype(BF16)


def _dft_inv_gate(inv, y, xg, x_col, u, u_col, skip):
    b, _, l, _ = y.shape
    tm = _tile(l, 512)
    out_spec = pl.BlockSpec((1, tm, GROUP_W), lambda i, j: (j, i, 0))
    return pl.pallas_call(
        _dft_inv_gate_kernel,
        grid=(l // tm, b),
        in_specs=[pl.BlockSpec((tm, 2 * l), lambda i, j: (i, 0)),
                  pl.BlockSpec((1, 2 * l, GROUP_W), lambda i, j: (j, 0, 0)),
                  pl.BlockSpec((1, tm, GROUP_W), lambda i, j: (j, i, x_col)),
                  pl.BlockSpec((1, tm, GROUP_W), lambda i, j: (j, i, u_col)),
                  pl.BlockSpec((1, GROUP_W), lambda i, j: (0, 0))],
        out_specs=[out_spec, out_spec],
        out_shape=[jax.ShapeDtypeStruct((b, l, GROUP_W), F32), jax.ShapeDtypeStruct((b, l, GROUP_W), BF16)],
        compiler_params=_cp("parallel", "parallel"),
        name="dft_inv_gate",
    )(inv, y.reshape(b, 2 * l, GROUP_W), xg, u, skip.reshape(1, GROUP_W))


def _head_norm(x, gain):
    return x * lax.rsqrt(_seg_mean(x * x, HEAD_DIM) + EPS) * gain


def _proj_out_kernel(dnf_ref, dnb_ref, gate_ref, df_ref, hy_ref, mlf_ref, mlb_ref, mo_ref, x_ref, mod_ref,
                     gdn_ref, gml_ref, w_ref, o_ref):
    gate = gate_ref[0]
    y_dn = _head_norm(dnf_ref[0] + dnb_ref[0], gdn_ref[...]) * (gate * _sigmoid(gate))
    y_ml = _head_norm(mlf_ref[0] + mlb_ref[0], gml_ref[...]) * _sigmoid(mo_ref[0])
    y = _dot(y_dn, w_ref[0:GROUP_W, :])
    for h in range(N_HEADS):
        lo = GROUP_W + h * HEAD_DIM
        y = y + _dot(df_ref[0, h], w_ref[lo:lo + HEAD_DIM, :])
    y = y + _dot(hy_ref[0], w_ref[2 * GROUP_W:3 * GROUP_W, :])
    y = y + _dot(y_ml, w_ref[3 * GROUP_W:, :])
    o_ref[0] = x_ref[0] + mod_ref[0, 2:3, :] * y


def _proj_out(o_dn, z, o_df, y_hy, h_ml, x, mod, g_dn, g_ml, w):
    b, l, d = x.shape
    tm = _tile(l, 512)
    per_batch = mod.shape[0] > 1
    group = lambda col: pl.BlockSpec((1, tm, GROUP_W), lambda i, j: (i, j, col))
    return pl.pallas_call(
        _proj_out_kernel,
        grid=(b, l // tm),
        in_specs=[group(0), group(0), group(COL_DGATE),
                  pl.BlockSpec((1, N_HEADS, tm, HEAD_DIM), lambda i, j: (i, 0, j, 0)),
                  group(0), group(0), group(0), group(COL_MO),
                  pl.BlockSpec((1, tm, d), lambda i, j: (i, j, 0)),
                  pl.BlockSpec((1, 6, d), lambda i, j: (i if per_batch else 0, 0, 0)),
                  pl.BlockSpec((1, GROUP_W), lambda i, j: (0, 0)),
                  pl.BlockSpec((1, GROUP_W), lambda i, j: (0, 0)),
                  pl.BlockSpec((d, d), lambda i, j: (0, 0))],
        out_specs=pl.BlockSpec((1, tm, d), lambda i, j: (i, j, 0)),
        out_shape=jax.ShapeDtypeStruct((b, l, d), F32),
        compiler_params=_cp("parallel", "parallel"),
        name="proj_out",
    )(o_dn[0], o_dn[1], z, o_df, y_hy, h_ml[0], h_ml[1], z, x, mod, jnp.tile(g_dn, N_HEADS).reshape(1, GROUP_W),
      jnp.tile(g_ml, N_HEADS).reshape(1, GROUP_W), w)


def _ffn_kernel(x_ref, mod_ref, g_ref, wg_ref, wu_ref, wd_ref, o_ref, h_ref, acc_ref):
    j = pl.program_id(2)

    @pl.when(j == 0)
    def _():
        h_ref[...] = _modulated_norm(x_ref[0], g_ref[...], mod_ref[0, 3:4, :], mod_ref[0, 4:5, :]).astype(BF16)
        acc_ref[...] = jnp.zeros(acc_ref.shape, F32)

    h = h_ref[...]
    g = jnp.dot(h, wg_ref[...], preferred_element_type=F32)
    u = jnp.dot(h, wu_ref[...], preferred_element_type=F32)
    acc_ref[...] += jnp.dot((g * _sigmoid(g) * u).astype(BF16), wd_ref[...], preferred_element_type=F32)

    @pl.when(j == pl.num_programs(2) - 1)
    def _():
        o_ref[0] = x_ref[0] + mod_ref[0, 5:6, :] * acc_ref[...]


def _ffn(x, mod, gain, wg, wu, wd):
    b, l, d = x.shape
    f = wg.shape[1]
    tm = _tile(l, 1024)
    tf = 256
    per_batch = mod.shape[0] > 1
    return pl.pallas_call(
        _ffn_kernel,
        grid=(b, l // tm, f // tf),
        in_specs=[pl.BlockSpec((1, tm, d), lambda i, a, j: (i, a, 0)),
                  pl.BlockSpec((1, 6, d), lambda i, a, j: (i if per_batch else 0, 0, 0)),
                  pl.BlockSpec((1, d), lambda i, a, j: (0, 0)),
                  pl.BlockSpec((d, tf), lambda i, a, j: (0, j)),
                  pl.BlockSpec((d, tf), lambda i, a, j: (0, j)),
                  pl.BlockSpec((tf, d), lambda i, a, j: (j, 0))],
        out_specs=pl.BlockSpec((1, tm, d), lambda i, a, j: (i, a, 0)),
        out_shape=jax.ShapeDtypeStruct((b, l, d), F32),
        scratch_shapes=[pltpu.VMEM((tm, d), BF16), pltpu.VMEM((tm, d), F32)],
        compiler_params=_cp("parallel", "parallel", "arbitrary"),
        name="ffn_dense",
    )(x, mod, gain.reshape(1, d), wg, wu, wd)


def _top2_combine(logits):
    lane = lax.broadcasted_iota(jnp.int32, logits.shape, 1)
    v1 = jnp.max(logits, axis=1, keepdims=True)
    i1 = jnp.min(jnp.where(logits == v1, lane, LANES), axis=1, keepdims=True)
    rest = jnp.where(lane == i1, NEG, logits)
    v2 = jnp.max(rest, axis=1, keepdims=True)
    i2 = jnp.min(jnp.where(rest == v2, lane, LANES), axis=1, keepdims=True)
    e2 = jnp.exp(v2 - v1)
    return jnp.where(lane == i1, 1.0 / (1.0 + e2), 0.0) + jnp.where(lane == i2, e2 / (1.0 + e2), 0.0)


MOE_ROW_TILE = 512


def _moe_route_kernel(x_ref, mod_ref, g_ref, r_ref, h_ref, comb_ref):
    hn = _modulated_norm(x_ref[0], g_ref[...], mod_ref[0, 3:4, :], mod_ref[0, 4:5, :])
    h_ref[0] = hn.astype(BF16)
    lane = lax.broadcasted_iota(jnp.int32, (hn.shape[0], LANES), 1)
    comb_ref[0] = _top2_combine(jnp.where(lane < N_EXPERTS, _dot_hi(hn, r_ref[...]), NEG))


def _moe_group_kernel(tile_expert_ref, used_ref, xs_ref, gate_ref, wg_ref, wu_ref, wd_ref, o_ref):
    del tile_expert_ref

    @pl.when(pl.program_id(0) < used_ref[0])
    def _():
        xs = xs_ref[...]
        g = jnp.dot(xs, wg_ref[0], preferred_element_type=F32)
        u = jnp.dot(xs, wu_ref[0], preferred_element_type=F32)
        a = (g * _sigmoid(g) * u * gate_ref[...]).astype(BF16)
        o_ref[...] = jnp.dot(a, wd_ref[0], preferred_element_type=F32).astype(BF16)

    @pl.when(pl.program_id(0) >= used_ref[0])
    def _():
        o_ref[...] = jnp.zeros(o_ref.shape, BF16)


def _moe_combine_kernel(x_ref, y_ref, mod_ref, o_ref):
    d = x_ref.shape[-1]
    y = y_ref[0]
    o_ref[0] = x_ref[0] + mod_ref[0, 5:6, :] * (y[:, :d].astype(F32) + y[:, d:].astype(F32))


def _as_words(x):
    return lax.bitcast_convert_type(x.reshape(x.shape[0], x.shape[1] // 2, 2), jnp.uint32)


def _as_bf16(x):
    return lax.bitcast_convert_type(x, BF16).reshape(x.shape[0], 2 * x.shape[1])


MOE_SLOT_ALIGN = 2048


def _moe(x, mod, gain, router, wg, wu, wd):
    b, l, d = x.shape
    ne, _, f = wg.shape
    t = b * l
    tm = _tile(l, 1024)
    tg = MOE_ROW_TILE
    per_batch = mod.shape[0] > 1
    router_p = jnp.pad(router, ((0, 0), (0, LANES - ne)))
    x_spec = pl.BlockSpec((1, tm, d), lambda i, j: (i, j, 0))
    mod_spec = pl.BlockSpec((1, 6, d), lambda i, j: (i if per_batch else 0, 0, 0))
    h, comb = pl.pallas_call(
        _moe_route_kernel,
        grid=(b, l // tm),
        in_specs=[x_spec, mod_spec, pl.BlockSpec((1, d), lambda i, j: (0, 0)),
                  pl.BlockSpec((d, LANES), lambda i, j: (0, 0))],
        out_specs=[x_spec, pl.BlockSpec((1, tm, LANES), lambda i, j: (i, j, 0))],
        out_shape=[jax.ShapeDtypeStruct((b, l, d), BF16), jax.ShapeDtypeStruct((b, l, LANES), F32)],
        compiler_params=_cp("parallel", "parallel"),
        name="moe_route",
    )(x, mod, gain.reshape(1, d), router_p)

    n_slots = -(-(2 * t + (ne + 1) * tg) // MOE_SLOT_ALIGN) * MOE_SLOT_ALIGN
    comb2 = comb.reshape(t, LANES)[:, :ne]
    routed = comb2 > 0.0
    grp = (jnp.sum(routed, axis=0, dtype=jnp.int32) + tg - 1) // tg * tg
    end = jnp.cumsum(grp)
    pos = (end - grp)[None, :] + jnp.cumsum(routed.astype(jnp.int32), axis=0) - 1
    target = jnp.where(routed, pos, n_slots).reshape(-1)
    token = jnp.broadcast_to(jnp.arange(t, dtype=jnp.int32)[:, None], (t, ne)).reshape(-1)
    slot_token = jnp.zeros((n_slots,), jnp.int32).at[target].set(token, mode="drop")
    slot_gate = jnp.zeros((n_slots,), F32).at[target].set(comb2.reshape(-1), mode="drop")
    tile_start = jnp.arange(n_slots // tg, dtype=jnp.int32) * tg
    tile_expert = jnp.minimum(jnp.sum(tile_start[:, None] >= end[None, :], axis=1), ne - 1).astype(jnp.int32)
    used_tiles = (end[-1:] // tg).astype(jnp.int32)
    first = jnp.argmax(comb2, axis=1)
    second = jnp.argmax(jnp.where(jnp.arange(ne)[None, :] == first[:, None], -1.0, comb2), axis=1)
    pick = lambda a, e: jnp.take_along_axis(a, e[:, None], axis=1)[:, 0]
    pair = jnp.stack([pick(pos, first), jnp.where(pick(comb2, second) > 0.0, pick(pos, second), n_slots - 1)],
                     axis=1).reshape(-1).astype(jnp.int32)

    xs = _as_bf16(_sc_gather_rows(_as_words(h.reshape(t, d)), slot_token))
    row_spec = pl.BlockSpec((tg, d), lambda i, te, used: (i, 0))
    ys = pl.pallas_call(
        _moe_group_kernel,
        grid_spec=pltpu.PrefetchScalarGridSpec(
            num_scalar_prefetch=2,
            grid=(n_slots // tg,),
            in_specs=[row_spec, pl.BlockSpec((tg, 1), lambda i, te, used: (i, 0)),
                      pl.BlockSpec((1, d, f), lambda i, te, used: (te[i], 0, 0)),
                      pl.BlockSpec((1, d, f), lambda i, te, used: (te[i], 0, 0)),
                      pl.BlockSpec((1, f, d), lambda i, te, used: (te[i], 0, 0))],
            out_specs=row_spec),
        out_shape=jax.ShapeDtypeStruct((n_slots, d), BF16),
        compiler_params=_cp("arbitrary"),
        name="moe_group",
    )(tile_expert, used_tiles, xs, slot_gate.reshape(n_slots, 1), wg, wu, wd)
    y2 = _as_bf16(_sc_gather_rows(_as_words(ys), pair)).reshape(b, l, 2 * d)
    return pl.pallas_call(
        _moe_combine_kernel,
        grid=(b, l // tm),
        in_specs=[x_spec, pl.BlockSpec((1, tm, 2 * d), lambda i, j: (i, j, 0)), mod_spec],
        out_specs=x_spec,
        out_shape=jax.ShapeDtypeStruct((b, l, d), F32),
        compiler_params=_cp("parallel", "parallel"),
        name="moe_combine",
    )(x, y2, mod)


SC_GATHER_WINDOW = 128
SC_GATHER_WORDS = 256


def _sc_gather_rows(table, idx):
    n, w = table.shape
    pieces = w // SC_GATHER_WORDS
    piece_idx = (idx[:, None] * pieces + jnp.arange(pieces, dtype=jnp.int32)[None, :]).reshape(-1)
    out = _sc_gather_pieces(table.reshape(n * pieces, SC_GATHER_WORDS), piece_idx)
    return out.reshape(idx.shape[0], w)


def _sc_gather_pieces(table, idx):
    m = idx.shape[0]
    w = table.shape[1]
    mesh = plsc.VectorSubcoreMesh(core_axis_name="core", subcore_axis_name="subcore")

    @functools.partial(pl.kernel, out_type=jax.ShapeDtypeStruct((m, w), table.dtype), mesh=mesh,
                       name="sc_gather_rows")
    def gather(t_hbm, i_hbm, o_hbm):
        def body(i_vmem, o_vmem):
            pltpu.sync_copy(t_hbm.at[i_vmem.at[0]], o_vmem)

        pltpu.emit_pipeline(
            body,
            grid=(m // SC_GATHER_WINDOW,),
            in_specs=[pl.BlockSpec((1, SC_GATHER_WINDOW), lambda i: (0, i))],
            out_specs=[pl.BlockSpec((SC_GATHER_WINDOW, w), lambda i: (i, 0))],
            core_axis_name=("core", "subcore"),
            dimension_semantics=(pltpu.PARALLEL,),
        )(i_hbm, o_hbm)

    return gather(table, idx.reshape(1, m))


def _lane_row(values, lane0):
    row = jnp.zeros((LANES,), F32)
    return lax.dynamic_update_slice(row, values.reshape(-1).astype(F32), (lane0,))


def _reorder_w_in(w_in):
    g = GROUP_W
    o = [0, g, 2 * g, 3 * g, 4 * g, 4 * g + 8, 4 * g + 16]
    dq_dk_dv = w_in[..., o[0]:o[3]]
    dgate = w_in[..., o[3]:o[4]]
    dbeta_da = w_in[..., o[4]:o[6]]
    base = o[6]
    df = w_in[..., base:base + 3 * g]
    hy = w_in[..., base + 3 * g:base + 6 * g]
    ml = w_in[..., base + 6 * g:base + 9 * g]
    mo = w_in[..., base + 9 * g:base + 10 * g]
    mi_mf = w_in[..., base + 10 * g:base + 10 * g + 16]
    pad = jnp.zeros(w_in.shape[:-1] + (LANES - 32,), w_in.dtype)
    return jnp.concatenate([dq_dk_dv, hy, ml, df, dgate, mo, dbeta_da, mi_mf, pad], axis=-1).astype(BF16)


def _layer(l, x, mod, cache, p, hy_consts, dft):
    b, seq, _ = x.shape
    latent = cache is not None
    z = _proj_in(x, mod, p["norm1_g"][l], p["w_in"][l])

    qkv, = _dwconv(z, p["dn_conv"][l], COL_DN, "deltanet")
    dn_par = jnp.zeros((8, LANES), F32).at[0].set(_lane_row(p["dn_a_log"][l], LANE_A)).at[1].set(
        _lane_row(p["dn_dt_bias"][l], LANE_A))
    s0 = cache[2] if latent else jnp.zeros((b, 2, N_HEADS, HEAD_DIM, HEAD_DIM), F32)
    o_dn_f, o_dn_b, s_dn = _deltanet(qkv, z, dn_par, s0)
    o_dn = (o_dn_f, o_dn_b)

    ml_par = jnp.zeros((8, LANES), F32).at[0].set(_lane_row(p["ml_i_bias"][l], LANE_I)).at[1].set(
        _lane_row(p["ml_f_bias"][l], LANE_F))
    if latent:
        c0 = cache[3]
        n0 = cache[4].reshape(b, 2, 1, GROUP_W)
        m0 = jnp.repeat(cache[5], HEAD_DIM, axis=-1).reshape(b, 2, 1, GROUP_W)
    else:
        c0 = jnp.zeros((b, 2, N_HEADS, HEAD_DIM, HEAD_DIM), F32)
        n0 = m0 = jnp.zeros((b, 2, 1, GROUP_W), F32)
    h_ml_f, h_ml_b, c_ml, n_ml, m_ml = _mlstm(z, ml_par, c0, n0, m0)
    h_ml = (h_ml_f, h_ml_b)
    n_ml = n_ml.reshape(b, 2, N_HEADS, HEAD_DIM)
    m_ml = m_ml.reshape(b, 2, N_HEADS, HEAD_DIM)[..., 0]

    q, kt, v, kd = _diff_prep(z, p["df_q_norm"][l], p["df_k_norm"][l], latent)
    if latent:
        ckt, cv = _cache_prep(cache[0].reshape(b, -1, GROUP_W), cache[1].reshape(b, -1, GROUP_W))
        kt = jnp.concatenate([kt, ckt], axis=-1)
        v = jnp.concatenate([v, cv], axis=2)
    lam_init = 0.8 - 0.6 * math.exp(-0.3 * l)
    o_df = _diff_attn(q, kt, v, p["df_lambda"][l], p["df_norm"][l], lam_init)

    feat, win = hy_consts
    fwd, inv = dft
    zc, zv_lowp = _dwconv(z, p["hy_conv"][l], COL_HY, "plain")
    filt = _hy_filters(feat, win, p["hy_w1"][l], p["hy_b1"][l], p["hy_w2"][l], p["hy_b2"][l], p["hy_w3"][l],
                       p["hy_freq"][l])
    hf = _dft_filter(fwd, filt)
    y1 = _dft_fwd_mul(fwd, zv_lowp, hf, 0)
    z1, z1_lowp = _dft_inv_gate(inv, y1, zc, 0, zc, 2, p["hy_skip"][l, 0])
    y2 = _dft_fwd_mul(fwd, z1_lowp, hf, 1)
    y_hy, _ = _dft_inv_gate(inv, y2, zc, 1, z1, 0, p["hy_skip"][l, 1])

    x = _proj_out(o_dn, z, o_df, y_hy, h_ml, x, mod, p["dn_norm"][l], p["ml_norm"][l], p["w_out"][l])
    j = l // 2
    if mod.shape[0] == 1:
        x = x.reshape(1, b * seq, -1)
    if l % 2 == 0:
        x = _ffn(x, mod, p["norm2_g"][l], p["ffn_w_gate"][j], p["ffn_w_up"][j], p["ffn_w_down"][j])
    else:
        x = _moe(x, mod, p["norm2_g"][l], p["moe_router"][j], p["moe_w_gate"][j], p["moe_w_up"][j],
                 p["moe_w_down"][j])
    x = x.reshape(b, seq, -1)
    fv =z[:, :, 3 * 3 * GROUP_W + 2 * GROUP_W:3 * 3 * GROUP_W + 3 * GROUP_W]
    return x, (kd, fv, s_dn, c_ml, n_ml, m_ml)


def kernel(x_prompt, x_sample, cache_diff_k, cache_diff_v, state_delta, state_mlstm_c, state_mlstm_n, state_mlstm_m, c, c_ctx, norm1_g, norm2_g, w_mod, b_mod, w_in, w_out, dn_conv, dn_a_log, dn_dt_bias, dn_norm, df_q_norm, df_k_norm, df_lambda, df_norm, hy_conv, hy_w1, hy_b1, hy_w2, hy_b2, hy_w3, hy_freq, hy_skip, ml_i_bias, ml_f_bias, ml_norm, ffn_w_gate, ffn_w_up, ffn_w_down, moe_router, moe_w_gate, moe_w_up, moe_w_down):
    depth = w_in.shape[0]
    d_model = x_prompt.shape[-1]
    batch, seq, _ = x_prompt.shape
    dec_batch, dec_seq, _ = x_sample.shape
    p = dict(norm1_g=norm1_g, norm2_g=norm2_g, w_in=_reorder_w_in(w_in), w_out=w_out.astype(BF16),
             dn_conv=dn_conv, dn_a_log=dn_a_log, dn_dt_bias=dn_dt_bias, dn_norm=dn_norm,
             df_q_norm=df_q_norm, df_k_norm=df_k_norm, df_lambda=df_lambda, df_norm=df_norm,
             hy_conv=hy_conv, hy_w1=hy_w1, hy_b1=hy_b1, hy_w2=hy_w2, hy_b2=hy_b2, hy_w3=hy_w3,
             hy_freq=hy_freq, hy_skip=hy_skip, ml_i_bias=ml_i_bias, ml_f_bias=ml_f_bias, ml_norm=ml_norm,
             ffn_w_gate=ffn_w_gate.astype(BF16), ffn_w_up=ffn_w_up.astype(BF16),
             ffn_w_down=ffn_w_down.astype(BF16), moe_router=moe_router,
             moe_w_gate=moe_w_gate.astype(BF16), moe_w_up=moe_w_up.astype(BF16),
             moe_w_down=moe_w_down.astype(BF16))

    n_cond = 1 + dec_batch
    rows = -(-n_cond // 8) * 8
    cond = jnp.concatenate([c_ctx[None, :], c, jnp.zeros((rows - n_cond, d_model), F32)], axis=0)
    mod = _modulation(cond, w_mod, b_mod).reshape(depth, rows, 6, d_model)

    hy_ctx, dft_ctx = _hy_constants(seq), _dft_matrices(seq)
    x = x_prompt
    ctx = []
    for l in range(depth):
        x, out = _layer(l, x, mod[l, 0:1], None, p, hy_ctx, dft_ctx)
        ctx.append(out)
    y_prompt = x
    new_k, new_v, new_s, new_c, new_n, new_m = (jnp.stack([o[i] for o in ctx], axis=1) for i in range(6))
    new_k = new_k.reshape(batch, depth, seq, N_HEADS, 2, DF_DH)
    new_v = new_v.reshape(batch, depth, seq, N_HEADS, HEAD_DIM)

    hy_lat, dft_lat = _hy_constants(dec_seq), _dft_matrices(dec_seq)
    x = x_sample
    for l in range(depth):
        cache = (cache_diff_k[:, l], cache_diff_v[:, l], state_delta[:, l], state_mlstm_c[:, l],
                 state_mlstm_n[:, l], state_mlstm_m[:, l])
        x, _ = _layer(l, x, mod[l, 1:1 + dec_batch], cache, p, hy_lat, dft_lat)
    return (y_prompt, x, new_k, new_v, new_s, new_c, new_n, new_m)
```

```python
import functools
import math
from typing import Any, NamedTuple

import jax
import jax.numpy as jnp
from jax import lax
from jax.experimental import pallas as pl
from jax.experimental.pallas import tpu as pltpu
from jax.experimental.pallas import tpu_sc as plsc

F32 = jnp.float32
BF16 = jnp.bfloat16

N_HEADS = 4
HEAD_DIM = 64
GROUP_W = N_HEADS * HEAD_DIM
DF_DH = 32
CHUNK = 64
GRID_W = 64
ROPE_THETA = 10000.0
HY_BANDS = 8
HY_FAST_DECAY = 0.3
HY_SLOW_DECAY = 1.5
HY_TARGET = 1e-2
HY_SHIFT = 0.05
N_EXPERTS = 8
EPS = 1e-6
NEG = -1e30
LANES = 128
VMEM_LIMIT = 56 * 1024 * 1024

COL_DN, COL_HY, COL_ML, COL_DF = 0, 1, 2, 3
COL_DGATE, COL_MO = 12, 13
COL_SMALL = 28
Z_WIDTH = 29 * LANES
LANE_BETA, LANE_A, LANE_I, LANE_F = 0, 8, 16, 24


def _tile(n, pref):
    t = min(n, pref)
    while n % t:
        t -= LANES if t > LANES else 8
    return t


def _cp(*sem):
    return pltpu.CompilerParams(dimension_semantics=sem, vmem_limit_bytes=VMEM_LIMIT)


def _split3(x):
    x1 = x.astype(BF16)
    r = x - x1.astype(F32)
    x2 = r.astype(BF16)
    r = r - x2.astype(F32)
    return x1, x2, r.astype(BF16)


def _dot(a, b):
    return jnp.dot(a.astype(BF16), b.astype(BF16), preferred_element_type=F32)


def _dot_nt(a, b):
    return lax.dot_general(a.astype(BF16), b.astype(BF16), (((1,), (1,)), ((), ())),
                           preferred_element_type=F32)


def _dot_tn(a, b):
    return lax.dot_general(a.astype(BF16), b.astype(BF16), (((0,), (0,)), ((), ())),
                           preferred_element_type=F32)


def _dot_exact_l(m, x):
    return sum(jnp.dot(m, p, preferred_element_type=F32) for p in _split3(x))


def _dot_exact_r(x, m):
    return sum(jnp.dot(p, m, preferred_element_type=F32) for p in _split3(x))


def _dot_hi(a, b):
    a1, a2, _ = _split3(a)
    b1, b2, _ = _split3(b)
    return (jnp.dot(a1, b1, preferred_element_type=F32) + jnp.dot(a1, b2, preferred_element_type=F32)
            + jnp.dot(a2, b1, preferred_element_type=F32))


def _seg_mean(x, seg):
    w = x.shape[-1]
    sh = int(math.log2(seg))
    r = lax.shift_right_logical(lax.broadcasted_iota(jnp.int32, (w, w), 0), sh)
    c = lax.shift_right_logical(lax.broadcasted_iota(jnp.int32, (w, w), 1), sh)
    bd = jnp.where(r == c, 1.0, 0.0).astype(BF16)
    return _dot_exact_r(x, bd) * (1.0 / seg)


def _row_bcast(col):
    n = col.shape[0]
    lane = lax.broadcasted_iota(jnp.int32, (n, LANES), 1)
    sel = jnp.where(lane == 0, 1.0, 0.0).astype(BF16)
    src = jnp.where(lane == 0, col, 0.0)
    return sum(lax.dot_general(sel, p, (((1,), (1,)), ((), ())), preferred_element_type=F32)
               for p in _split3(src))


def _lane_pick(x, lane_idx):
    lane = lax.broadcasted_iota(jnp.int32, x.shape, 1)
    return jnp.sum(jnp.where(lane == lane_idx, x, 0.0), axis=1, keepdims=True)


def _sigmoid(x):
    return 1.0 / (1.0 + jnp.exp(-x))


def _softplus(x):
    return jnp.maximum(x, 0.0) + jnp.log1p(jnp.exp(-jnp.abs(x)))


def _mod_kernel(c_ref, w_ref, b_ref, o_ref):
    c = c_ref[...]
    o_ref[0] = _dot(c * _sigmoid(c), w_ref[0]) + b_ref[0]


def _modulation(cond, w_mod, b_mod):
    depth, d, n = w_mod.shape
    r = cond.shape[0]
    tn = n // 4
    return pl.pallas_call(
        _mod_kernel,
        grid=(depth, n // tn),
        in_specs=[pl.BlockSpec((r, d), lambda l, j: (0, 0)),
                  pl.BlockSpec((1, d, tn), lambda l, j: (l, 0, j)),
                  pl.BlockSpec((1, 1, tn), lambda l, j: (l, 0, j))],
        out_specs=pl.BlockSpec((1, r, tn), lambda l, j: (l, 0, j)),
        out_shape=jax.ShapeDtypeStruct((depth, r, n), F32),
        compiler_params=_cp("parallel", "parallel"),
        name="modulation",
    )(cond, w_mod, b_mod.reshape(depth, 1, n))


def _modulated_norm(x, gain, shift, scale):
    var = jnp.mean(x * x, axis=-1, keepdims=True)
    return x * lax.rsqrt(var + EPS) * gain * (1.0 + scale) + shift


def _proj_in_kernel(x_ref, mod_ref, g_ref, w_ref, o_ref):
    h = _modulated_norm(x_ref[0], g_ref[...], mod_ref[0, 0:1, :], mod_ref[0, 1:2, :])
    o_ref[0] = jnp.dot(h.astype(BF16), w_ref[...], preferred_element_type=F32)


def _proj_in(x, mod, gain, w):
    b, l, d = x.shape
    tm = _tile(l, 256)
    per_batch = mod.shape[0] > 1
    return pl.pallas_call(
        _proj_in_kernel,
        grid=(b, l // tm),
        in_specs=[pl.BlockSpec((1, tm, d), lambda i, j: (i, j, 0)),
                  pl.BlockSpec((1, 6, d), lambda i, j: (i if per_batch else 0, 0, 0)),
                  pl.BlockSpec((1, d), lambda i, j: (0, 0)),
                  pl.BlockSpec((d, Z_WIDTH), lambda i, j: (0, 0))],
        out_specs=pl.BlockSpec((1, tm, Z_WIDTH), lambda i, j: (i, j, 0)),
        out_shape=jax.ShapeDtypeStruct((b, l, Z_WIDTH), F32),
        compiler_params=_cp("parallel", "parallel"),
        name="proj_in",
    )(x, mod, gain.reshape(1, d), w)


def _dwconv_kernel(z_ref, zp_ref, zn_ref, w_ref, o_ref, *lowp_refs, mode):
    i = pl.program_id(1)
    z = z_ref[0]
    tm = z.shape[0]
    prev_row = jnp.where(i > 0, zp_ref[0, 7:8, :], 0.0)
    next_row = jnp.where(i < pl.num_programs(1) - 1, zn_ref[0, 0:1, :], 0.0)
    rid = lax.broadcasted_iota(jnp.int32, z.shape, 0)
    zm1 = jnp.where(rid == 0, prev_row, pltpu.roll(z, 1, 0))
    zp1 = jnp.where(rid == tm - 1, next_row, pltpu.roll(z, tm - 1, 0))
    y = zm1 * w_ref[0:1, :] + z * w_ref[1:2, :] + zp1 * w_ref[2:3, :]
    if mode == "deltanet":
        y = y * _sigmoid(y)
        q, k, v = y[:, :GROUP_W], y[:, GROUP_W:2 * GROUP_W], y[:, 2 * GROUP_W:]
        q = q * lax.rsqrt(_seg_mean(q * q, HEAD_DIM) * HEAD_DIM + EPS) * (HEAD_DIM ** -0.5)
        k = k * lax.rsqrt(_seg_mean(k * k, HEAD_DIM) * HEAD_DIM + EPS)
        o_ref[0, :, 0:GROUP_W] = q
        o_ref[0, :, GROUP_W:2 * GROUP_W] = k
        o_ref[0, :, 2 * GROUP_W:] = v
    else:
        o_ref[0] = y
        lowp_refs[0][0] = y[:, 2 * GROUP_W:].astype(BF16)


def _dwconv(z, w, col_block, mode):
    b, l, _ = z.shape
    c = 3 * GROUP_W
    tm = _tile(l, 512)
    hb = tm // 8
    last = l // 8 - 1
    out_specs = [pl.BlockSpec((1, tm, c), lambda i, j: (i, j, 0))]
    out_shape = [jax.ShapeDtypeStruct((b, l, c), F32)]
    if mode == "plain":
        out_specs.append(pl.BlockSpec((1, tm, GROUP_W), lambda i, j: (i, j, 0)))
        out_shape.append(jax.ShapeDtypeStruct((b, l, GROUP_W), BF16))
    return pl.pallas_call(
        functools.partial(_dwconv_kernel, mode=mode),
        grid=(b, l // tm),
        in_specs=[pl.BlockSpec((1, tm, c), lambda i, j: (i, j, col_block)),
                  pl.BlockSpec((1, 8, c), lambda i, j: (i, jnp.maximum(j * hb - 1, 0), col_block)),
                  pl.BlockSpec((1, 8, c), lambda i, j: (i, jnp.minimum((j + 1) * hb, last), col_block)),
                  pl.BlockSpec((3, c), lambda i, j: (0, 0))],
        out_specs=out_specs,
        out_shape=out_shape,
        compiler_params=_cp("parallel", "parallel"),
        name="dwconv_" + mode,
    )(z, z, z, w)


STACK = N_HEADS * CHUNK


def _stack_heads(x):
    return jnp.concatenate([x] * N_HEADS, axis=0)


def _block_diag(x, head_eq):
    return jnp.where(head_eq, _stack_heads(x), 0.0)


def _fold_heads(x):
    return x[0:CHUNK] + x[CHUNK:2 * CHUNK] + x[2 * CHUNK:3 * CHUNK] + x[3 * CHUNK:]


def _chunk_masks(d):
    row = lax.broadcasted_iota(jnp.int32, (STACK, STACK), 0)
    col = lax.broadcasted_iota(jnp.int32, (STACK, STACK), 1)
    head_eq = lax.shift_right_logical(row, 6) == lax.shift_right_logical(col, 6)
    rel = ((row & (CHUNK - 1)) - (col & (CHUNK - 1))) * (1 - 2 * d)
    r64 = lax.broadcasted_iota(jnp.int32, (CHUNK, CHUNK), 0)
    c64 = lax.broadcasted_iota(jnp.int32, (CHUNK, CHUNK), 1)
    cum = jnp.where((r64 - c64) * (1 - 2 * d) >= 0, 1.0, 0.0).astype(BF16)
    return head_eq, head_eq & (rel >= 0), head_eq & (rel > 0), cum


def _head_cols(x, lane0):
    xs = _stack_heads(x)
    row = lax.broadcasted_iota(jnp.int32, xs.shape, 0)
    lane = lax.broadcasted_iota(jnp.int32, xs.shape, 1)
    return jnp.sum(jnp.where(lane == lane0 + lax.shift_right_logical(row, 6), xs, 0.0), axis=1, keepdims=True)


def _head_lanes(x, lane0):
    src = lax.broadcasted_iota(jnp.int32, (LANES, GROUP_W), 0)
    dst = lax.broadcasted_iota(jnp.int32, (LANES, GROUP_W), 1)
    expand = jnp.where(src == lane0 + lax.shift_right_logical(dst, 6), 1.0, 0.0).astype(BF16)
    return _dot_exact_r(x, expand)


def _lanes_to_col(x):
    row = lax.broadcasted_iota(jnp.int32, (STACK, GROUP_W), 0)
    lane = lax.broadcasted_iota(jnp.int32, (STACK, GROUP_W), 1)
    sel = jnp.where(lane == lax.shift_right_logical(row, 6) * HEAD_DIM, 1.0, 0.0).astype(BF16)
    xs = jnp.broadcast_to(x, (8, GROUP_W))
    out = sum(lax.dot_general(sel, p, (((1,), (1,)), ((), ())), preferred_element_type=F32) for p in _split3(xs))
    return out[:, 0:1]


def _load_diag_state(dst_ref, src_ref):
    dst_ref[...] = jnp.zeros(dst_ref.shape, F32)
    for r in range(dst_ref.shape[0]):
        for d in range(2):
            for h in range(N_HEADS):
                lo, hi = h * HEAD_DIM, (h + 1) * HEAD_DIM
                dst_ref[r, d, lo:hi, lo:hi] = src_ref[r, d, h]


def _store_diag_state(dst_ref, src_ref):
    for r in range(src_ref.shape[0]):
        for d in range(2):
            for h in range(N_HEADS):
                lo, hi = h * HEAD_DIM, (h + 1) * HEAD_DIM
                dst_ref[r, d, h] = src_ref[r, d, lo:hi, lo:hi]


def _rec_specs(b, l):
    rows = 2 if b % 2 == 0 else 1
    tile = _tile(l, 512)
    nt = l // tile
    fwd = lambda width, col: pl.BlockSpec((rows, tile, width), lambda i, j: (i, j, col))
    bwd = lambda width, col: pl.BlockSpec((rows, tile, width), lambda i, j: (i, nt - 1 - j, col))
    state = pl.BlockSpec((rows, 2, N_HEADS, HEAD_DIM, HEAD_DIM), lambda i, j: (i, 0, 0, 0, 0))
    return rows, tile, nt, fwd, bwd, state


class _Chain(NamedTuple):
    r: int
    d: int
    x_ref: Any
    sm_ref: Any
    o_ref: Any
    rows: Any


def _each(fn, *cols):
    return [fn(*args) for args in zip(*cols)]


def _unit_tri_inverses(mats):
    n = mats[0].shape[0]
    row = lax.broadcasted_iota(jnp.int32, (n, n), 0)
    col = lax.broadcasted_iota(jnp.int32, (n, n), 1)

    def same_block(log2_size):
        return lax.shift_right_logical(row, log2_size) == lax.shift_right_logical(col, log2_size)

    p = [jnp.where(same_block(3), -a, 0.0) for a in mats]
    t = [jnp.where(row == col, 1.0, 0.0) + x for x in p]
    for _ in range(2):
        p = _each(_dot, p, p)
        t = _each(jnp.add, t, _each(_dot, t, p))
    for log2_size in range(4, int(math.log2(CHUNK)) + 1):
        level = same_block(log2_size) & jnp.logical_not(same_block(log2_size - 1))
        off_t = [_dot(jnp.where(level, a, 0.0), x) for a, x in zip(mats, t)]
        t = _each(jnp.subtract, t, _each(_dot, t, off_t))
    return t


def _deltanet_chunks(chains, par_ref, s_ref):
    masks = {d: _chunk_masks(d) for d in (0, 1)}
    head_eq = masks[0][0]
    incl = [masks[c.d][1] for c in chains]
    strict = [masks[c.d][2] for c in chains]
    cum = [masks[c.d][3] for c in chains]
    lane_a = [LANE_A + c.d * N_HEADS for c in chains]
    lane_b = [LANE_BETA + c.d * N_HEADS for c in chains]
    q = [c.x_ref[c.r, c.rows, 0:GROUP_W] for c in chains]
    k = [c.x_ref[c.r, c.rows, GROUP_W:2 * GROUP_W] for c in chains]
    v = [c.x_ref[c.r, c.rows, 2 * GROUP_W:3 * GROUP_W] for c in chains]
    sm = [c.sm_ref[c.r, c.rows, :] for c in chains]
    beta_all = _each(_sigmoid, sm)
    g_all = [-jnp.exp(par_ref[0:1, :]) * _softplus(x + par_ref[1:2, :]) for x in sm]
    gc_all = _each(_dot_exact_l, cum, g_all)
    gc = _each(_head_lanes, gc_all, lane_a)
    beta = _each(_head_lanes, beta_all, lane_b)
    g_last = [x[CHUNK - 1:CHUNK] if c.d == 0 else x[0:1] for x, c in zip(gc, chains)]
    gc_col = _each(_head_cols, gc_all, lane_a)
    beta_col = _each(_head_cols, beta_all, lane_b)
    gc_row = _each(_row_bcast, gc_col)
    decay = [jnp.exp(jnp.where(m, x - y, NEG)) for m, x, y in zip(incl, gc_col, gc_row)]
    k_rows = _each(_stack_heads, k)
    kk = [_dot_nt(jnp.where(head_eq, x, 0.0), x) for x in k_rows]
    a = [jnp.where(m, b * x * dc, 0.0) for m, b, x, dc in zip(strict, beta_col, kk, decay)]
    t = _unit_tri_inverses(a)
    egc = _each(jnp.exp, gc)
    u = [_fold_heads(_dot(x, _block_diag(y * b, head_eq))) for x, y, b in zip(t, v, beta)]
    w = [_fold_heads(_dot(x, _block_diag(y * (b * e), head_eq))) for x, y, b, e in zip(t, k, beta, egc)]
    s = [s_ref[c.r, c.d] for c in chains]
    ws_qs = [_dot(jnp.concatenate([x, y * e], axis=0), z) for x, y, e, z in zip(w, q, egc, s)]
    v_new = [x - y[0:CHUNK] for x, y in zip(u, ws_qs)]
    qk = [_dot_nt(_block_diag(x, head_eq), y) * dc for x, y, dc in zip(q, k_rows, decay)]
    o_intra = [_fold_heads(_dot(x, _block_diag(y, head_eq))) for x, y in zip(qk, v_new)]
    s_add = [_dot_tn(x * jnp.exp(gl - g), y) for x, gl, g, y in zip(k, g_last, gc, v_new)]
    for c, x, y, z, gl, sa in zip(chains, ws_qs, o_intra, s, g_last, s_add):
        c.o_ref[c.r, c.rows, :] = x[CHUNK:] + y
        s_ref[c.r, c.d] = z * jnp.exp(gl) + jnp.where(head_eq, sa, 0.0)


def _chunk_chains(n_rows, n_chunks, c, f_refs, b_refs):
    rows_f = pl.ds(pl.multiple_of(c * CHUNK, CHUNK), CHUNK)
    rows_b = pl.ds(pl.multiple_of((n_chunks - 1 - c) * CHUNK, CHUNK), CHUNK)
    return [_Chain(r, d, *refs, rows) for r in range(n_rows)
            for d, refs, rows in ((0, f_refs, rows_f), (1, b_refs, rows_b))]


def _deltanet_kernel(qf_ref, smf_ref, qb_ref, smb_ref, par_ref, s0_ref, of_ref, ob_ref, sout_ref, s_ref, *,
                     n_chunks):
    j = pl.program_id(1)

    @pl.when(j == 0)
    def _():
        _load_diag_state(s_ref, s0_ref)

    def chunk_body(c, carry):
        chains = _chunk_chains(s_ref.shape[0], n_chunks, c, (qf_ref, smf_ref, of_ref), (qb_ref, smb_ref, ob_ref))
        _deltanet_chunks(chains, par_ref, s_ref)
        return carry

    lax.fori_loop(0, n_chunks, chunk_body, 0)

    @pl.when(j == pl.num_programs(1) - 1)
    def _():
        _store_diag_state(sout_ref, s_ref)


def _deltanet(qkv, z, par, s0):
    b, l, _ = qkv.shape
    rows, tile, nt, fwd, bwd, state = _rec_specs(b, l)
    return pl.pallas_call(
        functools.partial(_deltanet_kernel, n_chunks=tile // CHUNK),
        grid=(b // rows, nt),
        in_specs=[fwd(3 * GROUP_W, 0), fwd(LANES, COL_SMALL), bwd(3 * GROUP_W, 0), bwd(LANES, COL_SMALL),
                  pl.BlockSpec((8, LANES), lambda i, j: (0, 0)), state],
        out_specs=[fwd(GROUP_W, 0), bwd(GROUP_W, 0), state],
        out_shape=[jax.ShapeDtypeStruct((b, l, GROUP_W), F32), jax.ShapeDtypeStruct((b, l, GROUP_W), F32),
                   jax.ShapeDtypeStruct((b, 2, N_HEADS, HEAD_DIM, HEAD_DIM), F32)],
        scratch_shapes=[pltpu.VMEM((rows, 2, STACK, STACK), F32)],
        compiler_params=_cp("parallel", "arbitrary"),
        name="deltanet",
    )(qkv, z, qkv, z, par, s0)


def _mlstm_chunks(chains, par_ref, c_ref, n_ref, m_ref):
    masks = {d: _chunk_masks(d) for d in (0, 1)}
    head_eq = masks[0][0]
    incl = [masks[c.d][1] for c in chains]
    cum = [masks[c.d][3] for c in chains]
    lane_i = [LANE_I + c.d * N_HEADS for c in chains]
    lane_f = [LANE_F + c.d * N_HEADS for c in chains]
    q = [c.x_ref[c.r, c.rows, 0:GROUP_W] for c in chains]
    k = [c.x_ref[c.r, c.rows, GROUP_W:2 * GROUP_W] * (HEAD_DIM ** -0.5) for c in chains]
    v = [c.x_ref[c.r, c.rows, 2 * GROUP_W:3 * GROUP_W] for c in chains]
    sm = [c.sm_ref[c.r, c.rows, :] for c in chains]
    i_all = [x + par_ref[0:1, :] for x in sm]
    f_all = [-_softplus(-(x + par_ref[1:2, :])) for x in sm]
    bc_all = _each(_dot_exact_l, cum, f_all)
    ic = _each(_head_lanes, i_all, lane_i)
    bc = _each(_head_lanes, bc_all, lane_f)
    b_tot = [x[CHUNK - 1:CHUNK] if c.d == 0 else x[0:1] for x, c in zip(bc, chains)]
    ic_col = _each(_head_cols, i_all, lane_i)
    bc_col = _each(_head_cols, bc_all, lane_f)
    c_s = [c_ref[c.r, c.d] for c in chains]
    n_s = [n_ref[c.r, c.d] for c in chains]
    m_s = [m_ref[c.r, c.d] for c in chains]
    w_row = _each(_row_bcast, _each(jnp.subtract, ic_col, bc_col))
    dlog = [jnp.where(m, x + y, NEG) for m, x, y in zip(incl, bc_col, w_row)]
    a = _each(jnp.add, bc_col, _each(_lanes_to_col, m_s))
    m_t = [jnp.maximum(x, jnp.max(y, axis=1, keepdims=True)) for x, y in zip(a, dlog)]
    inter = [jnp.exp(x - y) for x, y in zip(a, m_t)]
    q_bd = [_block_diag(x, head_eq) for x in q]
    s = [_dot_nt(x, _stack_heads(y)) * jnp.exp(dl - mt) for x, y, dl, mt in zip(q_bd, k, dlog, m_t)]
    inter_part = _each(_dot, q_bd, c_s)
    intra_part = [_dot(x, _block_diag(y, head_eq)) for x, y in zip(s, v)]
    den = [it * jnp.sum(x * n, axis=1, keepdims=True) + jnp.sum(y, axis=1, keepdims=True)
           for it, x, n, y in zip(inter, q_bd, n_s, s)]
    wend = [bt - x + y for bt, x, y in zip(b_tot, bc, ic)]
    a_end = _each(jnp.add, b_tot, m_s)
    m_new = [jnp.maximum(x, jnp.max(y, axis=0, keepdims=True)) for x, y in zip(a_end, wend)]
    dec = [jnp.exp(x - y) for x, y in zip(a_end, m_new)]
    kw = [x * jnp.exp(y - z) for x, y, z in zip(k, wend, m_new)]
    c_add = _each(_dot_tn, kw, v)
    for i, c in enumerate(chains):
        num = inter[i] * inter_part[i] + intra_part[i]
        c.o_ref[c.r, c.rows, :] = _fold_heads(num / jnp.maximum(jnp.abs(den[i]), jnp.exp(-m_t[i])))
        c_ref[c.r, c.d] = dec[i] * c_s[i] + jnp.where(head_eq, c_add[i], 0.0)
        n_ref[c.r, c.d] = dec[i] * n_s[i] + jnp.sum(kw[i], axis=0, keepdims=True)
        m_ref[c.r, c.d] = m_new[i]


def _mlstm_kernel(zf_ref, smf_ref, zb_ref, smb_ref, par_ref, c0_ref, n0_ref, m0_ref,
                  of_ref, ob_ref, cout_ref, nout_ref, mout_ref, c_ref, n_ref, m_ref, *, n_chunks):
    j = pl.program_id(1)

    @pl.when(j == 0)
    def _():
        _load_diag_state(c_ref, c0_ref)
        n_ref[...] = n0_ref[...]
        m_ref[...] = m0_ref[...]

    def chunk_body(c, carry):
        chains = _chunk_chains(c_ref.shape[0], n_chunks, c, (zf_ref, smf_ref, of_ref), (zb_ref, smb_ref, ob_ref))
        _mlstm_chunks(chains, par_ref, c_ref, n_ref, m_ref)
        return carry

    lax.fori_loop(0, n_chunks, chunk_body, 0)

    @pl.when(j == pl.num_programs(1) - 1)
    def _():
        _store_diag_state(cout_ref, c_ref)
        nout_ref[...] = n_ref[...]
        mout_ref[...] = m_ref[...]


def _mlstm(z, par, c0, n0, m0):
    b, l, _ = z.shape
    rows, tile, nt, fwd, bwd, state = _rec_specs(b, l)
    vec = pl.BlockSpec((rows, 2, 1, GROUP_W), lambda i, j: (i, 0, 0, 0))
    return pl.pallas_call(
        functools.partial(_mlstm_kernel, n_chunks=tile // CHUNK),
        grid=(b // rows, nt),
        in_specs=[fwd(3 * GROUP_W, COL_ML), fwd(LANES, COL_SMALL), bwd(3 * GROUP_W, COL_ML), bwd(LANES, COL_SMALL),
                  pl.BlockSpec((8, LANES), lambda i, j: (0, 0)), state, vec, vec],
        out_specs=[fwd(GROUP_W, 0), bwd(GROUP_W, 0), state, vec, vec],
        out_shape=[jax.ShapeDtypeStruct((b, l, GROUP_W), F32), jax.ShapeDtypeStruct((b, l, GROUP_W), F32),
                   jax.ShapeDtypeStruct((b, 2, N_HEADS, HEAD_DIM, HEAD_DIM), F32),
                   jax.ShapeDtypeStruct((b, 2, 1, GROUP_W), F32), jax.ShapeDtypeStruct((b, 2, 1, GROUP_W), F32)],
        scratch_shapes=[pltpu.VMEM((rows, 2, STACK, STACK), F32), pltpu.VMEM((rows, 2, 1, GROUP_W), F32),
                        pltpu.VMEM((rows, 2, 1, GROUP_W), F32)],
        compiler_params=_cp("parallel", "arbitrary"),
        name="mlstm",
    )(z, z, z, z, par, c0, n0, m0)


def _diff_prep_kernel(z_ref, gq_ref, gk_ref, cos_ref, sin_ref, q_ref, kt_ref, v_ref, kd_ref, *, rope):
    z = z_ref[0]
    q, k, v = z[:, :GROUP_W], z[:, GROUP_W:2 * GROUP_W], z[:, 2 * GROUP_W:]
    q = q * lax.rsqrt(_seg_mean(q * q, DF_DH) + EPS) * gq_ref[...]
    k = k * lax.rsqrt(_seg_mean(k * k, DF_DH) + EPS) * gk_ref[...]
    kd_ref[0] = k
    if rope:
        lane = lax.broadcasted_iota(jnp.int32, q.shape, 1)
        first = (lane & (DF_DH - 1)) < DF_DH // 2

        def rot(x):
            swapped = jnp.where(first, pltpu.roll(x, GROUP_W - DF_DH // 2, 1), pltpu.roll(x, DF_DH // 2, 1))
            return x * cos_ref[...] + swapped * sin_ref[...]

        q, k = rot(q), rot(k)
    _store_attn_operands(q * (DF_DH ** -0.5), k, v, q_ref, kt_ref, v_ref)


def _store_attn_operands(q, k, v, q_ref, kt_ref, v_ref):
    kt = k.T
    lane = lax.broadcasted_iota(jnp.int32, (k.shape[0], HEAD_DIM), 1)
    ones = jnp.ones((k.shape[0], HEAD_DIM), BF16)
    for h in range(N_HEADS):
        lo, hi = h * HEAD_DIM, (h + 1) * HEAD_DIM
        if q is not None:
            q_ref[0, h, 0] = jnp.where(lane < DF_DH, q[:, lo:hi], 0.0).astype(BF16)
            q_ref[0, h, 1] = jnp.where(lane >= DF_DH, q[:, lo:hi], 0.0).astype(BF16)
        v_ref[0, h] = jnp.concatenate([v[:, lo:hi].astype(BF16), ones], axis=1)
        kt_ref[0, h] = kt[lo:hi, :].astype(BF16)


def _cache_prep_kernel(k_ref, v_ref, kt_ref, vh_ref):
    _store_attn_operands(None, k_ref[0], v_ref[0], None, kt_ref, vh_ref)


def _rope_tables(l):
    rows = l // GRID_W
    r = jnp.repeat(jnp.arange(rows, dtype=F32), GRID_W)
    col = jnp.tile(jnp.arange(GRID_W, dtype=F32), rows)
    n_freq = DF_DH // 4
    inv = ROPE_THETA ** (-jnp.arange(n_freq, dtype=F32) / n_freq)
    ang = jnp.concatenate([r[:, None] * inv, col[:, None] * inv], axis=-1)
    cos, sin = jnp.cos(ang), jnp.sin(ang)
    reps = GROUP_W // DF_DH
    return (jnp.tile(jnp.concatenate([cos, cos], axis=-1), (1, reps)),
            jnp.tile(jnp.concatenate([-sin, sin], axis=-1), (1, reps)))


def _diff_prep(z, gq, gk, rope):
    b, l, _ = z.shape
    tm = _tile(l, 512)
    if rope:
        cos, sin = _rope_tables(l)
    else:
        cos = sin = jnp.zeros((l, GROUP_W), F32)
    return pl.pallas_call(
        functools.partial(_diff_prep_kernel, rope=rope),
        grid=(b, l // tm),
        in_specs=[pl.BlockSpec((1, tm, 3 * GROUP_W), lambda i, j: (i, j, COL_DF)),
                  pl.BlockSpec((1, GROUP_W), lambda i, j: (0, 0)),
                  pl.BlockSpec((1, GROUP_W), lambda i, j: (0, 0)),
                  pl.BlockSpec((tm, GROUP_W), lambda i, j: (j, 0)),
                  pl.BlockSpec((tm, GROUP_W), lambda i, j: (j, 0))],
        out_specs=[pl.BlockSpec((1, N_HEADS, 2, tm, HEAD_DIM), lambda i, j: (i, 0, 0, j, 0)),
                   pl.BlockSpec((1, N_HEADS, HEAD_DIM, tm), lambda i, j: (i, 0, 0, j)),
                   pl.BlockSpec((1, N_HEADS, tm, 2 * HEAD_DIM), lambda i, j: (i, 0, j, 0)),
                   pl.BlockSpec((1, tm, GROUP_W), lambda i, j: (i, j, 0))],
        out_shape=[jax.ShapeDtypeStruct((b, N_HEADS, 2, l, HEAD_DIM), BF16),
                   jax.ShapeDtypeStruct((b, N_HEADS, HEAD_DIM, l), BF16),
                   jax.ShapeDtypeStruct((b, N_HEADS, l, 2 * HEAD_DIM), BF16),
                   jax.ShapeDtypeStruct((b, l, GROUP_W), F32)],
        compiler_params=_cp("parallel", "parallel"),
        name="diff_prep",
    )(z, jnp.tile(gq, GROUP_W // DF_DH).reshape(1, GROUP_W), jnp.tile(gk, GROUP_W // DF_DH).reshape(1, GROUP_W),
      cos, sin)


def _cache_prep(ck, cv):
    b, p, _ = ck.shape
    tm = _tile(p, 512)
    return pl.pallas_call(
        _cache_prep_kernel,
        grid=(b, p // tm),
        in_specs=[pl.BlockSpec((1, tm, GROUP_W), lambda i, j: (i, j, 0)),
                  pl.BlockSpec((1, tm, GROUP_W), lambda i, j: (i, j, 0))],
        out_specs=[pl.BlockSpec((1, N_HEADS, HEAD_DIM, tm), lambda i, j: (i, 0, 0, j)),
                   pl.BlockSpec((1, N_HEADS, tm, 2 * HEAD_DIM), lambda i, j: (i, 0, j, 0))],
        out_shape=[jax.ShapeDtypeStruct((b, N_HEADS, HEAD_DIM, p), BF16),
                   jax.ShapeDtypeStruct((b, N_HEADS, p, 2 * HEAD_DIM), BF16)],
        compiler_params=_cp("parallel", "parallel"),
        name="cache_prep",
    )(ck, cv)


def _diff_attn_kernel(q_ref, kt_ref, v_ref, lam_ref, g_ref, o_ref, m_ref, acc_ref, *, lam_init):
    ik = pl.program_id(3)

    @pl.when(ik == 0)
    def _():
        m_ref[...] = jnp.full(m_ref.shape, NEG, F32)
        acc_ref[...] = jnp.zeros(acc_ref.shape, F32)

    chains = [(h, m) for h in range(q_ref.shape[1]) for m in range(2)]
    s = [jnp.dot(q_ref[0, h, m], kt_ref[0, h], preferred_element_type=F32) for h, m in chains]
    m_old = [m_ref[h, m] for h, m in chains]
    m_new = [jnp.maximum(x, jnp.max(y, axis=1, keepdims=True)) for x, y in zip(m_old, s)]
    p = [jnp.exp(x - y[:, 0:1]).astype(BF16) for x, y in zip(s, m_new)]
    pv = [jnp.dot(x, v_ref[0, h], preferred_element_type=F32) for x, (h, m) in zip(p, chains)]
    for i, (h, m) in enumerate(chains):
        acc_ref[h, m] = jnp.exp(m_old[i] - m_new[i]) * acc_ref[h, m] + pv[i]
        m_ref[h, m] = m_new[i]

    @pl.when(ik == pl.num_programs(3) - 1)
    def _():
        lp = lam_ref[...]
        lam = (jnp.exp(jnp.sum(lp[0:1] * lp[1:2], axis=1, keepdims=True))
               - jnp.exp(jnp.sum(lp[2:3] * lp[3:4], axis=1, keepdims=True)) + lam_init)
        for h in range(q_ref.shape[1]):
            a0, a1 = acc_ref[h, 0], acc_ref[h, 1]
            o = (a0[:, :HEAD_DIM] / a0[:, HEAD_DIM:HEAD_DIM + 1]
                 - lam * (a1[:, :HEAD_DIM] / a1[:, HEAD_DIM:HEAD_DIM + 1]))
            var = jnp.mean(o * o, axis=1, keepdims=True)
            o_ref[0, h] = o * lax.rsqrt(var + EPS) * g_ref[...] * (1.0 - lam_init)


ATTN_HEADS_PER_STEP = 2


def _diff_attn(q, kt, v, lam_par, gain, lam_init):
    b, _, _, l, _ = q.shape
    lk = kt.shape[-1]
    hs = ATTN_HEADS_PER_STEP
    tq = _tile(l, 512)
    tk = _tile(lk, 2304)
    return pl.pallas_call(
        functools.partial(_diff_attn_kernel, lam_init=lam_init),
        grid=(b, N_HEADS // hs, l // tq, lk // tk),
        in_specs=[pl.BlockSpec((1, hs, 2, tq, HEAD_DIM), lambda i, h, a, c: (i, h, 0, a, 0)),
                  pl.BlockSpec((1, hs, HEAD_DIM, tk), lambda i, h, a, c: (i, h, 0, c)),
                  pl.BlockSpec((1, hs, tk, 2 * HEAD_DIM), lambda i, h, a, c: (i, h, c, 0)),
                  pl.BlockSpec((4, DF_DH), lambda i, h, a, c: (0, 0)),
                  pl.BlockSpec((1, HEAD_DIM), lambda i, h, a, c: (0, 0))],
        out_specs=pl.BlockSpec((1, hs, tq, HEAD_DIM), lambda i, h, a, c: (i, h, a, 0)),
        out_shape=jax.ShapeDtypeStruct((b, N_HEADS, l, HEAD_DIM), F32),
        scratch_shapes=[pltpu.VMEM((hs, 2, tq, LANES), F32), pltpu.VMEM((hs, 2, tq, 2 * HEAD_DIM), F32)],
        compiler_params=_cp("parallel", "parallel", "parallel", "arbitrary"),
        name="diff_attn",
    )(q, kt, v, lam_par, gain.reshape(1, HEAD_DIM))


def _hy_filter_kernel(feat_ref, win_ref, w1_ref, b1_ref, w2_ref, b2_ref, w3_ref, fr_ref, o_ref):
    h = jnp.sin(fr_ref[0:1, :] * (_dot_hi(feat_ref[...], w1_ref[...]) + b1_ref[...]))
    h = jnp.sin(fr_ref[1:2, :] * (_dot_hi(h, w2_ref[...]) + b2_ref[...]))
    h = _dot_hi(h, w3_ref[...]) * win_ref[...]
    o_ref[...] = h * lax.rsqrt(jnp.sum(h * h, axis=0, keepdims=True) + EPS)


def _hy_constants(l):
    pos = jnp.arange(l, dtype=F32)
    bands = jnp.arange(1, HY_BANDS + 1, dtype=F32)
    ang = (2.0 * math.pi / l) * pos[:, None] * bands[None, :]
    feat = jnp.concatenate([pos[:, None] / l, jnp.cos(ang), jnp.sin(ang)], axis=-1)
    feat = jnp.pad(feat, ((0, 0), (0, LANES - feat.shape[1])))
    rates = jnp.linspace(-math.log(HY_TARGET) / HY_FAST_DECAY, -math.log(HY_TARGET) / HY_SLOW_DECAY,
                         GROUP_W, dtype=F32)
    rates = jnp.tile(rates, 2)
    offset = jnp.abs(pos - l // 2) / l
    return feat, jnp.exp(-offset[:, None] * rates[None, :]) + HY_SHIFT


def _hy_filters(feat, win, w1, b1, w2, b2, w3, freq):
    l = feat.shape[0]
    hid = w2.shape[0]
    w1p = jnp.pad(w1, ((0, LANES - w1.shape[0]), (0, 0)))
    full = lambda shape: pl.BlockSpec(shape, lambda o: (0, 0))
    return pl.pallas_call(
        _hy_filter_kernel,
        grid=(2,),
        in_specs=[full((l, LANES)), pl.BlockSpec((l, GROUP_W), lambda o: (0, o)),
                  full((LANES, hid)), full((1, hid)), full((hid, hid)), full((1, hid)),
                  pl.BlockSpec((hid, GROUP_W), lambda o: (0, o)), full((2, hid))],
        out_specs=pl.BlockSpec((l, GROUP_W), lambda o: (0, o)),
        out_shape=jax.ShapeDtypeStruct((l, 2 * GROUP_W), F32),
        compiler_params=_cp("parallel"),
        name="hy_filters",
    )(feat, win, w1p, b1.reshape(1, hid), w2, b2.reshape(1, hid), w3, freq)


def _dft_matrices(l):
    n = 2 * l
    k = jnp.arange(l, dtype=jnp.int32)
    alt = jnp.where(k % 2 == 0, 1.0, -1.0).astype(F32)
    cos_f, sin_f = _cos_sin_grid(0, l, n)
    fwd_im = jnp.where(k[:, None] == 0, alt[None, :], -sin_f)
    fwd = jnp.stack([cos_f, fwd_im]).astype(BF16)
    t = k + l // 2
    cos_i, sin_i = _cos_sin_grid(l // 2, l, n)
    alt_t = jnp.where(t % 2 == 0, 1.0, -1.0).astype(F32)
    inv_re = jnp.where(k[None, :] == 0, 1.0 / n, (2.0 / n) * cos_i)
    inv_im = jnp.where(k[None, :] == 0, alt_t[:, None] / n, -(2.0 / n) * sin_i)
    inv = jnp.concatenate([inv_re, inv_im], axis=1).astype(BF16)
    return fwd, inv


def _cos_sin_grid(row0, l, n):
    s = 1 << (int(math.log2(l)) // 2)
    assert row0 % s == 0 and l % s == 0
    c = jnp.arange(l, dtype=jnp.int32)
    hi = row0 + jnp.arange(l // s, dtype=jnp.int32) * s
    lo = jnp.arange(s, dtype=jnp.int32)

    def angle(r):
        return (2.0 * math.pi / n) * ((r[:, None] * c[None, :]) % n).astype(F32)

    a, b = angle(hi), angle(lo)
    ca, sa = jnp.cos(a)[:, None, :], jnp.sin(a)[:, None, :]
    cb, sb = jnp.cos(b)[None, :, :], jnp.sin(b)[None, :, :]
    return (ca * cb - sa * sb).reshape(l, l), (sa * cb + ca * sb).reshape(l, l)


def _dft_filter_kernel(f_ref, h_ref, o_ref):
    h = h_ref[...].astype(BF16)
    o_ref[0] = jnp.dot(f_ref[0], h, preferred_element_type=F32)
    o_ref[1] = jnp.dot(f_ref[1], h, preferred_element_type=F32)


def _dft_filter(fwd, filt):
    l, c = filt.shape
    tm = _tile(l, 256)
    return pl.pallas_call(
        _dft_filter_kernel,
        grid=(l // tm,),
        in_specs=[pl.BlockSpec((2, tm, l), lambda i: (0, i, 0)), pl.BlockSpec((l, c), lambda i: (0, 0))],
        out_specs=pl.BlockSpec((2, tm, c), lambda i: (0, i, 0)),
        out_shape=jax.ShapeDtypeStruct((2, l, c), F32),
        compiler_params=_cp("parallel"),
        name="dft_filter",
    )(fwd, filt)


def _dft_fwd_mul_kernel(f_ref, u_ref, h_ref, o_ref):
    u = u_ref[0]
    ur = jnp.dot(f_ref[0], u, preferred_element_type=F32)
    ui = jnp.dot(f_ref[1], u, preferred_element_type=F32)
    hr, hi = h_ref[0], h_ref[1]
    row = lax.broadcasted_iota(jnp.int32, ur.shape, 0) + pl.program_id(0) * ur.shape[0]
    packed = row == 0
    o_ref[0, 0] = (ur * hr - jnp.where(packed, 0.0, ui * hi)).astype(BF16)
    o_ref[0, 1] = jnp.where(packed, ui * hi, ur * hi + ui * hr).astype(BF16)


def _dft_fwd_mul(fwd, u, hf, order):
    b, l, _ = u.shape
    tm = _tile(l, 512)
    return pl.pallas_call(
        _dft_fwd_mul_kernel,
        grid=(l // tm, b),
        in_specs=[pl.BlockSpec((2, tm, l), lambda i, j: (0, i, 0)),
                  pl.BlockSpec((1, l, GROUP_W), lambda i, j: (j, 0, 0)),
                  pl.BlockSpec((2, tm, GROUP_W), lambda i, j: (0, i, order))],
        out_specs=pl.BlockSpec((1, 2, tm, GROUP_W), lambda i, j: (j, 0, i, 0)),
        out_shape=jax.ShapeDtypeStruct((b, 2, l, GROUP_W), BF16),
        compiler_params=_cp("parallel", "parallel"),
        name="dft_fwd_mul",
    )(fwd, u, hf)


def _dft_inv_gate_kernel(g_ref, y_ref, x_ref, u_ref, skip_ref, o_ref, lowp_ref):
    conv = jnp.dot(g_ref[...], y_ref[0], preferred_element_type=F32)
    out = x_ref[0] * (conv + skip_ref[...] * u_ref[0])
    o_ref[0] = out
    lowp_ref[0] = out.astype(BF16)


def _dft_inv_gate(inv, y, xg, x_col, u, u_col, skip):
    b, _, l, _ = y.shape
    tm = _tile(l, 512)
    out_spec = pl.BlockSpec((1, tm, GROUP_W), lambda i, j: (j, i, 0))
    return pl.pallas_call(
        _dft_inv_gate_kernel,
        grid=(l // tm, b),
        in_specs=[pl.BlockSpec((tm, 2 * l), lambda i, j: (i, 0)),
                  pl.BlockSpec((1, 2 * l, GROUP_W), lambda i, j: (j, 0, 0)),
                  pl.BlockSpec((1, tm, GROUP_W), lambda i, j: (j, i, x_col)),
                  pl.BlockSpec((1, tm, GROUP_W), lambda i, j: (j, i, u_col)),
                  pl.BlockSpec((1, GROUP_W), lambda i, j: (0, 0))],
        out_specs=[out_spec, out_spec],
        out_shape=[jax.ShapeDtypeStruct((b, l, GROUP_W), F32), jax.ShapeDtypeStruct((b, l, GROUP_W), BF16)],
        compiler_params=_cp("parallel", "parallel"),
        name="dft_inv_gate",
    )(inv, y.reshape(b, 2 * l, GROUP_W), xg, u, skip.reshape(1, GROUP_W))


def _head_norm(x, gain):
    return x * lax.rsqrt(_seg_mean(x * x, HEAD_DIM) + EPS) * gain


def _proj_out_kernel(dnf_ref, dnb_ref, gate_ref, df_ref, hy_ref, mlf_ref, mlb_ref, mo_ref, x_ref, mod_ref,
                     gdn_ref, gml_ref, w_ref, o_ref):
    gate = gate_ref[0]
    y_dn = _head_norm(dnf_ref[0] + dnb_ref[0], gdn_ref[...]) * (gate * _sigmoid(gate))
    y_ml = _head_norm(mlf_ref[0] + mlb_ref[0], gml_ref[...]) * _sigmoid(mo_ref[0])
    y = _dot(y_dn, w_ref[0:GROUP_W, :])
    for h in range(N_HEADS):
        lo = GROUP_W + h * HEAD_DIM
        y = y + _dot(df_ref[0, h], w_ref[lo:lo + HEAD_DIM, :])
    y = y + _dot(hy_ref[0], w_ref[2 * GROUP_W:3 * GROUP_W, :])
    y = y + _dot(y_ml, w_ref[3 * GROUP_W:, :])
    o_ref[0] = x_ref[0] + mod_ref[0, 2:3, :] * y


def _proj_out(o_dn, z, o_df, y_hy, h_ml, x, mod, g_dn, g_ml, w):
    b, l, d = x.shape
    tm = _tile(l, 512)
    per_batch = mod.shape[0] > 1
    group = lambda col: pl.BlockSpec((1, tm, GROUP_W), lambda i, j: (i, j, col))
    return pl.pallas_call(
        _proj_out_kernel,
        grid=(b, l // tm),
        in_specs=[group(0), group(0), group(COL_DGATE),
                  pl.BlockSpec((1, N_HEADS, tm, HEAD_DIM), lambda i, j: (i, 0, j, 0)),
                  group(0), group(0), group(0), group(COL_MO),
                  pl.BlockSpec((1, tm, d), lambda i, j: (i, j, 0)),
                  pl.BlockSpec((1, 6, d), lambda i, j: (i if per_batch else 0, 0, 0)),
                  pl.BlockSpec((1, GROUP_W), lambda i, j: (0, 0)),
                  pl.BlockSpec((1, GROUP_W), lambda i, j: (0, 0)),
                  pl.BlockSpec((d, d), lambda i, j: (0, 0))],
        out_specs=pl.BlockSpec((1, tm, d), lambda i, j: (i, j, 0)),
        out_shape=jax.ShapeDtypeStruct((b, l, d), F32),
        compiler_params=_cp("parallel", "parallel"),
        name="proj_out",
    )(o_dn[0], o_dn[1], z, o_df, y_hy, h_ml[0], h_ml[1], z, x, mod, jnp.tile(g_dn, N_HEADS).reshape(1, GROUP_W),
      jnp.tile(g_ml, N_HEADS).reshape(1, GROUP_W), w)


def _ffn_kernel(x_ref, mod_ref, g_ref, wg_ref, wu_ref, wd_ref, o_ref, h_ref, acc_ref):
    j = pl.program_id(2)

    @pl.when(j == 0)
    def _():
        h_ref[...] = _modulated_norm(x_ref[0], g_ref[...], mod_ref[0, 3:4, :], mod_ref[0, 4:5, :]).astype(BF16)
        acc_ref[...] = jnp.zeros(acc_ref.shape, F32)

    h = h_ref[...]
    g = jnp.dot(h, wg_ref[...], preferred_element_type=F32)
    u = jnp.dot(h, wu_ref[...], preferred_element_type=F32)
    acc_ref[...] += jnp.dot((g * _sigmoid(g) * u).astype(BF16), wd_ref[...], preferred_element_type=F32)

    @pl.when(j == pl.num_programs(2) - 1)
    def _():
        o_ref[0] = x_ref[0] + mod_ref[0, 5:6, :] * acc_ref[...]


def _ffn(x, mod, gain, wg, wu, wd):
    b, l, d = x.shape
    f = wg.shape[1]
    tm = _tile(l, 1024)
    tf = 256
    per_batch = mod.shape[0] > 1
    return pl.pallas_call(
        _ffn_kernel,
        grid=(b, l // tm, f // tf),
        in_specs=[pl.BlockSpec((1, tm, d), lambda i, a, j: (i, a, 0)),
                  pl.BlockSpec((1, 6, d), lambda i, a, j: (i if per_batch else 0, 0, 0)),
                  pl.BlockSpec((1, d), lambda i, a, j: (0, 0)),
                  pl.BlockSpec((d, tf), lambda i, a, j: (0, j)),
                  pl.BlockSpec((d, tf), lambda i, a, j: (0, j)),
                  pl.BlockSpec((tf, d), lambda i, a, j: (j, 0))],
        out_specs=pl.BlockSpec((1, tm, d), lambda i, a, j: (i, a, 0)),
        out_shape=jax.ShapeDtypeStruct((b, l, d), F32),
        scratch_shapes=[pltpu.VMEM((tm, d), BF16), pltpu.VMEM((tm, d), F32)],
        compiler_params=_cp("parallel", "parallel", "arbitrary"),
        name="ffn_dense",
    )(x, mod, gain.reshape(1, d), wg, wu, wd)


def _top2_combine(logits):
    lane = lax.broadcasted_iota(jnp.int32, logits.shape, 1)
    v1 = jnp.max(logits, axis=1, keepdims=True)
    i1 = jnp.min(jnp.where(logits == v1, lane, LANES), axis=1, keepdims=True)
    rest = jnp.where(lane == i1, NEG, logits)
    v2 = jnp.max(rest, axis=1, keepdims=True)
    i2 = jnp.min(jnp.where(rest == v2, lane, LANES), axis=1, keepdims=True)
    e2 = jnp.exp(v2 - v1)
    return jnp.where(lane == i1, 1.0 / (1.0 + e2), 0.0) + jnp.where(lane == i2, e2 / (1.0 + e2), 0.0)


MOE_ROW_TILE = 512


def _moe_route_kernel(x_ref, mod_ref, g_ref, r_ref, h_ref, comb_ref, rank_ref, cnt_ref):
    @pl.when((pl.program_id(0) == 0) & (pl.program_id(1) == 0))
    def _():
        cnt_ref[...] = jnp.zeros(cnt_ref.shape, F32)

    hn = _modulated_norm(x_ref[0], g_ref[...], mod_ref[0, 3:4, :], mod_ref[0, 4:5, :])
    tm = hn.shape[0]
    for j in range(h_ref.shape[0]):
        h_ref[j, 0] = hn[:, j * LANES:(j + 1) * LANES]
    lane = lax.broadcasted_iota(jnp.int32, (tm, LANES), 1)
    comb = _top2_combine(jnp.where(lane < N_EXPERTS, _dot_hi(hn, r_ref[...]), NEG))
    comb_ref[0] = comb
    routed = jnp.where(comb > 0.0, 1.0, 0.0)
    row = lax.broadcasted_iota(jnp.int32, (tm, tm), 0)
    col = lax.broadcasted_iota(jnp.int32, (tm, tm), 1)
    earlier = jnp.where(col < row, 1.0, 0.0).astype(BF16)
    rank = cnt_ref[...] + jnp.dot(earlier, routed.astype(BF16), preferred_element_type=F32)
    rank_ref[0] = jnp.where(routed > 0.0, rank, -1.0)
    cnt_ref[...] += jnp.sum(routed, axis=0, keepdims=True)


def _moe_group_kernel(tile_expert_ref, used_ref, xs_ref, wg_ref, wu_ref, wd_ref, o_ref):
    del tile_expert_ref
    pieces = xs_ref.shape[0]

    @pl.when(pl.program_id(0) < used_ref[0])
    def _():
        xs = jnp.concatenate([xs_ref[j] for j in range(pieces)], axis=1).astype(BF16)
        g = jnp.dot(xs, wg_ref[0], preferred_element_type=F32)
        u = jnp.dot(xs, wu_ref[0], preferred_element_type=F32)
        y = jnp.dot((g * _sigmoid(g) * u).astype(BF16), wd_ref[0], preferred_element_type=F32)
        for j in range(pieces):
            o_ref[j] = y[:, j * LANES:(j + 1) * LANES]

    @pl.when(pl.program_id(0) >= used_ref[0])
    def _():
        o_ref[...] = jnp.zeros(o_ref.shape, F32)


def _moe_combine_kernel(x_ref, y_ref, comb_ref, mod_ref, o_ref):
    comb = comb_ref[0]
    gate_a = jnp.max(comb, axis=1, keepdims=True)
    gate_b = jnp.sum(comb, axis=1, keepdims=True) - gate_a
    pieces = y_ref.shape[0]
    y_a = jnp.concatenate([y_ref[j, 0, 0] for j in range(pieces)], axis=1)
    y_b = jnp.concatenate([y_ref[j, 1, 0] for j in range(pieces)], axis=1)
    o_ref[0] = x_ref[0] + mod_ref[0, 5:6, :] * (gate_a * y_a + gate_b * y_b)


MOE_SLOT_ALIGN = 2048


def _moe(x, mod, gain, router, wg, wu, wd):
    b, l, d = x.shape
    ne, _, f = wg.shape
    t = b * l
    tm = _tile(l, 1024)
    tg = MOE_ROW_TILE
    per_batch = mod.shape[0] > 1
    router_p = jnp.pad(router, ((0, 0), (0, LANES - ne)))
    x_spec = pl.BlockSpec((1, tm, d), lambda i, j: (i, j, 0))
    mod_spec = pl.BlockSpec((1, 6, d), lambda i, j: (i if per_batch else 0, 0, 0))
    pieces = d // LANES
    lane_spec = pl.BlockSpec((1, tm, LANES), lambda i, j: (i, j, 0))
    h, comb, rank, cnt = pl.pallas_call(
        _moe_route_kernel,
        grid=(b, l // tm),
        in_specs=[x_spec, mod_spec, pl.BlockSpec((1, d), lambda i, j: (0, 0)),
                  pl.BlockSpec((d, LANES), lambda i, j: (0, 0))],
        out_specs=[pl.BlockSpec((pieces, 1, tm, LANES), lambda i, j: (0, i, j, 0)), lane_spec, lane_spec,
                   pl.BlockSpec((1, LANES), lambda i, j: (0, 0))],
        out_shape=[jax.ShapeDtypeStruct((pieces, b, l, LANES), F32), jax.ShapeDtypeStruct((b, l, LANES), F32),
                   jax.ShapeDtypeStruct((b, l, LANES), F32), jax.ShapeDtypeStruct((1, LANES), F32)],
        compiler_params=_cp("arbitrary", "arbitrary"),
        name="moe_route",
    )(x, mod, gain.reshape(1, d), router_p)

    n_slots = -(-(2 * t + (ne + 1) * tg) // MOE_SLOT_ALIGN) * MOE_SLOT_ALIGN
    comb2 = comb.reshape(t, LANES)[:, :ne]
    rank2 = rank.reshape(t, LANES)[:, :ne]
    grp = (cnt[0, :ne].astype(jnp.int32) + tg - 1) // tg * tg
    end = jnp.cumsum(grp)
    off = end - grp
    tile_start = jnp.arange(n_slots // tg, dtype=jnp.int32) * tg
    tile_expert = jnp.minimum(jnp.sum(tile_start[:, None] >= end[None, :], axis=1), ne - 1).astype(jnp.int32)
    used_tiles = (end[-1:] // tg).astype(jnp.int32)
    first = jnp.argmax(comb2, axis=1)
    second = jnp.argmax(jnp.where(jnp.arange(ne)[None, :] == first[:, None], -1.0, comb2), axis=1)
    pick = lambda a, e: jnp.take_along_axis(a, e[:, None], axis=1)[:, 0]
    rank_b = pick(rank2, second)
    pos_a = off[first] + pick(rank2, first).astype(jnp.int32)
    pos_b = off[second] + rank_b.astype(jnp.int32)
    token = jnp.arange(t, dtype=jnp.int32)
    slot_token = jnp.zeros((n_slots,), jnp.int32).at[
        jnp.concatenate([pos_a, jnp.where(rank_b >= 0.0, pos_b, n_slots)])].set(
        jnp.concatenate([token, token]), mode="drop")
    piece_base = jnp.arange(pieces, dtype=jnp.int32)[:, None]

    xs = _sc_gather_pieces(h.reshape(pieces * t, LANES), (piece_base * t + slot_token[None, :]).reshape(-1))
    piece_rows = pl.BlockSpec((pieces, tg, LANES), lambda i, te, used: (0, i, 0))
    ys = pl.pallas_call(
        _moe_group_kernel,
        grid_spec=pltpu.PrefetchScalarGridSpec(
            num_scalar_prefetch=2,
            grid=(n_slots // tg,),
            in_specs=[piece_rows,
                      pl.BlockSpec((1, d, f), lambda i, te, used: (te[i], 0, 0)),
                      pl.BlockSpec((1, d, f), lambda i, te, used: (te[i], 0, 0)),
                      pl.BlockSpec((1, f, d), lambda i, te, used: (te[i], 0, 0))],
            out_specs=piece_rows),
        out_shape=jax.ShapeDtypeStruct((pieces, n_slots, LANES), F32),
        compiler_params=_cp("arbitrary"),
        name="moe_group",
    )(tile_expert, used_tiles, xs.reshape(pieces, n_slots, LANES), wg, wu, wd)
    pair = jnp.concatenate([pos_a, jnp.where(rank_b >= 0.0, pos_b, n_slots - 1)])
    y2 = _sc_gather_pieces(ys.reshape(pieces * n_slots, LANES), (piece_base * n_slots + pair[None, :]).reshape(-1))
    return pl.pallas_call(
        _moe_combine_kernel,
        grid=(b, l // tm),
        in_specs=[x_spec, pl.BlockSpec((pieces, 2, 1, tm, LANES), lambda i, j: (0, 0, i, j, 0)), lane_spec, mod_spec],
        out_specs=x_spec,
        out_shape=jax.ShapeDtypeStruct((b, l, d), F32),
        compiler_params=_cp("parallel", "parallel"),
        name="moe_combine",
    )(x, y2.reshape(pieces, 2, b, l, LANES), comb, mod)


SC_GATHER_WINDOW = 128


def _sc_gather_pieces(table, idx):
    m = idx.shape[0]
    w = table.shape[1]
    mesh = plsc.VectorSubcoreMesh(core_axis_name="core", subcore_axis_name="subcore")

    @functools.partial(pl.kernel, out_type=jax.ShapeDtypeStruct((m, w), table.dtype), mesh=mesh,
                       name="sc_gather_rows")
    def gather(t_hbm, i_hbm, o_hbm):
        def body(i_vmem, o_vmem):
            pltpu.sync_copy(t_hbm.at[i_vmem.at[0]], o_vmem)

        pltpu.emit_pipeline(
            body,
            grid=(m // SC_GATHER_WINDOW,),
            in_specs=[pl.BlockSpec((1, SC_GATHER_WINDOW), lambda i: (0, i))],
            out_specs=[pl.BlockSpec((SC_GATHER_WINDOW, w), lambda i: (i, 0))],
            core_axis_name=("core", "subcore"),
            dimension_semantics=(pltpu.PARALLEL,),
        )(i_hbm, o_hbm)

    return gather(table, idx.reshape(1, m))


def _lane_row(values, lane0):
    row = jnp.zeros((LANES,), F32)
    return lax.dynamic_update_slice(row, values.reshape(-1).astype(F32), (lane0,))


def _reorder_w_in(w_in):
    g = GROUP_W
    o = [0, g, 2 * g, 3 * g, 4 * g, 4 * g + 8, 4 * g + 16]
    dq_dk_dv = w_in[..., o[0]:o[3]]
    dgate = w_in[..., o[3]:o[4]]
    dbeta_da = w_in[..., o[4]:o[6]]
    base = o[6]
    df = w_in[..., base:base + 3 * g]
    hy = w_in[..., base + 3 * g:base + 6 * g]
    ml = w_in[..., base + 6 * g:base + 9 * g]
    mo = w_in[..., base + 9 * g:base + 10 * g]
    mi_mf = w_in[..., base + 10 * g:base + 10 * g + 16]
    pad = jnp.zeros(w_in.shape[:-1] + (LANES - 32,), w_in.dtype)
    return jnp.concatenate([dq_dk_dv, hy, ml, df, dgate, mo, dbeta_da, mi_mf, pad], axis=-1).astype(BF16)


def _layer(l, x, mod, cache, p, hy_consts, dft):
    b, seq, _ = x.shape
    latent = cache is not None
    z = _proj_in(x, mod, p["norm1_g"][l], p["w_in"][l])

    qkv, = _dwconv(z, p["dn_conv"][l], COL_DN, "deltanet")
    dn_par = jnp.zeros((8, LANES), F32).at[0].set(_lane_row(p["dn_a_log"][l], LANE_A)).at[1].set(
        _lane_row(p["dn_dt_bias"][l], LANE_A))
    s0 = cache[2] if latent else jnp.zeros((b, 2, N_HEADS, HEAD_DIM, HEAD_DIM), F32)
    o_dn_f, o_dn_b, s_dn = _deltanet(qkv, z, dn_par, s0)
    o_dn = (o_dn_f, o_dn_b)

    ml_par = jnp.zeros((8, LANES), F32).at[0].set(_lane_row(p["ml_i_bias"][l], LANE_I)).at[1].set(
        _lane_row(p["ml_f_bias"][l], LANE_F))
    if latent:
        c0 = cache[3]
        n0 = cache[4].reshape(b, 2, 1, GROUP_W)
        m0 = jnp.repeat(cache[5], HEAD_DIM, axis=-1).reshape(b, 2, 1, GROUP_W)
    else:
        c0 = jnp.zeros((b, 2, N_HEADS, HEAD_DIM, HEAD_DIM), F32)
        n0 = m0 = jnp.zeros((b, 2, 1, GROUP_W), F32)
    h_ml_f, h_ml_b, c_ml, n_ml, m_ml = _mlstm(z, ml_par, c0, n0, m0)
    h_ml = (h_ml_f, h_ml_b)
    n_ml = n_ml.reshape(b, 2, N_HEADS, HEAD_DIM)
    m_ml = m_ml.reshape(b, 2, N_HEADS, HEAD_DIM)[..., 0]

    q, kt, v, kd = _diff_prep(z, p["df_q_norm"][l], p["df_k_norm"][l], latent)
    if latent:
        ckt, cv = _cache_prep(cache[0].reshape(b, -1, GROUP_W), cache[1].reshape(b, -1, GROUP_W))
        kt = jnp.concatenate([kt, ckt], axis=-1)
        v = jnp.concatenate([v, cv], axis=2)
    lam_init = 0.8 - 0.6 * math.exp(-0.3 * l)
    o_df = _diff_attn(q, kt, v, p["df_lambda"][l], p["df_norm"][l], lam_init)

    feat, win = hy_consts
    fwd, inv = dft
    zc, zv_lowp = _dwconv(z, p["hy_conv"][l], COL_HY, "plain")
    filt = _hy_filters(feat, win, p["hy_w1"][l], p["hy_b1"][l], p["hy_w2"][l], p["hy_b2"][l], p["hy_w3"][l],
                       p["hy_freq"][l])
    hf = _dft_filter(fwd, filt)
    y1 = _dft_fwd_mul(fwd, zv_lowp, hf, 0)
    z1, z1_lowp = _dft_inv_gate(inv, y1, zc, 0, zc, 2, p["hy_skip"][l, 0])
    y2 = _dft_fwd_mul(fwd, z1_lowp, hf, 1)
    y_hy, _ = _dft_inv_gate(inv, y2, zc, 1, z1, 0, p["hy_skip"][l, 1])

    x = _proj_out(o_dn, z, o_df, y_hy, h_ml, x, mod, p["dn_norm"][l], p["ml_norm"][l], p["w_out"][l])
    j = l // 2
    if mod.shape[0] == 1:
        x = x.reshape(1, b * seq, -1)
    if l % 2 == 0:
        x = _ffn(x, mod, p["norm2_g"][l], p["ffn_w_gate"][j], p["ffn_w_up"][j], p["ffn_w_down"][j])
    else:
        x = _moe(x, mod, p["norm2_g"][l], p["moe_router"][j], p["moe_w_gate"][j], p["moe_w_up"][j],
                 p["moe_w_down"][j])
    x = x.reshape(b, seq, -1)
    fv =z[:, :, 3 * 3 * GROUP_W + 2 * GROUP_W:3 * 3 * GROUP_W + 3 * GROUP_W]
    return x, (kd, fv, s_dn, c_ml, n_ml, m_ml)


def kernel(x_prompt, x_sample, cache_diff_k, cache_diff_v, state_delta, state_mlstm_c, state_mlstm_n, state_mlstm_m, c, c_ctx, norm1_g, norm2_g, w_mod, b_mod, w_in, w_out, dn_conv, dn_a_log, dn_dt_bias, dn_norm, df_q_norm, df_k_norm, df_lambda, df_norm, hy_conv, hy_w1, hy_b1, hy_w2, hy_b2, hy_w3, hy_freq, hy_skip, ml_i_bias, ml_f_bias, ml_norm, ffn_w_gate, ffn_w_up, ffn_w_down, moe_router, moe_w_gate, moe_w_up, moe_w_down):
    depth = w_in.shape[0]
    d_model = x_prompt.shape[-1]
    batch, seq, _ = x_prompt.shape
    dec_batch, dec_seq, _ = x_sample.shape
    p = dict(norm1_g=norm1_g, norm2_g=norm2_g, w_in=_reorder_w_in(w_in), w_out=w_out.astype(BF16),
             dn_conv=dn_conv, dn_a_log=dn_a_log, dn_dt_bias=dn_dt_bias, dn_norm=dn_norm,
             df_q_norm=df_q_norm, df_k_norm=df_k_norm, df_lambda=df_lambda, df_norm=df_norm,
             hy_conv=hy_conv, hy_w1=hy_w1, hy_b1=hy_b1, hy_w2=hy_w2, hy_b2=hy_b2, hy_w3=hy_w3,
             hy_freq=hy_freq, hy_skip=hy_skip, ml_i_bias=ml_i_bias, ml_f_bias=ml_f_bias, ml_norm=ml_norm,
             ffn_w_gate=ffn_w_gate.astype(BF16), ffn_w_up=ffn_w_up.astype(BF16),
             ffn_w_down=ffn_w_down.astype(BF16), moe_router=moe_router,
             moe_w_gate=moe_w_gate.astype(BF16), moe_w_up=moe_w_up.astype(BF16),
             moe_w_down=moe_w_down.astype(BF16))

    n_cond = 1 + dec_batch
    rows = -(-n_cond // 8) * 8
    cond = jnp.concatenate([c_ctx[None, :], c, jnp.zeros((rows - n_cond, d_model), F32)], axis=0)
    mod = _modulation(cond, w_mod, b_mod).reshape(depth, rows, 6, d_model)

    hy_ctx, dft_ctx = _hy_constants(seq), _dft_matrices(seq)
    x = x_prompt
    ctx = []
    for l in range(depth):
        x, out = _layer(l, x, mod[l, 0:1], None, p, hy_ctx, dft_ctx)
        ctx.append(out)
    y_prompt = x
    new_k, new_v, new_s, new_c, new_n, new_m = (jnp.stack([o[i] for o in ctx], axis=1) for i in range(6))
    new_k = new_k.reshape(batch, depth, seq, N_HEADS, 2, DF_DH)
    new_v = new_v.reshape(batch, depth, seq, N_HEADS, HEAD_DIM)

    hy_lat, dft_lat = _hy_constants(dec_seq), _dft_matrices(dec_seq)
    x = x_sample
    for l in range(depth):
        cache = (cache_diff_k[:, l], cache_diff_v[:, l], state_delta[:, l], state_mlstm_c[:, l],
                 state_mlstm_n[:, l], state_mlstm_m[:, l])
        x, _ = _layer(l, x, mod[l, 1:1 + dec_batch], cache, p, hy_lat, dft_lat)
    return (y_prompt, x, new_k, new_v, new_s, new_c, new_n, new_m)
```

```python
import functools
import math
from typing import Any, NamedTuple

import jax
import jax.numpy as jnp
from jax import lax
from jax.experimental import pallas as pl
from jax.experimental.pallas import tpu as pltpu
from jax.experimental.pallas import tpu_sc as plsc

F32 = jnp.float32
BF16 = jnp.bfloat16

N_HEADS = 4
HEAD_DIM = 64
GROUP_W = N_HEADS * HEAD_DIM
DF_DH = 32
CHUNK = 64
GRID_W = 64
ROPE_THETA = 10000.0
HY_BANDS = 8
HY_FAST_DECAY = 0.3
HY_SLOW_DECAY = 1.5
HY_TARGET = 1e-2
HY_SHIFT = 0.05
N_EXPERTS = 8
EPS = 1e-6
NEG = -1e30
LANES = 128
VMEM_LIMIT = 56 * 1024 * 1024

COL_DN, COL_HY, COL_ML, COL_DF = 0, 1, 2, 3
COL_DGATE, COL_MO = 12, 13
COL_SMALL = 28
Z_WIDTH = 29 * LANES
LANE_BETA, LANE_A, LANE_I, LANE_F = 0, 8, 16, 24


def _tile(n, pref):
    t = min(n, pref)
    while n % t:
        t -= LANES if t > LANES else 8
    return t


def _cp(*sem):
    return pltpu.CompilerParams(dimension_semantics=sem, vmem_limit_bytes=VMEM_LIMIT)


def _split3(x):
    x1 = x.astype(BF16)
    r = x - x1.astype(F32)
    x2 = r.astype(BF16)
    r = r - x2.astype(F32)
    return x1, x2, r.astype(BF16)


def _dot(a, b):
    return jnp.dot(a.astype(BF16), b.astype(BF16), preferred_element_type=F32)


def _dot_nt(a, b):
    return lax.dot_general(a.astype(BF16), b.astype(BF16), (((1,), (1,)), ((), ())),
                           preferred_element_type=F32)


def _dot_tn(a, b):
    return lax.dot_general(a.astype(BF16), b.astype(BF16), (((0,), (0,)), ((), ())),
                           preferred_element_type=F32)


def _dot_exact_l(m, x):
    return sum(jnp.dot(m, p, preferred_element_type=F32) for p in _split3(x))


def _dot_exact_r(x, m):
    return sum(jnp.dot(p, m, preferred_element_type=F32) for p in _split3(x))


def _dot_hi(a, b):
    a1, a2, _ = _split3(a)
    b1, b2, _ = _split3(b)
    return (jnp.dot(a1, b1, preferred_element_type=F32) + jnp.dot(a1, b2, preferred_element_type=F32)
            + jnp.dot(a2, b1, preferred_element_type=F32))


def _seg_mean(x, seg):
    w = x.shape[-1]
    sh = int(math.log2(seg))
    r = lax.shift_right_logical(lax.broadcasted_iota(jnp.int32, (w, w), 0), sh)
    c = lax.shift_right_logical(lax.broadcasted_iota(jnp.int32, (w, w), 1), sh)
    bd = jnp.where(r == c, 1.0, 0.0).astype(BF16)
    return _dot_exact_r(x, bd) * (1.0 / seg)


def _row_bcast(col):
    n = col.shape[0]
    return jnp.broadcast_to(jnp.broadcast_to(col, (n, LANES)).T[0:1, :], (n, n))


def _lane_pick(x, lane_idx):
    lane = lax.broadcasted_iota(jnp.int32, x.shape, 1)
    return jnp.sum(jnp.where(lane == lane_idx, x, 0.0), axis=1, keepdims=True)


def _sigmoid(x):
    return 1.0 / (1.0 + jnp.exp(-x))


def _softplus(x):
    return jnp.maximum(x, 0.0) + jnp.log1p(jnp.exp(-jnp.abs(x)))


def _mod_kernel(c_ref, w_ref, b_ref, o_ref):
    c = c_ref[...]
    o_ref[0] = _dot(c * _sigmoid(c), w_ref[0]) + b_ref[0]


def _modulation(cond, w_mod, b_mod):
    depth, d, n = w_mod.shape
    r = cond.shape[0]
    tn = n // 4
    return pl.pallas_call(
        _mod_kernel,
        grid=(depth, n // tn),
        in_specs=[pl.BlockSpec((r, d), lambda l, j: (0, 0)),
                  pl.BlockSpec((1, d, tn), lambda l, j: (l, 0, j)),
                  pl.BlockSpec((1, 1, tn), lambda l, j: (l, 0, j))],
        out_specs=pl.BlockSpec((1, r, tn), lambda l, j: (l, 0, j)),
        out_shape=jax.ShapeDtypeStruct((depth, r, n), F32),
        compiler_params=_cp("parallel", "parallel"),
        name="modulation",
    )(cond, w_mod, b_mod.reshape(depth, 1, n))


def _modulated_norm(x, gain, shift, scale):
    var = jnp.mean(x * x, axis=-1, keepdims=True)
    return x * lax.rsqrt(var + EPS) * gain * (1.0 + scale) + shift


def _proj_in_kernel(x_ref, mod_ref, g_ref, w_ref, o_ref):
    h = _modulated_norm(x_ref[0], g_ref[...], mod_ref[0, 0:1, :], mod_ref[0, 1:2, :])
    o_ref[0] = jnp.dot(h.astype(BF16), w_ref[...], preferred_element_type=F32)


def _proj_in(x, mod, gain, w):
    b, l, d = x.shape
    tm = _tile(l, 256)
    per_batch = mod.shape[0] > 1
    return pl.pallas_call(
        _proj_in_kernel,
        grid=(b, l // tm),
        in_specs=[pl.BlockSpec((1, tm, d), lambda i, j: (i, j, 0)),
                  pl.BlockSpec((1, 6, d), lambda i, j: (i if per_batch else 0, 0, 0)),
                  pl.BlockSpec((1, d), lambda i, j: (0, 0)),
                  pl.BlockSpec((d, Z_WIDTH), lambda i, j: (0, 0))],
        out_specs=pl.BlockSpec((1, tm, Z_WIDTH), lambda i, j: (i, j, 0)),
        out_shape=jax.ShapeDtypeStruct((b, l, Z_WIDTH), F32),
        compiler_params=_cp("parallel", "parallel"),
        name="proj_in",
    )(x, mod, gain.reshape(1, d), w)


def _dwconv_kernel(z_ref, zp_ref, zn_ref, w_ref, o_ref, *lowp_refs, mode):
    i = pl.program_id(1)
    z = z_ref[0]
    tm = z.shape[0]
    prev_row = jnp.where(i > 0, zp_ref[0, 7:8, :], 0.0)
    next_row = jnp.where(i < pl.num_programs(1) - 1, zn_ref[0, 0:1, :], 0.0)
    rid = lax.broadcasted_iota(jnp.int32, z.shape, 0)
    zm1 = jnp.where(rid == 0, prev_row, pltpu.roll(z, 1, 0))
    zp1 = jnp.where(rid == tm - 1, next_row, pltpu.roll(z, tm - 1, 0))
    y = zm1 * w_ref[0:1, :] + z * w_ref[1:2, :] + zp1 * w_ref[2:3, :]
    if mode == "deltanet":
        y = y * _sigmoid(y)
        q, k, v = y[:, :GROUP_W], y[:, GROUP_W:2 * GROUP_W], y[:, 2 * GROUP_W:]
        q = q * lax.rsqrt(_seg_mean(q * q, HEAD_DIM) * HEAD_DIM + EPS) * (HEAD_DIM ** -0.5)
        k = k * lax.rsqrt(_seg_mean(k * k, HEAD_DIM) * HEAD_DIM + EPS)
        o_ref[0, :, 0:GROUP_W] = q
        o_ref[0, :, GROUP_W:2 * GROUP_W] = k
        o_ref[0, :, 2 * GROUP_W:] = v
    else:
        o_ref[0] = y
        lowp_refs[0][0] = y[:, 2 * GROUP_W:].astype(BF16)


def _dwconv(z, w, col_block, mode):
    b, l, _ = z.shape
    c = 3 * GROUP_W
    tm = _tile(l, 512)
    hb = tm // 8
    last = l // 8 - 1
    out_specs = [pl.BlockSpec((1, tm, c), lambda i, j: (i, j, 0))]
    out_shape = [jax.ShapeDtypeStruct((b, l, c), F32)]
    if mode == "plain":
        out_specs.append(pl.BlockSpec((1, tm, GROUP_W), lambda i, j: (i, j, 0)))
        out_shape.append(jax.ShapeDtypeStruct((b, l, GROUP_W), BF16))
    return pl.pallas_call(
        functools.partial(_dwconv_kernel, mode=mode),
        grid=(b, l // tm),
        in_specs=[pl.BlockSpec((1, tm, c), lambda i, j: (i, j, col_block)),
                  pl.BlockSpec((1, 8, c), lambda i, j: (i, jnp.maximum(j * hb - 1, 0), col_block)),
                  pl.BlockSpec((1, 8, c), lambda i, j: (i, jnp.minimum((j + 1) * hb, last), col_block)),
                  pl.BlockSpec((3, c), lambda i, j: (0, 0))],
        out_specs=out_specs,
        out_shape=out_shape,
        compiler_params=_cp("parallel", "parallel"),
        name="dwconv_" + mode,
    )(z, z, z, w)


STACK = N_HEADS * CHUNK


def _stack_heads(x):
    return jnp.concatenate([x] * N_HEADS, axis=0)


def _block_diag(x, head_eq):
    return jnp.where(head_eq, _stack_heads(x), 0.0)


def _fold_heads(x):
    return x[0:CHUNK] + x[CHUNK:2 * CHUNK] + x[2 * CHUNK:3 * CHUNK] + x[3 * CHUNK:]


def _chunk_masks(d):
    row = lax.broadcasted_iota(jnp.int32, (STACK, STACK), 0)
    col = lax.broadcasted_iota(jnp.int32, (STACK, STACK), 1)
    head_eq = lax.shift_right_logical(row, 6) == lax.shift_right_logical(col, 6)
    rel = ((row & (CHUNK - 1)) - (col & (CHUNK - 1))) * (1 - 2 * d)
    r64 = lax.broadcasted_iota(jnp.int32, (CHUNK, CHUNK), 0)
    c64 = lax.broadcasted_iota(jnp.int32, (CHUNK, CHUNK), 1)
    cum = jnp.where((r64 - c64) * (1 - 2 * d) >= 0, 1.0, 0.0).astype(BF16)
    return head_eq, head_eq & (rel >= 0), head_eq & (rel > 0), cum


def _head_cols(x, lane0):
    xs = _stack_heads(x)
    row = lax.broadcasted_iota(jnp.int32, xs.shape, 0)
    lane = lax.broadcasted_iota(jnp.int32, xs.shape, 1)
    return jnp.sum(jnp.where(lane == lane0 + lax.shift_right_logical(row, 6), xs, 0.0), axis=1, keepdims=True)


def _head_lanes(x, lane0):
    src = lax.broadcasted_iota(jnp.int32, (LANES, GROUP_W), 0)
    dst = lax.broadcasted_iota(jnp.int32, (LANES, GROUP_W), 1)
    expand = jnp.where(src == lane0 + lax.shift_right_logical(dst, 6), 1.0, 0.0).astype(BF16)
    return _dot_exact_r(x, expand)


def _lanes_to_col(x):
    return jnp.broadcast_to(x, (LANES, GROUP_W)).T[:, 0:1]


def _load_diag_state(dst_ref, src_ref):
    dst_ref[...] = jnp.zeros(dst_ref.shape, F32)
    for r in range(dst_ref.shape[0]):
        for d in range(2):
            for h in range(N_HEADS):
                lo, hi = h * HEAD_DIM, (h + 1) * HEAD_DIM
                dst_ref[r, d, lo:hi, lo:hi] = src_ref[r, d, h]


def _store_diag_state(dst_ref, src_ref):
    for r in range(src_ref.shape[0]):
        for d in range(2):
            for h in range(N_HEADS):
                lo, hi = h * HEAD_DIM, (h + 1) * HEAD_DIM
                dst_ref[r, d, h] = src_ref[r, d, lo:hi, lo:hi]


def _rec_specs(b, l):
    rows = max(r for r in (4, 2, 1) if b % r == 0)
    tile = _tile(l, 512)
    nt = l // tile
    fwd = lambda width, col: pl.BlockSpec((rows, tile, width), lambda i, j: (i, j, col))
    bwd = lambda width, col: pl.BlockSpec((rows, tile, width), lambda i, j: (i, nt - 1 - j, col))
    state = pl.BlockSpec((rows, 2, N_HEADS, HEAD_DIM, HEAD_DIM), lambda i, j: (i, 0, 0, 0, 0))
    return rows, tile, nt, fwd, bwd, state


class _Chain(NamedTuple):
    r: int
    d: int
    x_ref: Any
    sm_ref: Any
    o_ref: Any
    rows: Any


def _each(fn, *cols):
    return [fn(*args) for args in zip(*cols)]


def _unit_tri_inverses(mats):
    n = mats[0].shape[0]
    row = lax.broadcasted_iota(jnp.int32, (n, n), 0)
    col = lax.broadcasted_iota(jnp.int32, (n, n), 1)

    def same_block(log2_size):
        return lax.shift_right_logical(row, log2_size) == lax.shift_right_logical(col, log2_size)

    p = [jnp.where(same_block(3), -a, 0.0) for a in mats]
    t = [jnp.where(row == col, 1.0, 0.0) + x for x in p]
    for _ in range(2):
        p = _each(_dot, p, p)
        t = _each(jnp.add, t, _each(_dot, t, p))
    for log2_size in range(4, int(math.log2(CHUNK)) + 1):
        level = same_block(log2_size) & jnp.logical_not(same_block(log2_size - 1))
        off_t = [_dot(jnp.where(level, a, 0.0), x) for a, x in zip(mats, t)]
        t = _each(jnp.subtract, t, _each(_dot, t, off_t))
    return t


def _deltanet_chunks(chains, par_ref, s_ref):
    masks = {d: _chunk_masks(d) for d in (0, 1)}
    head_eq = masks[0][0]
    incl = [masks[c.d][1] for c in chains]
    strict = [masks[c.d][2] for c in chains]
    cum = [masks[c.d][3] for c in chains]
    lane_a = [LANE_A + c.d * N_HEADS for c in chains]
    lane_b = [LANE_BETA + c.d * N_HEADS for c in chains]
    q = [c.x_ref[c.r, c.rows, 0:GROUP_W] for c in chains]
    k = [c.x_ref[c.r, c.rows, GROUP_W:2 * GROUP_W] for c in chains]
    v = [c.x_ref[c.r, c.rows, 2 * GROUP_W:3 * GROUP_W] for c in chains]
    sm = [c.sm_ref[c.r, c.rows, :] for c in chains]
    beta_all = _each(_sigmoid, sm)
    g_all = [-jnp.exp(par_ref[0:1, :]) * _softplus(x + par_ref[1:2, :]) for x in sm]
    gc_all = _each(_dot_exact_l, cum, g_all)
    gc = _each(_head_lanes, gc_all, lane_a)
    beta = _each(_head_lanes, beta_all, lane_b)
    g_last = [x[CHUNK - 1:CHUNK] if c.d == 0 else x[0:1] for x, c in zip(gc, chains)]
    gc_col = _each(_head_cols, gc_all, lane_a)
    beta_col = _each(_head_cols, beta_all, lane_b)
    gc_row = _each(_row_bcast, gc_col)
    decay = [jnp.exp(jnp.where(m, x - y, NEG)) for m, x, y in zip(incl, gc_col, gc_row)]
    k_rows = _each(_stack_heads, k)
    kk = [_dot_nt(jnp.where(head_eq, x, 0.0), x) for x in k_rows]
    a = [jnp.where(m, b * x * dc, 0.0) for m, b, x, dc in zip(strict, beta_col, kk, decay)]
    t = _unit_tri_inverses(a)
    egc = _each(jnp.exp, gc)
    u = [_fold_heads(_dot(x, _block_diag(y * b, head_eq))) for x, y, b in zip(t, v, beta)]
    w = [_fold_heads(_dot(x, _block_diag(y * (b * e), head_eq))) for x, y, b, e in zip(t, k, beta, egc)]
    s = [s_ref[c.r, c.d] for c in chains]
    ws_qs = [_dot(jnp.concatenate([x, y * e], axis=0), z) for x, y, e, z in zip(w, q, egc, s)]
    v_new = [x - y[0:CHUNK] for x, y in zip(u, ws_qs)]
    qk = [_dot_nt(_block_diag(x, head_eq), y) * dc for x, y, dc in zip(q, k_rows, decay)]
    o_intra = [_fold_heads(_dot(x, _block_diag(y, head_eq))) for x, y in zip(qk, v_new)]
    s_add = [_dot_tn(x * jnp.exp(gl - g), y) for x, gl, g, y in zip(k, g_last, gc, v_new)]
    for c, x, y, z, gl, sa in zip(chains, ws_qs, o_intra, s, g_last, s_add):
        c.o_ref[c.r, c.rows, :] = x[CHUNK:] + y
        s_ref[c.r, c.d] = z * jnp.exp(gl) + jnp.where(head_eq, sa, 0.0)


def _chunk_chains(n_rows, n_chunks, c, f_refs, b_refs):
    rows_f = pl.ds(pl.multiple_of(c * CHUNK, CHUNK), CHUNK)
    rows_b = pl.ds(pl.multiple_of((n_chunks - 1 - c) * CHUNK, CHUNK), CHUNK)
    return [_Chain(r, d, *refs, rows) for r in range(n_rows)
            for d, refs, rows in ((0, f_refs, rows_f), (1, b_refs, rows_b))]


def _deltanet_kernel(qf_ref, smf_ref, qb_ref, smb_ref, par_ref, s0_ref, of_ref, ob_ref, sout_ref, s_ref, *,
                     n_chunks):
    j = pl.program_id(1)

    @pl.when(j == 0)
    def _():
        _load_diag_state(s_ref, s0_ref)

    def chunk_body(c, carry):
        chains = _chunk_chains(s_ref.shape[0], n_chunks, c, (qf_ref, smf_ref, of_ref), (qb_ref, smb_ref, ob_ref))
        _deltanet_chunks(chains, par_ref, s_ref)
        return carry

    lax.fori_loop(0, n_chunks, chunk_body, 0)

    @pl.when(j == pl.num_programs(1) - 1)
    def _():
        _store_diag_state(sout_ref, s_ref)


def _deltanet(qkv, z, par, s0):
    b, l, _ = qkv.shape
    rows, tile, nt, fwd, bwd, state = _rec_specs(b, l)
    return pl.pallas_call(
        functools.partial(_deltanet_kernel, n_chunks=tile // CHUNK),
        grid=(b // rows, nt),
        in_specs=[fwd(3 * GROUP_W, 0), fwd(LANES, COL_SMALL), bwd(3 * GROUP_W, 0), bwd(LANES, COL_SMALL),
                  pl.BlockSpec((8, LANES), lambda i, j: (0, 0)), state],
        out_specs=[fwd(GROUP_W, 0), bwd(GROUP_W, 0), state],
        out_shape=[jax.ShapeDtypeStruct((b, l, GROUP_W), F32), jax.ShapeDtypeStruct((b, l, GROUP_W), F32),
                   jax.ShapeDtypeStruct((b, 2, N_HEADS, HEAD_DIM, HEAD_DIM), F32)],
        scratch_shapes=[pltpu.VMEM((rows, 2, STACK, STACK), F32)],
        compiler_params=_cp("parallel", "arbitrary"),
        name="deltanet",
    )(qkv, z, qkv, z, par, s0)


def _mlstm_chunks(chains, par_ref, c_ref, n_ref, m_ref):
    masks = {d: _chunk_masks(d) for d in (0, 1)}
    head_eq = masks[0][0]
    incl = [masks[c.d][1] for c in chains]
    cum = [masks[c.d][3] for c in chains]
    lane_i = [LANE_I + c.d * N_HEADS for c in chains]
    lane_f = [LANE_F + c.d * N_HEADS for c in chains]
    q = [c.x_ref[c.r, c.rows, 0:GROUP_W] for c in chains]
    k = [c.x_ref[c.r, c.rows, GROUP_W:2 * GROUP_W] * (HEAD_DIM ** -0.5) for c in chains]
    v = [c.x_ref[c.r, c.rows, 2 * GROUP_W:3 * GROUP_W] for c in chains]
    sm = [c.sm_ref[c.r, c.rows, :] for c in chains]
    i_all = [x + par_ref[0:1, :] for x in sm]
    f_all = [-_softplus(-(x + par_ref[1:2, :])) for x in sm]
    bc_all = _each(_dot_exact_l, cum, f_all)
    ic = _each(_head_lanes, i_all, lane_i)
    bc = _each(_head_lanes, bc_all, lane_f)
    b_tot = [x[CHUNK - 1:CHUNK] if c.d == 0 else x[0:1] for x, c in zip(bc, chains)]
    ic_col = _each(_head_cols, i_all, lane_i)
    bc_col = _each(_head_cols, bc_all, lane_f)
    c_s = [c_ref[c.r, c.d] for c in chains]
    n_s = [n_ref[c.r, c.d] for c in chains]
    m_s = [m_ref[c.r, c.d] for c in chains]
    w_row = _each(_row_bcast, _each(jnp.subtract, ic_col, bc_col))
    dlog = [jnp.where(m, x + y, NEG) for m, x, y in zip(incl, bc_col, w_row)]
    a = _each(jnp.add, bc_col, _each(_lanes_to_col, m_s))
    m_t = [jnp.maximum(x, jnp.max(y, axis=1, keepdims=True)) for x, y in zip(a, dlog)]
    inter = [jnp.exp(x - y) for x, y in zip(a, m_t)]
    q_bd = [_block_diag(x, head_eq) for x in q]
    s = [_dot_nt(x, _stack_heads(y)) * jnp.exp(dl - mt) for x, y, dl, mt in zip(q_bd, k, dlog, m_t)]
    inter_part = _each(_dot, q_bd, c_s)
    intra_part = [_dot(x, _block_diag(y, head_eq)) for x, y in zip(s, v)]
    den = [it * jnp.sum(x * n, axis=1, keepdims=True) + jnp.sum(y, axis=1, keepdims=True)
           for it, x, n, y in zip(inter, q_bd, n_s, s)]
    wend = [bt - x + y for bt, x, y in zip(b_tot, bc, ic)]
    a_end = _each(jnp.add, b_tot, m_s)
    m_new = [jnp.maximum(x, jnp.max(y, axis=0, keepdims=True)) for x, y in zip(a_end, wend)]
    dec = [jnp.exp(x - y) for x, y in zip(a_end, m_new)]
    kw = [x * jnp.exp(y - z) for x, y, z in zip(k, wend, m_new)]
    c_add = _each(_dot_tn, kw, v)
    for i, c in enumerate(chains):
        num = inter[i] * inter_part[i] + intra_part[i]
        c.o_ref[c.r, c.rows, :] = _fold_heads(num / jnp.maximum(jnp.abs(den[i]), jnp.exp(-m_t[i])))
        c_ref[c.r, c.d] = dec[i] * c_s[i] + jnp.where(head_eq, c_add[i], 0.0)
        n_ref[c.r, c.d] = dec[i] * n_s[i] + jnp.sum(kw[i], axis=0, keepdims=True)
        m_ref[c.r, c.d] = m_new[i]


def _mlstm_kernel(zf_ref, smf_ref, zb_ref, smb_ref, par_ref, c0_ref, n0_ref, m0_ref,
                  of_ref, ob_ref, cout_ref, nout_ref, mout_ref, c_ref, n_ref, m_ref, *, n_chunks):
    j = pl.program_id(1)

    @pl.when(j == 0)
    def _():
        _load_diag_state(c_ref, c0_ref)
        n_ref[...] = n0_ref[...]
        m_ref[...] = m0_ref[...]

    def chunk_body(c, carry):
        chains = _chunk_chains(c_ref.shape[0], n_chunks, c, (zf_ref, smf_ref, of_ref), (zb_ref, smb_ref, ob_ref))
        _mlstm_chunks(chains, par_ref, c_ref, n_ref, m_ref)
        return carry

    lax.fori_loop(0, n_chunks, chunk_body, 0)

    @pl.when(j == pl.num_programs(1) - 1)
    def _():
        _store_diag_state(cout_ref, c_ref)
        nout_ref[...] = n_ref[...]
        mout_ref[...] = m_ref[...]


def _mlstm(z, par, c0, n0, m0):
    b, l, _ = z.shape
    rows, tile, nt, fwd, bwd, state = _rec_specs(b, l)
    vec = pl.BlockSpec((rows, 2, 1, GROUP_W), lambda i, j: (i, 0, 0, 0))
    return pl.pallas_call(
        functools.partial(_mlstm_kernel, n_chunks=tile // CHUNK),
        grid=(b // rows, nt),
        in_specs=[fwd(3 * GROUP_W, COL_ML), fwd(LANES, COL_SMALL), bwd(3 * GROUP_W, COL_ML), bwd(LANES, COL_SMALL),
                  pl.BlockSpec((8, LANES), lambda i, j: (0, 0)), state, vec, vec],
        out_specs=[fwd(GROUP_W, 0), bwd(GROUP_W, 0), state, vec, vec],
        out_shape=[jax.ShapeDtypeStruct((b, l, GROUP_W), F32), jax.ShapeDtypeStruct((b, l, GROUP_W), F32),
                   jax.ShapeDtypeStruct((b, 2, N_HEADS, HEAD_DIM, HEAD_DIM), F32),
                   jax.ShapeDtypeStruct((b, 2, 1, GROUP_W), F32), jax.ShapeDtypeStruct((b, 2, 1, GROUP_W), F32)],
        scratch_shapes=[pltpu.VMEM((rows, 2, STACK, STACK), F32), pltpu.VMEM((rows, 2, 1, GROUP_W), F32),
                        pltpu.VMEM((rows, 2, 1, GROUP_W), F32)],
        compiler_params=_cp("parallel", "arbitrary"),
        name="mlstm",
    )(z, z, z, z, par, c0, n0, m0)


def _diff_prep_kernel(z_ref, gq_ref, gk_ref, cos_ref, sin_ref, q_ref, kt_ref, v_ref, kd_ref, *, rope):
    z = z_ref[0]
    q, k, v = z[:, :GROUP_W], z[:, GROUP_W:2 * GROUP_W], z[:, 2 * GROUP_W:]
    q = q * lax.rsqrt(_seg_mean(q * q, DF_DH) + EPS) * gq_ref[...]
    k = k * lax.rsqrt(_seg_mean(k * k, DF_DH) + EPS) * gk_ref[...]
    kd_ref[0] = k
    if rope:
        lane = lax.broadcasted_iota(jnp.int32, q.shape, 1)
        first = (lane & (DF_DH - 1)) < DF_DH // 2

        def rot(x):
            swapped = jnp.where(first, pltpu.roll(x, GROUP_W - DF_DH // 2, 1), pltpu.roll(x, DF_DH // 2, 1))
            return x * cos_ref[...] + swapped * sin_ref[...]

        q, k = rot(q), rot(k)
    _store_attn_operands(q * (DF_DH ** -0.5), k, v, q_ref, kt_ref, v_ref)


def _store_attn_operands(q, k, v, q_ref, kt_ref, v_ref):
    kt = k.T
    lane = lax.broadcasted_iota(jnp.int32, (k.shape[0], HEAD_DIM), 1)
    ones = jnp.ones((k.shape[0], HEAD_DIM), BF16)
    for h in range(N_HEADS):
        lo, hi = h * HEAD_DIM, (h + 1) * HEAD_DIM
        if q is not None:
            q_ref[0, h, 0] = jnp.where(lane < DF_DH, q[:, lo:hi], 0.0).astype(BF16)
            q_ref[0, h, 1] = jnp.where(lane >= DF_DH, q[:, lo:hi], 0.0).astype(BF16)
        v_ref[0, h] = jnp.concatenate([v[:, lo:hi].astype(BF16), ones], axis=1)
        kt_ref[0, h] = kt[lo:hi, :].astype(BF16)


def _cache_prep_kernel(k_ref, v_ref, kt_ref, vh_ref):
    _store_attn_operands(None, k_ref[0], v_ref[0], None, kt_ref, vh_ref)


def _rope_tables(l):
    rows = l // GRID_W
    r = jnp.repeat(jnp.arange(rows, dtype=F32), GRID_W)
    col = jnp.tile(jnp.arange(GRID_W, dtype=F32), rows)
    n_freq = DF_DH // 4
    inv = ROPE_THETA ** (-jnp.arange(n_freq, dtype=F32) / n_freq)
    ang = jnp.concatenate([r[:, None] * inv, col[:, None] * inv], axis=-1)
    cos, sin = jnp.cos(ang), jnp.sin(ang)
    reps = GROUP_W // DF_DH
    return (jnp.tile(jnp.concatenate([cos, cos], axis=-1), (1, reps)),
            jnp.tile(jnp.concatenate([-sin, sin], axis=-1), (1, reps)))


def _diff_prep(z, gq, gk, rope):
    b, l, _ = z.shape
    tm = _tile(l, 512)
    if rope:
        cos, sin = _rope_tables(l)
    else:
        cos = sin = jnp.zeros((l, GROUP_W), F32)
    return pl.pallas_call(
        functools.partial(_diff_prep_kernel, rope=rope),
        grid=(b, l // tm),
        in_specs=[pl.BlockSpec((1, tm, 3 * GROUP_W), lambda i, j: (i, j, COL_DF)),
                  pl.BlockSpec((1, GROUP_W), lambda i, j: (0, 0)),
                  pl.BlockSpec((1, GROUP_W), lambda i, j: (0, 0)),
                  pl.BlockSpec((tm, GROUP_W), lambda i, j: (j, 0)),
                  pl.BlockSpec((tm, GROUP_W), lambda i, j: (j, 0))],
        out_specs=[pl.BlockSpec((1, N_HEADS, 2, tm, HEAD_DIM), lambda i, j: (i, 0, 0, j, 0)),
                   pl.BlockSpec((1, N_HEADS, HEAD_DIM, tm), lambda i, j: (i, 0, 0, j)),
                   pl.BlockSpec((1, N_HEADS, tm, 2 * HEAD_DIM), lambda i, j: (i, 0, j, 0)),
                   pl.BlockSpec((1, tm, GROUP_W), lambda i, j: (i, j, 0))],
        out_shape=[jax.ShapeDtypeStruct((b, N_HEADS, 2, l, HEAD_DIM), BF16),
                   jax.ShapeDtypeStruct((b, N_HEADS, HEAD_DIM, l), BF16),
                   jax.ShapeDtypeStruct((b, N_HEADS, l, 2 * HEAD_DIM), BF16),
                   jax.ShapeDtypeStruct((b, l, GROUP_W), F32)],
        compiler_params=_cp("parallel", "parallel"),
        name="diff_prep",
    )(z, jnp.tile(gq, GROUP_W // DF_DH).reshape(1, GROUP_W), jnp.tile(gk, GROUP_W // DF_DH).reshape(1, GROUP_W),
      cos, sin)


def _cache_prep(ck, cv):
    b, p, _ = ck.shape
    tm = _tile(p, 512)
    return pl.pallas_call(
        _cache_prep_kernel,
        grid=(b, p // tm),
        in_specs=[pl.BlockSpec((1, tm, GROUP_W), lambda i, j: (i, j, 0)),
                  pl.BlockSpec((1, tm, GROUP_W), lambda i, j: (i, j, 0))],
        out_specs=[pl.BlockSpec((1, N_HEADS, HEAD_DIM, tm), lambda i, j: (i, 0, 0, j)),
                   pl.BlockSpec((1, N_HEADS, tm, 2 * HEAD_DIM), lambda i, j: (i, 0, j, 0))],
        out_shape=[jax.ShapeDtypeStruct((b, N_HEADS, HEAD_DIM, p), BF16),
                   jax.ShapeDtypeStruct((b, N_HEADS, p, 2 * HEAD_DIM), BF16)],
        compiler_params=_cp("parallel", "parallel"),
        name="cache_prep",
    )(ck, cv)


def _diff_attn_kernel(q_ref, kt_ref, v_ref, lam_ref, g_ref, o_ref, m_ref, acc_ref, *, lam_init):
    ik = pl.program_id(3)

    @pl.when(ik == 0)
    def _():
        m_ref[...] = jnp.full(m_ref.shape, NEG, F32)
        acc_ref[...] = jnp.zeros(acc_ref.shape, F32)

    chains = [(h, m) for h in range(q_ref.shape[1]) for m in range(2)]
    s = [jnp.dot(q_ref[0, h, m], kt_ref[0, h], preferred_element_type=F32) for h, m in chains]
    m_old = [m_ref[h, m] for h, m in chains]
    m_new = [jnp.maximum(x, jnp.max(y, axis=1, keepdims=True)) for x, y in zip(m_old, s)]
    p = [jnp.exp(x - y[:, 0:1]).astype(BF16) for x, y in zip(s, m_new)]
    pv = [jnp.dot(x, v_ref[0, h], preferred_element_type=F32) for x, (h, m) in zip(p, chains)]
    for i, (h, m) in enumerate(chains):
        acc_ref[h, m] = jnp.exp(m_old[i] - m_new[i]) * acc_ref[h, m] + pv[i]
        m_ref[h, m] = m_new[i]

    @pl.when(ik == pl.num_programs(3) - 1)
    def _():
        lp = lam_ref[...]
        lam = (jnp.exp(jnp.sum(lp[0:1] * lp[1:2], axis=1, keepdims=True))
               - jnp.exp(jnp.sum(lp[2:3] * lp[3:4], axis=1, keepdims=True)) + lam_init)
        for h in range(q_ref.shape[1]):
            a0, a1 = acc_ref[h, 0], acc_ref[h, 1]
            o = (a0[:, :HEAD_DIM] / a0[:, HEAD_DIM:HEAD_DIM + 1]
                 - lam * (a1[:, :HEAD_DIM] / a1[:, HEAD_DIM:HEAD_DIM + 1]))
            var = jnp.mean(o * o, axis=1, keepdims=True)
            o_ref[0, h] = o * lax.rsqrt(var + EPS) * g_ref[...] * (1.0 - lam_init)


ATTN_HEADS_PER_STEP = 2


def _diff_attn(q, kt, v, lam_par, gain, lam_init):
    b, _, _, l, _ = q.shape
    lk = kt.shape[-1]
    hs = ATTN_HEADS_PER_STEP
    tq = _tile(l, 512)
    tk = _tile(lk, 2304)
    return pl.pallas_call(
        functools.partial(_diff_attn_kernel, lam_init=lam_init),
        grid=(b, N_HEADS // hs, l // tq, lk // tk),
        in_specs=[pl.BlockSpec((1, hs, 2, tq, HEAD_DIM), lambda i, h, a, c: (i, h, 0, a, 0)),
                  pl.BlockSpec((1, hs, HEAD_DIM, tk), lambda i, h, a, c: (i, h, 0, c)),
                  pl.BlockSpec((1, hs, tk, 2 * HEAD_DIM), lambda i, h, a, c: (i, h, c, 0)),
                  pl.BlockSpec((4, DF_DH), lambda i, h, a, c: (0, 0)),
                  pl.BlockSpec((1, HEAD_DIM), lambda i, h, a, c: (0, 0))],
        out_specs=pl.BlockSpec((1, hs, tq, HEAD_DIM), lambda i, h, a, c: (i, h, a, 0)),
        out_shape=jax.ShapeDtypeStruct((b, N_HEADS, l, HEAD_DIM), F32),
        scratch_shapes=[pltpu.VMEM((hs, 2, tq, LANES), F32), pltpu.VMEM((hs, 2, tq, 2 * HEAD_DIM), F32)],
        compiler_params=_cp("parallel", "parallel", "parallel", "arbitrary"),
        name="diff_attn",
    )(q, kt, v, lam_par, gain.reshape(1, HEAD_DIM))


def _hy_filter_kernel(feat_ref, win_ref, w1_ref, b1_ref, w2_ref, b2_ref, w3_ref, fr_ref, o_ref):
    h = jnp.sin(fr_ref[0:1, :] * (_dot_hi(feat_ref[...], w1_ref[...]) + b1_ref[...]))
    h = jnp.sin(fr_ref[1:2, :] * (_dot_hi(h, w2_ref[...]) + b2_ref[...]))
    h = _dot_hi(h, w3_ref[...]) * win_ref[...]
    o_ref[...] = h * lax.rsqrt(jnp.sum(h * h, axis=0, keepdims=True) + EPS)


def _hy_constants(l):
    pos = jnp.arange(l, dtype=F32)
    bands = jnp.arange(1, HY_BANDS + 1, dtype=F32)
    ang = (2.0 * math.pi / l) * pos[:, None] * bands[None, :]
    feat = jnp.concatenate([pos[:, None] / l, jnp.cos(ang), jnp.sin(ang)], axis=-1)
    feat = jnp.pad(feat, ((0, 0), (0, LANES - feat.shape[1])))
    rates = jnp.linspace(-math.log(HY_TARGET) / HY_FAST_DECAY, -math.log(HY_TARGET) / HY_SLOW_DECAY,
                         GROUP_W, dtype=F32)
    rates = jnp.tile(rates, 2)
    offset = jnp.abs(pos - l // 2) / l
    return feat, jnp.exp(-offset[:, None] * rates[None, :]) + HY_SHIFT


def _hy_filters(feat, win, w1, b1, w2, b2, w3, freq):
    l = feat.shape[0]
    hid = w2.shape[0]
    w1p = jnp.pad(w1, ((0, LANES - w1.shape[0]), (0, 0)))
    full = lambda shape: pl.BlockSpec(shape, lambda o: (0, 0))
    return pl.pallas_call(
        _hy_filter_kernel,
        grid=(2,),
        in_specs=[full((l, LANES)), pl.BlockSpec((l, GROUP_W), lambda o: (0, o)),
                  full((LANES, hid)), full((1, hid)), full((hid, hid)), full((1, hid)),
                  pl.BlockSpec((hid, GROUP_W), lambda o: (0, o)), full((2, hid))],
        out_specs=pl.BlockSpec((l, GROUP_W), lambda o: (0, o)),
        out_shape=jax.ShapeDtypeStruct((l, 2 * GROUP_W), F32),
        compiler_params=_cp("parallel"),
        name="hy_filters",
    )(feat, win, w1p, b1.reshape(1, hid), w2, b2.reshape(1, hid), w3, freq)


def _dft_matrices(l):
    n = 2 * l
    k = jnp.arange(l, dtype=jnp.int32)
    alt = jnp.where(k % 2 == 0, 1.0, -1.0).astype(F32)
    cos_f, sin_f = _cos_sin_grid(0, l, n)
    fwd_im = jnp.where(k[:, None] == 0, alt[None, :], -sin_f)
    fwd = jnp.stack([cos_f, fwd_im]).astype(BF16)
    t = k + l // 2
    cos_i, sin_i = _cos_sin_grid(l // 2, l, n)
    alt_t = jnp.where(t % 2 == 0, 1.0, -1.0).astype(F32)
    inv_re = jnp.where(k[None, :] == 0, 1.0 / n, (2.0 / n) * cos_i)
    inv_im = jnp.where(k[None, :] == 0, alt_t[:, None] / n, -(2.0 / n) * sin_i)
    inv = jnp.concatenate([inv_re, inv_im], axis=1).astype(BF16)
    return fwd, inv


def _cos_sin_grid(row0, l, n):
    s = 1 << (int(math.log2(l)) // 2)
    assert row0 % s == 0 and l % s == 0
    c = jnp.arange(l, dtype=jnp.int32)
    hi = row0 + jnp.arange(l // s, dtype=jnp.int32) * s
    lo = jnp.arange(s, dtype=jnp.int32)

    def angle(r):
        return (2.0 * math.pi / n) * ((r[:, None] * c[None, :]) % n).astype(F32)

    a, b = angle(hi), angle(lo)
    ca, sa = jnp.cos(a)[:, None, :], jnp.sin(a)[:, None, :]
    cb, sb = jnp.cos(b)[None, :, :], jnp.sin(b)[None, :, :]
    return (ca * cb - sa * sb).reshape(l, l), (sa * cb + ca * sb).reshape(l, l)


def _dft_filter_kernel(f_ref, h_ref, o_ref):
    h = h_ref[...].astype(BF16)
    o_ref[0] = jnp.dot(f_ref[0], h, preferred_element_type=F32)
    o_ref[1] = jnp.dot(f_ref[1], h, preferred_element_type=F32)


def _dft_filter(fwd, filt):
    l, c = filt.shape
    tm = _tile(l, 256)
    return pl.pallas_call(
        _dft_filter_kernel,
        grid=(l // tm,),
        in_specs=[pl.BlockSpec((2, tm, l), lambda i: (0, i, 0)), pl.BlockSpec((l, c), lambda i: (0, 0))],
        out_specs=pl.BlockSpec((2, tm, c), lambda i: (0, i, 0)),
        out_shape=jax.ShapeDtypeStruct((2, l, c), F32),
        compiler_params=_cp("parallel"),
        name="dft_filter",
    )(fwd, filt)


def _dft_fwd_mul_kernel(f_ref, u_ref, h_ref, o_ref):
    u = u_ref[0]
    ur = jnp.dot(f_ref[0], u, preferred_element_type=F32)
    ui = jnp.dot(f_ref[1], u, preferred_element_type=F32)
    hr, hi = h_ref[0], h_ref[1]
    row = lax.broadcasted_iota(jnp.int32, ur.shape, 0) + pl.program_id(0) * ur.shape[0]
    packed = row == 0
    o_ref[0, 0] = (ur * hr - jnp.where(packed, 0.0, ui * hi)).astype(BF16)
    o_ref[0, 1] = jnp.where(packed, ui * hi, ur * hi + ui * hr).astype(BF16)


def _dft_fwd_mul(fwd, u, hf, order):
    b, l, _ = u.shape
    tm = _tile(l, 512)
    return pl.pallas_call(
        _dft_fwd_mul_kernel,
        grid=(l // tm, b),
        in_specs=[pl.BlockSpec((2, tm, l), lambda i, j: (0, i, 0)),
                  pl.BlockSpec((1, l, GROUP_W), lambda i, j: (j, 0, 0)),
                  pl.BlockSpec((2, tm, GROUP_W), lambda i, j: (0, i, order))],
        out_specs=pl.BlockSpec((1, 2, tm, GROUP_W), lambda i, j: (j, 0, i, 0)),
        out_shape=jax.ShapeDtypeStruct((b, 2, l, GROUP_W), BF16),
        compiler_params=_cp("parallel", "parallel"),
        name="dft_fwd_mul",
    )(fwd, u, hf)


def _dft_inv_gate_kernel(g_ref, y_ref, x_ref, u_ref, skip_ref, o_ref, lowp_ref):
    conv = jnp.dot(g_ref[...], y_ref[0], preferred_element_type=F32)
    out = x_ref[0] * (conv + skip_ref[...] * u_ref[0])
    o_ref[0] = out
    lowp_ref[0] = out.astype(BF16)


def _dft_inv_gate(inv, y, xg, x_col, u, u_col, skip):
    b, _, l, _ = y.shape
    tm = _tile(l, 512)
    out_spec = pl.BlockSpec((1, tm, GROUP_W), lambda i, j: (j, i, 0))
    return pl.pallas_call(
        _dft_inv_gate_kernel,
        grid=(l // tm, b),
        in_specs=[pl.BlockSpec((tm, 2 * l), lambda i, j: (i, 0)),
                  pl.BlockSpec((1, 2 * l, GROUP_W), lambda i, j: (j, 0, 0)),
                  pl.BlockSpec((1, tm, GROUP_W), lambda i, j: (j, i, x_col)),
                  pl.BlockSpec((1, tm, GROUP_W), lambda i, j: (j, i, u_col)),
                  pl.BlockSpec((1, GROUP_W), lambda i, j: (0, 0))],
        out_specs=[out_spec, out_spec],
        out_shape=[jax.ShapeDtypeStruct((b, l, GROUP_W), F32), jax.ShapeDtypeStruct((b, l, GROUP_W), BF16)],
        compiler_params=_cp("parallel", "parallel"),
        name="dft_inv_gate",
    )(inv, y.reshape(b, 2 * l, GROUP_W), xg, u, skip.reshape(1, GROUP_W))


def _head_norm(x, gain):
    return x * lax.rsqrt(_seg_mean(x * x, HEAD_DIM) + EPS) * gain


def _proj_out_kernel(dnf_ref, dnb_ref, gate_ref, df_ref, hy_ref, mlf_ref, mlb_ref, mo_ref, x_ref, mod_ref,
                     gdn_ref, gml_ref, w_ref, o_ref):
    gate = gate_ref[0]
    y_dn = _head_norm(dnf_ref[0] + dnb_ref[0], gdn_ref[...]) * (gate * _sigmoid(gate))
    y_ml = _head_norm(mlf_ref[0] + mlb_ref[0], gml_ref[...]) * _sigmoid(mo_ref[0])
    y = _dot(y_dn, w_ref[0:GROUP_W, :])
    for h in range(N_HEADS):
        lo = GROUP_W + h * HEAD_DIM
        y = y + _dot(df_ref[0, h], w_ref[lo:lo + HEAD_DIM, :])
    y = y + _dot(hy_ref[0], w_ref[2 * GROUP_W:3 * GROUP_W, :])
    y = y + _dot(y_ml, w_ref[3 * GROUP_W:, :])
    o_ref[0] = x_ref[0] + mod_ref[0, 2:3, :] * y


def _proj_out(o_dn, z, o_df, y_hy, h_ml, x, mod, g_dn, g_ml, w):
    b, l, d = x.shape
    tm = _tile(l, 512)
    per_batch = mod.shape[0] > 1
    group = lambda col: pl.BlockSpec((1, tm, GROUP_W), lambda i, j: (i, j, col))
    return pl.pallas_call(
        _proj_out_kernel,
        grid=(b, l // tm),
        in_specs=[group(0), group(0), group(COL_DGATE),
                  pl.BlockSpec((1, N_HEADS, tm, HEAD_DIM), lambda i, j: (i, 0, j, 0)),
                  group(0), group(0), group(0), group(COL_MO),
                  pl.BlockSpec((1, tm, d), lambda i, j: (i, j, 0)),
                  pl.BlockSpec((1, 6, d), lambda i, j: (i if per_batch else 0, 0, 0)),
                  pl.BlockSpec((1, GROUP_W), lambda i, j: (0, 0)),
                  pl.BlockSpec((1, GROUP_W), lambda i, j: (0, 0)),
                  pl.BlockSpec((d, d), lambda i, j: (0, 0))],
        out_specs=pl.BlockSpec((1, tm, d), lambda i, j: (i, j, 0)),
        out_shape=jax.ShapeDtypeStruct((b, l, d), F32),
        compiler_params=_cp("parallel", "parallel"),
        name="proj_out",
    )(o_dn[0], o_dn[1], z, o_df, y_hy, h_ml[0], h_ml[1], z, x, mod, jnp.tile(g_dn, N_HEADS).reshape(1, GROUP_W),
      jnp.tile(g_ml, N_HEADS).reshape(1, GROUP_W), w)


def _ffn_kernel(x_ref, mod_ref, g_ref, wg_ref, wu_ref, wd_ref, o_ref, h_ref, acc_ref):
    j = pl.program_id(2)

    @pl.when(j == 0)
    def _():
        h_ref[...] = _modulated_norm(x_ref[0], g_ref[...], mod_ref[0, 3:4, :], mod_ref[0, 4:5, :]).astype(BF16)
        acc_ref[...] = jnp.zeros(acc_ref.shape, F32)

    h = h_ref[...]
    g = jnp.dot(h, wg_ref[...], preferred_element_type=F32)
    u = jnp.dot(h, wu_ref[...], preferred_element_type=F32)
    acc_ref[...] += jnp.dot((g * _sigmoid(g) * u).astype(BF16), wd_ref[...], preferred_element_type=F32)

    @pl.when(j == pl.num_programs(2) - 1)
    def _():
        o_ref[0] = x_ref[0] + mod_ref[0, 5:6, :] * acc_ref[...]


def _ffn(x, mod, gain, wg, wu, wd):
    b, l, d = x.shape
    f = wg.shape[1]
    tm = _tile(l, 1024)
    tf = 256
    per_batch = mod.shape[0] > 1
    return pl.pallas_call(
        _ffn_kernel,
        grid=(b, l // tm, f // tf),
        in_specs=[pl.BlockSpec((1, tm, d), lambda i, a, j: (i, a, 0)),
                  pl.BlockSpec((1, 6, d), lambda i, a, j: (i if per_batch else 0, 0, 0)),
                  pl.BlockSpec((1, d), lambda i, a, j: (0, 0)),
                  pl.BlockSpec((d, tf), lambda i, a, j: (0, j)),
                  pl.BlockSpec((d, tf), lambda i, a, j: (0, j)),
                  pl.BlockSpec((tf, d), lambda i, a, j: (j, 0))],
        out_specs=pl.BlockSpec((1, tm, d), lambda i, a, j: (i, a, 0)),
        out_shape=jax.ShapeDtypeStruct((b, l, d), F32),
        scratch_shapes=[pltpu.VMEM((tm, d), BF16), pltpu.VMEM((tm, d), F32)],
        compiler_params=_cp("parallel", "parallel", "arbitrary"),
        name="ffn_dense",
    )(x, mod, gain.reshape(1, d), wg, wu, wd)


def _top2_combine(logits):
    lane = lax.broadcasted_iota(jnp.int32, logits.shape, 1)
    v1 = jnp.max(logits, axis=1, keepdims=True)
    i1 = jnp.min(jnp.where(logits == v1, lane, LANES), axis=1, keepdims=True)
    rest = jnp.where(lane == i1, NEG, logits)
    v2 = jnp.max(rest, axis=1, keepdims=True)
    i2 = jnp.min(jnp.where(rest == v2, lane, LANES), axis=1, keepdims=True)
    e2 = jnp.exp(v2 - v1)
    return jnp.where(lane == i1, 1.0 / (1.0 + e2), 0.0) + jnp.where(lane == i2, e2 / (1.0 + e2), 0.0)


MOE_ROW_TILE = 512


def _moe_route_kernel(x_ref, mod_ref, g_ref, r_ref, h_ref, comb_ref, rank_ref, cnt_ref):
    @pl.when((pl.program_id(0) == 0) & (pl.program_id(1) == 0))
    def _():
        cnt_ref[...] = jnp.zeros(cnt_ref.shape, F32)

    hn = _modulated_norm(x_ref[0], g_ref[...], mod_ref[0, 3:4, :], mod_ref[0, 4:5, :])
    tm = hn.shape[0]
    for j in range(h_ref.shape[0]):
        h_ref[j, 0] = hn[:, j * LANES:(j + 1) * LANES]
    lane = lax.broadcasted_iota(jnp.int32, (tm, LANES), 1)
    comb = _top2_combine(jnp.where(lane < N_EXPERTS, _dot_hi(hn, r_ref[...]), NEG))
    comb_ref[0] = comb
    routed = jnp.where(comb > 0.0, 1.0, 0.0)
    row = lax.broadcasted_iota(jnp.int32, (tm, tm), 0)
    col = lax.broadcasted_iota(jnp.int32, (tm, tm), 1)
    earlier = jnp.where(col < row, 1.0, 0.0).astype(BF16)
    rank = cnt_ref[...] + jnp.dot(earlier, routed.astype(BF16), preferred_element_type=F32)
    rank_ref[0] = jnp.where(routed > 0.0, rank, -1.0)
    cnt_ref[...] += jnp.sum(routed, axis=0, keepdims=True)


def _moe_group_kernel(tile_expert_ref, used_ref, xs_ref, wg_ref, wu_ref, wd_ref, o_ref):
    del tile_expert_ref
    pieces = xs_ref.shape[0]

    @pl.when(pl.program_id(0) < used_ref[0])
    def _():
        xs = jnp.concatenate([xs_ref[j] for j in range(pieces)], axis=1).astype(BF16)
        g = jnp.dot(xs, wg_ref[0], preferred_element_type=F32)
        u = jnp.dot(xs, wu_ref[0], preferred_element_type=F32)
        y = jnp.dot((g * _sigmoid(g) * u).astype(BF16), wd_ref[0], preferred_element_type=F32)
        for j in range(pieces):
            o_ref[j] = y[:, j * LANES:(j + 1) * LANES]

    @pl.when(pl.program_id(0) >= used_ref[0])
    def _():
        o_ref[...] = jnp.zeros(o_ref.shape, F32)


def _moe_combine_kernel(x_ref, y_ref, comb_ref, mod_ref, o_ref):
    comb = comb_ref[0]
    gate_a = jnp.max(comb, axis=1, keepdims=True)
    gate_b = jnp.sum(comb, axis=1, keepdims=True) - gate_a
    pieces = y_ref.shape[0]
    y_a = jnp.concatenate([y_ref[j, 0, 0] for j in range(pieces)], axis=1)
    y_b = jnp.concatenate([y_ref[j, 1, 0] for j in range(pieces)], axis=1)
    o_ref[0] = x_ref[0] + mod_ref[0, 5:6, :] * (gate_a * y_a + gate_b * y_b)


MOE_SLOT_ALIGN = 2048


def _moe(x, mod, gain, router, wg, wu, wd):
    b, l, d = x.shape
    ne, _, f = wg.shape
    t = b * l
    tm = _tile(l, 1024)
    tg = MOE_ROW_TILE
    per_batch = mod.shape[0] > 1
    router_p = jnp.pad(router, ((0, 0), (0, LANES - ne)))
    x_spec = pl.BlockSpec((1, tm, d), lambda i, j: (i, j, 0))
    mod_spec = pl.BlockSpec((1, 6, d), lambda i, j: (i if per_batch else 0, 0, 0))
    pieces = d // LANES
    lane_spec = pl.BlockSpec((1, tm, LANES), lambda i, j: (i, j, 0))
    h, comb, rank, cnt = pl.pallas_call(
        _moe_route_kernel,
        grid=(b, l // tm),
        in_specs=[x_spec, mod_spec, pl.BlockSpec((1, d), lambda i, j: (0, 0)),
                  pl.BlockSpec((d, LANES), lambda i, j: (0, 0))],
        out_specs=[pl.BlockSpec((pieces, 1, tm, LANES), lambda i, j: (0, i, j, 0)), lane_spec, lane_spec,
                   pl.BlockSpec((1, LANES), lambda i, j: (0, 0))],
        out_shape=[jax.ShapeDtypeStruct((pieces, b, l, LANES), F32), jax.ShapeDtypeStruct((b, l, LANES), F32),
                   jax.ShapeDtypeStruct((b, l, LANES), F32), jax.ShapeDtypeStruct((1, LANES), F32)],
        compiler_params=_cp("arbitrary", "arbitrary"),
        name="moe_route",
    )(x, mod, gain.reshape(1, d), router_p)

    n_slots = -(-(2 * t + (ne + 1) * tg) // MOE_SLOT_ALIGN) * MOE_SLOT_ALIGN
    comb2 = comb.reshape(t, LANES)[:, :ne]
    rank2 = rank.reshape(t, LANES)[:, :ne]
    grp = (cnt[0, :ne].astype(jnp.int32) + tg - 1) // tg * tg
    end = jnp.cumsum(grp)
    off = end - grp
    tile_start = jnp.arange(n_slots // tg, dtype=jnp.int32) * tg
    tile_expert = jnp.minimum(jnp.sum(tile_start[:, None] >= end[None, :], axis=1), ne - 1).astype(jnp.int32)
    used_tiles = (end[-1:] // tg).astype(jnp.int32)
    first = jnp.argmax(comb2, axis=1)
    second = jnp.argmax(jnp.where(jnp.arange(ne)[None, :] == first[:, None], -1.0, comb2), axis=1)
    pick = lambda a, e: jnp.take_along_axis(a, e[:, None], axis=1)[:, 0]
    rank_b = pick(rank2, second)
    pos_a = off[first] + pick(rank2, first).astype(jnp.int32)
    pos_b = off[second] + rank_b.astype(jnp.int32)
    token = jnp.arange(t, dtype=jnp.int32)
    slot_token = jnp.zeros((n_slots,), jnp.int32).at[
        jnp.concatenate([pos_a, jnp.where(rank_b >= 0.0, pos_b, n_slots)])].set(
        jnp.concatenate([token, token]), mode="drop")
    piece_base = jnp.arange(pieces, dtype=jnp.int32)[:, None]

    xs = _sc_gather_pieces(h.reshape(pieces * t, LANES), (piece_base * t + slot_token[None, :]).reshape(-1))
    piece_rows = pl.BlockSpec((pieces, tg, LANES), lambda i, te, used: (0, i, 0))
    ys = pl.pallas_call(
        _moe_group_kernel,
        grid_spec=pltpu.PrefetchScalarGridSpec(
            num_scalar_prefetch=2,
            grid=(n_slots // tg,),
            in_specs=[piece_rows,
                      pl.BlockSpec((1, d, f), lambda i, te, used: (te[i], 0, 0)),
                      pl.BlockSpec((1, d, f), lambda i, te, used: (te[i], 0, 0)),
                      pl.BlockSpec((1, f, d), lambda i, te, used: (te[i], 0, 0))],
            out_specs=piece_rows),
        out_shape=jax.ShapeDtypeStruct((pieces, n_slots, LANES), F32),
        compiler_params=_cp("arbitrary"),
        name="moe_group",
    )(tile_expert, used_tiles, xs.reshape(pieces, n_slots, LANES), wg, wu, wd)
    pair = jnp.concatenate([pos_a, jnp.where(rank_b >= 0.0, pos_b, n_slots - 1)])
    y2 = _sc_gather_pieces(ys.reshape(pieces * n_slots, LANES), (piece_base * n_slots + pair[None, :]).reshape(-1))
    return pl.pallas_call(
        _moe_combine_kernel,
        grid=(b, l // tm),
        in_specs=[x_spec, pl.BlockSpec((pieces, 2, 1, tm, LANES), lambda i, j: (0, 0, i, j, 0)), lane_spec, mod_spec],
        out_specs=x_spec,
        out_shape=jax.ShapeDtypeStruct((b, l, d), F32),
        compiler_params=_cp("parallel", "parallel"),
        name="moe_combine",
    )(x, y2.reshape(pieces, 2, b, l, LANES), comb, mod)


SC_GATHER_WINDOW = 128


def _sc_gather_pieces(table, idx):
    m = idx.shape[0]
    w = table.shape[1]
    mesh = plsc.VectorSubcoreMesh(core_axis_name="core", subcore_axis_name="subcore")

    @functools.partial(pl.kernel, out_type=jax.ShapeDtypeStruct((m, w), table.dtype), mesh=mesh,
                       name="sc_gather_rows")
    def gather(t_hbm, i_hbm, o_hbm):
        def body(i_vmem, o_vmem):
            pltpu.sync_copy(t_hbm.at[i_vmem.at[0]], o_vmem)

        pltpu.emit_pipeline(
            body,
            grid=(m // SC_GATHER_WINDOW,),
            in_specs=[pl.BlockSpec((1, SC_GATHER_WINDOW), lambda i: (0, i))],
            out_specs=[pl.BlockSpec((SC_GATHER_WINDOW, w), lambda i: (i, 0))],
            core_axis_name=("core", "subcore"),
            dimension_semantics=(pltpu.PARALLEL,),
        )(i_hbm, o_hbm)

    return gather(table, idx.reshape(1, m))


def _lane_row(values, lane0):
    row = jnp.zeros((LANES,), F32)
    return lax.dynamic_update_slice(row, values.reshape(-1).astype(F32), (lane0,))


def _reorder_w_in(w_in):
    g = GROUP_W
    o = [0, g, 2 * g, 3 * g, 4 * g, 4 * g + 8, 4 * g + 16]
    dq_dk_dv = w_in[..., o[0]:o[3]]
    dgate = w_in[..., o[3]:o[4]]
    dbeta_da = w_in[..., o[4]:o[6]]
    base = o[6]
    df = w_in[..., base:base + 3 * g]
    hy = w_in[..., base + 3 * g:base + 6 * g]
    ml = w_in[..., base + 6 * g:base + 9 * g]
    mo = w_in[..., base + 9 * g:base + 10 * g]
    mi_mf = w_in[..., base + 10 * g:base + 10 * g + 16]
    pad = jnp.zeros(w_in.shape[:-1] + (LANES - 32,), w_in.dtype)
    return jnp.concatenate([dq_dk_dv, hy, ml, df, dgate, mo, dbeta_da, mi_mf, pad], axis=-1).astype(BF16)


def _layer(l, x, mod, cache, p, hy_consts, dft):
    b, seq, _ = x.shape
    latent = cache is not None
    z = _proj_in(x, mod, p["norm1_g"][l], p["w_in"][l])

    qkv, = _dwconv(z, p["dn_conv"][l], COL_DN, "deltanet")
    dn_par = jnp.zeros((8, LANES), F32).at[0].set(_lane_row(p["dn_a_log"][l], LANE_A)).at[1].set(
        _lane_row(p["dn_dt_bias"][l], LANE_A))
    s0 = cache[2] if latent else jnp.zeros((b, 2, N_HEADS, HEAD_DIM, HEAD_DIM), F32)
    o_dn_f, o_dn_b, s_dn = _deltanet(qkv, z, dn_par, s0)
    o_dn = (o_dn_f, o_dn_b)

    ml_par = jnp.zeros((8, LANES), F32).at[0].set(_lane_row(p["ml_i_bias"][l], LANE_I)).at[1].set(
        _lane_row(p["ml_f_bias"][l], LANE_F))
    if latent:
        c0 = cache[3]
        n0 = cache[4].reshape(b, 2, 1, GROUP_W)
        m0 = jnp.repeat(cache[5], HEAD_DIM, axis=-1).reshape(b, 2, 1, GROUP_W)
    else:
        c0 = jnp.zeros((b, 2, N_HEADS, HEAD_DIM, HEAD_DIM), F32)
        n0 = m0 = jnp.zeros((b, 2, 1, GROUP_W), F32)
    h_ml_f, h_ml_b, c_ml, n_ml, m_ml = _mlstm(z, ml_par, c0, n0, m0)
    h_ml = (h_ml_f, h_ml_b)
    n_ml = n_ml.reshape(b, 2, N_HEADS, HEAD_DIM)
    m_ml = m_ml.reshape(b, 2, N_HEADS, HEAD_DIM)[..., 0]

    q, kt, v, kd = _diff_prep(z, p["df_q_norm"][l], p["df_k_norm"][l], latent)
    if latent:
        ckt, cv = _cache_prep(cache[0].reshape(b, -1, GROUP_W), cache[1].reshape(b, -1, GROUP_W))
        kt = jnp.concatenate([kt, ckt], axis=-1)
        v = jnp.concatenate([v, cv], axis=2)
    lam_init = 0.8 - 0.6 * math.exp(-0.3 * l)
    o_df = _diff_attn(q, kt, v, p["df_lambda"][l], p["df_norm"][l], lam_init)

    feat, win = hy_consts
    fwd, inv = dft
    zc, zv_lowp = _dwconv(z, p["hy_conv"][l], COL_HY, "plain")
    filt = _hy_filters(feat, win, p["hy_w1"][l], p["hy_b1"][l], p["hy_w2"][l], p["hy_b2"][l], p["hy_w3"][l],
                       p["hy_freq"][l])
    hf = _dft_filter(fwd, filt)
    y1 = _dft_fwd_mul(fwd, zv_lowp, hf, 0)
    z1, z1_lowp = _dft_inv_gate(inv, y1, zc, 0, zc, 2, p["hy_skip"][l, 0])
    y2 = _dft_fwd_mul(fwd, z1_lowp, hf, 1)
    y_hy, _ = _dft_inv_gate(inv, y2, zc, 1, z1, 0, p["hy_skip"][l, 1])

    x = _proj_out(o_dn, z, o_df, y_hy, h_ml, x, mod, p["dn_norm"][l], p["ml_norm"][l], p["w_out"][l])
    j = l // 2
    if mod.shape[0] == 1:
        x = x.reshape(1, b * seq, -1)
    if l % 2 == 0:
        x = _ffn(x, mod, p["norm2_g"][l], p["ffn_w_gate"][j], p["ffn_w_up"][j], p["ffn_w_down"][j])
    else:
        x = _moe(x, mod, p["norm2_g"][l], p["moe_router"][j], p["moe_w_gate"][j], p["moe_w_up"][j],
                 p["moe_w_down"][j])
    x = x.reshape(b, seq, -1)
    fv =z[:, :, 3 * 3 * GROUP_W + 2 * GROUP_W:3 * 3 * GROUP_W + 3 * GROUP_W]
    return x, (kd, fv, s_dn, c_ml, n_ml, m_ml)


def kernel(x_prompt, x_sample, cache_diff_k, cache_diff_v, state_delta, state_mlstm_c, state_mlstm_n, state_mlstm_m, c, c_ctx, norm1_g, norm2_g, w_mod, b_mod, w_in, w_out, dn_conv, dn_a_log, dn_dt_bias, dn_norm, df_q_norm, df_k_norm, df_lambda, df_norm, hy_conv, hy_w1, hy_b1, hy_w2, hy_b2, hy_w3, hy_freq, hy_skip, ml_i_bias, ml_f_bias, ml_norm, ffn_w_gate, ffn_w_up, ffn_w_down, moe_router, moe_w_gate, moe_w_up, moe_w_down):
    depth = w_in.shape[0]
    d_model = x_prompt.shape[-1]
    batch, seq, _ = x_prompt.shape
    dec_batch, dec_seq, _ = x_sample.shape
    p = dict(norm1_g=norm1_g, norm2_g=norm2_g, w_in=_reorder_w_in(w_in), w_out=w_out.astype(BF16),
             dn_conv=dn_conv, dn_a_log=dn_a_log, dn_dt_bias=dn_dt_bias, dn_norm=dn_norm,
             df_q_norm=df_q_norm, df_k_norm=df_k_norm, df_lambda=df_lambda, df_norm=df_norm,
             hy_conv=hy_conv, hy_w1=hy_w1, hy_b1=hy_b1, hy_w2=hy_w2, hy_b2=hy_b2, hy_w3=hy_w3,
             hy_freq=hy_freq, hy_skip=hy_skip, ml_i_bias=ml_i_bias, ml_f_bias=ml_f_bias, ml_norm=ml_norm,
             ffn_w_gate=ffn_w_gate.astype(BF16), ffn_w_up=ffn_w_up.astype(BF16),
             ffn_w_down=ffn_w_down.astype(BF16), moe_router=moe_router,
             moe_w_gate=moe_w_gate.astype(BF16), moe_w_up=moe_w_up.astype(BF16),
             moe_w_down=moe_w_down.astype(BF16))

    n_cond = 1 + dec_batch
    rows = -(-n_cond // 8) * 8
    cond = jnp.concatenate([c_ctx[None, :], c, jnp.zeros((rows - n_cond, d_model), F32)], axis=0)
    mod = _modulation(cond, w_mod, b_mod).reshape(depth, rows, 6, d_model)

    hy_ctx, dft_ctx = _hy_constants(seq), _dft_matrices(seq)
    x = x_prompt
    ctx = []
    for l in range(depth):
        x, out = _layer(l, x, mod[l, 0:1], None, p, hy_ctx, dft_ctx)
        ctx.append(out)
    y_prompt = x
    new_k, new_v, new_s, new_c, new_n, new_m = (jnp.stack([o[i] for o in ctx], axis=1) for i in range(6))
    new_k = new_k.reshape(batch, depth, seq, N_HEADS, 2, DF_DH)
    new_v = new_v.reshape(batch, depth, seq, N_HEADS, HEAD_DIM)

    hy_lat, dft_lat = _hy_constants(dec_seq), _dft_matrices(dec_seq)
    x = x_sample
    for l in range(depth):
        cache = (cache_diff_k[:, l], cache_diff_v[:, l], state_delta[:, l], state_mlstm_c[:, l],
                 state_mlstm_n[:, l], state_mlstm_m[:, l])
        x, _ = _layer(l, x, mod[l, 1:1 + dec_batch], cache, p, hy_lat, dft_lat)
    return (y_prompt, x, new_k, new_v, new_s, new_c, new_n, new_m)
```

```python
import functools
import math
from typing import Any, NamedTuple

import jax
import jax.numpy as jnp
from jax import lax
from jax.experimental import pallas as pl
from jax.experimental.pallas import tpu as pltpu
from jax.experimental.pallas import tpu_sc as plsc

F32 = jnp.float32
BF16 = jnp.bfloat16

N_HEADS = 4
HEAD_DIM = 64
GROUP_W = N_HEADS * HEAD_DIM
DF_DH = 32
CHUNK = 64
GRID_W = 64
ROPE_THETA = 10000.0
HY_BANDS = 8
HY_FAST_DECAY = 0.3
HY_SLOW_DECAY = 1.5
HY_TARGET = 1e-2
HY_SHIFT = 0.05
N_EXPERTS = 8
EPS = 1e-6
NEG = -1e30
LANES = 128
VMEM_LIMIT = 56 * 1024 * 1024

COL_DN, COL_HY, COL_ML, COL_DF = 0, 1, 2, 3
COL_DGATE, COL_MO = 12, 13
COL_SMALL = 28
Z_WIDTH = 29 * LANES
LANE_BETA, LANE_A, LANE_I, LANE_F = 0, 8, 16, 24


def _tile(n, pref):
    t = min(n, pref)
    while n % t:
        t -= LANES if t > LANES else 8
    return t


def _cp(*sem):
    return pltpu.CompilerParams(dimension_semantics=sem, vmem_limit_bytes=VMEM_LIMIT)


def _split3(x):
    x1 = x.astype(BF16)
    r = x - x1.astype(F32)
    x2 = r.astype(BF16)
    r = r - x2.astype(F32)
    return x1, x2, r.astype(BF16)


def _dot(a, b):
    return jnp.dot(a.astype(BF16), b.astype(BF16), preferred_element_type=F32)


def _dot_nt(a, b):
    return lax.dot_general(a.astype(BF16), b.astype(BF16), (((1,), (1,)), ((), ())),
                           preferred_element_type=F32)


def _dot_tn(a, b):
    return lax.dot_general(a.astype(BF16), b.astype(BF16), (((0,), (0,)), ((), ())),
                           preferred_element_type=F32)


def _dot_exact_l(m, x):
    return sum(jnp.dot(m, p, preferred_element_type=F32) for p in _split3(x))


def _dot_exact_r(x, m):
    return sum(jnp.dot(p, m, preferred_element_type=F32) for p in _split3(x))


def _dot_hi(a, b):
    a1, a2, _ = _split3(a)
    b1, b2, _ = _split3(b)
    return (jnp.dot(a1, b1, preferred_element_type=F32) + jnp.dot(a1, b2, preferred_element_type=F32)
            + jnp.dot(a2, b1, preferred_element_type=F32))


def _seg_mean(x, seg):
    w = x.shape[-1]
    sh = int(math.log2(seg))
    r = lax.shift_right_logical(lax.broadcasted_iota(jnp.int32, (w, w), 0), sh)
    c = lax.shift_right_logical(lax.broadcasted_iota(jnp.int32, (w, w), 1), sh)
    bd = jnp.where(r == c, 1.0, 0.0).astype(BF16)
    return _dot_exact_r(x, bd) * (1.0 / seg)


def _row_bcast(col):
    n = col.shape[0]
    return jnp.broadcast_to(jnp.broadcast_to(col, (n, LANES)).T[0:1, :], (n, n))


def _lane_pick(x, lane_idx):
    lane = lax.broadcasted_iota(jnp.int32, x.shape, 1)
    return jnp.sum(jnp.where(lane == lane_idx, x, 0.0), axis=1, keepdims=True)


def _sigmoid(x):
    return 1.0 / (1.0 + jnp.exp(-x))


def _softplus(x):
    return jnp.maximum(x, 0.0) + jnp.log1p(jnp.exp(-jnp.abs(x)))


def _mod_kernel(c_ref, w_ref, b_ref, o_ref):
    c = c_ref[...]
    o_ref[0] = _dot(c * _sigmoid(c), w_ref[0]) + b_ref[0]


def _modulation(cond, w_mod, b_mod):
    depth, d, n = w_mod.shape
    r = cond.shape[0]
    tn = n // 4
    return pl.pallas_call(
        _mod_kernel,
        grid=(depth, n // tn),
        in_specs=[pl.BlockSpec((r, d), lambda l, j: (0, 0)),
                  pl.BlockSpec((1, d, tn), lambda l, j: (l, 0, j)),
                  pl.BlockSpec((1, 1, tn), lambda l, j: (l, 0, j))],
        out_specs=pl.BlockSpec((1, r, tn), lambda l, j: (l, 0, j)),
        out_shape=jax.ShapeDtypeStruct((depth, r, n), F32),
        compiler_params=_cp("parallel", "parallel"),
        name="modulation",
    )(cond, w_mod, b_mod.reshape(depth, 1, n))


def _modulated_norm(x, gain, shift, scale):
    var = jnp.mean(x * x, axis=-1, keepdims=True)
    return x * lax.rsqrt(var + EPS) * gain * (1.0 + scale) + shift


def _proj_in_kernel(x_ref, mod_ref, g_ref, w_ref, o_ref):
    h = _modulated_norm(x_ref[0], g_ref[...], mod_ref[0, 0:1, :], mod_ref[0, 1:2, :])
    o_ref[0] = jnp.dot(h.astype(BF16), w_ref[...], preferred_element_type=F32)


def _proj_in(x, mod, gain, w):
    b, l, d = x.shape
    tm = _tile(l, 256)
    per_batch = mod.shape[0] > 1
    return pl.pallas_call(
        _proj_in_kernel,
        grid=(b, l // tm),
        in_specs=[pl.BlockSpec((1, tm, d), lambda i, j: (i, j, 0)),
                  pl.BlockSpec((1, 6, d), lambda i, j: (i if per_batch else 0, 0, 0)),
                  pl.BlockSpec((1, d), lambda i, j: (0, 0)),
                  pl.BlockSpec((d, Z_WIDTH), lambda i, j: (0, 0))],
        out_specs=pl.BlockSpec((1, tm, Z_WIDTH), lambda i, j: (i, j, 0)),
        out_shape=jax.ShapeDtypeStruct((b, l, Z_WIDTH), F32),
        compiler_params=_cp("parallel", "parallel"),
        name="proj_in",
    )(x, mod, gain.reshape(1, d), w)


def _dwconv_kernel(z_ref, zp_ref, zn_ref, w_ref, o_ref, *lowp_refs, mode):
    i = pl.program_id(1)
    z = z_ref[0]
    tm = z.shape[0]
    prev_row = jnp.where(i > 0, zp_ref[0, 7:8, :], 0.0)
    next_row = jnp.where(i < pl.num_programs(1) - 1, zn_ref[0, 0:1, :], 0.0)
    rid = lax.broadcasted_iota(jnp.int32, z.shape, 0)
    zm1 = jnp.where(rid == 0, prev_row, pltpu.roll(z, 1, 0))
    zp1 = jnp.where(rid == tm - 1, next_row, pltpu.roll(z, tm - 1, 0))
    y = zm1 * w_ref[0:1, :] + z * w_ref[1:2, :] + zp1 * w_ref[2:3, :]
    if mode == "deltanet":
        y = y * _sigmoid(y)
        q, k, v = y[:, :GROUP_W], y[:, GROUP_W:2 * GROUP_W], y[:, 2 * GROUP_W:]
        q = q * lax.rsqrt(_seg_mean(q * q, HEAD_DIM) * HEAD_DIM + EPS) * (HEAD_DIM ** -0.5)
        k = k * lax.rsqrt(_seg_mean(k * k, HEAD_DIM) * HEAD_DIM + EPS)
        o_ref[0, :, 0:GROUP_W] = q
        o_ref[0, :, GROUP_W:2 * GROUP_W] = k
        o_ref[0, :, 2 * GROUP_W:] = v
    else:
        o_ref[0] = y
        lowp_refs[0][0] = y[:, 2 * GROUP_W:].astype(BF16)


def _dwconv(z, w, col_block, mode):
    b, l, _ = z.shape
    c = 3 * GROUP_W
    tm = _tile(l, 512)
    hb = tm // 8
    last = l // 8 - 1
    out_specs = [pl.BlockSpec((1, tm, c), lambda i, j: (i, j, 0))]
    out_shape = [jax.ShapeDtypeStruct((b, l, c), F32)]
    if mode == "plain":
        out_specs.append(pl.BlockSpec((1, tm, GROUP_W), lambda i, j: (i, j, 0)))
        out_shape.append(jax.ShapeDtypeStruct((b, l, GROUP_W), BF16))
    return pl.pallas_call(
        functools.partial(_dwconv_kernel, mode=mode),
        grid=(b, l // tm),
        in_specs=[pl.BlockSpec((1, tm, c), lambda i, j: (i, j, col_block)),
                  pl.BlockSpec((1, 8, c), lambda i, j: (i, jnp.maximum(j * hb - 1, 0), col_block)),
                  pl.BlockSpec((1, 8, c), lambda i, j: (i, jnp.minimum((j + 1) * hb, last), col_block)),
                  pl.BlockSpec((3, c), lambda i, j: (0, 0))],
        out_specs=out_specs,
        out_shape=out_shape,
        compiler_params=_cp("parallel", "parallel"),
        name="dwconv_" + mode,
    )(z, z, z, w)


STACK = N_HEADS * CHUNK


def _stack_heads(x):
    return jnp.concatenate([x] * N_HEADS, axis=0)


def _block_diag(x, head_eq):
    return jnp.where(head_eq, _stack_heads(x), 0.0)


def _fold_heads(x):
    return x[0:CHUNK] + x[CHUNK:2 * CHUNK] + x[2 * CHUNK:3 * CHUNK] + x[3 * CHUNK:]


def _chunk_masks(d):
    row = lax.broadcasted_iota(jnp.int32, (STACK, STACK), 0)
    col = lax.broadcasted_iota(jnp.int32, (STACK, STACK), 1)
    head_eq = lax.shift_right_logical(row, 6) == lax.shift_right_logical(col, 6)
    rel = ((row & (CHUNK - 1)) - (col & (CHUNK - 1))) * (1 - 2 * d)
    r64 = lax.broadcasted_iota(jnp.int32, (CHUNK, CHUNK), 0)
    c64 = lax.broadcasted_iota(jnp.int32, (CHUNK, CHUNK), 1)
    cum = jnp.where((r64 - c64) * (1 - 2 * d) >= 0, 1.0, 0.0).astype(BF16)
    return head_eq, head_eq & (rel >= 0), head_eq & (rel > 0), cum


def _head_cols(x, lane0):
    xs = _stack_heads(x)
    row = lax.broadcasted_iota(jnp.int32, xs.shape, 0)
    lane = lax.broadcasted_iota(jnp.int32, xs.shape, 1)
    return jnp.sum(jnp.where(lane == lane0 + lax.shift_right_logical(row, 6), xs, 0.0), axis=1, keepdims=True)


def _head_lanes(x, lane0):
    src = lax.broadcasted_iota(jnp.int32, (LANES, GROUP_W), 0)
    dst = lax.broadcasted_iota(jnp.int32, (LANES, GROUP_W), 1)
    expand = jnp.where(src == lane0 + lax.shift_right_logical(dst, 6), 1.0, 0.0).astype(BF16)
    return _dot_exact_r(x, expand)


def _lanes_to_col(x):
    return jnp.broadcast_to(x, (LANES, GROUP_W)).T[:, 0:1]


def _load_diag_state(dst_ref, src_ref):
    dst_ref[...] = jnp.zeros(dst_ref.shape, F32)
    for r in range(dst_ref.shape[0]):
        for d in range(2):
            for h in range(N_HEADS):
                lo, hi = h * HEAD_DIM, (h + 1) * HEAD_DIM
                dst_ref[r, d, lo:hi, lo:hi] = src_ref[r, d, h]


def _store_diag_state(dst_ref, src_ref):
    for r in range(src_ref.shape[0]):
        for d in range(2):
            for h in range(N_HEADS):
                lo, hi = h * HEAD_DIM, (h + 1) * HEAD_DIM
                dst_ref[r, d, h] = src_ref[r, d, lo:hi, lo:hi]


def _rec_specs(b, l):
    rows = max(r for r in (4, 2, 1) if b % r == 0)
    tile = _tile(l, 512)
    nt = l // tile
    fwd = lambda width, col: pl.BlockSpec((rows, tile, width), lambda i, j: (i, j, col))
    bwd = lambda width, col: pl.BlockSpec((rows, tile, width), lambda i, j: (i, nt - 1 - j, col))
    state = pl.BlockSpec((rows, 2, N_HEADS, HEAD_DIM, HEAD_DIM), lambda i, j: (i, 0, 0, 0, 0))
    return rows, tile, nt, fwd, bwd, state


class _Chain(NamedTuple):
    r: int
    d: int
    x_ref: Any
    sm_ref: Any
    o_ref: Any
    rows: Any


def _each(fn, *cols):
    return [fn(*args) for args in zip(*cols)]


def _unit_tri_inverses(mats):
    n = mats[0].shape[0]
    row = lax.broadcasted_iota(jnp.int32, (n, n), 0)
    col = lax.broadcasted_iota(jnp.int32, (n, n), 1)

    def same_block(log2_size):
        return lax.shift_right_logical(row, log2_size) == lax.shift_right_logical(col, log2_size)

    p = [jnp.where(same_block(3), -a, 0.0) for a in mats]
    t = [jnp.where(row == col, 1.0, 0.0) + x for x in p]
    for _ in range(2):
        p = _each(_dot, p, p)
        t = _each(jnp.add, t, _each(_dot, t, p))
    for log2_size in range(4, int(math.log2(CHUNK)) + 1):
        level = same_block(log2_size) & jnp.logical_not(same_block(log2_size - 1))
        off_t = [_dot(jnp.where(level, a, 0.0), x) for a, x in zip(mats, t)]
        t = _each(jnp.subtract, t, _each(_dot, t, off_t))
    return t


def _deltanet_chunks(chains, par_ref, s_ref):
    masks = {d: _chunk_masks(d) for d in (0, 1)}
    head_eq = masks[0][0]
    incl = [masks[c.d][1] for c in chains]
    strict = [masks[c.d][2] for c in chains]
    cum = [masks[c.d][3] for c in chains]
    lane_a = [LANE_A + c.d * N_HEADS for c in chains]
    lane_b = [LANE_BETA + c.d * N_HEADS for c in chains]
    q = [c.x_ref[c.r, c.rows, 0:GROUP_W] for c in chains]
    k = [c.x_ref[c.r, c.rows, GROUP_W:2 * GROUP_W] for c in chains]
    v = [c.x_ref[c.r, c.rows, 2 * GROUP_W:3 * GROUP_W] for c in chains]
    sm = [c.sm_ref[c.r, c.rows, :] for c in chains]
    beta_all = _each(_sigmoid, sm)
    g_all = [-jnp.exp(par_ref[0:1, :]) * _softplus(x + par_ref[1:2, :]) for x in sm]
    gc_all = _each(_dot_exact_l, cum, g_all)
    gc = _each(_head_lanes, gc_all, lane_a)
    beta = _each(_head_lanes, beta_all, lane_b)
    g_last = [x[CHUNK - 1:CHUNK] if c.d == 0 else x[0:1] for x, c in zip(gc, chains)]
    gc_col = _each(_head_cols, gc_all, lane_a)
    beta_col = _each(_head_cols, beta_all, lane_b)
    gc_row = _each(_row_bcast, gc_col)
    decay = [jnp.exp(jnp.where(m, x - y, NEG)) for m, x, y in zip(incl, gc_col, gc_row)]
    k_rows = _each(_stack_heads, k)
    kk = [_dot_nt(jnp.where(head_eq, x, 0.0), x) for x in k_rows]
    a = [jnp.where(m, b * x * dc, 0.0) for m, b, x, dc in zip(strict, beta_col, kk, decay)]
    t = _unit_tri_inverses(a)
    egc = _each(jnp.exp, gc)
    u = [_fold_heads(_dot(x, _block_diag(y * b, head_eq))) for x, y, b in zip(t, v, beta)]
    w = [_fold_heads(_dot(x, _block_diag(y * (b * e), head_eq))) for x, y, b, e in zip(t, k, beta, egc)]
    s = [s_ref[c.r, c.d] for c in chains]
    ws_qs = [_dot(jnp.concatenate([x, y * e], axis=0), z) for x, y, e, z in zip(w, q, egc, s)]
    v_new = [x - y[0:CHUNK] for x, y in zip(u, ws_qs)]
    qk = [_dot_nt(_block_diag(x, head_eq), y) * dc for x, y, dc in zip(q, k_rows, decay)]
    o_intra = [_fold_heads(_dot(x, _block_diag(y, head_eq))) for x, y in zip(qk, v_new)]
    s_add = [_dot_tn(x * jnp.exp(gl - g), y) for x, gl, g, y in zip(k, g_last, gc, v_new)]
    for c, x, y, z, gl, sa in zip(chains, ws_qs, o_intra, s, g_last, s_add):
        c.o_ref[c.r, c.rows, :] = x[CHUNK:] + y
        s_ref[c.r, c.d] = z * jnp.exp(gl) + jnp.where(head_eq, sa, 0.0)


def _chunk_chains(n_rows, n_chunks, c, f_refs, b_refs):
    rows_f = pl.ds(pl.multiple_of(c * CHUNK, CHUNK), CHUNK)
    rows_b = pl.ds(pl.multiple_of((n_chunks - 1 - c) * CHUNK, CHUNK), CHUNK)
    return [_Chain(r, d, *refs, rows) for r in range(n_rows)
            for d, refs, rows in ((0, f_refs, rows_f), (1, b_refs, rows_b))]


def _deltanet_kernel(qf_ref, smf_ref, qb_ref, smb_ref, par_ref, s0_ref, of_ref, ob_ref, sout_ref, s_ref, *,
                     n_chunks):
    j = pl.program_id(1)

    @pl.when(j == 0)
    def _():
        _load_diag_state(s_ref, s0_ref)

    def chunk_body(c, carry):
        chains = _chunk_chains(s_ref.shape[0], n_chunks, c, (qf_ref, smf_ref, of_ref), (qb_ref, smb_ref, ob_ref))
        _deltanet_chunks(chains, par_ref, s_ref)
        return carry

    lax.fori_loop(0, n_chunks, chunk_body, 0)

    @pl.when(j == pl.num_programs(1) - 1)
    def _():
        _store_diag_state(sout_ref, s_ref)


def _deltanet(qkv, z, par, s0):
    b, l, _ = qkv.shape
    rows, tile, nt, fwd, bwd, state = _rec_specs(b, l)
    return pl.pallas_call(
        functools.partial(_deltanet_kernel, n_chunks=tile // CHUNK),
        grid=(b // rows, nt),
        in_specs=[fwd(3 * GROUP_W, 0), fwd(LANES, COL_SMALL), bwd(3 * GROUP_W, 0), bwd(LANES, COL_SMALL),
                  pl.BlockSpec((8, LANES), lambda i, j: (0, 0)), state],
        out_specs=[fwd(GROUP_W, 0), bwd(GROUP_W, 0), state],
        out_shape=[jax.ShapeDtypeStruct((b, l, GROUP_W), F32), jax.ShapeDtypeStruct((b, l, GROUP_W), F32),
                   jax.ShapeDtypeStruct((b, 2, N_HEADS, HEAD_DIM, HEAD_DIM), F32)],
        scratch_shapes=[pltpu.VMEM((rows, 2, STACK, STACK), F32)],
        compiler_params=_cp("parallel", "arbitrary"),
        name="deltanet",
    )(qkv, z, qkv, z, par, s0)


def _mlstm_chunks(chains, par_ref, c_ref, n_ref, m_ref):
    masks = {d: _chunk_masks(d) for d in (0, 1)}
    head_eq = masks[0][0]
    incl = [masks[c.d][1] for c in chains]
    cum = [masks[c.d][3] for c in chains]
    lane_i = [LANE_I + c.d * N_HEADS for c in chains]
    lane_f = [LANE_F + c.d * N_HEADS for c in chains]
    q = [c.x_ref[c.r, c.rows, 0:GROUP_W] for c in chains]
    k = [c.x_ref[c.r, c.rows, GROUP_W:2 * GROUP_W] * (HEAD_DIM ** -0.5) for c in chains]
    v = [c.x_ref[c.r, c.rows, 2 * GROUP_W:3 * GROUP_W] for c in chains]
    sm = [c.sm_ref[c.r, c.rows, :] for c in chains]
    i_all = [x + par_ref[0:1, :] for x in sm]
    f_all = [-_softplus(-(x + par_ref[1:2, :])) for x in sm]
    bc_all = _each(_dot_exact_l, cum, f_all)
    ic = _each(_head_lanes, i_all, lane_i)
    bc = _each(_head_lanes, bc_all, lane_f)
    b_tot = [x[CHUNK - 1:CHUNK] if c.d == 0 else x[0:1] for x, c in zip(bc, chains)]
    ic_col = _each(_head_cols, i_all, lane_i)
    bc_col = _each(_head_cols, bc_all, lane_f)
    c_s = [c_ref[c.r, c.d] for c in chains]
    n_s = [n_ref[c.r, c.d] for c in chains]
    m_s = [m_ref[c.r, c.d] for c in chains]
    w_row = _each(_row_bcast, _each(jnp.subtract, ic_col, bc_col))
    dlog = [jnp.where(m, x + y, NEG) for m, x, y in zip(incl, bc_col, w_row)]
    a = _each(jnp.add, bc_col, _each(_lanes_to_col, m_s))
    m_t = [jnp.maximum(x, jnp.max(y, axis=1, keepdims=True)) for x, y in zip(a, dlog)]
    inter = [jnp.exp(x - y) for x, y in zip(a, m_t)]
    q_bd = [_block_diag(x, head_eq) for x in q]
    s = [_dot_nt(x, _stack_heads(y)) * jnp.exp(dl - mt) for x, y, dl, mt in zip(q_bd, k, dlog, m_t)]
    inter_part = _each(_dot, q_bd, c_s)
    intra_part = [_dot(x, _block_diag(y, head_eq)) for x, y in zip(s, v)]
    den = [it * jnp.sum(x * n, axis=1, keepdims=True) + jnp.sum(y, axis=1, keepdims=True)
           for it, x, n, y in zip(inter, q_bd, n_s, s)]
    wend = [bt - x + y for bt, x, y in zip(b_tot, bc, ic)]
    a_end = _each(jnp.add, b_tot, m_s)
    m_new = [jnp.maximum(x, jnp.max(y, axis=0, keepdims=True)) for x, y in zip(a_end, wend)]
    dec = [jnp.exp(x - y) for x, y in zip(a_end, m_new)]
    kw = [x * jnp.exp(y - z) for x, y, z in zip(k, wend, m_new)]
    c_add = _each(_dot_tn, kw, v)
    for i, c in enumerate(chains):
        num = inter[i] * inter_part[i] + intra_part[i]
        c.o_ref[c.r, c.rows, :] = _fold_heads(num / jnp.maximum(jnp.abs(den[i]), jnp.exp(-m_t[i])))
        c_ref[c.r, c.d] = dec[i] * c_s[i] + jnp.where(head_eq, c_add[i], 0.0)
        n_ref[c.r, c.d] = dec[i] * n_s[i] + jnp.sum(kw[i], axis=0, keepdims=True)
        m_ref[c.r, c.d] = m_new[i]


def _mlstm_kernel(zf_ref, smf_ref, zb_ref, smb_ref, par_ref, c0_ref, n0_ref, m0_ref,
                  of_ref, ob_ref, cout_ref, nout_ref, mout_ref, c_ref, n_ref, m_ref, *, n_chunks):
    j = pl.program_id(1)

    @pl.when(j == 0)
    def _():
        _load_diag_state(c_ref, c0_ref)
        n_ref[...] = n0_ref[...]
        m_ref[...] = m0_ref[...]

    def chunk_body(c, carry):
        chains = _chunk_chains(c_ref.shape[0], n_chunks, c, (zf_ref, smf_ref, of_ref), (zb_ref, smb_ref, ob_ref))
        _mlstm_chunks(chains, par_ref, c_ref, n_ref, m_ref)
        return carry

    lax.fori_loop(0, n_chunks, chunk_body, 0)

    @pl.when(j == pl.num_programs(1) - 1)
    def _():
        _store_diag_state(cout_ref, c_ref)
        nout_ref[...] = n_ref[...]
        mout_ref[...] = m_ref[...]


def _mlstm(z, par, c0, n0, m0):
    b, l, _ = z.shape
    rows, tile, nt, fwd, bwd, state = _rec_specs(b, l)
    vec = pl.BlockSpec((rows, 2, 1, GROUP_W), lambda i, j: (i, 0, 0, 0))
    return pl.pallas_call(
        functools.partial(_mlstm_kernel, n_chunks=tile // CHUNK),
        grid=(b // rows, nt),
        in_specs=[fwd(3 * GROUP_W, COL_ML), fwd(LANES, COL_SMALL), bwd(3 * GROUP_W, COL_ML), bwd(LANES, COL_SMALL),
                  pl.BlockSpec((8, LANES), lambda i, j: (0, 0)), state, vec, vec],
        out_specs=[fwd(GROUP_W, 0), bwd(GROUP_W, 0), state, vec, vec],
        out_shape=[jax.ShapeDtypeStruct((b, l, GROUP_W), F32), jax.ShapeDtypeStruct((b, l, GROUP_W), F32),
                   jax.ShapeDtypeStruct((b, 2, N_HEADS, HEAD_DIM, HEAD_DIM), F32),
                   jax.ShapeDtypeStruct((b, 2, 1, GROUP_W), F32), jax.ShapeDtypeStruct((b, 2, 1, GROUP_W), F32)],
        scratch_shapes=[pltpu.VMEM((rows, 2, STACK, STACK), F32), pltpu.VMEM((rows, 2, 1, GROUP_W), F32),
                        pltpu.VMEM((rows, 2, 1, GROUP_W), F32)],
        compiler_params=_cp("parallel", "arbitrary"),
        name="mlstm",
    )(z, z, z, z, par, c0, n0, m0)


def _diff_prep_kernel(z_ref, gq_ref, gk_ref, cos_ref, sin_ref, q_ref, kt_ref, v_ref, kd_ref, *, rope):
    z = z_ref[0]
    q, k, v = z[:, :GROUP_W], z[:, GROUP_W:2 * GROUP_W], z[:, 2 * GROUP_W:]
    q = q * lax.rsqrt(_seg_mean(q * q, DF_DH) + EPS) * gq_ref[...]
    k = k * lax.rsqrt(_seg_mean(k * k, DF_DH) + EPS) * gk_ref[...]
    kd_ref[0] = k
    if rope:
        lane = lax.broadcasted_iota(jnp.int32, q.shape, 1)
        first = (lane & (DF_DH - 1)) < DF_DH // 2

        def rot(x):
            swapped = jnp.where(first, pltpu.roll(x, GROUP_W - DF_DH // 2, 1), pltpu.roll(x, DF_DH // 2, 1))
            return x * cos_ref[...] + swapped * sin_ref[...]

        q, k = rot(q), rot(k)
    _store_attn_operands(q * (DF_DH ** -0.5), k, v, q_ref, kt_ref, v_ref)


def _store_attn_operands(q, k, v, q_ref, kt_ref, v_ref):
    kt = k.T
    lane = lax.broadcasted_iota(jnp.int32, (k.shape[0], HEAD_DIM), 1)
    ones = jnp.ones((k.shape[0], HEAD_DIM), BF16)
    for h in range(N_HEADS):
        lo, hi = h * HEAD_DIM, (h + 1) * HEAD_DIM
        if q is not None:
            q_ref[0, h, 0] = jnp.where(lane < DF_DH, q[:, lo:hi], 0.0).astype(BF16)
            q_ref[0, h, 1] = jnp.where(lane >= DF_DH, q[:, lo:hi], 0.0).astype(BF16)
        v_ref[0, h] = jnp.concatenate([v[:, lo:hi].astype(BF16), ones], axis=1)
        kt_ref[0, h] = kt[lo:hi, :].astype(BF16)


def _cache_prep_kernel(k_ref, v_ref, kt_ref, vh_ref):
    _store_attn_operands(None, k_ref[0], v_ref[0], None, kt_ref, vh_ref)


def _rope_tables(l):
    rows = l // GRID_W
    r = jnp.repeat(jnp.arange(rows, dtype=F32), GRID_W)
    col = jnp.tile(jnp.arange(GRID_W, dtype=F32), rows)
    n_freq = DF_DH // 4
    inv = ROPE_THETA ** (-jnp.arange(n_freq, dtype=F32) / n_freq)
    ang = jnp.concatenate([r[:, None] * inv, col[:, None] * inv], axis=-1)
    cos, sin = jnp.cos(ang), jnp.sin(ang)
    reps = GROUP_W // DF_DH
    return (jnp.tile(jnp.concatenate([cos, cos], axis=-1), (1, reps)),
            jnp.tile(jnp.concatenate([-sin, sin], axis=-1), (1, reps)))


def _diff_prep(z, gq, gk, rope):
    b, l, _ = z.shape
    tm = _tile(l, 512)
    if rope:
        cos, sin = _rope_tables(l)
    else:
        cos = sin = jnp.zeros((l, GROUP_W), F32)
    return pl.pallas_call(
        functools.partial(_diff_prep_kernel, rope=rope),
        grid=(b, l // tm),
        in_specs=[pl.BlockSpec((1, tm, 3 * GROUP_W), lambda i, j: (i, j, COL_DF)),
                  pl.BlockSpec((1, GROUP_W), lambda i, j: (0, 0)),
                  pl.BlockSpec((1, GROUP_W), lambda i, j: (0, 0)),
                  pl.BlockSpec((tm, GROUP_W), lambda i, j: (j, 0)),
                  pl.BlockSpec((tm, GROUP_W), lambda i, j: (j, 0))],
        out_specs=[pl.BlockSpec((1, N_HEADS, 2, tm, HEAD_DIM), lambda i, j: (i, 0, 0, j, 0)),
                   pl.BlockSpec((1, N_HEADS, HEAD_DIM, tm), lambda i, j: (i, 0, 0, j)),
                   pl.BlockSpec((1, N_HEADS, tm, 2 * HEAD_DIM), lambda i, j: (i, 0, j, 0)),
                   pl.BlockSpec((1, tm, GROUP_W), lambda i, j: (i, j, 0))],
        out_shape=[jax.ShapeDtypeStruct((b, N_HEADS, 2, l, HEAD_DIM), BF16),
                   jax.ShapeDtypeStruct((b, N_HEADS, HEAD_DIM, l), BF16),
                   jax.ShapeDtypeStruct((b, N_HEADS, l, 2 * HEAD_DIM), BF16),
                   jax.ShapeDtypeStruct((b, l, GROUP_W), F32)],
        compiler_params=_cp("parallel", "parallel"),
        name="diff_prep",
    )(z, jnp.tile(gq, GROUP_W // DF_DH).reshape(1, GROUP_W), jnp.tile(gk, GROUP_W // DF_DH).reshape(1, GROUP_W),
      cos, sin)


def _cache_prep(ck, cv):
    b, p, _ = ck.shape
    tm = _tile(p, 512)
    return pl.pallas_call(
        _cache_prep_kernel,
        grid=(b, p // tm),
        in_specs=[pl.BlockSpec((1, tm, GROUP_W), lambda i, j: (i, j, 0)),
                  pl.BlockSpec((1, tm, GROUP_W), lambda i, j: (i, j, 0))],
        out_specs=[pl.BlockSpec((1, N_HEADS, HEAD_DIM, tm), lambda i, j: (i, 0, 0, j)),
                   pl.BlockSpec((1, N_HEADS, tm, 2 * HEAD_DIM), lambda i, j: (i, 0, j, 0))],
        out_shape=[jax.ShapeDtypeStruct((b, N_HEADS, HEAD_DIM, p), BF16),
                   jax.ShapeDtypeStruct((b, N_HEADS, p, 2 * HEAD_DIM), BF16)],
        compiler_params=_cp("parallel", "parallel"),
        name="cache_prep",
    )(ck, cv)


def _diff_attn_kernel(q_ref, kt_ref, v_ref, lam_ref, g_ref, o_ref, m_ref, acc_ref, *, lam_init):
    ik = pl.program_id(3)

    @pl.when(ik == 0)
    def _():
        m_ref[...] = jnp.full(m_ref.shape, NEG, F32)
        acc_ref[...] = jnp.zeros(acc_ref.shape, F32)

    chains = [(h, m) for h in range(q_ref.shape[1]) for m in range(2)]
    s = [jnp.dot(q_ref[0, h, m], kt_ref[0, h], preferred_element_type=F32) for h, m in chains]
    m_old = [m_ref[h, m] for h, m in chains]
    m_new = [jnp.maximum(x, jnp.max(y, axis=1, keepdims=True)) for x, y in zip(m_old, s)]
    p = [jnp.exp(x - y[:, 0:1]).astype(BF16) for x, y in zip(s, m_new)]
    pv = [jnp.dot(x, v_ref[0, h], preferred_element_type=F32) for x, (h, m) in zip(p, chains)]
    for i, (h, m) in enumerate(chains):
        acc_ref[h, m] = jnp.exp(m_old[i] - m_new[i]) * acc_ref[h, m] + pv[i]
        m_ref[h, m] = m_new[i]

    @pl.when(ik == pl.num_programs(3) - 1)
    def _():
        lp = lam_ref[...]
        lam = (jnp.exp(jnp.sum(lp[0:1] * lp[1:2], axis=1, keepdims=True))
               - jnp.exp(jnp.sum(lp[2:3] * lp[3:4], axis=1, keepdims=True)) + lam_init)
        for h in range(q_ref.shape[1]):
            a0, a1 = acc_ref[h, 0], acc_ref[h, 1]
            o = (a0[:, :HEAD_DIM] / a0[:, HEAD_DIM:HEAD_DIM + 1]
                 - lam * (a1[:, :HEAD_DIM] / a1[:, HEAD_DIM:HEAD_DIM + 1]))
            var = jnp.mean(o * o, axis=1, keepdims=True)
            o_ref[0, h] = o * lax.rsqrt(var + EPS) * g_ref[...] * (1.0 - lam_init)


ATTN_HEADS_PER_STEP = 2


def _diff_attn(q, kt, v, lam_par, gain, lam_init):
    b, _, _, l, _ = q.shape
    lk = kt.shape[-1]
    hs = ATTN_HEADS_PER_STEP
    tq = _tile(l, 512)
    tk = _tile(lk, 2304)
    return pl.pallas_call(
        functools.partial(_diff_attn_kernel, lam_init=lam_init),
        grid=(b, N_HEADS // hs, l // tq, lk // tk),
        in_specs=[pl.BlockSpec((1, hs, 2, tq, HEAD_DIM), lambda i, h, a, c: (i, h, 0, a, 0)),
                  pl.BlockSpec((1, hs, HEAD_DIM, tk), lambda i, h, a, c: (i, h, 0, c)),
                  pl.BlockSpec((1, hs, tk, 2 * HEAD_DIM), lambda i, h, a, c: (i, h, c, 0)),
                  pl.BlockSpec((4, DF_DH), lambda i, h, a, c: (0, 0)),
                  pl.BlockSpec((1, HEAD_DIM), lambda i, h, a, c: (0, 0))],
        out_specs=pl.BlockSpec((1, hs, tq, HEAD_DIM), lambda i, h, a, c: (i, h, a, 0)),
        out_shape=jax.ShapeDtypeStruct((b, N_HEADS, l, HEAD_DIM), F32),
        scratch_shapes=[pltpu.VMEM((hs, 2, tq, LANES), F32), pltpu.VMEM((hs, 2, tq, 2 * HEAD_DIM), F32)],
        compiler_params=_cp("parallel", "parallel", "parallel", "arbitrary"),
        name="diff_attn",
    )(q, kt, v, lam_par, gain.reshape(1, HEAD_DIM))


def _hy_filter_kernel(feat_ref, win_ref, w1_ref, b1_ref, w2_ref, b2_ref, w3_ref, fr_ref, o_ref):
    h = jnp.sin(fr_ref[0:1, :] * (_dot_hi(feat_ref[...], w1_ref[...]) + b1_ref[...]))
    h = jnp.sin(fr_ref[1:2, :] * (_dot_hi(h, w2_ref[...]) + b2_ref[...]))
    h = _dot_hi(h, w3_ref[...]) * win_ref[...]
    o_ref[...] = h * lax.rsqrt(jnp.sum(h * h, axis=0, keepdims=True) + EPS)


def _hy_constants(l):
    pos = jnp.arange(l, dtype=F32)
    bands = jnp.arange(1, HY_BANDS + 1, dtype=F32)
    ang = (2.0 * math.pi / l) * pos[:, None] * bands[None, :]
    feat = jnp.concatenate([pos[:, None] / l, jnp.cos(ang), jnp.sin(ang)], axis=-1)
    feat = jnp.pad(feat, ((0, 0), (0, LANES - feat.shape[1])))
    rates = jnp.linspace(-math.log(HY_TARGET) / HY_FAST_DECAY, -math.log(HY_TARGET) / HY_SLOW_DECAY,
                         GROUP_W, dtype=F32)
    rates = jnp.tile(rates, 2)
    offset = jnp.abs(pos - l // 2) / l
    return feat, jnp.exp(-offset[:, None] * rates[None, :]) + HY_SHIFT


def _hy_filters(feat, win, w1, b1, w2, b2, w3, freq):
    l = feat.shape[0]
    hid = w2.shape[0]
    w1p = jnp.pad(w1, ((0, LANES - w1.shape[0]), (0, 0)))
    full = lambda shape: pl.BlockSpec(shape, lambda o: (0, 0))
    return pl.pallas_call(
        _hy_filter_kernel,
        grid=(2,),
        in_specs=[full((l, LANES)), pl.BlockSpec((l, GROUP_W), lambda o: (0, o)),
                  full((LANES, hid)), full((1, hid)), full((hid, hid)), full((1, hid)),
                  pl.BlockSpec((hid, GROUP_W), lambda o: (0, o)), full((2, hid))],
        out_specs=pl.BlockSpec((l, GROUP_W), lambda o: (0, o)),
        out_shape=jax.ShapeDtypeStruct((l, 2 * GROUP_W), F32),
        compiler_params=_cp("parallel"),
        name="hy_filters",
    )(feat, win, w1p, b1.reshape(1, hid), w2, b2.reshape(1, hid), w3, freq)


def _dft_matrices(l):
    n = 2 * l
    k = jnp.arange(l, dtype=jnp.int32)
    alt = jnp.where(k % 2 == 0, 1.0, -1.0).astype(F32)
    cos_f, sin_f = _cos_sin_grid(0, l, n)
    fwd_im = jnp.where(k[:, None] == 0, alt[None, :], -sin_f)
    fwd = jnp.stack([cos_f, fwd_im]).astype(BF16)
    t = k + l // 2
    cos_i, sin_i = _cos_sin_grid(l // 2, l, n)
    alt_t = jnp.where(t % 2 == 0, 1.0, -1.0).astype(F32)
    inv_re = jnp.where(k[None, :] == 0, 1.0 / n, (2.0 / n) * cos_i)
    inv_im = jnp.where(k[None, :] == 0, alt_t[:, None] / n, -(2.0 / n) * sin_i)
    inv = jnp.concatenate([inv_re, inv_im], axis=1).astype(BF16)
    return fwd, inv


def _cos_sin_grid(row0, l, n):
    s = 1 << (int(math.log2(l)) // 2)
    assert row0 % s == 0 and l % s == 0
    c = jnp.arange(l, dtype=jnp.int32)
    hi = row0 + jnp.arange(l // s, dtype=jnp.int32) * s
    lo = jnp.arange(s, dtype=jnp.int32)

    def angle(r):
        return (2.0 * math.pi / n) * ((r[:, None] * c[None, :]) % n).astype(F32)

    a, b = angle(hi), angle(lo)
    ca, sa = jnp.cos(a)[:, None, :], jnp.sin(a)[:, None, :]
    cb, sb = jnp.cos(b)[None, :, :], jnp.sin(b)[None, :, :]
    return (ca * cb - sa * sb).reshape(l, l), (sa * cb + ca * sb).reshape(l, l)


def _dft_filter_kernel(f_ref, h_ref, o_ref):
    h = h_ref[...].astype(BF16)
    o_ref[0] = jnp.dot(f_ref[0], h, preferred_element_type=F32)
    o_ref[1] = jnp.dot(f_ref[1], h, preferred_element_type=F32)


def _dft_filter(fwd, filt):
    l, c = filt.shape
    tm = _tile(l, 256)
    return pl.pallas_call(
        _dft_filter_kernel,
        grid=(l // tm,),
        in_specs=[pl.BlockSpec((2, tm, l), lambda i: (0, i, 0)), pl.BlockSpec((l, c), lambda i: (0, 0))],
        out_specs=pl.BlockSpec((2, tm, c), lambda i: (0, i, 0)),
        out_shape=jax.ShapeDtypeStruct((2, l, c), F32),
        compiler_params=_cp("parallel"),
        name="dft_filter",
    )(fwd, filt)


def _dft_fwd_mul_kernel(f_ref, u_ref, h_ref, o_ref):
    u = u_ref[0]
    ur = jnp.dot(f_ref[0], u, preferred_element_type=F32)
    ui = jnp.dot(f_ref[1], u, preferred_element_type=F32)
    hr, hi = h_ref[0], h_ref[1]
    row = lax.broadcasted_iota(jnp.int32, ur.shape, 0) + pl.program_id(0) * ur.shape[0]
    packed = row == 0
    o_ref[0, 0] = (ur * hr - jnp.where(packed, 0.0, ui * hi)).astype(BF16)
    o_ref[0, 1] = jnp.where(packed, ui * hi, ur * hi + ui * hr).astype(BF16)


def _dft_fwd_mul(fwd, u, hf, order):
    b, l, _ = u.shape
    tm = _tile(l, 1024)
    return pl.pallas_call(
        _dft_fwd_mul_kernel,
        grid=(l // tm, b),
        in_specs=[pl.BlockSpec((2, tm, l), lambda i, j: (0, i, 0)),
                  pl.BlockSpec((1, l, GROUP_W), lambda i, j: (j, 0, 0)),
                  pl.BlockSpec((2, tm, GROUP_W), lambda i, j: (0, i, order))],
        out_specs=pl.BlockSpec((1, 2, tm, GROUP_W), lambda i, j: (j, 0, i, 0)),
        out_shape=jax.ShapeDtypeStruct((b, 2, l, GROUP_W), BF16),
        compiler_params=_cp("parallel", "parallel"),
        name="dft_fwd_mul",
    )(fwd, u, hf)


def _dft_inv_gate_kernel(g_ref, y_ref, x_ref, u_ref, skip_ref, o_ref, lowp_ref):
    conv = jnp.dot(g_ref[...], y_ref[0], preferred_element_type=F32)
    out = x_ref[0] * (conv + skip_ref[...] * u_ref[0])
    o_ref[0] = out
    lowp_ref[0] = out.astype(BF16)


def _dft_inv_gate(inv, y, xg, x_col, u, u_col, skip):
    b, _, l, _ = y.shape
    tm = _tile(l, 1024)
    out_spec = pl.BlockSpec((1, tm, GROUP_W), lambda i, j: (j, i, 0))
    return pl.pallas_call(
        _dft_inv_gate_kernel,
        grid=(l // tm, b),
        in_specs=[pl.BlockSpec((tm, 2 * l), lambda i, j: (i, 0)),
                  pl.BlockSpec((1, 2 * l, GROUP_W), lambda i, j: (j, 0, 0)),
                  pl.BlockSpec((1, tm, GROUP_W), lambda i, j: (j, i, x_col)),
                  pl.BlockSpec((1, tm, GROUP_W), lambda i, j: (j, i, u_col)),
                  pl.BlockSpec((1, GROUP_W), lambda i, j: (0, 0))],
        out_specs=[out_spec, out_spec],
        out_shape=[jax.ShapeDtypeStruct((b, l, GROUP_W), F32), jax.ShapeDtypeStruct((b, l, GROUP_W), BF16)],
        compiler_params=_cp("parallel", "parallel"),
        name="dft_inv_gate",
    )(inv, y.reshape(b, 2 * l, GROUP_W), xg, u, skip.reshape(1, GROUP_W))


def _head_norm(x, gain):
    return x * lax.rsqrt(_seg_mean(x * x, HEAD_DIM) + EPS) * gain


def _proj_out_kernel(dnf_ref, dnb_ref, gate_ref, df_ref, hy_ref, mlf_ref, mlb_ref, mo_ref, x_ref, mod_ref,
                     gdn_ref, gml_ref, w_ref, o_ref):
    gate = gate_ref[0]
    y_dn = _head_norm(dnf_ref[0] + dnb_ref[0], gdn_ref[...]) * (gate * _sigmoid(gate))
    y_ml = _head_norm(mlf_ref[0] + mlb_ref[0], gml_ref[...]) * _sigmoid(mo_ref[0])
    y = _dot(y_dn, w_ref[0:GROUP_W, :])
    for h in range(N_HEADS):
        lo = GROUP_W + h * HEAD_DIM
        y = y + _dot(df_ref[0, h], w_ref[lo:lo + HEAD_DIM, :])
    y = y + _dot(hy_ref[0], w_ref[2 * GROUP_W:3 * GROUP_W, :])
    y = y + _dot(y_ml, w_ref[3 * GROUP_W:, :])
    o_ref[0] = x_ref[0] + mod_ref[0, 2:3, :] * y


def _proj_out(o_dn, z, o_df, y_hy, h_ml, x, mod, g_dn, g_ml, w):
    b, l, d = x.shape
    tm = _tile(l, 512)
    per_batch = mod.shape[0] > 1
    group = lambda col: pl.BlockSpec((1, tm, GROUP_W), lambda i, j: (i, j, col))
    return pl.pallas_call(
        _proj_out_kernel,
        grid=(b, l // tm),
        in_specs=[group(0), group(0), group(COL_DGATE),
                  pl.BlockSpec((1, N_HEADS, tm, HEAD_DIM), lambda i, j: (i, 0, j, 0)),
                  group(0), group(0), group(0), group(COL_MO),
                  pl.BlockSpec((1, tm, d), lambda i, j: (i, j, 0)),
                  pl.BlockSpec((1, 6, d), lambda i, j: (i if per_batch else 0, 0, 0)),
                  pl.BlockSpec((1, GROUP_W), lambda i, j: (0, 0)),
                  pl.BlockSpec((1, GROUP_W), lambda i, j: (0, 0)),
                  pl.BlockSpec((d, d), lambda i, j: (0, 0))],
        out_specs=pl.BlockSpec((1, tm, d), lambda i, j: (i, j, 0)),
        out_shape=jax.ShapeDtypeStruct((b, l, d), F32),
        compiler_params=_cp("parallel", "parallel"),
        name="proj_out",
    )(o_dn[0], o_dn[1], z, o_df, y_hy, h_ml[0], h_ml[1], z, x, mod, jnp.tile(g_dn, N_HEADS).reshape(1, GROUP_W),
      jnp.tile(g_ml, N_HEADS).reshape(1, GROUP_W), w)


def _ffn_kernel(x_ref, mod_ref, g_ref, wg_ref, wu_ref, wd_ref, o_ref, h_ref, acc_ref):
    j = pl.program_id(2)

    @pl.when(j == 0)
    def _():
        h_ref[...] = _modulated_norm(x_ref[0], g_ref[...], mod_ref[0, 3:4, :], mod_ref[0, 4:5, :]).astype(BF16)
        acc_ref[...] = jnp.zeros(acc_ref.shape, F32)

    h = h_ref[...]
    g = jnp.dot(h, wg_ref[...], preferred_element_type=F32)
    u = jnp.dot(h, wu_ref[...], preferred_element_type=F32)
    acc_ref[...] += jnp.dot((g * _sigmoid(g) * u).astype(BF16), wd_ref[...], preferred_element_type=F32)

    @pl.when(j == pl.num_programs(2) - 1)
    def _():
        o_ref[0] = x_ref[0] + mod_ref[0, 5:6, :] * acc_ref[...]


def _ffn(x, mod, gain, wg, wu, wd):
    b, l, d = x.shape
    f = wg.shape[1]
    tm = _tile(l, 1024)
    tf = 256
    per_batch = mod.shape[0] > 1
    return pl.pallas_call(
        _ffn_kernel,
        grid=(b, l // tm, f // tf),
        in_specs=[pl.BlockSpec((1, tm, d), lambda i, a, j: (i, a, 0)),
                  pl.BlockSpec((1, 6, d), lambda i, a, j: (i if per_batch else 0, 0, 0)),
                  pl.BlockSpec((1, d), lambda i, a, j: (0, 0)),
                  pl.BlockSpec((d, tf), lambda i, a, j: (0, j)),
                  pl.BlockSpec((d, tf), lambda i, a, j: (0, j)),
                  pl.BlockSpec((tf, d), lambda i, a, j: (j, 0))],
        out_specs=pl.BlockSpec((1, tm, d), lambda i, a, j: (i, a, 0)),
        out_shape=jax.ShapeDtypeStruct((b, l, d), F32),
        scratch_shapes=[pltpu.VMEM((tm, d), BF16), pltpu.VMEM((tm, d), F32)],
        compiler_params=_cp("parallel", "parallel", "arbitrary"),
        name="ffn_dense",
    )(x, mod, gain.reshape(1, d), wg, wu, wd)


def _top2_combine(logits):
    lane = lax.broadcasted_iota(jnp.int32, logits.shape, 1)
    v1 = jnp.max(logits, axis=1, keepdims=True)
    i1 = jnp.min(jnp.where(logits == v1, lane, LANES), axis=1, keepdims=True)
    rest = jnp.where(lane == i1, NEG, logits)
    v2 = jnp.max(rest, axis=1, keepdims=True)
    i2 = jnp.min(jnp.where(rest == v2, lane, LANES), axis=1, keepdims=True)
    e2 = jnp.exp(v2 - v1)
    return jnp.where(lane == i1, 1.0 / (1.0 + e2), 0.0) + jnp.where(lane == i2, e2 / (1.0 + e2), 0.0)


MOE_ROW_TILE = 512


def _moe_route_kernel(x_ref, mod_ref, g_ref, r_ref, h_ref, comb_ref, rank_ref, cnt_ref):
    @pl.when((pl.program_id(0) == 0) & (pl.program_id(1) == 0))
    def _():
        cnt_ref[...] = jnp.zeros(cnt_ref.shape, F32)

    hn = _modulated_norm(x_ref[0], g_ref[...], mod_ref[0, 3:4, :], mod_ref[0, 4:5, :])
    tm = hn.shape[0]
    for j in range(h_ref.shape[0]):
        h_ref[j, 0] = hn[:, j * LANES:(j + 1) * LANES]
    lane = lax.broadcasted_iota(jnp.int32, (tm, LANES), 1)
    comb = _top2_combine(jnp.where(lane < N_EXPERTS, _dot_hi(hn, r_ref[...]), NEG))
    comb_ref[0] = comb
    routed = jnp.where(comb > 0.0, 1.0, 0.0)
    row = lax.broadcasted_iota(jnp.int32, (tm, tm), 0)
    col = lax.broadcasted_iota(jnp.int32, (tm, tm), 1)
    earlier = jnp.where(col < row, 1.0, 0.0).astype(BF16)
    rank = cnt_ref[...] + jnp.dot(earlier, routed.astype(BF16), preferred_element_type=F32)
    rank_ref[0] = jnp.where(routed > 0.0, rank, -1.0)
    cnt_ref[...] += jnp.sum(routed, axis=0, keepdims=True)


def _moe_group_kernel(tile_expert_ref, used_ref, xs_ref, wg_ref, wu_ref, wd_ref, o_ref):
    del tile_expert_ref
    pieces = xs_ref.shape[0]

    @pl.when(pl.program_id(0) < used_ref[0])
    def _():
        xs = jnp.concatenate([xs_ref[j] for j in range(pieces)], axis=1).astype(BF16)
        g = jnp.dot(xs, wg_ref[0], preferred_element_type=F32)
        u = jnp.dot(xs, wu_ref[0], preferred_element_type=F32)
        y = jnp.dot((g * _sigmoid(g) * u).astype(BF16), wd_ref[0], preferred_element_type=F32)
        for j in range(pieces):
            o_ref[j] = y[:, j * LANES:(j + 1) * LANES]

    @pl.when(pl.program_id(0) >= used_ref[0])
    def _():
        o_ref[...] = jnp.zeros(o_ref.shape, F32)


def _moe_combine_kernel(x_ref, y_ref, comb_ref, mod_ref, o_ref):
    comb = comb_ref[0]
    gate_a = jnp.max(comb, axis=1, keepdims=True)
    gate_b = jnp.sum(comb, axis=1, keepdims=True) - gate_a
    pieces = y_ref.shape[0]
    y_a = jnp.concatenate([y_ref[j, 0, 0] for j in range(pieces)], axis=1)
    y_b = jnp.concatenate([y_ref[j, 1, 0] for j in range(pieces)], axis=1)
    o_ref[0] = x_ref[0] + mod_ref[0, 5:6, :] * (gate_a * y_a + gate_b * y_b)


MOE_SLOT_ALIGN = 2048


def _moe(x, mod, gain, router, wg, wu, wd):
    b, l, d = x.shape
    ne, _, f = wg.shape
    t = b * l
    tm = _tile(l, 1024)
    tg = MOE_ROW_TILE
    per_batch = mod.shape[0] > 1
    router_p = jnp.pad(router, ((0, 0), (0, LANES - ne)))
    x_spec = pl.BlockSpec((1, tm, d), lambda i, j: (i, j, 0))
    mod_spec = pl.BlockSpec((1, 6, d), lambda i, j: (i if per_batch else 0, 0, 0))
    pieces = d // LANES
    lane_spec = pl.BlockSpec((1, tm, LANES), lambda i, j: (i, j, 0))
    h, comb, rank, cnt = pl.pallas_call(
        _moe_route_kernel,
        grid=(b, l // tm),
        in_specs=[x_spec, mod_spec, pl.BlockSpec((1, d), lambda i, j: (0, 0)),
                  pl.BlockSpec((d, LANES), lambda i, j: (0, 0))],
        out_specs=[pl.BlockSpec((pieces, 1, tm, LANES), lambda i, j: (0, i, j, 0)), lane_spec, lane_spec,
                   pl.BlockSpec((1, LANES), lambda i, j: (0, 0))],
        out_shape=[jax.ShapeDtypeStruct((pieces, b, l, LANES), F32), jax.ShapeDtypeStruct((b, l, LANES), F32),
                   jax.ShapeDtypeStruct((b, l, LANES), F32), jax.ShapeDtypeStruct((1, LANES), F32)],
        compiler_params=_cp("arbitrary", "arbitrary"),
        name="moe_route",
    )(x, mod, gain.reshape(1, d), router_p)

    n_slots = -(-(2 * t + (ne + 1) * tg) // MOE_SLOT_ALIGN) * MOE_SLOT_ALIGN
    comb2 = comb.reshape(t, LANES)[:, :ne]
    rank2 = rank.reshape(t, LANES)[:, :ne]
    grp = (cnt[0, :ne].astype(jnp.int32) + tg - 1) // tg * tg
    end = jnp.cumsum(grp)
    off = end - grp
    tile_start = jnp.arange(n_slots // tg, dtype=jnp.int32) * tg
    tile_expert = jnp.minimum(jnp.sum(tile_start[:, None] >= end[None, :], axis=1), ne - 1).astype(jnp.int32)
    used_tiles = (end[-1:] // tg).astype(jnp.int32)
    first = jnp.argmax(comb2, axis=1)
    second = jnp.argmax(jnp.where(jnp.arange(ne)[None, :] == first[:, None], -1.0, comb2), axis=1)
    pick = lambda a, e: jnp.take_along_axis(a, e[:, None], axis=1)[:, 0]
    rank_b = pick(rank2, second)
    pos_a = off[first] + pick(rank2, first).astype(jnp.int32)
    pos_b = off[second] + rank_b.astype(jnp.int32)
    token = jnp.arange(t, dtype=jnp.int32)
    slot_token = jnp.zeros((n_slots,), jnp.int32).at[
        jnp.concatenate([pos_a, jnp.where(rank_b >= 0.0, pos_b, n_slots)])].set(
        jnp.concatenate([token, token]), mode="drop")
    piece_base = jnp.arange(pieces, dtype=jnp.int32)[:, None]

    xs = _sc_gather_pieces(h.reshape(pieces * t, LANES), (piece_base * t + slot_token[None, :]).reshape(-1))
    piece_rows = pl.BlockSpec((pieces, tg, LANES), lambda i, te, used: (0, i, 0))
    ys = pl.pallas_call(
        _moe_group_kernel,
        grid_spec=pltpu.PrefetchScalarGridSpec(
            num_scalar_prefetch=2,
            grid=(n_slots // tg,),
            in_specs=[piece_rows,
                      pl.BlockSpec((1, d, f), lambda i, te, used: (te[i], 0, 0)),
                      pl.BlockSpec((1, d, f), lambda i, te, used: (te[i], 0, 0)),
                      pl.BlockSpec((1, f, d), lambda i, te, used: (te[i], 0, 0))],
            out_specs=piece_rows),
        out_shape=jax.ShapeDtypeStruct((pieces, n_slots, LANES), F32),
        compiler_params=_cp("arbitrary"),
        name="moe_group",
    )(tile_expert, used_tiles, xs.reshape(pieces, n_slots, LANES), wg, wu, wd)
    pair = jnp.concatenate([pos_a, jnp.where(rank_b >= 0.0, pos_b, n_slots - 1)])
    y2 = _sc_gather_pieces(ys.reshape(pieces * n_slots, LANES), (piece_base * n_slots + pair[None, :]).reshape(-1))
    return pl.pallas_call(
        _moe_combine_kernel,
        grid=(b, l // tm),
        in_specs=[x_spec, pl.BlockSpec((pieces, 2, 1, tm, LANES), lambda i, j: (0, 0, i, j, 0)), lane_spec, mod_spec],
        out_specs=x_spec,
        out_shape=jax.ShapeDtypeStruct((b, l, d), F32),
        compiler_params=_cp("parallel", "parallel"),
        name="moe_combine",
    )(x, y2.reshape(pieces, 2, b, l, LANES), comb, mod)


SC_GATHER_WINDOW = 128


def _sc_gather_pieces(table, idx):
    m = idx.shape[0]
    w = table.shape[1]
    mesh = plsc.VectorSubcoreMesh(core_axis_name="core", subcore_axis_name="subcore")

    @functools.partial(pl.kernel, out_type=jax.ShapeDtypeStruct((m, w), table.dtype), mesh=mesh,
                       name="sc_gather_rows")
    def gather(t_hbm, i_hbm, o_hbm):
        def body(i_vmem, o_vmem):
            pltpu.sync_copy(t_hbm.at[i_vmem.at[0]], o_vmem)

        pltpu.emit_pipeline(
            body,
            grid=(m // SC_GATHER_WINDOW,),
            in_specs=[pl.BlockSpec((1, SC_GATHER_WINDOW), lambda i: (0, i))],
            out_specs=[pl.BlockSpec((SC_GATHER_WINDOW, w), lambda i: (i, 0))],
            core_axis_name=("core", "subcore"),
            dimension_semantics=(pltpu.PARALLEL,),
        )(i_hbm, o_hbm)

    return gather(table, idx.reshape(1, m))


def _lane_row(values, lane0):
    row = jnp.zeros((LANES,), F32)
    return lax.dynamic_update_slice(row, values.reshape(-1).astype(F32), (lane0,))


def _reorder_w_in(w_in):
    g = GROUP_W
    o = [0, g, 2 * g, 3 * g, 4 * g, 4 * g + 8, 4 * g + 16]
    dq_dk_dv = w_in[..., o[0]:o[3]]
    dgate = w_in[..., o[3]:o[4]]
    dbeta_da = w_in[..., o[4]:o[6]]
    base = o[6]
    df = w_in[..., base:base + 3 * g]
    hy = w_in[..., base + 3 * g:base + 6 * g]
    ml = w_in[..., base + 6 * g:base + 9 * g]
    mo = w_in[..., base + 9 * g:base + 10 * g]
    mi_mf = w_in[..., base + 10 * g:base + 10 * g + 16]
    pad = jnp.zeros(w_in.shape[:-1] + (LANES - 32,), w_in.dtype)
    return jnp.concatenate([dq_dk_dv, hy, ml, df, dgate, mo, dbeta_da, mi_mf, pad], axis=-1).astype(BF16)


def _layer(l, x, mod, cache, p, hy_consts, dft):
    b, seq, _ = x.shape
    latent = cache is not None
    z = _proj_in(x, mod, p["norm1_g"][l], p["w_in"][l])

    qkv, = _dwconv(z, p["dn_conv"][l], COL_DN, "deltanet")
    dn_par = jnp.zeros((8, LANES), F32).at[0].set(_lane_row(p["dn_a_log"][l], LANE_A)).at[1].set(
        _lane_row(p["dn_dt_bias"][l], LANE_A))
    s0 = cache[2] if latent else jnp.zeros((b, 2, N_HEADS, HEAD_DIM, HEAD_DIM), F32)
    o_dn_f, o_dn_b, s_dn = _deltanet(qkv, z, dn_par, s0)
    o_dn = (o_dn_f, o_dn_b)

    ml_par = jnp.zeros((8, LANES), F32).at[0].set(_lane_row(p["ml_i_bias"][l], LANE_I)).at[1].set(
        _lane_row(p["ml_f_bias"][l], LANE_F))
    if latent:
        c0 = cache[3]
        n0 = cache[4].reshape(b, 2, 1, GROUP_W)
        m0 = jnp.repeat(cache[5], HEAD_DIM, axis=-1).reshape(b, 2, 1, GROUP_W)
    else:
        c0 = jnp.zeros((b, 2, N_HEADS, HEAD_DIM, HEAD_DIM), F32)
        n0 = m0 = jnp.zeros((b, 2, 1, GROUP_W), F32)
    h_ml_f, h_ml_b, c_ml, n_ml, m_ml = _mlstm(z, ml_par, c0, n0, m0)
    h_ml = (h_ml_f, h_ml_b)
    n_ml = n_ml.reshape(b, 2, N_HEADS, HEAD_DIM)
    m_ml = m_ml.reshape(b, 2, N_HEADS, HEAD_DIM)[..., 0]

    q, kt, v, kd = _diff_prep(z, p["df_q_norm"][l], p["df_k_norm"][l], latent)
    if latent:
        ckt, cv = _cache_prep(cache[0].reshape(b, -1, GROUP_W), cache[1].reshape(b, -1, GROUP_W))
        kt = jnp.concatenate([kt, ckt], axis=-1)
        v = jnp.concatenate([v, cv], axis=2)
    lam_init = 0.8 - 0.6 * math.exp(-0.3 * l)
    o_df = _diff_attn(q, kt, v, p["df_lambda"][l], p["df_norm"][l], lam_init)

    feat, win = hy_consts
    fwd, inv = dft
    zc, zv_lowp = _dwconv(z, p["hy_conv"][l], COL_HY, "plain")
    filt = _hy_filters(feat, win, p["hy_w1"][l], p["hy_b1"][l], p["hy_w2"][l], p["hy_b2"][l], p["hy_w3"][l],
                       p["hy_freq"][l])
    hf = _dft_filter(fwd, filt)
    y1 = _dft_fwd_mul(fwd, zv_lowp, hf, 0)
    z1, z1_lowp = _dft_inv_gate(inv, y1, zc, 0, zc, 2, p["hy_skip"][l, 0])
    y2 = _dft_fwd_mul(fwd, z1_lowp, hf, 1)
    y_hy, _ = _dft_inv_gate(inv, y2, zc, 1, z1, 0, p["hy_skip"][l, 1])

    x = _proj_out(o_dn, z, o_df, y_hy, h_ml, x, mod, p["dn_norm"][l], p["ml_norm"][l], p["w_out"][l])
    j = l // 2
    if mod.shape[0] == 1:
        x = x.reshape(1, b * seq, -1)
    if l % 2 == 0:
        x = _ffn(x, mod, p["norm2_g"][l], p["ffn_w_gate"][j], p["ffn_w_up"][j], p["ffn_w_down"][j])
    else:
        x = _moe(x, mod, p["norm2_g"][l], p["moe_router"][j], p["moe_w_gate"][j], p["moe_w_up"][j],
                 p["moe_w_down"][j])
    x = x.reshape(b, seq, -1)
    fv =z[:, :, 3 * 3 * GROUP_W + 2 * GROUP_W:3 * 3 * GROUP_W + 3 * GROUP_W]
    return x, (kd, fv, s_dn, c_ml, n_ml, m_ml)


def kernel(x_prompt, x_sample, cache_diff_k, cache_diff_v, state_delta, state_mlstm_c, state_mlstm_n, state_mlstm_m, c, c_ctx, norm1_g, norm2_g, w_mod, b_mod, w_in, w_out, dn_conv, dn_a_log, dn_dt_bias, dn_norm, df_q_norm, df_k_norm, df_lambda, df_norm, hy_conv, hy_w1, hy_b1, hy_w2, hy_b2, hy_w3, hy_freq, hy_skip, ml_i_bias, ml_f_bias, ml_norm, ffn_w_gate, ffn_w_up, ffn_w_down, moe_router, moe_w_gate, moe_w_up, moe_w_down):
    depth = w_in.shape[0]
    d_model = x_prompt.shape[-1]
    batch, seq, _ = x_prompt.shape
    dec_batch, dec_seq, _ = x_sample.shape
    p = dict(norm1_g=norm1_g, norm2_g=norm2_g, w_in=_reorder_w_in(w_in), w_out=w_out.astype(BF16),
             dn_conv=dn_conv, dn_a_log=dn_a_log, dn_dt_bias=dn_dt_bias, dn_norm=dn_norm,
             df_q_norm=df_q_norm, df_k_norm=df_k_norm, df_lambda=df_lambda, df_norm=df_norm,
             hy_conv=hy_conv, hy_w1=hy_w1, hy_b1=hy_b1, hy_w2=hy_w2, hy_b2=hy_b2, hy_w3=hy_w3,
             hy_freq=hy_freq, hy_skip=hy_skip, ml_i_bias=ml_i_bias, ml_f_bias=ml_f_bias, ml_norm=ml_norm,
             ffn_w_gate=ffn_w_gate.astype(BF16), ffn_w_up=ffn_w_up.astype(BF16),
             ffn_w_down=ffn_w_down.astype(BF16), moe_router=moe_router,
             moe_w_gate=moe_w_gate.astype(BF16), moe_w_up=moe_w_up.astype(BF16),
             moe_w_down=moe_w_down.astype(BF16))

    n_cond = 1 + dec_batch
    rows = -(-n_cond // 8) * 8
    cond = jnp.concatenate([c_ctx[None, :], c, jnp.zeros((rows - n_cond, d_model), F32)], axis=0)
    mod = _modulation(cond, w_mod, b_mod).reshape(depth, rows, 6, d_model)

    hy_ctx, dft_ctx = _hy_constants(seq), _dft_matrices(seq)
    x = x_prompt
    ctx = []
    for l in range(depth):
        x, out = _layer(l, x, mod[l, 0:1], None, p, hy_ctx, dft_ctx)
        ctx.append(out)
    y_prompt = x
    new_k, new_v, new_s, new_c, new_n, new_m = (jnp.stack([o[i] for o in ctx], axis=1) for i in range(6))
    new_k = new_k.reshape(batch, depth, seq, N_HEADS, 2, DF_DH)
    new_v = new_v.reshape(batch, depth, seq, N_HEADS, HEAD_DIM)

    hy_lat, dft_lat = _hy_constants(dec_seq), _dft_matrices(dec_seq)
    x = x_sample
    for l in range(depth):
        cache = (cache_diff_k[:, l], cache_diff_v[:, l], state_delta[:, l], state_mlstm_c[:, l],
                 state_mlstm_n[:, l], state_mlstm_m[:, l])
        x, _ = _layer(l, x, mod[l, 1:1 + dec_batch], cache, p, hy_lat, dft_lat)
    return (y_prompt, x, new_k, new_v, new_s, new_c, new_n, new_m)
```

```python
import functools
import math
from typing import Any, NamedTuple

import jax
import jax.numpy as jnp
from jax import lax
from jax.experimental import pallas as pl
from jax.experimental.pallas import tpu as pltpu
from jax.experimental.pallas import tpu_sc as plsc

F32 = jnp.float32
BF16 = jnp.bfloat16

N_HEADS = 4
HEAD_DIM = 64
GROUP_W = N_HEADS * HEAD_DIM
DF_DH = 32
CHUNK = 64
GRID_W = 64
ROPE_THETA = 10000.0
HY_BANDS = 8
HY_FAST_DECAY = 0.3
HY_SLOW_DECAY = 1.5
HY_TARGET = 1e-2
HY_SHIFT = 0.05
N_EXPERTS = 8
EPS = 1e-6
NEG = -1e30
LANES = 128
VMEM_LIMIT = 56 * 1024 * 1024

COL_DN, COL_HY, COL_ML, COL_DF = 0, 1, 2, 3
COL_DGATE, COL_MO = 12, 13
COL_SMALL = 28
Z_WIDTH = 29 * LANES
LANE_BETA, LANE_A, LANE_I, LANE_F = 0, 8, 16, 24


def _tile(n, pref):
    t = min(n, pref)
    while n % t:
        t -= LANES if t > LANES else 8
    return t


def _cp(*sem):
    return pltpu.CompilerParams(dimension_semantics=sem, vmem_limit_bytes=VMEM_LIMIT)


def _split3(x):
    x1 = x.astype(BF16)
    r = x - x1.astype(F32)
    x2 = r.astype(BF16)
    r = r - x2.astype(F32)
    return x1, x2, r.astype(BF16)


def _dot(a, b):
    return jnp.dot(a.astype(BF16), b.astype(BF16), preferred_element_type=F32)


def _dot_nt(a, b):
    return lax.dot_general(a.astype(BF16), b.astype(BF16), (((1,), (1,)), ((), ())),
                           preferred_element_type=F32)


def _dot_tn(a, b):
    return lax.dot_general(a.astype(BF16), b.astype(BF16), (((0,), (0,)), ((), ())),
                           preferred_element_type=F32)


def _dot_exact_l(m, x):
    return sum(jnp.dot(m, p, preferred_element_type=F32) for p in _split3(x))


def _dot_exact_r(x, m):
    return sum(jnp.dot(p, m, preferred_element_type=F32) for p in _split3(x))


def _dot_hi(a, b):
    a1, a2, _ = _split3(a)
    b1, b2, _ = _split3(b)
    return (jnp.dot(a1, b1, preferred_element_type=F32) + jnp.dot(a1, b2, preferred_element_type=F32)
            + jnp.dot(a2, b1, preferred_element_type=F32))


def _seg_mean(x, seg):
    w = x.shape[-1]
    sh = int(math.log2(seg))
    r = lax.shift_right_logical(lax.broadcasted_iota(jnp.int32, (w, w), 0), sh)
    c = lax.shift_right_logical(lax.broadcasted_iota(jnp.int32, (w, w), 1), sh)
    bd = jnp.where(r == c, 1.0, 0.0).astype(BF16)
    return _dot_exact_r(x, bd) * (1.0 / seg)


def _row_bcast(col):
    n = col.shape[0]
    return jnp.broadcast_to(jnp.broadcast_to(col, (n, LANES)).T[0:1, :], (n, n))


def _lane_pick(x, lane_idx):
    lane = lax.broadcasted_iota(jnp.int32, x.shape, 1)
    return jnp.sum(jnp.where(lane == lane_idx, x, 0.0), axis=1, keepdims=True)


def _sigmoid(x):
    return 1.0 / (1.0 + jnp.exp(-x))


def _softplus(x):
    return jnp.maximum(x, 0.0) + jnp.log1p(jnp.exp(-jnp.abs(x)))


def _mod_kernel(c_ref, w_ref, b_ref, o_ref):
    c = c_ref[...]
    o_ref[0] = _dot(c * _sigmoid(c), w_ref[0]) + b_ref[0]


def _modulation(cond, w_mod, b_mod):
    depth, d, n = w_mod.shape
    r = cond.shape[0]
    tn = n // 4
    return pl.pallas_call(
        _mod_kernel,
        grid=(depth, n // tn),
        in_specs=[pl.BlockSpec((r, d), lambda l, j: (0, 0)),
                  pl.BlockSpec((1, d, tn), lambda l, j: (l, 0, j)),
                  pl.BlockSpec((1, 1, tn), lambda l, j: (l, 0, j))],
        out_specs=pl.BlockSpec((1, r, tn), lambda l, j: (l, 0, j)),
        out_shape=jax.ShapeDtypeStruct((depth, r, n), F32),
        compiler_params=_cp("parallel", "parallel"),
        name="modulation",
    )(cond, w_mod, b_mod.reshape(depth, 1, n))


def _modulated_norm(x, gain, shift, scale):
    var = jnp.mean(x * x, axis=-1, keepdims=True)
    return x * lax.rsqrt(var + EPS) * gain * (1.0 + scale) + shift


def _proj_in_kernel(x_ref, mod_ref, g_ref, w_ref, o_ref):
    h = _modulated_norm(x_ref[0], g_ref[...], mod_ref[0, 0:1, :], mod_ref[0, 1:2, :])
    o_ref[0] = jnp.dot(h.astype(BF16), w_ref[...], preferred_element_type=F32)


def _proj_in(x, mod, gain, w):
    b, l, d = x.shape
    tm = _tile(l, 512)
    per_batch = mod.shape[0] > 1
    return pl.pallas_call(
        _proj_in_kernel,
        grid=(b, l // tm),
        in_specs=[pl.BlockSpec((1, tm, d), lambda i, j: (i, j, 0)),
                  pl.BlockSpec((1, 6, d), lambda i, j: (i if per_batch else 0, 0, 0)),
                  pl.BlockSpec((1, d), lambda i, j: (0, 0)),
                  pl.BlockSpec((d, Z_WIDTH), lambda i, j: (0, 0))],
        out_specs=pl.BlockSpec((1, tm, Z_WIDTH), lambda i, j: (i, j, 0)),
        out_shape=jax.ShapeDtypeStruct((b, l, Z_WIDTH), F32),
        compiler_params=_cp("parallel", "parallel"),
        name="proj_in",
    )(x, mod, gain.reshape(1, d), w)


def _dwconv_kernel(z_ref, zp_ref, zn_ref, w_ref, o_ref, *lowp_refs, mode):
    i = pl.program_id(1)
    z = z_ref[0]
    tm = z.shape[0]
    prev_row = jnp.where(i > 0, zp_ref[0, 7:8, :], 0.0)
    next_row = jnp.where(i < pl.num_programs(1) - 1, zn_ref[0, 0:1, :], 0.0)
    rid = lax.broadcasted_iota(jnp.int32, z.shape, 0)
    zm1 = jnp.where(rid == 0, prev_row, pltpu.roll(z, 1, 0))
    zp1 = jnp.where(rid == tm - 1, next_row, pltpu.roll(z, tm - 1, 0))
    y = zm1 * w_ref[0:1, :] + z * w_ref[1:2, :] + zp1 * w_ref[2:3, :]
    if mode == "deltanet":
        y = y * _sigmoid(y)
        q, k, v = y[:, :GROUP_W], y[:, GROUP_W:2 * GROUP_W], y[:, 2 * GROUP_W:]
        q = q * lax.rsqrt(_seg_mean(q * q, HEAD_DIM) * HEAD_DIM + EPS) * (HEAD_DIM ** -0.5)
        k = k * lax.rsqrt(_seg_mean(k * k, HEAD_DIM) * HEAD_DIM + EPS)
        o_ref[0, :, 0:GROUP_W] = q
        o_ref[0, :, GROUP_W:2 * GROUP_W] = k
        o_ref[0, :, 2 * GROUP_W:] = v
    else:
        o_ref[0] = y
        lowp_refs[0][0] = y[:, 2 * GROUP_W:].astype(BF16)


def _dwconv(z, w, col_block, mode):
    b, l, _ = z.shape
    c = 3 * GROUP_W
    tm = _tile(l, 1024)
    hb = tm // 8
    last = l // 8 - 1
    out_specs = [pl.BlockSpec((1, tm, c), lambda i, j: (i, j, 0))]
    out_shape = [jax.ShapeDtypeStruct((b, l, c), F32)]
    if mode == "plain":
        out_specs.append(pl.BlockSpec((1, tm, GROUP_W), lambda i, j: (i, j, 0)))
        out_shape.append(jax.ShapeDtypeStruct((b, l, GROUP_W), BF16))
    return pl.pallas_call(
        functools.partial(_dwconv_kernel, mode=mode),
        grid=(b, l // tm),
        in_specs=[pl.BlockSpec((1, tm, c), lambda i, j: (i, j, col_block)),
                  pl.BlockSpec((1, 8, c), lambda i, j: (i, jnp.maximum(j * hb - 1, 0), col_block)),
                  pl.BlockSpec((1, 8, c), lambda i, j: (i, jnp.minimum((j + 1) * hb, last), col_block)),
                  pl.BlockSpec((3, c), lambda i, j: (0, 0))],
        out_specs=out_specs,
        out_shape=out_shape,
        compiler_params=_cp("parallel", "parallel"),
        name="dwconv_" + mode,
    )(z, z, z, w)


STACK = N_HEADS * CHUNK


def _stack_heads(x):
    return jnp.concatenate([x] * N_HEADS, axis=0)


def _block_diag(x, head_eq):
    return jnp.where(head_eq, _stack_heads(x), 0.0)


def _fold_heads(x):
    return x[0:CHUNK] + x[CHUNK:2 * CHUNK] + x[2 * CHUNK:3 * CHUNK] + x[3 * CHUNK:]


def _chunk_masks(d):
    row = lax.broadcasted_iota(jnp.int32, (STACK, STACK), 0)
    col = lax.broadcasted_iota(jnp.int32, (STACK, STACK), 1)
    head_eq = lax.shift_right_logical(row, 6) == lax.shift_right_logical(col, 6)
    rel = ((row & (CHUNK - 1)) - (col & (CHUNK - 1))) * (1 - 2 * d)
    r64 = lax.broadcasted_iota(jnp.int32, (CHUNK, CHUNK), 0)
    c64 = lax.broadcasted_iota(jnp.int32, (CHUNK, CHUNK), 1)
    cum = jnp.where((r64 - c64) * (1 - 2 * d) >= 0, 1.0, 0.0).astype(BF16)
    return head_eq, head_eq & (rel >= 0), head_eq & (rel > 0), cum


def _head_cols(x, lane0):
    xs = _stack_heads(x)
    row = lax.broadcasted_iota(jnp.int32, xs.shape, 0)
    lane = lax.broadcasted_iota(jnp.int32, xs.shape, 1)
    return jnp.sum(jnp.where(lane == lane0 + lax.shift_right_logical(row, 6), xs, 0.0), axis=1, keepdims=True)


def _head_lanes(x, lane0):
    src = lax.broadcasted_iota(jnp.int32, (LANES, GROUP_W), 0)
    dst = lax.broadcasted_iota(jnp.int32, (LANES, GROUP_W), 1)
    expand = jnp.where(src == lane0 + lax.shift_right_logical(dst, 6), 1.0, 0.0).astype(BF16)
    return _dot_exact_r(x, expand)


def _lanes_to_col(x):
    return jnp.broadcast_to(x, (LANES, GROUP_W)).T[:, 0:1]


def _load_diag_state(dst_ref, src_ref):
    dst_ref[...] = jnp.zeros(dst_ref.shape, F32)
    for r in range(dst_ref.shape[0]):
        for d in range(2):
            for h in range(N_HEADS):
                lo, hi = h * HEAD_DIM, (h + 1) * HEAD_DIM
                dst_ref[r, d, lo:hi, lo:hi] = src_ref[r, d, h]


def _store_diag_state(dst_ref, src_ref):
    for r in range(src_ref.shape[0]):
        for d in range(2):
            for h in range(N_HEADS):
                lo, hi = h * HEAD_DIM, (h + 1) * HEAD_DIM
                dst_ref[r, d, h] = src_ref[r, d, lo:hi, lo:hi]


def _rec_specs(b, l):
    rows = max(r for r in (4, 2, 1) if b % r == 0)
    tile = _tile(l, 512)
    nt = l // tile
    fwd = lambda width, col: pl.BlockSpec((rows, tile, width), lambda i, j: (i, j, col))
    bwd = lambda width, col: pl.BlockSpec((rows, tile, width), lambda i, j: (i, nt - 1 - j, col))
    state = pl.BlockSpec((rows, 2, N_HEADS, HEAD_DIM, HEAD_DIM), lambda i, j: (i, 0, 0, 0, 0))
    return rows, tile, nt, fwd, bwd, state


class _Chain(NamedTuple):
    r: int
    d: int
    x_ref: Any
    sm_ref: Any
    o_ref: Any
    rows: Any


def _each(fn, *cols):
    return [fn(*args) for args in zip(*cols)]


def _unit_tri_inverses(mats):
    n = mats[0].shape[0]
    row = lax.broadcasted_iota(jnp.int32, (n, n), 0)
    col = lax.broadcasted_iota(jnp.int32, (n, n), 1)

    def same_block(log2_size):
        return lax.shift_right_logical(row, log2_size) == lax.shift_right_logical(col, log2_size)

    p = [jnp.where(same_block(3), -a, 0.0) for a in mats]
    t = [jnp.where(row == col, 1.0, 0.0) + x for x in p]
    for _ in range(2):
        p = _each(_dot, p, p)
        t = _each(jnp.add, t, _each(_dot, t, p))
    for log2_size in range(4, int(math.log2(CHUNK)) + 1):
        level = same_block(log2_size) & jnp.logical_not(same_block(log2_size - 1))
        off_t = [_dot(jnp.where(level, a, 0.0), x) for a, x in zip(mats, t)]
        t = _each(jnp.subtract, t, _each(_dot, t, off_t))
    return t


def _deltanet_chunks(chains, par_ref, s_ref):
    masks = {d: _chunk_masks(d) for d in (0, 1)}
    head_eq = masks[0][0]
    incl = [masks[c.d][1] for c in chains]
    strict = [masks[c.d][2] for c in chains]
    cum = [masks[c.d][3] for c in chains]
    lane_a = [LANE_A + c.d * N_HEADS for c in chains]
    lane_b = [LANE_BETA + c.d * N_HEADS for c in chains]
    q = [c.x_ref[c.r, c.rows, 0:GROUP_W] for c in chains]
    k = [c.x_ref[c.r, c.rows, GROUP_W:2 * GROUP_W] for c in chains]
    v = [c.x_ref[c.r, c.rows, 2 * GROUP_W:3 * GROUP_W] for c in chains]
    sm = [c.sm_ref[c.r, c.rows, :] for c in chains]
    beta_all = _each(_sigmoid, sm)
    g_all = [-jnp.exp(par_ref[0:1, :]) * _softplus(x + par_ref[1:2, :]) for x in sm]
    gc_all = _each(_dot_exact_l, cum, g_all)
    gc = _each(_head_lanes, gc_all, lane_a)
    beta = _each(_head_lanes, beta_all, lane_b)
    g_last = [x[CHUNK - 1:CHUNK] if c.d == 0 else x[0:1] for x, c in zip(gc, chains)]
    gc_col = _each(_head_cols, gc_all, lane_a)
    beta_col = _each(_head_cols, beta_all, lane_b)
    gc_row = _each(_row_bcast, gc_col)
    decay = [jnp.exp(jnp.where(m, x - y, NEG)) for m, x, y in zip(incl, gc_col, gc_row)]
    k_rows = _each(_stack_heads, k)
    kk = [_dot_nt(jnp.where(head_eq, x, 0.0), x) for x in k_rows]
    a = [jnp.where(m, b * x * dc, 0.0) for m, b, x, dc in zip(strict, beta_col, kk, decay)]
    t = _unit_tri_inverses(a)
    egc = _each(jnp.exp, gc)
    u = [_fold_heads(_dot(x, _block_diag(y * b, head_eq))) for x, y, b in zip(t, v, beta)]
    w = [_fold_heads(_dot(x, _block_diag(y * (b * e), head_eq))) for x, y, b, e in zip(t, k, beta, egc)]
    s = [s_ref[c.r, c.d] for c in chains]
    ws_qs = [_dot(jnp.concatenate([x, y * e], axis=0), z) for x, y, e, z in zip(w, q, egc, s)]
    v_new = [x - y[0:CHUNK] for x, y in zip(u, ws_qs)]
    qk = [_dot_nt(_block_diag(x, head_eq), y) * dc for x, y, dc in zip(q, k_rows, decay)]
    o_intra = [_fold_heads(_dot(x, _block_diag(y, head_eq))) for x, y in zip(qk, v_new)]
    s_add = [_dot_tn(x * jnp.exp(gl - g), y) for x, gl, g, y in zip(k, g_last, gc, v_new)]
    for c, x, y, z, gl, sa in zip(chains, ws_qs, o_intra, s, g_last, s_add):
        c.o_ref[c.r, c.rows, :] = x[CHUNK:] + y
        s_ref[c.r, c.d] = z * jnp.exp(gl) + jnp.where(head_eq, sa, 0.0)


def _chunk_chains(n_rows, n_chunks, c, f_refs, b_refs):
    rows_f = pl.ds(pl.multiple_of(c * CHUNK, CHUNK), CHUNK)
    rows_b = pl.ds(pl.multiple_of((n_chunks - 1 - c) * CHUNK, CHUNK), CHUNK)
    return [_Chain(r, d, *refs, rows) for r in range(n_rows)
            for d, refs, rows in ((0, f_refs, rows_f), (1, b_refs, rows_b))]


def _deltanet_kernel(qf_ref, smf_ref, qb_ref, smb_ref, par_ref, s0_ref, of_ref, ob_ref, sout_ref, s_ref, *,
                     n_chunks):
    j = pl.program_id(1)

    @pl.when(j == 0)
    def _():
        _load_diag_state(s_ref, s0_ref)

    def chunk_body(c, carry):
        chains = _chunk_chains(s_ref.shape[0], n_chunks, c, (qf_ref, smf_ref, of_ref), (qb_ref, smb_ref, ob_ref))
        _deltanet_chunks(chains, par_ref, s_ref)
        return carry

    lax.fori_loop(0, n_chunks, chunk_body, 0)

    @pl.when(j == pl.num_programs(1) - 1)
    def _():
        _store_diag_state(sout_ref, s_ref)


def _deltanet(qkv, z, par, s0):
    b, l, _ = qkv.shape
    rows, tile, nt, fwd, bwd, state = _rec_specs(b, l)
    return pl.pallas_call(
        functools.partial(_deltanet_kernel, n_chunks=tile // CHUNK),
        grid=(b // rows, nt),
        in_specs=[fwd(3 * GROUP_W, 0), fwd(LANES, COL_SMALL), bwd(3 * GROUP_W, 0), bwd(LANES, COL_SMALL),
                  pl.BlockSpec((8, LANES), lambda i, j: (0, 0)), state],
        out_specs=[fwd(GROUP_W, 0), bwd(GROUP_W, 0), state],
        out_shape=[jax.ShapeDtypeStruct((b, l, GROUP_W), F32), jax.ShapeDtypeStruct((b, l, GROUP_W), F32),
                   jax.ShapeDtypeStruct((b, 2, N_HEADS, HEAD_DIM, HEAD_DIM), F32)],
        scratch_shapes=[pltpu.VMEM((rows, 2, STACK, STACK), F32)],
        compiler_params=_cp("parallel", "arbitrary"),
        name="deltanet",
    )(qkv, z, qkv, z, par, s0)


def _mlstm_chunks(chains, par_ref, c_ref, n_ref, m_ref):
    masks = {d: _chunk_masks(d) for d in (0, 1)}
    head_eq = masks[0][0]
    incl = [masks[c.d][1] for c in chains]
    cum = [masks[c.d][3] for c in chains]
    lane_i = [LANE_I + c.d * N_HEADS for c in chains]
    lane_f = [LANE_F + c.d * N_HEADS for c in chains]
    q = [c.x_ref[c.r, c.rows, 0:GROUP_W] for c in chains]
    k = [c.x_ref[c.r, c.rows, GROUP_W:2 * GROUP_W] * (HEAD_DIM ** -0.5) for c in chains]
    v = [c.x_ref[c.r, c.rows, 2 * GROUP_W:3 * GROUP_W] for c in chains]
    sm = [c.sm_ref[c.r, c.rows, :] for c in chains]
    i_all = [x + par_ref[0:1, :] for x in sm]
    f_all = [-_softplus(-(x + par_ref[1:2, :])) for x in sm]
    bc_all = _each(_dot_exact_l, cum, f_all)
    ic = _each(_head_lanes, i_all, lane_i)
    bc = _each(_head_lanes, bc_all, lane_f)
    b_tot = [x[CHUNK - 1:CHUNK] if c.d == 0 else x[0:1] for x, c in zip(bc, chains)]
    ic_col = _each(_head_cols, i_all, lane_i)
    bc_col = _each(_head_cols, bc_all, lane_f)
    c_s = [c_ref[c.r, c.d] for c in chains]
    n_s = [n_ref[c.r, c.d] for c in chains]
    m_s = [m_ref[c.r, c.d] for c in chains]
    w_row = _each(_row_bcast, _each(jnp.subtract, ic_col, bc_col))
    dlog = [jnp.where(m, x + y, NEG) for m, x, y in zip(incl, bc_col, w_row)]
    a = _each(jnp.add, bc_col, _each(_lanes_to_col, m_s))
    m_t = [jnp.maximum(x, jnp.max(y, axis=1, keepdims=True)) for x, y in zip(a, dlog)]
    inter = [jnp.exp(x - y) for x, y in zip(a, m_t)]
    q_bd = [_block_diag(x, head_eq) for x in q]
    s = [_dot_nt(x, _stack_heads(y)) * jnp.exp(dl - mt) for x, y, dl, mt in zip(q_bd, k, dlog, m_t)]
    inter_part = _each(_dot, q_bd, c_s)
    intra_part = [_dot(x, _block_diag(y, head_eq)) for x, y in zip(s, v)]
    den = [it * jnp.sum(x * n, axis=1, keepdims=True) + jnp.sum(y, axis=1, keepdims=True)
           for it, x, n, y in zip(inter, q_bd, n_s, s)]
    wend = [bt - x + y for bt, x, y in zip(b_tot, bc, ic)]
    a_end = _each(jnp.add, b_tot, m_s)
    m_new = [jnp.maximum(x, jnp.max(y, axis=0, keepdims=True)) for x, y in zip(a_end, wend)]
    dec = [jnp.exp(x - y) for x, y in zip(a_end, m_new)]
    kw = [x * jnp.exp(y - z) for x, y, z in zip(k, wend, m_new)]
    c_add = _each(_dot_tn, kw, v)
    for i, c in enumerate(chains):
        num = inter[i] * inter_part[i] + intra_part[i]
        c.o_ref[c.r, c.rows, :] = _fold_heads(num / jnp.maximum(jnp.abs(den[i]), jnp.exp(-m_t[i])))
        c_ref[c.r, c.d] = dec[i] * c_s[i] + jnp.where(head_eq, c_add[i], 0.0)
        n_ref[c.r, c.d] = dec[i] * n_s[i] + jnp.sum(kw[i], axis=0, keepdims=True)
        m_ref[c.r, c.d] = m_new[i]


def _mlstm_kernel(zf_ref, smf_ref, zb_ref, smb_ref, par_ref, c0_ref, n0_ref, m0_ref,
                  of_ref, ob_ref, cout_ref, nout_ref, mout_ref, c_ref, n_ref, m_ref, *, n_chunks):
    j = pl.program_id(1)

    @pl.when(j == 0)
    def _():
        _load_diag_state(c_ref, c0_ref)
        n_ref[...] = n0_ref[...]
        m_ref[...] = m0_ref[...]

    def chunk_body(c, carry):
        chains = _chunk_chains(c_ref.shape[0], n_chunks, c, (zf_ref, smf_ref, of_ref), (zb_ref, smb_ref, ob_ref))
        _mlstm_chunks(chains, par_ref, c_ref, n_ref, m_ref)
        return carry

    lax.fori_loop(0, n_chunks, chunk_body, 0)

    @pl.when(j == pl.num_programs(1) - 1)
    def _():
        _store_diag_state(cout_ref, c_ref)
        nout_ref[...] = n_ref[...]
        mout_ref[...] = m_ref[...]


def _mlstm(z, par, c0, n0, m0):
    b, l, _ = z.shape
    rows, tile, nt, fwd, bwd, state = _rec_specs(b, l)
    vec = pl.BlockSpec((rows, 2, 1, GROUP_W), lambda i, j: (i, 0, 0, 0))
    return pl.pallas_call(
        functools.partial(_mlstm_kernel, n_chunks=tile // CHUNK),
        grid=(b // rows, nt),
        in_specs=[fwd(3 * GROUP_W, COL_ML), fwd(LANES, COL_SMALL), bwd(3 * GROUP_W, COL_ML), bwd(LANES, COL_SMALL),
                  pl.BlockSpec((8, LANES), lambda i, j: (0, 0)), state, vec, vec],
        out_specs=[fwd(GROUP_W, 0), bwd(GROUP_W, 0), state, vec, vec],
        out_shape=[jax.ShapeDtypeStruct((b, l, GROUP_W), F32), jax.ShapeDtypeStruct((b, l, GROUP_W), F32),
                   jax.ShapeDtypeStruct((b, 2, N_HEADS, HEAD_DIM, HEAD_DIM), F32),
                   jax.ShapeDtypeStruct((b, 2, 1, GROUP_W), F32), jax.ShapeDtypeStruct((b, 2, 1, GROUP_W), F32)],
        scratch_shapes=[pltpu.VMEM((rows, 2, STACK, STACK), F32), pltpu.VMEM((rows, 2, 1, GROUP_W), F32),
                        pltpu.VMEM((rows, 2, 1, GROUP_W), F32)],
        compiler_params=_cp("parallel", "arbitrary"),
        name="mlstm",
    )(z, z, z, z, par, c0, n0, m0)


def _diff_prep_kernel(z_ref, gq_ref, gk_ref, cos_ref, sin_ref, q_ref, kt_ref, v_ref, kd_ref, *, rope):
    z = z_ref[0]
    q, k, v = z[:, :GROUP_W], z[:, GROUP_W:2 * GROUP_W], z[:, 2 * GROUP_W:]
    q = q * lax.rsqrt(_seg_mean(q * q, DF_DH) + EPS) * gq_ref[...]
    k = k * lax.rsqrt(_seg_mean(k * k, DF_DH) + EPS) * gk_ref[...]
    kd_ref[0] = k
    if rope:
        lane = lax.broadcasted_iota(jnp.int32, q.shape, 1)
        first = (lane & (DF_DH - 1)) < DF_DH // 2

        def rot(x):
            swapped = jnp.where(first, pltpu.roll(x, GROUP_W - DF_DH // 2, 1), pltpu.roll(x, DF_DH // 2, 1))
            return x * cos_ref[...] + swapped * sin_ref[...]

        q, k = rot(q), rot(k)
    _store_attn_operands(q * (DF_DH ** -0.5), k, v, q_ref, kt_ref, v_ref)


def _store_attn_operands(q, k, v, q_ref, kt_ref, v_ref):
    kt = k.T
    lane = lax.broadcasted_iota(jnp.int32, (k.shape[0], HEAD_DIM), 1)
    ones = jnp.ones((k.shape[0], HEAD_DIM), BF16)
    for h in range(N_HEADS):
        lo, hi = h * HEAD_DIM, (h + 1) * HEAD_DIM
        if q is not None:
            q_ref[0, h, 0] = jnp.where(lane < DF_DH, q[:, lo:hi], 0.0).astype(BF16)
            q_ref[0, h, 1] = jnp.where(lane >= DF_DH, q[:, lo:hi], 0.0).astype(BF16)
        v_ref[0, h] = jnp.concatenate([v[:, lo:hi].astype(BF16), ones], axis=1)
        kt_ref[0, h] = kt[lo:hi, :].astype(BF16)


def _cache_prep_kernel(k_ref, v_ref, kt_ref, vh_ref):
    _store_attn_operands(None, k_ref[0], v_ref[0], None, kt_ref, vh_ref)


def _rope_tables(l):
    rows = l // GRID_W
    r = jnp.repeat(jnp.arange(rows, dtype=F32), GRID_W)
    col = jnp.tile(jnp.arange(GRID_W, dtype=F32), rows)
    n_freq = DF_DH // 4
    inv = ROPE_THETA ** (-jnp.arange(n_freq, dtype=F32) / n_freq)
    ang = jnp.concatenate([r[:, None] * inv, col[:, None] * inv], axis=-1)
    cos, sin = jnp.cos(ang), jnp.sin(ang)
    reps = GROUP_W // DF_DH
    return (jnp.tile(jnp.concatenate([cos, cos], axis=-1), (1, reps)),
            jnp.tile(jnp.concatenate([-sin, sin], axis=-1), (1, reps)))


def _diff_prep(z, gq, gk, rope):
    b, l, _ = z.shape
    tm = _tile(l, 1024)
    if rope:
        cos, sin = _rope_tables(l)
    else:
        cos = sin = jnp.zeros((l, GROUP_W), F32)
    return pl.pallas_call(
        functools.partial(_diff_prep_kernel, rope=rope),
        grid=(b, l // tm),
        in_specs=[pl.BlockSpec((1, tm, 3 * GROUP_W), lambda i, j: (i, j, COL_DF)),
                  pl.BlockSpec((1, GROUP_W), lambda i, j: (0, 0)),
                  pl.BlockSpec((1, GROUP_W), lambda i, j: (0, 0)),
                  pl.BlockSpec((tm, GROUP_W), lambda i, j: (j, 0)),
                  pl.BlockSpec((tm, GROUP_W), lambda i, j: (j, 0))],
        out_specs=[pl.BlockSpec((1, N_HEADS, 2, tm, HEAD_DIM), lambda i, j: (i, 0, 0, j, 0)),
                   pl.BlockSpec((1, N_HEADS, HEAD_DIM, tm), lambda i, j: (i, 0, 0, j)),
                   pl.BlockSpec((1, N_HEADS, tm, 2 * HEAD_DIM), lambda i, j: (i, 0, j, 0)),
                   pl.BlockSpec((1, tm, GROUP_W), lambda i, j: (i, j, 0))],
        out_shape=[jax.ShapeDtypeStruct((b, N_HEADS, 2, l, HEAD_DIM), BF16),
                   jax.ShapeDtypeStruct((b, N_HEADS, HEAD_DIM, l), BF16),
                   jax.ShapeDtypeStruct((b, N_HEADS, l, 2 * HEAD_DIM), BF16),
                   jax.ShapeDtypeStruct((b, l, GROUP_W), F32)],
        compiler_params=_cp("parallel", "parallel"),
        name="diff_prep",
    )(z, jnp.tile(gq, GROUP_W // DF_DH).reshape(1, GROUP_W), jnp.tile(gk, GROUP_W // DF_DH).reshape(1, GROUP_W),
      cos, sin)


def _cache_prep(ck, cv):
    b, p, _ = ck.shape
    tm = _tile(p, 512)
    return pl.pallas_call(
        _cache_prep_kernel,
        grid=(b, p // tm),
        in_specs=[pl.BlockSpec((1, tm, GROUP_W), lambda i, j: (i, j, 0)),
                  pl.BlockSpec((1, tm, GROUP_W), lambda i, j: (i, j, 0))],
        out_specs=[pl.BlockSpec((1, N_HEADS, HEAD_DIM, tm), lambda i, j: (i, 0, 0, j)),
                   pl.BlockSpec((1, N_HEADS, tm, 2 * HEAD_DIM), lambda i, j: (i, 0, j, 0))],
        out_shape=[jax.ShapeDtypeStruct((b, N_HEADS, HEAD_DIM, p), BF16),
                   jax.ShapeDtypeStruct((b, N_HEADS, p, 2 * HEAD_DIM), BF16)],
        compiler_params=_cp("parallel", "parallel"),
        name="cache_prep",
    )(ck, cv)


def _diff_attn_kernel(q_ref, kt_ref, v_ref, lam_ref, g_ref, o_ref, m_ref, acc_ref, *, lam_init):
    ik = pl.program_id(3)

    @pl.when(ik == 0)
    def _():
        m_ref[...] = jnp.full(m_ref.shape, NEG, F32)
        acc_ref[...] = jnp.zeros(acc_ref.shape, F32)

    chains = [(h, m) for h in range(q_ref.shape[1]) for m in range(2)]
    s = [jnp.dot(q_ref[0, h, m], kt_ref[0, h], preferred_element_type=F32) for h, m in chains]
    m_old = [m_ref[h, m] for h, m in chains]
    m_new = [jnp.maximum(x, jnp.max(y, axis=1, keepdims=True)) for x, y in zip(m_old, s)]
    p = [jnp.exp(x - y[:, 0:1]).astype(BF16) for x, y in zip(s, m_new)]
    pv = [jnp.dot(x, v_ref[0, h], preferred_element_type=F32) for x, (h, m) in zip(p, chains)]
    for i, (h, m) in enumerate(chains):
        acc_ref[h, m] = jnp.exp(m_old[i] - m_new[i]) * acc_ref[h, m] + pv[i]
        m_ref[h, m] = m_new[i]

    @pl.when(ik == pl.num_programs(3) - 1)
    def _():
        lp = lam_ref[...]
        lam = (jnp.exp(jnp.sum(lp[0:1] * lp[1:2], axis=1, keepdims=True))
               - jnp.exp(jnp.sum(lp[2:3] * lp[3:4], axis=1, keepdims=True)) + lam_init)
        for h in range(q_ref.shape[1]):
            a0, a1 = acc_ref[h, 0], acc_ref[h, 1]
            o = (a0[:, :HEAD_DIM] / a0[:, HEAD_DIM:HEAD_DIM + 1]
                 - lam * (a1[:, :HEAD_DIM] / a1[:, HEAD_DIM:HEAD_DIM + 1]))
            var = jnp.mean(o * o, axis=1, keepdims=True)
            o_ref[0, h] = o * lax.rsqrt(var + EPS) * g_ref[...] * (1.0 - lam_init)


ATTN_HEADS_PER_STEP = 2


def _diff_attn(q, kt, v, lam_par, gain, lam_init):
    b, _, _, l, _ = q.shape
    lk = kt.shape[-1]
    hs = ATTN_HEADS_PER_STEP
    tq = _tile(l, 512)
    tk = _tile(lk, 2304)
    return pl.pallas_call(
        functools.partial(_diff_attn_kernel, lam_init=lam_init),
        grid=(b, N_HEADS // hs, l // tq, lk // tk),
        in_specs=[pl.BlockSpec((1, hs, 2, tq, HEAD_DIM), lambda i, h, a, c: (i, h, 0, a, 0)),
                  pl.BlockSpec((1, hs, HEAD_DIM, tk), lambda i, h, a, c: (i, h, 0, c)),
                  pl.BlockSpec((1, hs, tk, 2 * HEAD_DIM), lambda i, h, a, c: (i, h, c, 0)),
                  pl.BlockSpec((4, DF_DH), lambda i, h, a, c: (0, 0)),
                  pl.BlockSpec((1, HEAD_DIM), lambda i, h, a, c: (0, 0))],
        out_specs=pl.BlockSpec((1, hs, tq, HEAD_DIM), lambda i, h, a, c: (i, h, a, 0)),
        out_shape=jax.ShapeDtypeStruct((b, N_HEADS, l, HEAD_DIM), F32),
        scratch_shapes=[pltpu.VMEM((hs, 2, tq, LANES), F32), pltpu.VMEM((hs, 2, tq, 2 * HEAD_DIM), F32)],
        compiler_params=_cp("parallel", "parallel", "parallel", "arbitrary"),
        name="diff_attn",
    )(q, kt, v, lam_par, gain.reshape(1, HEAD_DIM))


def _hy_filter_kernel(feat_ref, win_ref, w1_ref, b1_ref, w2_ref, b2_ref, w3_ref, fr_ref, o_ref):
    h = jnp.sin(fr_ref[0:1, :] * (_dot_hi(feat_ref[...], w1_ref[...]) + b1_ref[...]))
    h = jnp.sin(fr_ref[1:2, :] * (_dot_hi(h, w2_ref[...]) + b2_ref[...]))
    h = _dot_hi(h, w3_ref[...]) * win_ref[...]
    o_ref[...] = h * lax.rsqrt(jnp.sum(h * h, axis=0, keepdims=True) + EPS)


def _hy_constants(l):
    pos = jnp.arange(l, dtype=F32)
    bands = jnp.arange(1, HY_BANDS + 1, dtype=F32)
    ang = (2.0 * math.pi / l) * pos[:, None] * bands[None, :]
    feat = jnp.concatenate([pos[:, None] / l, jnp.cos(ang), jnp.sin(ang)], axis=-1)
    feat = jnp.pad(feat, ((0, 0), (0, LANES - feat.shape[1])))
    rates = jnp.linspace(-math.log(HY_TARGET) / HY_FAST_DECAY, -math.log(HY_TARGET) / HY_SLOW_DECAY,
                         GROUP_W, dtype=F32)
    rates = jnp.tile(rates, 2)
    offset = jnp.abs(pos - l // 2) / l
    return feat, jnp.exp(-offset[:, None] * rates[None, :]) + HY_SHIFT


def _hy_filters(feat, win, w1, b1, w2, b2, w3, freq):
    l = feat.shape[0]
    hid = w2.shape[0]
    w1p = jnp.pad(w1, ((0, LANES - w1.shape[0]), (0, 0)))
    full = lambda shape: pl.BlockSpec(shape, lambda o: (0, 0))
    return pl.pallas_call(
        _hy_filter_kernel,
        grid=(2,),
        in_specs=[full((l, LANES)), pl.BlockSpec((l, GROUP_W), lambda o: (0, o)),
                  full((LANES, hid)), full((1, hid)), full((hid, hid)), full((1, hid)),
                  pl.BlockSpec((hid, GROUP_W), lambda o: (0, o)), full((2, hid))],
        out_specs=pl.BlockSpec((l, GROUP_W), lambda o: (0, o)),
        out_shape=jax.ShapeDtypeStruct((l, 2 * GROUP_W), F32),
        compiler_params=_cp("parallel"),
        name="hy_filters",
    )(feat, win, w1p, b1.reshape(1, hid), w2, b2.reshape(1, hid), w3, freq)


def _dft_matrices(l):
    n = 2 * l
    k = jnp.arange(l, dtype=jnp.int32)
    alt = jnp.where(k % 2 == 0, 1.0, -1.0).astype(F32)
    cos_f, sin_f = _cos_sin_grid(0, l, n)
    fwd_im = jnp.where(k[:, None] == 0, alt[None, :], -sin_f)
    fwd = jnp.stack([cos_f, fwd_im]).astype(BF16)
    t = k + l // 2
    cos_i, sin_i = _cos_sin_grid(l // 2, l, n)
    alt_t = jnp.where(t % 2 == 0, 1.0, -1.0).astype(F32)
    inv_re = jnp.where(k[None, :] == 0, 1.0 / n, (2.0 / n) * cos_i)
    inv_im = jnp.where(k[None, :] == 0, alt_t[:, None] / n, -(2.0 / n) * sin_i)
    inv = jnp.concatenate([inv_re, inv_im], axis=1).astype(BF16)
    return fwd, inv


def _cos_sin_grid(row0, l, n):
    s = 1 << (int(math.log2(l)) // 2)
    assert row0 % s == 0 and l % s == 0
    c = jnp.arange(l, dtype=jnp.int32)
    hi = row0 + jnp.arange(l // s, dtype=jnp.int32) * s
    lo = jnp.arange(s, dtype=jnp.int32)

    def angle(r):
        return (2.0 * math.pi / n) * ((r[:, None] * c[None, :]) % n).astype(F32)

    a, b = angle(hi), angle(lo)
    ca, sa = jnp.cos(a)[:, None, :], jnp.sin(a)[:, None, :]
    cb, sb = jnp.cos(b)[None, :, :], jnp.sin(b)[None, :, :]
    return (ca * cb - sa * sb).reshape(l, l), (sa * cb + ca * sb).reshape(l, l)


def _dft_filter_kernel(f_ref, h_ref, o_ref):
    h = h_ref[...].astype(BF16)
    o_ref[0] = jnp.dot(f_ref[0], h, preferred_element_type=F32)
    o_ref[1] = jnp.dot(f_ref[1], h, preferred_element_type=F32)


def _dft_filter(fwd, filt):
    l, c = filt.shape
    tm = _tile(l, 256)
    return pl.pallas_call(
        _dft_filter_kernel,
        grid=(l // tm,),
        in_specs=[pl.BlockSpec((2, tm, l), lambda i: (0, i, 0)), pl.BlockSpec((l, c), lambda i: (0, 0))],
        out_specs=pl.BlockSpec((2, tm, c), lambda i: (0, i, 0)),
        out_shape=jax.ShapeDtypeStruct((2, l, c), F32),
        compiler_params=_cp("parallel"),
        name="dft_filter",
    )(fwd, filt)


def _dft_fwd_mul_kernel(f_ref, u_ref, h_ref, o_ref):
    u = u_ref[0]
    ur = jnp.dot(f_ref[0], u, preferred_element_type=F32)
    ui = jnp.dot(f_ref[1], u, preferred_element_type=F32)
    hr, hi = h_ref[0], h_ref[1]
    row = lax.broadcasted_iota(jnp.int32, ur.shape, 0) + pl.program_id(0) * ur.shape[0]
    packed = row == 0
    o_ref[0, 0] = (ur * hr - jnp.where(packed, 0.0, ui * hi)).astype(BF16)
    o_ref[0, 1] = jnp.where(packed, ui * hi, ur * hi + ui * hr).astype(BF16)


def _dft_fwd_mul(fwd, u, hf, order):
    b, l, _ = u.shape
    tm = _tile(l, 1024)
    return pl.pallas_call(
        _dft_fwd_mul_kernel,
        grid=(l // tm, b),
        in_specs=[pl.BlockSpec((2, tm, l), lambda i, j: (0, i, 0)),
                  pl.BlockSpec((1, l, GROUP_W), lambda i, j: (j, 0, 0)),
                  pl.BlockSpec((2, tm, GROUP_W), lambda i, j: (0, i, order))],
        out_specs=pl.BlockSpec((1, 2, tm, GROUP_W), lambda i, j: (j, 0, i, 0)),
        out_shape=jax.ShapeDtypeStruct((b, 2, l, GROUP_W), BF16),
        compiler_params=_cp("parallel", "parallel"),
        name="dft_fwd_mul",
    )(fwd, u, hf)


def _dft_inv_gate_kernel(g_ref, y_ref, x_ref, u_ref, skip_ref, o_ref, lowp_ref):
    conv = jnp.dot(g_ref[...], y_ref[0], preferred_element_type=F32)
    out = x_ref[0] * (conv + skip_ref[...] * u_ref[0])
    o_ref[0] = out
    lowp_ref[0] = out.astype(BF16)


def _dft_inv_gate(inv, y, xg, x_col, u, u_col, skip):
    b, _, l, _ = y.shape
    tm = _tile(l, 1024)
    out_spec = pl.BlockSpec((1, tm, GROUP_W), lambda i, j: (j, i, 0))
    return pl.pallas_call(
        _dft_inv_gate_kernel,
        grid=(l // tm, b),
        in_specs=[pl.BlockSpec((tm, 2 * l), lambda i, j: (i, 0)),
                  pl.BlockSpec((1, 2 * l, GROUP_W), lambda i, j: (j, 0, 0)),
                  pl.BlockSpec((1, tm, GROUP_W), lambda i, j: (j, i, x_col)),
                  pl.BlockSpec((1, tm, GROUP_W), lambda i, j: (j, i, u_col)),
                  pl.BlockSpec((1, GROUP_W), lambda i, j: (0, 0))],
        out_specs=[out_spec, out_spec],
        out_shape=[jax.ShapeDtypeStruct((b, l, GROUP_W), F32), jax.ShapeDtypeStruct((b, l, GROUP_W), BF16)],
        compiler_params=_cp("parallel", "parallel"),
        name="dft_inv_gate",
    )(inv, y.reshape(b, 2 * l, GROUP_W), xg, u, skip.reshape(1, GROUP_W))


def _head_norm(x, gain):
    return x * lax.rsqrt(_seg_mean(x * x, HEAD_DIM) + EPS) * gain


def _proj_out_kernel(dnf_ref, dnb_ref, gate_ref, df_ref, hy_ref, mlf_ref, mlb_ref, mo_ref, x_ref, mod_ref,
                     gdn_ref, gml_ref, w_ref, o_ref):
    gate = gate_ref[0]
    y_dn = _head_norm(dnf_ref[0] + dnb_ref[0], gdn_ref[...]) * (gate * _sigmoid(gate))
    y_ml = _head_norm(mlf_ref[0] + mlb_ref[0], gml_ref[...]) * _sigmoid(mo_ref[0])
    y = _dot(y_dn, w_ref[0:GROUP_W, :])
    for h in range(N_HEADS):
        lo = GROUP_W + h * HEAD_DIM
        y = y + _dot(df_ref[0, h], w_ref[lo:lo + HEAD_DIM, :])
    y = y + _dot(hy_ref[0], w_ref[2 * GROUP_W:3 * GROUP_W, :])
    y = y + _dot(y_ml, w_ref[3 * GROUP_W:, :])
    o_ref[0] = x_ref[0] + mod_ref[0, 2:3, :] * y


def _proj_out(o_dn, z, o_df, y_hy, h_ml, x, mod, g_dn, g_ml, w):
    b, l, d = x.shape
    tm = _tile(l, 512)
    per_batch = mod.shape[0] > 1
    group = lambda col: pl.BlockSpec((1, tm, GROUP_W), lambda i, j: (i, j, col))
    return pl.pallas_call(
        _proj_out_kernel,
        grid=(b, l // tm),
        in_specs=[group(0), group(0), group(COL_DGATE),
                  pl.BlockSpec((1, N_HEADS, tm, HEAD_DIM), lambda i, j: (i, 0, j, 0)),
                  group(0), group(0), group(0), group(COL_MO),
                  pl.BlockSpec((1, tm, d), lambda i, j: (i, j, 0)),
                  pl.BlockSpec((1, 6, d), lambda i, j: (i if per_batch else 0, 0, 0)),
                  pl.BlockSpec((1, GROUP_W), lambda i, j: (0, 0)),
                  pl.BlockSpec((1, GROUP_W), lambda i, j: (0, 0)),
                  pl.BlockSpec((d, d), lambda i, j: (0, 0))],
        out_specs=pl.BlockSpec((1, tm, d), lambda i, j: (i, j, 0)),
        out_shape=jax.ShapeDtypeStruct((b, l, d), F32),
        compiler_params=_cp("parallel", "parallel"),
        name="proj_out",
    )(o_dn[0], o_dn[1], z, o_df, y_hy, h_ml[0], h_ml[1], z, x, mod, jnp.tile(g_dn, N_HEADS).reshape(1, GROUP_W),
      jnp.tile(g_ml, N_HEADS).reshape(1, GROUP_W), w)


def _ffn_kernel(x_ref, mod_ref, g_ref, wg_ref, wu_ref, wd_ref, o_ref, h_ref, acc_ref):
    j = pl.program_id(2)

    @pl.when(j == 0)
    def _():
        h_ref[...] = _modulated_norm(x_ref[0], g_ref[...], mod_ref[0, 3:4, :], mod_ref[0, 4:5, :]).astype(BF16)
        acc_ref[...] = jnp.zeros(acc_ref.shape, F32)

    h = h_ref[...]
    g = jnp.dot(h, wg_ref[...], preferred_element_type=F32)
    u = jnp.dot(h, wu_ref[...], preferred_element_type=F32)
    acc_ref[...] += jnp.dot((g * _sigmoid(g) * u).astype(BF16), wd_ref[...], preferred_element_type=F32)

    @pl.when(j == pl.num_programs(2) - 1)
    def _():
        o_ref[0] = x_ref[0] + mod_ref[0, 5:6, :] * acc_ref[...]


def _ffn(x, mod, gain, wg, wu, wd):
    b, l, d = x.shape
    f = wg.shape[1]
    tm = _tile(l, 1024)
    tf = 256
    per_batch = mod.shape[0] > 1
    return pl.pallas_call(
        _ffn_kernel,
        grid=(b, l // tm, f // tf),
        in_specs=[pl.BlockSpec((1, tm, d), lambda i, a, j: (i, a, 0)),
                  pl.BlockSpec((1, 6, d), lambda i, a, j: (i if per_batch else 0, 0, 0)),
                  pl.BlockSpec((1, d), lambda i, a, j: (0, 0)),
                  pl.BlockSpec((d, tf), lambda i, a, j: (0, j)),
                  pl.BlockSpec((d, tf), lambda i, a, j: (0, j)),
                  pl.BlockSpec((tf, d), lambda i, a, j: (j, 0))],
        out_specs=pl.BlockSpec((1, tm, d), lambda i, a, j: (i, a, 0)),
        out_shape=jax.ShapeDtypeStruct((b, l, d), F32),
        scratch_shapes=[pltpu.VMEM((tm, d), BF16), pltpu.VMEM((tm, d), F32)],
        compiler_params=_cp("parallel", "parallel", "arbitrary"),
        name="ffn_dense",
    )(x, mod, gain.reshape(1, d), wg, wu, wd)


def _top2_combine(logits):
    lane = lax.broadcasted_iota(jnp.int32, logits.shape, 1)
    v1 = jnp.max(logits, axis=1, keepdims=True)
    i1 = jnp.min(jnp.where(logits == v1, lane, LANES), axis=1, keepdims=True)
    rest = jnp.where(lane == i1, NEG, logits)
    v2 = jnp.max(rest, axis=1, keepdims=True)
    i2 = jnp.min(jnp.where(rest == v2, lane, LANES), axis=1, keepdims=True)
    e2 = jnp.exp(v2 - v1)
    return jnp.where(lane == i1, 1.0 / (1.0 + e2), 0.0) + jnp.where(lane == i2, e2 / (1.0 + e2), 0.0)


MOE_ROW_TILE = 512


def _moe_route_kernel(x_ref, mod_ref, g_ref, r_ref, h_ref, comb_ref, rank_ref, cnt_ref):
    @pl.when((pl.program_id(0) == 0) & (pl.program_id(1) == 0))
    def _():
        cnt_ref[...] = jnp.zeros(cnt_ref.shape, F32)

    hn = _modulated_norm(x_ref[0], g_ref[...], mod_ref[0, 3:4, :], mod_ref[0, 4:5, :])
    tm = hn.shape[0]
    for j in range(h_ref.shape[0]):
        h_ref[j, 0] = hn[:, j * LANES:(j + 1) * LANES]
    lane = lax.broadcasted_iota(jnp.int32, (tm, LANES), 1)
    comb = _top2_combine(jnp.where(lane < N_EXPERTS, _dot_hi(hn, r_ref[...]), NEG))
    comb_ref[0] = comb
    routed = jnp.where(comb > 0.0, 1.0, 0.0)
    row = lax.broadcasted_iota(jnp.int32, (tm, tm), 0)
    col = lax.broadcasted_iota(jnp.int32, (tm, tm), 1)
    earlier = jnp.where(col < row, 1.0, 0.0).astype(BF16)
    rank = cnt_ref[...] + jnp.dot(earlier, routed.astype(BF16), preferred_element_type=F32)
    rank_ref[0] = jnp.where(routed > 0.0, rank, -1.0)
    cnt_ref[...] += jnp.sum(routed, axis=0, keepdims=True)


def _moe_group_kernel(tile_expert_ref, used_ref, xs_ref, wg_ref, wu_ref, wd_ref, o_ref):
    del tile_expert_ref
    pieces = xs_ref.shape[0]

    @pl.when(pl.program_id(0) < used_ref[0])
    def _():
        xs = jnp.concatenate([xs_ref[j] for j in range(pieces)], axis=1).astype(BF16)
        g = jnp.dot(xs, wg_ref[0], preferred_element_type=F32)
        u = jnp.dot(xs, wu_ref[0], preferred_element_type=F32)
        y = jnp.dot((g * _sigmoid(g) * u).astype(BF16), wd_ref[0], preferred_element_type=F32)
        for j in range(pieces):
            o_ref[j] = y[:, j * LANES:(j + 1) * LANES]

    @pl.when(pl.program_id(0) >= used_ref[0])
    def _():
        o_ref[...] = jnp.zeros(o_ref.shape, F32)


def _moe_combine_kernel(x_ref, y_ref, comb_ref, mod_ref, o_ref):
    comb = comb_ref[0]
    gate_a = jnp.max(comb, axis=1, keepdims=True)
    gate_b = jnp.sum(comb, axis=1, keepdims=True) - gate_a
    pieces = y_ref.shape[0]
    y_a = jnp.concatenate([y_ref[j, 0, 0] for j in range(pieces)], axis=1)
    y_b = jnp.concatenate([y_ref[j, 1, 0] for j in range(pieces)], axis=1)
    o_ref[0] = x_ref[0] + mod_ref[0, 5:6, :] * (gate_a * y_a + gate_b * y_b)


MOE_SLOT_ALIGN = 2048


def _moe(x, mod, gain, router, wg, wu, wd):
    b, l, d = x.shape
    ne, _, f = wg.shape
    t = b * l
    tm = _tile(l, 1024)
    tg = MOE_ROW_TILE
    per_batch = mod.shape[0] > 1
    router_p = jnp.pad(router, ((0, 0), (0, LANES - ne)))
    x_spec = pl.BlockSpec((1, tm, d), lambda i, j: (i, j, 0))
    mod_spec = pl.BlockSpec((1, 6, d), lambda i, j: (i if per_batch else 0, 0, 0))
    pieces = d // LANES
    lane_spec = pl.BlockSpec((1, tm, LANES), lambda i, j: (i, j, 0))
    h, comb, rank, cnt = pl.pallas_call(
        _moe_route_kernel,
        grid=(b, l // tm),
        in_specs=[x_spec, mod_spec, pl.BlockSpec((1, d), lambda i, j: (0, 0)),
                  pl.BlockSpec((d, LANES), lambda i, j: (0, 0))],
        out_specs=[pl.BlockSpec((pieces, 1, tm, LANES), lambda i, j: (0, i, j, 0)), lane_spec, lane_spec,
                   pl.BlockSpec((1, LANES), lambda i, j: (0, 0))],
        out_shape=[jax.ShapeDtypeStruct((pieces, b, l, LANES), F32), jax.ShapeDtypeStruct((b, l, LANES), F32),
                   jax.ShapeDtypeStruct((b, l, LANES), F32), jax.ShapeDtypeStruct((1, LANES), F32)],
        compiler_params=_cp("arbitrary", "arbitrary"),
        name="moe_route",
    )(x, mod, gain.reshape(1, d), router_p)

    n_slots = -(-(2 * t + (ne + 1) * tg) // MOE_SLOT_ALIGN) * MOE_SLOT_ALIGN
    comb2 = comb.reshape(t, LANES)[:, :ne]
    rank2 = rank.reshape(t, LANES)[:, :ne]
    grp = (cnt[0, :ne].astype(jnp.int32) + tg - 1) // tg * tg
    end = jnp.cumsum(grp)
    off = end - grp
    tile_start = jnp.arange(n_slots // tg, dtype=jnp.int32) * tg
    tile_expert = jnp.minimum(jnp.sum(tile_start[:, None] >= end[None, :], axis=1), ne - 1).astype(jnp.int32)
    used_tiles = (end[-1:] // tg).astype(jnp.int32)
    first = jnp.argmax(comb2, axis=1)
    second = jnp.argmax(jnp.where(jnp.arange(ne)[None, :] == first[:, None], -1.0, comb2), axis=1)
    pick = lambda a, e: jnp.take_along_axis(a, e[:, None], axis=1)[:, 0]
    rank_b = pick(rank2, second)
    pos_a = off[first] + pick(rank2, first).astype(jnp.int32)
    pos_b = off[second] + rank_b.astype(jnp.int32)
    token = jnp.arange(t, dtype=jnp.int32)
    slot_token = jnp.zeros((n_slots,), jnp.int32).at[
        jnp.concatenate([pos_a, jnp.where(rank_b >= 0.0, pos_b, n_slots)])].set(
        jnp.concatenate([token, token]), mode="drop")
    piece_base = jnp.arange(pieces, dtype=jnp.int32)[:, None]

    xs = _sc_gather_pieces(h.reshape(pieces * t, LANES), (piece_base * t + slot_token[None, :]).reshape(-1))
    piece_rows = pl.BlockSpec((pieces, tg, LANES), lambda i, te, used: (0, i, 0))
    ys = pl.pallas_call(
        _moe_group_kernel,
        grid_spec=pltpu.PrefetchScalarGridSpec(
            num_scalar_prefetch=2,
            grid=(n_slots // tg,),
            in_specs=[piece_rows,
                      pl.BlockSpec((1, d, f), lambda i, te, used: (te[i], 0, 0)),
                      pl.BlockSpec((1, d, f), lambda i, te, used: (te[i], 0, 0)),
                      pl.BlockSpec((1, f, d), lambda i, te, used: (te[i], 0, 0))],
            out_specs=piece_rows),
        out_shape=jax.ShapeDtypeStruct((pieces, n_slots, LANES), F32),
        compiler_params=_cp("arbitrary"),
        name="moe_group",
    )(tile_expert, used_tiles, xs.reshape(pieces, n_slots, LANES), wg, wu, wd)
    pair = jnp.concatenate([pos_a, jnp.where(rank_b >= 0.0, pos_b, n_slots - 1)])
    y2 = _sc_gather_pieces(ys.reshape(pieces * n_slots, LANES), (piece_base * n_slots + pair[None, :]).reshape(-1))
    return pl.pallas_call(
        _moe_combine_kernel,
        grid=(b, l // tm),
        in_specs=[x_spec, pl.BlockSpec((pieces, 2, 1, tm, LANES), lambda i, j: (0, 0, i, j, 0)), lane_spec, mod_spec],
        out_specs=x_spec,
        out_shape=jax.ShapeDtypeStruct((b, l, d), F32),
        compiler_params=_cp("parallel", "parallel"),
        name="moe_combine",
    )(x, y2.reshape(pieces, 2, b, l, LANES), comb, mod)


SC_GATHER_WINDOW = 128


def _sc_gather_pieces(table, idx):
    m = idx.shape[0]
    w = table.shape[1]
    mesh = plsc.VectorSubcoreMesh(core_axis_name="core", subcore_axis_name="subcore")

    @functools.partial(pl.kernel, out_type=jax.ShapeDtypeStruct((m, w), table.dtype), mesh=mesh,
                       name="sc_gather_rows")
    def gather(t_hbm, i_hbm, o_hbm):
        def body(i_vmem, o_vmem):
            pltpu.sync_copy(t_hbm.at[i_vmem.at[0]], o_vmem)

        pltpu.emit_pipeline(
            body,
            grid=(m // SC_GATHER_WINDOW,),
            in_specs=[pl.BlockSpec((1, SC_GATHER_WINDOW), lambda i: (0, i))],
            out_specs=[pl.BlockSpec((SC_GATHER_WINDOW, w), lambda i: (i, 0))],
            core_axis_name=("core", "subcore"),
            dimension_semantics=(pltpu.PARALLEL,),
        )(i_hbm, o_hbm)

    return gather(table, idx.reshape(1, m))


def _lane_row(values, lane0):
    row = jnp.zeros((LANES,), F32)
    return lax.dynamic_update_slice(row, values.reshape(-1).astype(F32), (lane0,))


def _reorder_w_in(w_in):
    g = GROUP_W
    o = [0, g, 2 * g, 3 * g, 4 * g, 4 * g + 8, 4 * g + 16]
    dq_dk_dv = w_in[..., o[0]:o[3]]
    dgate = w_in[..., o[3]:o[4]]
    dbeta_da = w_in[..., o[4]:o[6]]
    base = o[6]
    df = w_in[..., base:base + 3 * g]
    hy = w_in[..., base + 3 * g:base + 6 * g]
    ml = w_in[..., base + 6 * g:base + 9 * g]
    mo = w_in[..., base + 9 * g:base + 10 * g]
    mi_mf = w_in[..., base + 10 * g:base + 10 * g + 16]
    pad = jnp.zeros(w_in.shape[:-1] + (LANES - 32,), w_in.dtype)
    return jnp.concatenate([dq_dk_dv, hy, ml, df, dgate, mo, dbeta_da, mi_mf, pad], axis=-1).astype(BF16)


def _layer(l, x, mod, cache, p, hy_consts, dft):
    b, seq, _ = x.shape
    latent = cache is not None
    z = _proj_in(x, mod, p["norm1_g"][l], p["w_in"][l])

    qkv, = _dwconv(z, p["dn_conv"][l], COL_DN, "deltanet")
    dn_par = jnp.zeros((8, LANES), F32).at[0].set(_lane_row(p["dn_a_log"][l], LANE_A)).at[1].set(
        _lane_row(p["dn_dt_bias"][l], LANE_A))
    s0 = cache[2] if latent else jnp.zeros((b, 2, N_HEADS, HEAD_DIM, HEAD_DIM), F32)
    o_dn_f, o_dn_b, s_dn = _deltanet(qkv, z, dn_par, s0)
    o_dn = (o_dn_f, o_dn_b)

    ml_par = jnp.zeros((8, LANES), F32).at[0].set(_lane_row(p["ml_i_bias"][l], LANE_I)).at[1].set(
        _lane_row(p["ml_f_bias"][l], LANE_F))
    if latent:
        c0 = cache[3]
        n0 = cache[4].reshape(b, 2, 1, GROUP_W)
        m0 = jnp.repeat(cache[5], HEAD_DIM, axis=-1).reshape(b, 2, 1, GROUP_W)
    else:
        c0 = jnp.zeros((b, 2, N_HEADS, HEAD_DIM, HEAD_DIM), F32)
        n0 = m0 = jnp.zeros((b, 2, 1, GROUP_W), F32)
    h_ml_f, h_ml_b, c_ml, n_ml, m_ml = _mlstm(z, ml_par, c0, n0, m0)
    h_ml = (h_ml_f, h_ml_b)
    n_ml = n_ml.reshape(b, 2, N_HEADS, HEAD_DIM)
    m_ml = m_ml.reshape(b, 2, N_HEADS, HEAD_DIM)[..., 0]

    q, kt, v, kd = _diff_prep(z, p["df_q_norm"][l], p["df_k_norm"][l], latent)
    if latent:
        ckt, cv = _cache_prep(cache[0].reshape(b, -1, GROUP_W), cache[1].reshape(b, -1, GROUP_W))
        kt = jnp.concatenate([kt, ckt], axis=-1)
        v = jnp.concatenate([v, cv], axis=2)
    lam_init = 0.8 - 0.6 * math.exp(-0.3 * l)
    o_df = _diff_attn(q, kt, v, p["df_lambda"][l], p["df_norm"][l], lam_init)

    feat, win = hy_consts
    fwd, inv = dft
    zc, zv_lowp = _dwconv(z, p["hy_conv"][l], COL_HY, "plain")
    filt = _hy_filters(feat, win, p["hy_w1"][l], p["hy_b1"][l], p["hy_w2"][l], p["hy_b2"][l], p["hy_w3"][l],
                       p["hy_freq"][l])
    hf = _dft_filter(fwd, filt)
    y1 = _dft_fwd_mul(fwd, zv_lowp, hf, 0)
    z1, z1_lowp = _dft_inv_gate(inv, y1, zc, 0, zc, 2, p["hy_skip"][l, 0])
    y2 = _dft_fwd_mul(fwd, z1_lowp, hf, 1)
    y_hy, _ = _dft_inv_gate(inv, y2, zc, 1, z1, 0, p["hy_skip"][l, 1])

    x = _proj_out(o_dn, z, o_df, y_hy, h_ml, x, mod, p["dn_norm"][l], p["ml_norm"][l], p["w_out"][l])
    j = l // 2
    if mod.shape[0] == 1:
        x = x.reshape(1, b * seq, -1)
    if l % 2 == 0:
        x = _ffn(x, mod, p["norm2_g"][l], p["ffn_w_gate"][j], p["ffn_w_up"][j], p["ffn_w_down"][j])
    else:
        x = _moe(x, mod, p["norm2_g"][l], p["moe_router"][j], p["moe_w_gate"][j], p["moe_w_up"][j],
                 p["moe_w_down"][j])
    x = x.reshape(b, seq, -1)
    fv =z[:, :, 3 * 3 * GROUP_W + 2 * GROUP_W:3 * 3 * GROUP_W + 3 * GROUP_W]
    return x, (kd, fv, s_dn, c_ml, n_ml, m_ml)


def kernel(x_prompt, x_sample, cache_diff_k, cache_diff_v, state_delta, state_mlstm_c, state_mlstm_n, state_mlstm_m, c, c_ctx, norm1_g, norm2_g, w_mod, b_mod, w_in, w_out, dn_conv, dn_a_log, dn_dt_bias, dn_norm, df_q_norm, df_k_norm, df_lambda, df_norm, hy_conv, hy_w1, hy_b1, hy_w2, hy_b2, hy_w3, hy_freq, hy_skip, ml_i_bias, ml_f_bias, ml_norm, ffn_w_gate, ffn_w_up, ffn_w_down, moe_router, moe_w_gate, moe_w_up, moe_w_down):
    depth = w_in.shape[0]
    d_model = x_prompt.shape[-1]
    batch, seq, _ = x_prompt.shape
    dec_batch, dec_seq, _ = x_sample.shape
    p = dict(norm1_g=norm1_g, norm2_g=norm2_g, w_in=_reorder_w_in(w_in), w_out=w_out.astype(BF16),
             dn_conv=dn_conv, dn_a_log=dn_a_log, dn_dt_bias=dn_dt_bias, dn_norm=dn_norm,
             df_q_norm=df_q_norm, df_k_norm=df_k_norm, df_lambda=df_lambda, df_norm=df_norm,
             hy_conv=hy_conv, hy_w1=hy_w1, hy_b1=hy_b1, hy_w2=hy_w2, hy_b2=hy_b2, hy_w3=hy_w3,
             hy_freq=hy_freq, hy_skip=hy_skip, ml_i_bias=ml_i_bias, ml_f_bias=ml_f_bias, ml_norm=ml_norm,
             ffn_w_gate=ffn_w_gate.astype(BF16), ffn_w_up=ffn_w_up.astype(BF16),
             ffn_w_down=ffn_w_down.astype(BF16), moe_router=moe_router,
             moe_w_gate=moe_w_gate.astype(BF16), moe_w_up=moe_w_up.astype(BF16),
             moe_w_down=moe_w_down.astype(BF16))

    n_cond = 1 + dec_batch
    rows = -(-n_cond // 8) * 8
    cond = jnp.concatenate([c_ctx[None, :], c, jnp.zeros((rows - n_cond, d_model), F32)], axis=0)
    mod = _modulation(cond, w_mod, b_mod).reshape(depth, rows, 6, d_model)

    hy_ctx, dft_ctx = _hy_constants(seq), _dft_matrices(seq)
    x = x_prompt
    ctx = []
    for l in range(depth):
        x, out = _layer(l, x, mod[l, 0:1], None, p, hy_ctx, dft_ctx)
        ctx.append(out)
    y_prompt = x
    new_k, new_v, new_s, new_c, new_n, new_m = (jnp.stack([o[i] for o in ctx], axis=1) for i in range(6))
    new_k = new_k.reshape(batch, depth, seq, N_HEADS, 2, DF_DH)
    new_v = new_v.reshape(batch, depth, seq, N_HEADS, HEAD_DIM)

    hy_lat, dft_lat = _hy_constants(dec_seq), _dft_matrices(dec_seq)
    x = x_sample
    for l in range(depth):
        cache = (cache_diff_k[:, l], cache_diff_v[:, l], state_delta[:, l], state_mlstm_c[:, l],
                 state_mlstm_n[:, l], state_mlstm_m[:, l])
        x, _ = _layer(l, x, mod[l, 1:1 + dec_batch], cache, p, hy_lat, dft_lat)
    return (y_prompt, x, new_k, new_v, new_s, new_c, new_n, new_m)
```

```python
import functools
import math
from typing import Any, NamedTuple

import jax
import jax.numpy as jnp
from jax import lax
from jax.experimental import pallas as pl
from jax.experimental.pallas import tpu as pltpu
from jax.experimental.pallas import tpu_sc as plsc

F32 = jnp.float32
BF16 = jnp.bfloat16

N_HEADS = 4
HEAD_DIM = 64
GROUP_W = N_HEADS * HEAD_DIM
DF_DH = 32
CHUNK = 64
GRID_W = 64
ROPE_THETA = 10000.0
HY_BANDS = 8
HY_FAST_DECAY = 0.3
HY_SLOW_DECAY = 1.5
HY_TARGET = 1e-2
HY_SHIFT = 0.05
N_EXPERTS = 8
EPS = 1e-6
NEG = -1e30
LANES = 128
VMEM_LIMIT = 56 * 1024 * 1024

COL_DN, COL_HY, COL_ML, COL_DF = 0, 1, 2, 3
COL_DGATE, COL_MO = 12, 13
COL_SMALL = 28
Z_WIDTH = 29 * LANES
LANE_BETA, LANE_A, LANE_I, LANE_F = 0, 8, 16, 24


def _tile(n, pref):
    t = min(n, pref)
    while n % t:
        t -= LANES if t > LANES else 8
    return t


def _cp(*sem, fuse_inputs=None):
    return pltpu.CompilerParams(dimension_semantics=sem, vmem_limit_bytes=VMEM_LIMIT, allow_input_fusion=fuse_inputs)


def _split3(x):
    x1 = x.astype(BF16)
    r = x - x1.astype(F32)
    x2 = r.astype(BF16)
    r = r - x2.astype(F32)
    return x1, x2, r.astype(BF16)


def _dot(a, b):
    return jnp.dot(a.astype(BF16), b.astype(BF16), preferred_element_type=F32)


def _dot_nt(a, b):
    return lax.dot_general(a.astype(BF16), b.astype(BF16), (((1,), (1,)), ((), ())),
                           preferred_element_type=F32)


def _dot_tn(a, b):
    return lax.dot_general(a.astype(BF16), b.astype(BF16), (((0,), (0,)), ((), ())),
                           preferred_element_type=F32)


def _dot_exact_l(m, x):
    return sum(jnp.dot(m, p, preferred_element_type=F32) for p in _split3(x))


def _dot_exact_r(x, m):
    return sum(jnp.dot(p, m, preferred_element_type=F32) for p in _split3(x))


def _dot_hi(a, b):
    a1, a2, _ = _split3(a)
    b1, b2, _ = _split3(b)
    return (jnp.dot(a1, b1, preferred_element_type=F32) + jnp.dot(a1, b2, preferred_element_type=F32)
            + jnp.dot(a2, b1, preferred_element_type=F32))


def _seg_mean(x, seg):
    w = x.shape[-1]
    sh = int(math.log2(seg))
    r = lax.shift_right_logical(lax.broadcasted_iota(jnp.int32, (w, w), 0), sh)
    c = lax.shift_right_logical(lax.broadcasted_iota(jnp.int32, (w, w), 1), sh)
    bd = jnp.where(r == c, 1.0, 0.0).astype(BF16)
    return _dot_exact_r(x, bd) * (1.0 / seg)


def _row_bcast(col):
    n = col.shape[0]
    return jnp.broadcast_to(jnp.broadcast_to(col, (n, LANES)).T[0:1, :], (n, n))


def _lane_pick(x, lane_idx):
    lane = lax.broadcasted_iota(jnp.int32, x.shape, 1)
    return jnp.sum(jnp.where(lane == lane_idx, x, 0.0), axis=1, keepdims=True)


def _sigmoid(x):
    return 1.0 / (1.0 + jnp.exp(-x))


def _softplus(x):
    return jnp.maximum(x, 0.0) + jnp.log1p(jnp.exp(-jnp.abs(x)))


def _mod_kernel(c_ref, w_ref, b_ref, o_ref):
    c = c_ref[...]
    o_ref[0] = _dot(c * _sigmoid(c), w_ref[0]) + b_ref[0]


def _modulation(cond, w_mod, b_mod):
    depth, d, n = w_mod.shape
    r = cond.shape[0]
    tn = n // 4
    return pl.pallas_call(
        _mod_kernel,
        grid=(depth, n // tn),
        in_specs=[pl.BlockSpec((r, d), lambda l, j: (0, 0)),
                  pl.BlockSpec((1, d, tn), lambda l, j: (l, 0, j)),
                  pl.BlockSpec((1, 1, tn), lambda l, j: (l, 0, j))],
        out_specs=pl.BlockSpec((1, r, tn), lambda l, j: (l, 0, j)),
        out_shape=jax.ShapeDtypeStruct((depth, r, n), F32),
        compiler_params=_cp("parallel", "parallel"),
        name="modulation",
    )(cond, w_mod, b_mod.reshape(depth, 1, n))


def _modulated_norm(x, gain, shift, scale):
    var = jnp.mean(x * x, axis=-1, keepdims=True)
    return x * lax.rsqrt(var + EPS) * gain * (1.0 + scale) + shift


def _proj_in_kernel(x_ref, mod_ref, g_ref, w_ref, o_ref):
    h = _modulated_norm(x_ref[0], g_ref[...], mod_ref[0, 0:1, :], mod_ref[0, 1:2, :])
    o_ref[0] = jnp.dot(h.astype(BF16), w_ref[...], preferred_element_type=F32)


def _proj_in(x, mod, gain, w):
    b, l, d = x.shape
    tm = _tile(l, 512)
    per_batch = mod.shape[0] > 1
    return pl.pallas_call(
        _proj_in_kernel,
        grid=(b, l // tm),
        in_specs=[pl.BlockSpec((1, tm, d), lambda i, j: (i, j, 0)),
                  pl.BlockSpec((1, 6, d), lambda i, j: (i if per_batch else 0, 0, 0)),
                  pl.BlockSpec((1, d), lambda i, j: (0, 0)),
                  pl.BlockSpec((d, Z_WIDTH), lambda i, j: (0, 0))],
        out_specs=pl.BlockSpec((1, tm, Z_WIDTH), lambda i, j: (i, j, 0)),
        out_shape=jax.ShapeDtypeStruct((b, l, Z_WIDTH), F32),
        compiler_params=_cp("parallel", "parallel"),
        name="proj_in",
    )(x, mod, gain.reshape(1, d), w)


def _dwconv_kernel(z_ref, zp_ref, zn_ref, w_ref, o_ref, *lowp_refs, mode):
    i = pl.program_id(1)
    z = z_ref[0]
    tm = z.shape[0]
    prev_row = jnp.where(i > 0, zp_ref[0, 7:8, :], 0.0)
    next_row = jnp.where(i < pl.num_programs(1) - 1, zn_ref[0, 0:1, :], 0.0)
    rid = lax.broadcasted_iota(jnp.int32, z.shape, 0)
    zm1 = jnp.where(rid == 0, prev_row, pltpu.roll(z, 1, 0))
    zp1 = jnp.where(rid == tm - 1, next_row, pltpu.roll(z, tm - 1, 0))
    y = zm1 * w_ref[0:1, :] + z * w_ref[1:2, :] + zp1 * w_ref[2:3, :]
    if mode == "deltanet":
        y = y * _sigmoid(y)
        q, k, v = y[:, :GROUP_W], y[:, GROUP_W:2 * GROUP_W], y[:, 2 * GROUP_W:]
        q = q * lax.rsqrt(_seg_mean(q * q, HEAD_DIM) * HEAD_DIM + EPS) * (HEAD_DIM ** -0.5)
        k = k * lax.rsqrt(_seg_mean(k * k, HEAD_DIM) * HEAD_DIM + EPS)
        o_ref[0, :, 0:GROUP_W] = q
        o_ref[0, :, GROUP_W:2 * GROUP_W] = k
        o_ref[0, :, 2 * GROUP_W:] = v
    else:
        o_ref[0] = y
        lowp_refs[0][0] = y[:, 2 * GROUP_W:].astype(BF16)


def _dwconv(z, w, col_block, mode):
    b, l, _ = z.shape
    c = 3 * GROUP_W
    tm = _tile(l, 1024)
    hb = tm // 8
    last = l // 8 - 1
    out_specs = [pl.BlockSpec((1, tm, c), lambda i, j: (i, j, 0))]
    out_shape = [jax.ShapeDtypeStruct((b, l, c), F32)]
    if mode == "plain":
        out_specs.append(pl.BlockSpec((1, tm, GROUP_W), lambda i, j: (i, j, 0)))
        out_shape.append(jax.ShapeDtypeStruct((b, l, GROUP_W), BF16))
    return pl.pallas_call(
        functools.partial(_dwconv_kernel, mode=mode),
        grid=(b, l // tm),
        in_specs=[pl.BlockSpec((1, tm, c), lambda i, j: (i, j, col_block)),
                  pl.BlockSpec((1, 8, c), lambda i, j: (i, jnp.maximum(j * hb - 1, 0), col_block)),
                  pl.BlockSpec((1, 8, c), lambda i, j: (i, jnp.minimum((j + 1) * hb, last), col_block)),
                  pl.BlockSpec((3, c), lambda i, j: (0, 0))],
        out_specs=out_specs,
        out_shape=out_shape,
        compiler_params=_cp("parallel", "parallel"),
        name="dwconv_" + mode,
    )(z, z, z, w)


STACK = N_HEADS * CHUNK


def _stack_heads(x):
    return jnp.concatenate([x] * N_HEADS, axis=0)


def _block_diag(x, head_eq):
    return jnp.where(head_eq, _stack_heads(x), 0.0)


def _fold_heads(x):
    return x[0:CHUNK] + x[CHUNK:2 * CHUNK] + x[2 * CHUNK:3 * CHUNK] + x[3 * CHUNK:]


def _chunk_masks(d):
    row = lax.broadcasted_iota(jnp.int32, (STACK, STACK), 0)
    col = lax.broadcasted_iota(jnp.int32, (STACK, STACK), 1)
    head_eq = lax.shift_right_logical(row, 6) == lax.shift_right_logical(col, 6)
    rel = ((row & (CHUNK - 1)) - (col & (CHUNK - 1))) * (1 - 2 * d)
    r64 = lax.broadcasted_iota(jnp.int32, (CHUNK, CHUNK), 0)
    c64 = lax.broadcasted_iota(jnp.int32, (CHUNK, CHUNK), 1)
    cum = jnp.where((r64 - c64) * (1 - 2 * d) >= 0, 1.0, 0.0).astype(BF16)
    return head_eq, head_eq & (rel >= 0), head_eq & (rel > 0), cum


def _head_cols(x, lane0):
    xs = _stack_heads(x)
    row = lax.broadcasted_iota(jnp.int32, xs.shape, 0)
    lane = lax.broadcasted_iota(jnp.int32, xs.shape, 1)
    return jnp.sum(jnp.where(lane == lane0 + lax.shift_right_logical(row, 6), xs, 0.0), axis=1, keepdims=True)


def _head_lanes(x, lane0):
    src = lax.broadcasted_iota(jnp.int32, (LANES, GROUP_W), 0)
    dst = lax.broadcasted_iota(jnp.int32, (LANES, GROUP_W), 1)
    expand = jnp.where(src == lane0 + lax.shift_right_logical(dst, 6), 1.0, 0.0).astype(BF16)
    return _dot_exact_r(x, expand)


def _lanes_to_col(x):
    return jnp.broadcast_to(x, (LANES, GROUP_W)).T[:, 0:1]


def _load_diag_state(dst_ref, src_ref):
    dst_ref[...] = jnp.zeros(dst_ref.shape, F32)
    for r in range(dst_ref.shape[0]):
        for d in range(2):
            for h in range(N_HEADS):
                lo, hi = h * HEAD_DIM, (h + 1) * HEAD_DIM
                dst_ref[r, d, lo:hi, lo:hi] = src_ref[r, d, h]


def _store_diag_state(dst_ref, src_ref):
    for r in range(src_ref.shape[0]):
        for d in range(2):
            for h in range(N_HEADS):
                lo, hi = h * HEAD_DIM, (h + 1) * HEAD_DIM
                dst_ref[r, d, h] = src_ref[r, d, lo:hi, lo:hi]


def _rec_specs(b, l):
    rows = max(r for r in (4, 2, 1) if b % r == 0)
    tile = _tile(l, 512)
    nt = l // tile
    fwd = lambda width, col: pl.BlockSpec((rows, tile, width), lambda i, j: (i, j, col))
    bwd = lambda width, col: pl.BlockSpec((rows, tile, width), lambda i, j: (i, nt - 1 - j, col))
    state = pl.BlockSpec((rows, 2, N_HEADS, HEAD_DIM, HEAD_DIM), lambda i, j: (i, 0, 0, 0, 0))
    return rows, tile, nt, fwd, bwd, state


class _Chain(NamedTuple):
    r: int
    d: int
    x_ref: Any
    sm_ref: Any
    o_ref: Any
    rows: Any


def _each(fn, *cols):
    return [fn(*args) for args in zip(*cols)]


def _unit_tri_inverses(mats):
    n = mats[0].shape[0]
    row = lax.broadcasted_iota(jnp.int32, (n, n), 0)
    col = lax.broadcasted_iota(jnp.int32, (n, n), 1)

    def same_block(log2_size):
        return lax.shift_right_logical(row, log2_size) == lax.shift_right_logical(col, log2_size)

    p = [jnp.where(same_block(3), -a, 0.0) for a in mats]
    t = [jnp.where(row == col, 1.0, 0.0) + x for x in p]
    for _ in range(2):
        p = _each(_dot, p, p)
        t = _each(jnp.add, t, _each(_dot, t, p))
    for log2_size in range(4, int(math.log2(CHUNK)) + 1):
        level = same_block(log2_size) & jnp.logical_not(same_block(log2_size - 1))
        off_t = [_dot(jnp.where(level, a, 0.0), x) for a, x in zip(mats, t)]
        t = _each(jnp.subtract, t, _each(_dot, t, off_t))
    return t


def _deltanet_chunks(chains, par_ref, s_ref):
    masks = {d: _chunk_masks(d) for d in (0, 1)}
    head_eq = masks[0][0]
    incl = [masks[c.d][1] for c in chains]
    strict = [masks[c.d][2] for c in chains]
    cum = [masks[c.d][3] for c in chains]
    lane_a = [LANE_A + c.d * N_HEADS for c in chains]
    lane_b = [LANE_BETA + c.d * N_HEADS for c in chains]
    q = [c.x_ref[c.r, c.rows, 0:GROUP_W] for c in chains]
    k = [c.x_ref[c.r, c.rows, GROUP_W:2 * GROUP_W] for c in chains]
    v = [c.x_ref[c.r, c.rows, 2 * GROUP_W:3 * GROUP_W] for c in chains]
    sm = [c.sm_ref[c.r, c.rows, :] for c in chains]
    beta_all = _each(_sigmoid, sm)
    g_all = [-jnp.exp(par_ref[0:1, :]) * _softplus(x + par_ref[1:2, :]) for x in sm]
    gc_all = _each(_dot_exact_l, cum, g_all)
    gc = _each(_head_lanes, gc_all, lane_a)
    beta = _each(_head_lanes, beta_all, lane_b)
    g_last = [x[CHUNK - 1:CHUNK] if c.d == 0 else x[0:1] for x, c in zip(gc, chains)]
    gc_col = _each(_head_cols, gc_all, lane_a)
    beta_col = _each(_head_cols, beta_all, lane_b)
    gc_row = _each(_row_bcast, gc_col)
    decay = [jnp.exp(jnp.where(m, x - y, NEG)) for m, x, y in zip(incl, gc_col, gc_row)]
    k_rows = _each(_stack_heads, k)
    kk = [_dot_nt(jnp.where(head_eq, x, 0.0), x) for x in k_rows]
    a = [jnp.where(m, b * x * dc, 0.0) for m, b, x, dc in zip(strict, beta_col, kk, decay)]
    t = _unit_tri_inverses(a)
    egc = _each(jnp.exp, gc)
    u = [_fold_heads(_dot(x, _block_diag(y * b, head_eq))) for x, y, b in zip(t, v, beta)]
    w = [_fold_heads(_dot(x, _block_diag(y * (b * e), head_eq))) for x, y, b, e in zip(t, k, beta, egc)]
    s = [s_ref[c.r, c.d] for c in chains]
    ws_qs = [_dot(jnp.concatenate([x, y * e], axis=0), z) for x, y, e, z in zip(w, q, egc, s)]
    v_new = [x - y[0:CHUNK] for x, y in zip(u, ws_qs)]
    qk = [_dot_nt(_block_diag(x, head_eq), y) * dc for x, y, dc in zip(q, k_rows, decay)]
    o_intra = [_fold_heads(_dot(x, _block_diag(y, head_eq))) for x, y in zip(qk, v_new)]
    s_add = [_dot_tn(x * jnp.exp(gl - g), y) for x, gl, g, y in zip(k, g_last, gc, v_new)]
    for c, x, y, z, gl, sa in zip(chains, ws_qs, o_intra, s, g_last, s_add):
        c.o_ref[c.r, c.rows, :] = x[CHUNK:] + y
        s_ref[c.r, c.d] = z * jnp.exp(gl) + jnp.where(head_eq, sa, 0.0)


def _chunk_chains(n_rows, n_chunks, c, f_refs, b_refs):
    rows_f = pl.ds(pl.multiple_of(c * CHUNK, CHUNK), CHUNK)
    rows_b = pl.ds(pl.multiple_of((n_chunks - 1 - c) * CHUNK, CHUNK), CHUNK)
    return [_Chain(r, d, *refs, rows) for r in range(n_rows)
            for d, refs, rows in ((0, f_refs, rows_f), (1, b_refs, rows_b))]


def _deltanet_kernel(qf_ref, smf_ref, qb_ref, smb_ref, par_ref, s0_ref, of_ref, ob_ref, sout_ref, s_ref, *,
                     n_chunks):
    j = pl.program_id(1)

    @pl.when(j == 0)
    def _():
        _load_diag_state(s_ref, s0_ref)

    def chunk_body(c, carry):
        chains = _chunk_chains(s_ref.shape[0], n_chunks, c, (qf_ref, smf_ref, of_ref), (qb_ref, smb_ref, ob_ref))
        _deltanet_chunks(chains, par_ref, s_ref)
        return carry

    lax.fori_loop(0, n_chunks, chunk_body, 0)

    @pl.when(j == pl.num_programs(1) - 1)
    def _():
        _store_diag_state(sout_ref, s_ref)


def _deltanet(qkv, z, par, s0):
    b, l, _ = qkv.shape
    rows, tile, nt, fwd, bwd, state = _rec_specs(b, l)
    return pl.pallas_call(
        functools.partial(_deltanet_kernel, n_chunks=tile // CHUNK),
        grid=(b // rows, nt),
        in_specs=[fwd(3 * GROUP_W, 0), fwd(LANES, COL_SMALL), bwd(3 * GROUP_W, 0), bwd(LANES, COL_SMALL),
                  pl.BlockSpec((8, LANES), lambda i, j: (0, 0)), state],
        out_specs=[fwd(GROUP_W, 0), bwd(GROUP_W, 0), state],
        out_shape=[jax.ShapeDtypeStruct((b, l, GROUP_W), F32), jax.ShapeDtypeStruct((b, l, GROUP_W), F32),
                   jax.ShapeDtypeStruct((b, 2, N_HEADS, HEAD_DIM, HEAD_DIM), F32)],
        scratch_shapes=[pltpu.VMEM((rows, 2, STACK, STACK), F32)],
        compiler_params=_cp("parallel", "arbitrary"),
        name="deltanet",
    )(qkv, z, qkv, z, par, s0)


def _mlstm_chunks(chains, par_ref, c_ref, n_ref, m_ref):
    masks = {d: _chunk_masks(d) for d in (0, 1)}
    head_eq = masks[0][0]
    incl = [masks[c.d][1] for c in chains]
    cum = [masks[c.d][3] for c in chains]
    lane_i = [LANE_I + c.d * N_HEADS for c in chains]
    lane_f = [LANE_F + c.d * N_HEADS for c in chains]
    q = [c.x_ref[c.r, c.rows, 0:GROUP_W] for c in chains]
    k = [c.x_ref[c.r, c.rows, GROUP_W:2 * GROUP_W] * (HEAD_DIM ** -0.5) for c in chains]
    v = [c.x_ref[c.r, c.rows, 2 * GROUP_W:3 * GROUP_W] for c in chains]
    sm = [c.sm_ref[c.r, c.rows, :] for c in chains]
    i_all = [x + par_ref[0:1, :] for x in sm]
    f_all = [-_softplus(-(x + par_ref[1:2, :])) for x in sm]
    bc_all = _each(_dot_exact_l, cum, f_all)
    ic = _each(_head_lanes, i_all, lane_i)
    bc = _each(_head_lanes, bc_all, lane_f)
    b_tot = [x[CHUNK - 1:CHUNK] if c.d == 0 else x[0:1] for x, c in zip(bc, chains)]
    ic_col = _each(_head_cols, i_all, lane_i)
    bc_col = _each(_head_cols, bc_all, lane_f)
    c_s = [c_ref[c.r, c.d] for c in chains]
    n_s = [n_ref[c.r, c.d] for c in chains]
    m_s = [m_ref[c.r, c.d] for c in chains]
    w_row = _each(_row_bcast, _each(jnp.subtract, ic_col, bc_col))
    dlog = [jnp.where(m, x + y, NEG) for m, x, y in zip(incl, bc_col, w_row)]
    a = _each(jnp.add, bc_col, _each(_lanes_to_col, m_s))
    m_t = [jnp.maximum(x, jnp.max(y, axis=1, keepdims=True)) for x, y in zip(a, dlog)]
    inter = [jnp.exp(x - y) for x, y in zip(a, m_t)]
    q_bd = [_block_diag(x, head_eq) for x in q]
    s = [_dot_nt(x, _stack_heads(y)) * jnp.exp(dl - mt) for x, y, dl, mt in zip(q_bd, k, dlog, m_t)]
    inter_part = _each(_dot, q_bd, c_s)
    intra_part = [_dot(x, _block_diag(y, head_eq)) for x, y in zip(s, v)]
    den = [it * jnp.sum(x * n, axis=1, keepdims=True) + jnp.sum(y, axis=1, keepdims=True)
           for it, x, n, y in zip(inter, q_bd, n_s, s)]
    wend = [bt - x + y for bt, x, y in zip(b_tot, bc, ic)]
    a_end = _each(jnp.add, b_tot, m_s)
    m_new = [jnp.maximum(x, jnp.max(y, axis=0, keepdims=True)) for x, y in zip(a_end, wend)]
    dec = [jnp.exp(x - y) for x, y in zip(a_end, m_new)]
    kw = [x * jnp.exp(y - z) for x, y, z in zip(k, wend, m_new)]
    c_add = _each(_dot_tn, kw, v)
    for i, c in enumerate(chains):
        num = inter[i] * inter_part[i] + intra_part[i]
        c.o_ref[c.r, c.rows, :] = _fold_heads(num / jnp.maximum(jnp.abs(den[i]), jnp.exp(-m_t[i])))
        c_ref[c.r, c.d] = dec[i] * c_s[i] + jnp.where(head_eq, c_add[i], 0.0)
        n_ref[c.r, c.d] = dec[i] * n_s[i] + jnp.sum(kw[i], axis=0, keepdims=True)
        m_ref[c.r, c.d] = m_new[i]


def _mlstm_kernel(zf_ref, smf_ref, zb_ref, smb_ref, par_ref, c0_ref, n0_ref, m0_ref,
                  of_ref, ob_ref, cout_ref, nout_ref, mout_ref, c_ref, n_ref, m_ref, *, n_chunks):
    j = pl.program_id(1)

    @pl.when(j == 0)
    def _():
        _load_diag_state(c_ref, c0_ref)
        n_ref[...] = n0_ref[...]
        m_ref[...] = m0_ref[...]

    def chunk_body(c, carry):
        chains = _chunk_chains(c_ref.shape[0], n_chunks, c, (zf_ref, smf_ref, of_ref), (zb_ref, smb_ref, ob_ref))
        _mlstm_chunks(chains, par_ref, c_ref, n_ref, m_ref)
        return carry

    lax.fori_loop(0, n_chunks, chunk_body, 0)

    @pl.when(j == pl.num_programs(1) - 1)
    def _():
        _store_diag_state(cout_ref, c_ref)
        nout_ref[...] = n_ref[...]
        mout_ref[...] = m_ref[...]


def _mlstm(z, par, c0, n0, m0):
    b, l, _ = z.shape
    rows, tile, nt, fwd, bwd, state = _rec_specs(b, l)
    vec = pl.BlockSpec((rows, 2, 1, GROUP_W), lambda i, j: (i, 0, 0, 0))
    return pl.pallas_call(
        functools.partial(_mlstm_kernel, n_chunks=tile // CHUNK),
        grid=(b // rows, nt),
        in_specs=[fwd(3 * GROUP_W, COL_ML), fwd(LANES, COL_SMALL), bwd(3 * GROUP_W, COL_ML), bwd(LANES, COL_SMALL),
                  pl.BlockSpec((8, LANES), lambda i, j: (0, 0)), state, vec, vec],
        out_specs=[fwd(GROUP_W, 0), bwd(GROUP_W, 0), state, vec, vec],
        out_shape=[jax.ShapeDtypeStruct((b, l, GROUP_W), F32), jax.ShapeDtypeStruct((b, l, GROUP_W), F32),
                   jax.ShapeDtypeStruct((b, 2, N_HEADS, HEAD_DIM, HEAD_DIM), F32),
                   jax.ShapeDtypeStruct((b, 2, 1, GROUP_W), F32), jax.ShapeDtypeStruct((b, 2, 1, GROUP_W), F32)],
        scratch_shapes=[pltpu.VMEM((rows, 2, STACK, STACK), F32), pltpu.VMEM((rows, 2, 1, GROUP_W), F32),
                        pltpu.VMEM((rows, 2, 1, GROUP_W), F32)],
        compiler_params=_cp("parallel", "arbitrary"),
        name="mlstm",
    )(z, z, z, z, par, c0, n0, m0)


def _diff_prep_kernel(z_ref, gq_ref, gk_ref, cos_ref, sin_ref, q_ref, kt_ref, v_ref, kd_ref, *, rope):
    z = z_ref[0]
    q, k, v = z[:, :GROUP_W], z[:, GROUP_W:2 * GROUP_W], z[:, 2 * GROUP_W:]
    q = q * lax.rsqrt(_seg_mean(q * q, DF_DH) + EPS) * gq_ref[...]
    k = k * lax.rsqrt(_seg_mean(k * k, DF_DH) + EPS) * gk_ref[...]
    kd_ref[0] = k
    if rope:
        lane = lax.broadcasted_iota(jnp.int32, q.shape, 1)
        first = (lane & (DF_DH - 1)) < DF_DH // 2

        def rot(x):
            swapped = jnp.where(first, pltpu.roll(x, GROUP_W - DF_DH // 2, 1), pltpu.roll(x, DF_DH // 2, 1))
            return x * cos_ref[...] + swapped * sin_ref[...]

        q, k = rot(q), rot(k)
    _store_attn_operands(q * (DF_DH ** -0.5), k, v, q_ref, kt_ref, v_ref)


def _store_attn_operands(q, k, v, q_ref, kt_ref, v_ref):
    kt = k.T
    lane = lax.broadcasted_iota(jnp.int32, (k.shape[0], HEAD_DIM), 1)
    ones = jnp.ones((k.shape[0], HEAD_DIM), BF16)
    for h in range(N_HEADS):
        lo, hi = h * HEAD_DIM, (h + 1) * HEAD_DIM
        if q is not None:
            q_ref[0, h, 0] = jnp.where(lane < DF_DH, q[:, lo:hi], 0.0).astype(BF16)
            q_ref[0, h, 1] = jnp.where(lane >= DF_DH, q[:, lo:hi], 0.0).astype(BF16)
        v_ref[0, h] = jnp.concatenate([v[:, lo:hi].astype(BF16), ones], axis=1)
        kt_ref[0, h] = kt[lo:hi, :].astype(BF16)


def _cache_prep_kernel(k_ref, v_ref, kt_ref, vh_ref):
    _store_attn_operands(None, k_ref[0], v_ref[0], None, kt_ref, vh_ref)


def _rope_tables(l):
    rows = l // GRID_W
    r = jnp.repeat(jnp.arange(rows, dtype=F32), GRID_W)
    col = jnp.tile(jnp.arange(GRID_W, dtype=F32), rows)
    n_freq = DF_DH // 4
    inv = ROPE_THETA ** (-jnp.arange(n_freq, dtype=F32) / n_freq)
    ang = jnp.concatenate([r[:, None] * inv, col[:, None] * inv], axis=-1)
    cos, sin = jnp.cos(ang), jnp.sin(ang)
    reps = GROUP_W // DF_DH
    return (jnp.tile(jnp.concatenate([cos, cos], axis=-1), (1, reps)),
            jnp.tile(jnp.concatenate([-sin, sin], axis=-1), (1, reps)))


def _diff_prep(z, gq, gk, rope):
    b, l, _ = z.shape
    tm = _tile(l, 1024)
    if rope:
        cos, sin = _rope_tables(l)
    else:
        cos = sin = jnp.zeros((l, GROUP_W), F32)
    return pl.pallas_call(
        functools.partial(_diff_prep_kernel, rope=rope),
        grid=(b, l // tm),
        in_specs=[pl.BlockSpec((1, tm, 3 * GROUP_W), lambda i, j: (i, j, COL_DF)),
                  pl.BlockSpec((1, GROUP_W), lambda i, j: (0, 0)),
                  pl.BlockSpec((1, GROUP_W), lambda i, j: (0, 0)),
                  pl.BlockSpec((tm, GROUP_W), lambda i, j: (j, 0)),
                  pl.BlockSpec((tm, GROUP_W), lambda i, j: (j, 0))],
        out_specs=[pl.BlockSpec((1, N_HEADS, 2, tm, HEAD_DIM), lambda i, j: (i, 0, 0, j, 0)),
                   pl.BlockSpec((1, N_HEADS, HEAD_DIM, tm), lambda i, j: (i, 0, 0, j)),
                   pl.BlockSpec((1, N_HEADS, tm, 2 * HEAD_DIM), lambda i, j: (i, 0, j, 0)),
                   pl.BlockSpec((1, tm, GROUP_W), lambda i, j: (i, j, 0))],
        out_shape=[jax.ShapeDtypeStruct((b, N_HEADS, 2, l, HEAD_DIM), BF16),
                   jax.ShapeDtypeStruct((b, N_HEADS, HEAD_DIM, l), BF16),
                   jax.ShapeDtypeStruct((b, N_HEADS, l, 2 * HEAD_DIM), BF16),
                   jax.ShapeDtypeStruct((b, l, GROUP_W), F32)],
        compiler_params=_cp("parallel", "parallel"),
        name="diff_prep",
    )(z, jnp.tile(gq, GROUP_W // DF_DH).reshape(1, GROUP_W), jnp.tile(gk, GROUP_W // DF_DH).reshape(1, GROUP_W),
      cos, sin)


def _cache_prep(ck, cv):
    b, p, _ = ck.shape
    tm = _tile(p, 512)
    return pl.pallas_call(
        _cache_prep_kernel,
        grid=(b, p // tm),
        in_specs=[pl.BlockSpec((1, tm, GROUP_W), lambda i, j: (i, j, 0)),
                  pl.BlockSpec((1, tm, GROUP_W), lambda i, j: (i, j, 0))],
        out_specs=[pl.BlockSpec((1, N_HEADS, HEAD_DIM, tm), lambda i, j: (i, 0, 0, j)),
                   pl.BlockSpec((1, N_HEADS, tm, 2 * HEAD_DIM), lambda i, j: (i, 0, j, 0))],
        out_shape=[jax.ShapeDtypeStruct((b, N_HEADS, HEAD_DIM, p), BF16),
                   jax.ShapeDtypeStruct((b, N_HEADS, p, 2 * HEAD_DIM), BF16)],
        compiler_params=_cp("parallel", "parallel"),
        name="cache_prep",
    )(ck, cv)


def _diff_attn_kernel(q_ref, kt_ref, v_ref, lam_ref, g_ref, o_ref, m_ref, acc_ref, *, lam_init):
    ik = pl.program_id(3)

    @pl.when(ik == 0)
    def _():
        m_ref[...] = jnp.full(m_ref.shape, NEG, F32)
        acc_ref[...] = jnp.zeros(acc_ref.shape, F32)

    chains = [(h, m) for h in range(q_ref.shape[1]) for m in range(2)]
    s = [jnp.dot(q_ref[0, h, m], kt_ref[0, h], preferred_element_type=F32) for h, m in chains]
    m_old = [m_ref[h, m] for h, m in chains]
    m_new = [jnp.maximum(x, jnp.max(y, axis=1, keepdims=True)) for x, y in zip(m_old, s)]
    p = [jnp.exp(x - y[:, 0:1]).astype(BF16) for x, y in zip(s, m_new)]
    pv = [jnp.dot(x, v_ref[0, h], preferred_element_type=F32) for x, (h, m) in zip(p, chains)]
    for i, (h, m) in enumerate(chains):
        acc_ref[h, m] = jnp.exp(m_old[i] - m_new[i]) * acc_ref[h, m] + pv[i]
        m_ref[h, m] = m_new[i]

    @pl.when(ik == pl.num_programs(3) - 1)
    def _():
        lp = lam_ref[...]
        lam = (jnp.exp(jnp.sum(lp[0:1] * lp[1:2], axis=1, keepdims=True))
               - jnp.exp(jnp.sum(lp[2:3] * lp[3:4], axis=1, keepdims=True)) + lam_init)
        for h in range(q_ref.shape[1]):
            a0, a1 = acc_ref[h, 0], acc_ref[h, 1]
            o = (a0[:, :HEAD_DIM] / a0[:, HEAD_DIM:HEAD_DIM + 1]
                 - lam * (a1[:, :HEAD_DIM] / a1[:, HEAD_DIM:HEAD_DIM + 1]))
            var = jnp.mean(o * o, axis=1, keepdims=True)
            o_ref[0, h] = o * lax.rsqrt(var + EPS) * g_ref[...] * (1.0 - lam_init)


ATTN_HEADS_PER_STEP = 2


def _diff_attn(q, kt, v, lam_par, gain, lam_init):
    b, _, _, l, _ = q.shape
    lk = kt.shape[-1]
    hs = ATTN_HEADS_PER_STEP
    tq = _tile(l, 512)
    tk = _tile(lk, 2304)
    return pl.pallas_call(
        functools.partial(_diff_attn_kernel, lam_init=lam_init),
        grid=(b, N_HEADS // hs, l // tq, lk // tk),
        in_specs=[pl.BlockSpec((1, hs, 2, tq, HEAD_DIM), lambda i, h, a, c: (i, h, 0, a, 0)),
                  pl.BlockSpec((1, hs, HEAD_DIM, tk), lambda i, h, a, c: (i, h, 0, c)),
                  pl.BlockSpec((1, hs, tk, 2 * HEAD_DIM), lambda i, h, a, c: (i, h, c, 0)),
                  pl.BlockSpec((4, DF_DH), lambda i, h, a, c: (0, 0)),
                  pl.BlockSpec((1, HEAD_DIM), lambda i, h, a, c: (0, 0))],
        out_specs=pl.BlockSpec((1, hs, tq, HEAD_DIM), lambda i, h, a, c: (i, h, a, 0)),
        out_shape=jax.ShapeDtypeStruct((b, N_HEADS, l, HEAD_DIM), F32),
        scratch_shapes=[pltpu.VMEM((hs, 2, tq, LANES), F32), pltpu.VMEM((hs, 2, tq, 2 * HEAD_DIM), F32)],
        compiler_params=_cp("parallel", "parallel", "parallel", "arbitrary"),
        name="diff_attn",
    )(q, kt, v, lam_par, gain.reshape(1, HEAD_DIM))


def _hy_filter_kernel(feat_ref, win_ref, w1_ref, b1_ref, w2_ref, b2_ref, w3_ref, fr_ref, o_ref):
    h = jnp.sin(fr_ref[0:1, :] * (_dot_hi(feat_ref[...], w1_ref[...]) + b1_ref[...]))
    h = jnp.sin(fr_ref[1:2, :] * (_dot_hi(h, w2_ref[...]) + b2_ref[...]))
    h = _dot_hi(h, w3_ref[...]) * win_ref[...]
    o_ref[...] = h * lax.rsqrt(jnp.sum(h * h, axis=0, keepdims=True) + EPS)


def _hy_constants(l):
    pos = jnp.arange(l, dtype=F32)
    bands = jnp.arange(1, HY_BANDS + 1, dtype=F32)
    ang = (2.0 * math.pi / l) * pos[:, None] * bands[None, :]
    feat = jnp.concatenate([pos[:, None] / l, jnp.cos(ang), jnp.sin(ang)], axis=-1)
    feat = jnp.pad(feat, ((0, 0), (0, LANES - feat.shape[1])))
    rates = jnp.linspace(-math.log(HY_TARGET) / HY_FAST_DECAY, -math.log(HY_TARGET) / HY_SLOW_DECAY,
                         GROUP_W, dtype=F32)
    rates = jnp.tile(rates, 2)
    offset = jnp.abs(pos - l // 2) / l
    return feat, jnp.exp(-offset[:, None] * rates[None, :]) + HY_SHIFT


def _hy_filters(feat, win, w1, b1, w2, b2, w3, freq):
    l = feat.shape[0]
    hid = w2.shape[0]
    w1p = jnp.pad(w1, ((0, LANES - w1.shape[0]), (0, 0)))
    full = lambda shape: pl.BlockSpec(shape, lambda o: (0, 0))
    return pl.pallas_call(
        _hy_filter_kernel,
        grid=(2,),
        in_specs=[full((l, LANES)), pl.BlockSpec((l, GROUP_W), lambda o: (0, o)),
                  full((LANES, hid)), full((1, hid)), full((hid, hid)), full((1, hid)),
                  pl.BlockSpec((hid, GROUP_W), lambda o: (0, o)), full((2, hid))],
        out_specs=pl.BlockSpec((l, GROUP_W), lambda o: (0, o)),
        out_shape=jax.ShapeDtypeStruct((l, 2 * GROUP_W), F32),
        compiler_params=_cp("parallel"),
        name="hy_filters",
    )(feat, win, w1p, b1.reshape(1, hid), w2, b2.reshape(1, hid), w3, freq)


def _dft_matrices(l):
    n = 2 * l
    k = jnp.arange(l, dtype=jnp.int32)
    alt = jnp.where(k % 2 == 0, 1.0, -1.0).astype(F32)
    cos_f, sin_f = _cos_sin_grid(0, l, n)
    fwd_im = jnp.where(k[:, None] == 0, alt[None, :], -sin_f)
    fwd = jnp.stack([cos_f, fwd_im]).astype(BF16)
    t = k + l // 2
    cos_i, sin_i = _cos_sin_grid(l // 2, l, n)
    alt_t = jnp.where(t % 2 == 0, 1.0, -1.0).astype(F32)
    inv_re = jnp.where(k[None, :] == 0, 1.0 / n, (2.0 / n) * cos_i)
    inv_im = jnp.where(k[None, :] == 0, alt_t[:, None] / n, -(2.0 / n) * sin_i)
    inv = jnp.concatenate([inv_re, inv_im], axis=1).astype(BF16)
    return fwd, inv


def _cos_sin_grid(row0, l, n):
    s = 1 << (int(math.log2(l)) // 2)
    assert row0 % s == 0 and l % s == 0
    c = jnp.arange(l, dtype=jnp.int32)
    hi = row0 + jnp.arange(l // s, dtype=jnp.int32) * s
    lo = jnp.arange(s, dtype=jnp.int32)

    def angle(r):
        return (2.0 * math.pi / n) * ((r[:, None] * c[None, :]) % n).astype(F32)

    a, b = angle(hi), angle(lo)
    ca, sa = jnp.cos(a)[:, None, :], jnp.sin(a)[:, None, :]
    cb, sb = jnp.cos(b)[None, :, :], jnp.sin(b)[None, :, :]
    return (ca * cb - sa * sb).reshape(l, l), (sa * cb + ca * sb).reshape(l, l)


def _dft_filter_kernel(f_ref, h_ref, o_ref):
    h = h_ref[...].astype(BF16)
    o_ref[0] = jnp.dot(f_ref[0], h, preferred_element_type=F32)
    o_ref[1] = jnp.dot(f_ref[1], h, preferred_element_type=F32)


def _dft_filter(fwd, filt):
    l, c = filt.shape
    tm = _tile(l, 256)
    return pl.pallas_call(
        _dft_filter_kernel,
        grid=(l // tm,),
        in_specs=[pl.BlockSpec((2, tm, l), lambda i: (0, i, 0)), pl.BlockSpec((l, c), lambda i: (0, 0))],
        out_specs=pl.BlockSpec((2, tm, c), lambda i: (0, i, 0)),
        out_shape=jax.ShapeDtypeStruct((2, l, c), F32),
        compiler_params=_cp("parallel"),
        name="dft_filter",
    )(fwd, filt)


def _dft_fwd_mul_kernel(f_ref, u_ref, h_ref, o_ref):
    u = u_ref[0]
    ur = jnp.dot(f_ref[0], u, preferred_element_type=F32)
    ui = jnp.dot(f_ref[1], u, preferred_element_type=F32)
    hr, hi = h_ref[0], h_ref[1]
    row = lax.broadcasted_iota(jnp.int32, ur.shape, 0) + pl.program_id(0) * ur.shape[0]
    packed = row == 0
    o_ref[0, 0] = (ur * hr - jnp.where(packed, 0.0, ui * hi)).astype(BF16)
    o_ref[0, 1] = jnp.where(packed, ui * hi, ur * hi + ui * hr).astype(BF16)


def _dft_fwd_mul(fwd, u, hf, order):
    b, l, _ = u.shape
    tm = _tile(l, 1024)
    return pl.pallas_call(
        _dft_fwd_mul_kernel,
        grid=(l // tm, b),
        in_specs=[pl.BlockSpec((2, tm, l), lambda i, j: (0, i, 0)),
                  pl.BlockSpec((1, l, GROUP_W), lambda i, j: (j, 0, 0)),
                  pl.BlockSpec((2, tm, GROUP_W), lambda i, j: (0, i, order))],
        out_specs=pl.BlockSpec((1, 2, tm, GROUP_W), lambda i, j: (j, 0, i, 0)),
        out_shape=jax.ShapeDtypeStruct((b, 2, l, GROUP_W), BF16),
        compiler_params=_cp("parallel", "parallel"),
        name="dft_fwd_mul",
    )(fwd, u, hf)


def _dft_inv_gate_kernel(g_ref, y_ref, x_ref, u_ref, skip_ref, o_ref, lowp_ref):
    conv = jnp.dot(g_ref[...], y_ref[0], preferred_element_type=F32)
    out = x_ref[0] * (conv + skip_ref[...] * u_ref[0])
    o_ref[0] = out
    lowp_ref[0] = out.astype(BF16)


def _dft_inv_gate(inv, y, xg, x_col, u, u_col, skip):
    b, _, l, _ = y.shape
    tm = _tile(l, 1024)
    out_spec = pl.BlockSpec((1, tm, GROUP_W), lambda i, j: (j, i, 0))
    return pl.pallas_call(
        _dft_inv_gate_kernel,
        grid=(l // tm, b),
        in_specs=[pl.BlockSpec((tm, 2 * l), lambda i, j: (i, 0)),
                  pl.BlockSpec((1, 2 * l, GROUP_W), lambda i, j: (j, 0, 0)),
                  pl.BlockSpec((1, tm, GROUP_W), lambda i, j: (j, i, x_col)),
                  pl.BlockSpec((1, tm, GROUP_W), lambda i, j: (j, i, u_col)),
                  pl.BlockSpec((1, GROUP_W), lambda i, j: (0, 0))],
        out_specs=[out_spec, out_spec],
        out_shape=[jax.ShapeDtypeStruct((b, l, GROUP_W), F32), jax.ShapeDtypeStruct((b, l, GROUP_W), BF16)],
        compiler_params=_cp("parallel", "parallel"),
        name="dft_inv_gate",
    )(inv, y.reshape(b, 2 * l, GROUP_W), xg, u, skip.reshape(1, GROUP_W))


def _head_norm(x, gain):
    return x * lax.rsqrt(_seg_mean(x * x, HEAD_DIM) + EPS) * gain


def _proj_out_kernel(dnf_ref, dnb_ref, gate_ref, df_ref, hy_ref, mlf_ref, mlb_ref, mo_ref, x_ref, mod_ref,
                     gdn_ref, gml_ref, w_ref, o_ref):
    gate = gate_ref[0]
    y_dn = _head_norm(dnf_ref[0] + dnb_ref[0], gdn_ref[...]) * (gate * _sigmoid(gate))
    y_ml = _head_norm(mlf_ref[0] + mlb_ref[0], gml_ref[...]) * _sigmoid(mo_ref[0])
    y = _dot(y_dn, w_ref[0:GROUP_W, :])
    for h in range(N_HEADS):
        lo = GROUP_W + h * HEAD_DIM
        y = y + _dot(df_ref[0, h], w_ref[lo:lo + HEAD_DIM, :])
    y = y + _dot(hy_ref[0], w_ref[2 * GROUP_W:3 * GROUP_W, :])
    y = y + _dot(y_ml, w_ref[3 * GROUP_W:, :])
    o_ref[0] = x_ref[0] + mod_ref[0, 2:3, :] * y


def _proj_out(o_dn, z, o_df, y_hy, h_ml, x, mod, g_dn, g_ml, w):
    b, l, d = x.shape
    tm = _tile(l, 512)
    per_batch = mod.shape[0] > 1
    group = lambda col: pl.BlockSpec((1, tm, GROUP_W), lambda i, j: (i, j, col))
    return pl.pallas_call(
        _proj_out_kernel,
        grid=(b, l // tm),
        in_specs=[group(0), group(0), group(COL_DGATE),
                  pl.BlockSpec((1, N_HEADS, tm, HEAD_DIM), lambda i, j: (i, 0, j, 0)),
                  group(0), group(0), group(0), group(COL_MO),
                  pl.BlockSpec((1, tm, d), lambda i, j: (i, j, 0)),
                  pl.BlockSpec((1, 6, d), lambda i, j: (i if per_batch else 0, 0, 0)),
                  pl.BlockSpec((1, GROUP_W), lambda i, j: (0, 0)),
                  pl.BlockSpec((1, GROUP_W), lambda i, j: (0, 0)),
                  pl.BlockSpec((d, d), lambda i, j: (0, 0))],
        out_specs=pl.BlockSpec((1, tm, d), lambda i, j: (i, j, 0)),
        out_shape=jax.ShapeDtypeStruct((b, l, d), F32),
        compiler_params=_cp("parallel", "parallel", fuse_inputs=[False] * 12 + [True]),
        name="proj_out",
    )(o_dn[0], o_dn[1], z, o_df, y_hy, h_ml[0], h_ml[1], z, x, mod, jnp.tile(g_dn, N_HEADS).reshape(1, GROUP_W),
      jnp.tile(g_ml, N_HEADS).reshape(1, GROUP_W), w)


def _ffn_kernel(x_ref, mod_ref, g_ref, wg_ref, wu_ref, wd_ref, o_ref, h_ref, acc_ref):
    j = pl.program_id(2)

    @pl.when(j == 0)
    def _():
        h_ref[...] = _modulated_norm(x_ref[0], g_ref[...], mod_ref[0, 3:4, :], mod_ref[0, 4:5, :]).astype(BF16)
        acc_ref[...] = jnp.zeros(acc_ref.shape, F32)

    h = h_ref[...]
    g = jnp.dot(h, wg_ref[...], preferred_element_type=F32)
    u = jnp.dot(h, wu_ref[...], preferred_element_type=F32)
    acc_ref[...] += jnp.dot((g * _sigmoid(g) * u).astype(BF16), wd_ref[...], preferred_element_type=F32)

    @pl.when(j == pl.num_programs(2) - 1)
    def _():
        o_ref[0] = x_ref[0] + mod_ref[0, 5:6, :] * acc_ref[...]


def _ffn(x, mod, gain, wg, wu, wd):
    b, l, d = x.shape
    f = wg.shape[1]
    tm = _tile(l, 1024)
    tf = 256
    per_batch = mod.shape[0] > 1
    return pl.pallas_call(
        _ffn_kernel,
        grid=(b, l // tm, f // tf),
        in_specs=[pl.BlockSpec((1, tm, d), lambda i, a, j: (i, a, 0)),
                  pl.BlockSpec((1, 6, d), lambda i, a, j: (i if per_batch else 0, 0, 0)),
                  pl.BlockSpec((1, d), lambda i, a, j: (0, 0)),
                  pl.BlockSpec((d, tf), lambda i, a, j: (0, j)),
                  pl.BlockSpec((d, tf), lambda i, a, j: (0, j)),
                  pl.BlockSpec((tf, d), lambda i, a, j: (j, 0))],
        out_specs=pl.BlockSpec((1, tm, d), lambda i, a, j: (i, a, 0)),
        out_shape=jax.ShapeDtypeStruct((b, l, d), F32),
        scratch_shapes=[pltpu.VMEM((tm, d), BF16), pltpu.VMEM((tm, d), F32)],
        compiler_params=_cp("parallel", "parallel", "arbitrary"),
        name="ffn_dense",
    )(x, mod, gain.reshape(1, d), wg, wu, wd)


def _top2_combine(logits):
    lane = lax.broadcasted_iota(jnp.int32, logits.shape, 1)
    v1 = jnp.max(logits, axis=1, keepdims=True)
    i1 = jnp.min(jnp.where(logits == v1, lane, LANES), axis=1, keepdims=True)
    rest = jnp.where(lane == i1, NEG, logits)
    v2 = jnp.max(rest, axis=1, keepdims=True)
    i2 = jnp.min(jnp.where(rest == v2, lane, LANES), axis=1, keepdims=True)
    e2 = jnp.exp(v2 - v1)
    return jnp.where(lane == i1, 1.0 / (1.0 + e2), 0.0) + jnp.where(lane == i2, e2 / (1.0 + e2), 0.0)


MOE_ROW_TILE = 512


def _moe_route_kernel(x_ref, mod_ref, g_ref, r_ref, h_ref, comb_ref, rank_ref, cnt_ref):
    @pl.when((pl.program_id(0) == 0) & (pl.program_id(1) == 0))
    def _():
        cnt_ref[...] = jnp.zeros(cnt_ref.shape, F32)

    hn = _modulated_norm(x_ref[0], g_ref[...], mod_ref[0, 3:4, :], mod_ref[0, 4:5, :])
    tm = hn.shape[0]
    for j in range(h_ref.shape[0]):
        h_ref[j, 0] = hn[:, j * LANES:(j + 1) * LANES]
    lane = lax.broadcasted_iota(jnp.int32, (tm, LANES), 1)
    comb = _top2_combine(jnp.where(lane < N_EXPERTS, _dot_hi(hn, r_ref[...]), NEG))
    comb_ref[0] = comb
    routed = jnp.where(comb > 0.0, 1.0, 0.0)
    row = lax.broadcasted_iota(jnp.int32, (tm, tm), 0)
    col = lax.broadcasted_iota(jnp.int32, (tm, tm), 1)
    earlier = jnp.where(col < row, 1.0, 0.0).astype(BF16)
    rank = cnt_ref[...] + jnp.dot(earlier, routed.astype(BF16), preferred_element_type=F32)
    rank_ref[0] = jnp.where(routed > 0.0, rank, -1.0)
    cnt_ref[...] += jnp.sum(routed, axis=0, keepdims=True)


def _moe_group_kernel(tile_expert_ref, used_ref, xs_ref, wg_ref, wu_ref, wd_ref, o_ref):
    del tile_expert_ref
    pieces = xs_ref.shape[0]

    @pl.when(pl.program_id(0) < used_ref[0])
    def _():
        xs = jnp.concatenate([xs_ref[j] for j in range(pieces)], axis=1).astype(BF16)
        g = jnp.dot(xs, wg_ref[0], preferred_element_type=F32)
        u = jnp.dot(xs, wu_ref[0], preferred_element_type=F32)
        y = jnp.dot((g * _sigmoid(g) * u).astype(BF16), wd_ref[0], preferred_element_type=F32)
        for j in range(pieces):
            o_ref[j] = y[:, j * LANES:(j + 1) * LANES]

    @pl.when(pl.program_id(0) >= used_ref[0])
    def _():
        o_ref[...] = jnp.zeros(o_ref.shape, F32)


def _moe_combine_kernel(x_ref, y_ref, comb_ref, mod_ref, o_ref):
    comb = comb_ref[0]
    gate_a = jnp.max(comb, axis=1, keepdims=True)
    gate_b = jnp.sum(comb, axis=1, keepdims=True) - gate_a
    pieces = y_ref.shape[0]
    y_a = jnp.concatenate([y_ref[j, 0, 0] for j in range(pieces)], axis=1)
    y_b = jnp.concatenate([y_ref[j, 1, 0] for j in range(pieces)], axis=1)
    o_ref[0] = x_ref[0] + mod_ref[0, 5:6, :] * (gate_a * y_a + gate_b * y_b)


MOE_SLOT_ALIGN = 2048


def _moe(x, mod, gain, router, wg, wu, wd):
    b, l, d = x.shape
    ne, _, f = wg.shape
    t = b * l
    tm = _tile(l, 1024)
    tg = MOE_ROW_TILE
    per_batch = mod.shape[0] > 1
    router_p = jnp.pad(router, ((0, 0), (0, LANES - ne)))
    x_spec = pl.BlockSpec((1, tm, d), lambda i, j: (i, j, 0))
    mod_spec = pl.BlockSpec((1, 6, d), lambda i, j: (i if per_batch else 0, 0, 0))
    pieces = d // LANES
    lane_spec = pl.BlockSpec((1, tm, LANES), lambda i, j: (i, j, 0))
    h, comb, rank, cnt = pl.pallas_call(
        _moe_route_kernel,
        grid=(b, l // tm),
        in_specs=[x_spec, mod_spec, pl.BlockSpec((1, d), lambda i, j: (0, 0)),
                  pl.BlockSpec((d, LANES), lambda i, j: (0, 0))],
        out_specs=[pl.BlockSpec((pieces, 1, tm, LANES), lambda i, j: (0, i, j, 0)), lane_spec, lane_spec,
                   pl.BlockSpec((1, LANES), lambda i, j: (0, 0))],
        out_shape=[jax.ShapeDtypeStruct((pieces, b, l, LANES), F32), jax.ShapeDtypeStruct((b, l, LANES), F32),
                   jax.ShapeDtypeStruct((b, l, LANES), F32), jax.ShapeDtypeStruct((1, LANES), F32)],
        compiler_params=_cp("arbitrary", "arbitrary"),
        name="moe_route",
    )(x, mod, gain.reshape(1, d), router_p)

    n_slots = -(-(2 * t + (ne + 1) * tg) // MOE_SLOT_ALIGN) * MOE_SLOT_ALIGN
    comb2 = comb.reshape(t, LANES)[:, :ne]
    rank2 = rank.reshape(t, LANES)[:, :ne]
    grp = (cnt[0, :ne].astype(jnp.int32) + tg - 1) // tg * tg
    end = jnp.cumsum(grp)
    off = end - grp
    tile_start = jnp.arange(n_slots // tg, dtype=jnp.int32) * tg
    tile_expert = jnp.minimum(jnp.sum(tile_start[:, None] >= end[None, :], axis=1), ne - 1).astype(jnp.int32)
    used_tiles = (end[-1:] // tg).astype(jnp.int32)
    first = jnp.argmax(comb2, axis=1)
    second = jnp.argmax(jnp.where(jnp.arange(ne)[None, :] == first[:, None], -1.0, comb2), axis=1)
    pick = lambda a, e: jnp.take_along_axis(a, e[:, None], axis=1)[:, 0]
    rank_b = pick(rank2, second)
    pos_a = off[first] + pick(rank2, first).astype(jnp.int32)
    pos_b = off[second] + rank_b.astype(jnp.int32)
    token = jnp.arange(t, dtype=jnp.int32)
    slot_token = jnp.zeros((n_slots,), jnp.int32).at[
        jnp.concatenate([pos_a, jnp.where(rank_b >= 0.0, pos_b, n_slots)])].set(
        jnp.concatenate([token, token]), mode="drop")
    piece_base = jnp.arange(pieces, dtype=jnp.int32)[:, None]

    xs = _sc_gather_pieces(h.reshape(pieces * t, LANES), (piece_base * t + slot_token[None, :]).reshape(-1))
    piece_rows = pl.BlockSpec((pieces, tg, LANES), lambda i, te, used: (0, i, 0))
    ys = pl.pallas_call(
        _moe_group_kernel,
        grid_spec=pltpu.PrefetchScalarGridSpec(
            num_scalar_prefetch=2,
            grid=(n_slots // tg,),
            in_specs=[piece_rows,
                      pl.BlockSpec((1, d, f), lambda i, te, used: (te[i], 0, 0)),
                      pl.BlockSpec((1, d, f), lambda i, te, used: (te[i], 0, 0)),
                      pl.BlockSpec((1, f, d), lambda i, te, used: (te[i], 0, 0))],
            out_specs=piece_rows),
        out_shape=jax.ShapeDtypeStruct((pieces, n_slots, LANES), F32),
        compiler_params=_cp("arbitrary", fuse_inputs=[False, False, False, True, True, True]),
        name="moe_group",
    )(tile_expert, used_tiles, xs.reshape(pieces, n_slots, LANES), wg, wu, wd)
    pair = jnp.concatenate([pos_a, jnp.where(rank_b >= 0.0, pos_b, n_slots - 1)])
    y2 = _sc_gather_pieces(ys.reshape(pieces * n_slots, LANES), (piece_base * n_slots + pair[None, :]).reshape(-1))
    return pl.pallas_call(
        _moe_combine_kernel,
        grid=(b, l // tm),
        in_specs=[x_spec, pl.BlockSpec((pieces, 2, 1, tm, LANES), lambda i, j: (0, 0, i, j, 0)), lane_spec, mod_spec],
        out_specs=x_spec,
        out_shape=jax.ShapeDtypeStruct((b, l, d), F32),
        compiler_params=_cp("parallel", "parallel"),
        name="moe_combine",
    )(x, y2.reshape(pieces, 2, b, l, LANES), comb, mod)


SC_GATHER_WINDOW = 128


def _sc_gather_pieces(table, idx):
    m = idx.shape[0]
    w = table.shape[1]
    mesh = plsc.VectorSubcoreMesh(core_axis_name="core", subcore_axis_name="subcore")

    @functools.partial(pl.kernel, out_type=jax.ShapeDtypeStruct((m, w), table.dtype), mesh=mesh,
                       name="sc_gather_rows")
    def gather(t_hbm, i_hbm, o_hbm):
        def body(i_vmem, o_vmem):
            pltpu.sync_copy(t_hbm.at[i_vmem.at[0]], o_vmem)

        pltpu.emit_pipeline(
            body,
            grid=(m // SC_GATHER_WINDOW,),
            in_specs=[pl.BlockSpec((1, SC_GATHER_WINDOW), lambda i: (0, i))],
            out_specs=[pl.BlockSpec((SC_GATHER_WINDOW, w), lambda i: (i, 0))],
            core_axis_name=("core", "subcore"),
            dimension_semantics=(pltpu.PARALLEL,),
        )(i_hbm, o_hbm)

    return gather(table, idx.reshape(1, m))


def _lane_row(values, lane0):
    row = jnp.zeros((LANES,), F32)
    return lax.dynamic_update_slice(row, values.reshape(-1).astype(F32), (lane0,))


def _reorder_w_in(w_in):
    g = GROUP_W
    o = [0, g, 2 * g, 3 * g, 4 * g, 4 * g + 8, 4 * g + 16]
    dq_dk_dv = w_in[..., o[0]:o[3]]
    dgate = w_in[..., o[3]:o[4]]
    dbeta_da = w_in[..., o[4]:o[6]]
    base = o[6]
    df = w_in[..., base:base + 3 * g]
    hy = w_in[..., base + 3 * g:base + 6 * g]
    ml = w_in[..., base + 6 * g:base + 9 * g]
    mo = w_in[..., base + 9 * g:base + 10 * g]
    mi_mf = w_in[..., base + 10 * g:base + 10 * g + 16]
    pad = jnp.zeros(w_in.shape[:-1] + (LANES - 32,), w_in.dtype)
    return jnp.concatenate([dq_dk_dv, hy, ml, df, dgate, mo, dbeta_da, mi_mf, pad], axis=-1).astype(BF16)


def _layer(l, x, mod, cache, p, hy_consts, dft):
    b, seq, _ = x.shape
    latent = cache is not None
    z = _proj_in(x, mod, p["norm1_g"][l], p["w_in"][l])

    qkv, = _dwconv(z, p["dn_conv"][l], COL_DN, "deltanet")
    dn_par = jnp.zeros((8, LANES), F32).at[0].set(_lane_row(p["dn_a_log"][l], LANE_A)).at[1].set(
        _lane_row(p["dn_dt_bias"][l], LANE_A))
    s0 = cache[2] if latent else jnp.zeros((b, 2, N_HEADS, HEAD_DIM, HEAD_DIM), F32)
    o_dn_f, o_dn_b, s_dn = _deltanet(qkv, z, dn_par, s0)
    o_dn = (o_dn_f, o_dn_b)

    ml_par = jnp.zeros((8, LANES), F32).at[0].set(_lane_row(p["ml_i_bias"][l], LANE_I)).at[1].set(
        _lane_row(p["ml_f_bias"][l], LANE_F))
    if latent:
        c0 = cache[3]
        n0 = cache[4].reshape(b, 2, 1, GROUP_W)
        m0 = jnp.repeat(cache[5], HEAD_DIM, axis=-1).reshape(b, 2, 1, GROUP_W)
    else:
        c0 = jnp.zeros((b, 2, N_HEADS, HEAD_DIM, HEAD_DIM), F32)
        n0 = m0 = jnp.zeros((b, 2, 1, GROUP_W), F32)
    h_ml_f, h_ml_b, c_ml, n_ml, m_ml = _mlstm(z, ml_par, c0, n0, m0)
    h_ml = (h_ml_f, h_ml_b)
    n_ml = n_ml.reshape(b, 2, N_HEADS, HEAD_DIM)
    m_ml = m_ml.reshape(b, 2, N_HEADS, HEAD_DIM)[..., 0]

    q, kt, v, kd = _diff_prep(z, p["df_q_norm"][l], p["df_k_norm"][l], latent)
    if latent:
        ckt, cv = _cache_prep(cache[0].reshape(b, -1, GROUP_W), cache[1].reshape(b, -1, GROUP_W))
        kt = jnp.concatenate([kt, ckt], axis=-1)
        v = jnp.concatenate([v, cv], axis=2)
    lam_init = 0.8 - 0.6 * math.exp(-0.3 * l)
    o_df = _diff_attn(q, kt, v, p["df_lambda"][l], p["df_norm"][l], lam_init)

    feat, win = hy_consts
    fwd, inv = dft
    zc, zv_lowp = _dwconv(z, p["hy_conv"][l], COL_HY, "plain")
    filt = _hy_filters(feat, win, p["hy_w1"][l], p["hy_b1"][l], p["hy_w2"][l], p["hy_b2"][l], p["hy_w3"][l],
                       p["hy_freq"][l])
    hf = _dft_filter(fwd, filt)
    y1 = _dft_fwd_mul(fwd, zv_lowp, hf, 0)
    z1, z1_lowp = _dft_inv_gate(inv, y1, zc, 0, zc, 2, p["hy_skip"][l, 0])
    y2 = _dft_fwd_mul(fwd, z1_lowp, hf, 1)
    y_hy, _ = _dft_inv_gate(inv, y2, zc, 1, z1, 0, p["hy_skip"][l, 1])

    x = _proj_out(o_dn, z, o_df, y_hy, h_ml, x, mod, p["dn_norm"][l], p["ml_norm"][l], p["w_out"][l])
    j = l // 2
    if mod.shape[0] == 1:
        x = x.reshape(1, b * seq, -1)
    if l % 2 == 0:
        x = _ffn(x, mod, p["norm2_g"][l], p["ffn_w_gate"][j], p["ffn_w_up"][j], p["ffn_w_down"][j])
    else:
        x = _moe(x, mod, p["norm2_g"][l], p["moe_router"][j], p["moe_w_gate"][j], p["moe_w_up"][j],
                 p["moe_w_down"][j])
    x = x.reshape(b, seq, -1)
    fv =z[:, :, 3 * 3 * GROUP_W + 2 * GROUP_W:3 * 3 * GROUP_W + 3 * GROUP_W]
    return x, (kd, fv, s_dn, c_ml, n_ml, m_ml)


def kernel(x_prompt, x_sample, cache_diff_k, cache_diff_v, state_delta, state_mlstm_c, state_mlstm_n, state_mlstm_m, c, c_ctx, norm1_g, norm2_g, w_mod, b_mod, w_in, w_out, dn_conv, dn_a_log, dn_dt_bias, dn_norm, df_q_norm, df_k_norm, df_lambda, df_norm, hy_conv, hy_w1, hy_b1, hy_w2, hy_b2, hy_w3, hy_freq, hy_skip, ml_i_bias, ml_f_bias, ml_norm, ffn_w_gate, ffn_w_up, ffn_w_down, moe_router, moe_w_gate, moe_w_up, moe_w_down):
    depth = w_in.shape[0]
    d_model = x_prompt.shape[-1]
    batch, seq, _ = x_prompt.shape
    dec_batch, dec_seq, _ = x_sample.shape
    p = dict(norm1_g=norm1_g, norm2_g=norm2_g, w_in=_reorder_w_in(w_in), w_out=w_out.astype(BF16),
             dn_conv=dn_conv, dn_a_log=dn_a_log, dn_dt_bias=dn_dt_bias, dn_norm=dn_norm,
             df_q_norm=df_q_norm, df_k_norm=df_k_norm, df_lambda=df_lambda, df_norm=df_norm,
             hy_conv=hy_conv, hy_w1=hy_w1, hy_b1=hy_b1, hy_w2=hy_w2, hy_b2=hy_b2, hy_w3=hy_w3,
             hy_freq=hy_freq, hy_skip=hy_skip, ml_i_bias=ml_i_bias, ml_f_bias=ml_f_bias, ml_norm=ml_norm,
             ffn_w_gate=ffn_w_gate.astype(BF16), ffn_w_up=ffn_w_up.astype(BF16),
             ffn_w_down=ffn_w_down.astype(BF16), moe_router=moe_router,
             moe_w_gate=moe_w_gate.astype(BF16), moe_w_up=moe_w_up.astype(BF16),
             moe_w_down=moe_w_down.astype(BF16))

    n_cond = 1 + dec_batch
    rows = -(-n_cond // 8) * 8
    cond = jnp.concatenate([c_ctx[None, :], c, jnp.zeros((rows - n_cond, d_model), F32)], axis=0)
    mod = _modulation(cond, w_mod, b_mod).reshape(depth, rows, 6, d_model)

    hy_ctx, dft_ctx = _hy_constants(seq), _dft_matrices(seq)
    x = x_prompt
    ctx = []
    for l in range(depth):
        x, out = _layer(l, x, mod[l, 0:1], None, p, hy_ctx, dft_ctx)
        ctx.append(out)
    y_prompt = x
    new_k, new_v, new_s, new_c, new_n, new_m = (jnp.stack([o[i] for o in ctx], axis=1) for i in range(6))
    new_k = new_k.reshape(batch, depth, seq, N_HEADS, 2, DF_DH)
    new_v = new_v.reshape(batch, depth, seq, N_HEADS, HEAD_DIM)

    hy_lat, dft_lat = _hy_constants(dec_seq), _dft_matrices(dec_seq)
    x = x_sample
    for l in range(depth):
        cache = (cache_diff_k[:, l], cache_diff_v[:, l], state_delta[:, l], state_mlstm_c[:, l],
                 state_mlstm_n[:, l], state_mlstm_m[:, l])
        x, _ = _layer(l, x, mod[l, 1:1 + dec_batch], cache, p, hy_lat, dft_lat)
    return (y_prompt, x, new_k, new_v, new_s, new_c, new_n, new_m)
```
